```python
import math
import jax
import jax.numpy as jnp
from jax import lax
import numpy as np

D_MODEL = 1024
BATCH = 16
SEQ = 2048
DEPTH = 1

CTX_LEN = 256
GRID_W = 64
D_MIX = D_MODEL
D_S5 = D_MIX // 2
S5_GROUP = 16
S5_GROUPS = D_S5 // S5_GROUP
S5_STATE = 64
D_GDN = D_MIX - D_S5
GDN_HEAD = 128
GDN_HEADS = D_GDN // GDN_HEAD
CHUNK = 64
CONV_K = 3
N_DIR = 2
P_IN = 2 * D_S5 + 4 * D_GDN + 2 * N_DIR * GDN_HEADS
DEEPNORM_ALPHA = (2.0 * DEPTH) ** 0.25
DEEPNORM_BETA = (8.0 * DEPTH) ** -0.25
LN_EPS = 1e-5
NORM_EPS = 1e-6

kernel_name = "hybrid_s5_gdn_prefix_dit_block"


def _proj_splits():
    widths = (D_S5, D_S5, 3 * D_GDN, D_GDN, N_DIR * GDN_HEADS, N_DIR * GDN_HEADS)
    return tuple(int(v) for v in np.cumsum(widths)[:-1])


def _layer_norm(x, g, b):
    xf = x.astype(jnp.float32)
    mu = jnp.mean(xf, -1, keepdims=True)
    var = jnp.mean(jnp.square(xf - mu), -1, keepdims=True)
    return ((xf - mu) * lax.rsqrt(var + LN_EPS) * g.astype(jnp.float32) + b.astype(jnp.float32)).astype(x.dtype)


def _ada(cond, w_ada, b_ada):
    m = jax.nn.silu(cond) @ w_ada + b_ada
    return jnp.split(m, 3, axis=-1)


def _maybe_flip(t, rev):
    return jnp.flip(t, 1) if rev else t


def _s5_zoh(lam_re, lam_im, log_dt, b_re, b_im):
    lam = lax.complex(lam_re.astype(jnp.float32), lam_im.astype(jnp.float32))
    dt = jnp.exp(log_dt.astype(jnp.float32))[:, None]
    abar = jnp.exp(lam * dt)
    bmat = lax.complex(b_re.astype(jnp.float32), b_im.astype(jnp.float32))
    bbar = ((abar - 1.0) / lam)[..., None] * bmat
    return abar, bbar


def _s5_combine(e1, e2):
    a1, b1 = e1
    a2, b2 = e2
    return a1 * a2, a2 * b1 + b2


def _s5_states(u, abar, bbar, h0, rev):
    bu = jnp.einsum('blgc,gpc->blgp', u, bbar)
    bu = _maybe_flip(bu, rev)
    bu = bu.at[:, 0].add(abar * h0)
    a = jnp.broadcast_to(abar, (1,) + bu.shape[1:])
    _, h = lax.associative_scan(_s5_combine, (a, bu), axis=1)
    return _maybe_flip(h, rev), h[:, -1]


def _s5_readout(h, c_re, c_im):
    cmat = lax.complex(c_re.astype(jnp.float32), c_im.astype(jnp.float32))
    y = jnp.real(jnp.einsum('blgp,gcp->blgc', h, cmat))
    return y.reshape(y.shape[0], y.shape[1], D_S5)


def _s5_glu(y, w_glu, b_glu):
    g = jax.nn.gelu(y, approximate=False)
    return g * jax.nn.sigmoid(g @ w_glu.astype(jnp.float32) + b_glu.astype(jnp.float32))


def _s5_mixer(u, uc, z, zc, p, with_ctx):
    B_ = u.shape[0]
    uf = u.astype(jnp.float32)
    ucf = uc.astype(jnp.float32)
    ul = uf.reshape(B_, -1, S5_GROUPS, S5_GROUP)
    ucg = ucf.reshape(B_, -1, S5_GROUPS, S5_GROUP)
    d_skip = p['s5_d'].astype(jnp.float32)
    y_lat = d_skip * uf
    y_ctx = d_skip * ucf if with_ctx else None
    for d in range(N_DIR):
        abar, bbar = _s5_zoh(p['s5_lambda_re'][d], p['s5_lambda_im'][d], p['s5_log_dt'][d],
                             p['s5_b_re'][d], p['s5_b_im'][d])
        h0 = jnp.zeros((B_, S5_GROUPS, S5_STATE), jnp.complex64)
        hs_ctx, h_ctx_end = _s5_states(ucg, abar, bbar, h0, d == 1)
        hs_lat, _ = _s5_states(ul, abar, bbar, h_ctx_end, d == 1)
        y_lat = y_lat + _s5_readout(hs_lat, p['s5_c_re'][d], p['s5_c_im'][d])
        if with_ctx:
            y_ctx = y_ctx + _s5_readout(hs_ctx, p['s5_c_re'][d], p['s5_c_im'][d])
    out_lat = _s5_glu(y_lat, p['w_glu'], p['b_glu']) * jax.nn.silu(z.astype(jnp.float32))
    out_ctx = None
    if with_ctx:
        out_ctx = _s5_glu(y_ctx, p['w_glu'], p['b_glu']) * jax.nn.silu(zc.astype(jnp.float32))
    return out_lat, out_ctx


def _conv_latent(t, w):
    B_, L, C = t.shape
    rows = L // GRID_W
    img = t.reshape(B_, rows, GRID_W, C)
    out = lax.conv_general_dilated(img, w[:, :, None, :].astype(t.dtype), (1, 1), 'SAME',
                                   dimension_numbers=('NHWC', 'HWIO', 'NHWC'), feature_group_count=C)
    return out.reshape(B_, L, C)


def _conv_context(t, w):
    C = t.shape[-1]
    return lax.conv_general_dilated(t, w[CONV_K // 2][:, None, :].astype(t.dtype), (1,), 'SAME',
                                    dimension_numbers=('NWC', 'WIO', 'NWC'), feature_group_count=C)


def _l2norm(t):
    return t * lax.rsqrt(jnp.sum(jnp.square(t), -1, keepdims=True) + NORM_EPS)


def _gdn_qkv_heads(qkv):
    B_, L, _ = qkv.shape
    q, k, v = jnp.split(qkv.astype(jnp.float32), 3, axis=-1)
    q = _l2norm(q.reshape(B_, L, GDN_HEADS, GDN_HEAD)) * (GDN_HEAD ** -0.5)
    k = _l2norm(k.reshape(B_, L, GDN_HEADS, GDN_HEAD))
    v = v.reshape(B_, L, GDN_HEADS, GDN_HEAD)
    return q, k, v


def _gdn_gates(beta_logit, alpha_logit, a_log, dt_bias):
    B_, L, _ = beta_logit.shape
    beta = jax.nn.sigmoid(beta_logit.astype(jnp.float32)).reshape(B_, L, N_DIR, GDN_HEADS)
    a = alpha_logit.astype(jnp.float32).reshape(B_, L, N_DIR, GDN_HEADS)
    g = -jnp.exp(a_log.astype(jnp.float32)) * jax.nn.softplus(a + dt_bias.astype(jnp.float32))
    return beta, g


def _gdn_chunked(q, k, v, beta, g, s0):
    B_, L, H, _ = q.shape
    Dv = v.shape[-1]
    N = L // CHUNK

    def chunk(t):
        return t.reshape(B_, N, CHUNK, H, -1).transpose(1, 0, 3, 2, 4)

    qc, kc, vc = chunk(q), chunk(k), chunk(v)
    bc = chunk(beta[..., None])[..., 0]
    gcum = jnp.cumsum(chunk(g[..., None])[..., 0], axis=-1)
    idx = jnp.arange(CHUNK)
    lower = idx[:, None] >= idx[None, :]
    strict = idx[:, None] > idx[None, :]
    decay = jnp.exp(jnp.where(lower, gcum[..., :, None] - gcum[..., None, :], -jnp.inf))
    kk = jnp.einsum('nbhik,nbhjk->nbhij', kc, kc)
    a_mat = jnp.where(strict, bc[..., :, None] * kk * decay, 0.0)
    gamma = jnp.exp(gcum)
    rhs = jnp.concatenate([bc[..., None] * vc, (bc * gamma)[..., None] * kc], axis=-1)
    eye = jnp.eye(CHUNK, dtype=jnp.float32)
    sol = lax.linalg.triangular_solve(a_mat + eye, rhs, left_side=True, lower=True, unit_diagonal=True)
    u0, w = sol[..., :Dv], sol[..., Dv:]
    qk = jnp.einsum('nbhik,nbhjk->nbhij', qc, kc) * decay
    k_out = kc * jnp.exp(gcum[..., -1:] - gcum)[..., None]
    gamma_last = gamma[..., -1]

    def step(s, inp):
        qi, ki, ui0, wi, qki, gi, gli = inp
        u = ui0 - jnp.einsum('bhck,bhkv->bhcv', wi, s)
        o = gi[..., None] * jnp.einsum('bhck,bhkv->bhcv', qi, s) + jnp.einsum('bhij,bhjv->bhiv', qki, u)
        s = gli[..., None, None] * s + jnp.einsum('bhck,bhcv->bhkv', ki, u)
        return s, o

    s_fin, o = lax.scan(step, s0, (qc, k_out, u0, w, qk, gamma, gamma_last))
    o = o.transpose(1, 0, 3, 2, 4).reshape(B_, L, H, Dv)
    return o, s_fin


def _gated_rmsnorm(o, z, w):
    o = o * lax.rsqrt(jnp.mean(jnp.square(o), -1, keepdims=True) + NORM_EPS) * w.astype(jnp.float32)
    return o.reshape(z.shape) * jax.nn.silu(z.astype(jnp.float32))


def _gdn_mixer(qkv, qkvc, z, zc, beta_l, beta_c, alpha_l, alpha_c, p, with_ctx):
    qkv = jax.nn.silu(_conv_latent(qkv, p['conv_w']))
    qkvc = jax.nn.silu(_conv_context(qkvc, p['conv_w']))
    ql, kl, vl = _gdn_qkv_heads(qkv)
    qc, kc, vc = _gdn_qkv_heads(qkvc)
    bl, gl = _gdn_gates(beta_l, alpha_l, p['gdn_a_log'], p['gdn_dt_bias'])
    bcx, gcx = _gdn_gates(beta_c, alpha_c, p['gdn_a_log'], p['gdn_dt_bias'])
    B_ = ql.shape[0]
    o_lat = jnp.zeros(vl.shape, jnp.float32)
    o_ctx = jnp.zeros(vc.shape, jnp.float32)
    for d in range(N_DIR):
        rev = d == 1
        f = lambda t: _maybe_flip(t, rev)
        s0 = jnp.zeros((B_, GDN_HEADS, GDN_HEAD, GDN_HEAD), jnp.float32)
        oc, s_ctx = _gdn_chunked(f(qc), f(kc), f(vc), f(bcx[:, :, d]), f(gcx[:, :, d]), s0)
        ol, _ = _gdn_chunked(f(ql), f(kl), f(vl), f(bl[:, :, d]), f(gl[:, :, d]), s_ctx)
        o_lat = o_lat + f(ol)
        o_ctx = o_ctx + f(oc)
    out_lat = _gated_rmsnorm(o_lat, z, p['gdn_norm_w'])
    out_ctx = _gated_rmsnorm(o_ctx, zc, p['gdn_norm_w']) if with_ctx else None
    return out_lat, out_ctx


def _layer(x, ctx, c, c_ctx, p, with_ctx):
    shift, scale, gate = _ada(c, p['w_ada'], p['b_ada'])
    shift_c, scale_c, gate_c = _ada(c_ctx, p['w_ada'], p['b_ada'])
    h = x * (1.0 + scale[:, None]) + shift[:, None]
    hc = ctx * (1.0 + scale_c) + shift_c
    u, z_s5, qkv, z_gdn, beta_l, alpha_l = jnp.split(h @ p['w_in'], _proj_splits(), axis=-1)
    uc, z_s5c, qkvc, z_gdnc, beta_c, alpha_c = jnp.split(hc @ p['w_in'], _proj_splits(), axis=-1)
    s5_lat, s5_ctx = _s5_mixer(u, uc, z_s5, z_s5c, p, with_ctx)
    gdn_lat, gdn_ctx = _gdn_mixer(qkv, qkvc, z_gdn, z_gdnc, beta_l, beta_c, alpha_l, alpha_c, p, with_ctx)
    w_out = p['w_out'].astype(jnp.float32)
    y = jnp.concatenate([s5_lat, gdn_lat], axis=-1) @ w_out
    x_new = _layer_norm(DEEPNORM_ALPHA * x + gate[:, None] * y.astype(x.dtype), p['ln_g'], p['ln_b'])
    if not with_ctx:
        return x_new, ctx
    yc = jnp.concatenate([s5_ctx, gdn_ctx], axis=-1) @ w_out
    ctx_new = _layer_norm(DEEPNORM_ALPHA * ctx + gate_c * yc.astype(ctx.dtype), p['ln_g'], p['ln_b'])
    return x_new, ctx_new


def setup_inputs(seed: int = 0) -> dict:
    key = jax.random.key(seed)
    ks = jax.random.split(key, 24)
    f32 = jnp.float32

    def nrm(k, shape, s):
        return jax.random.normal(k, shape, f32) * s

    lam_shape = (DEPTH, N_DIR, S5_GROUPS, S5_STATE)
    n = jnp.arange(S5_STATE, dtype=f32)
    dt = jnp.exp(jax.random.uniform(ks[19], (DEPTH, N_DIR, GDN_HEADS), f32, math.log(1e-3), math.log(1e-1)))
    return {
        "x": nrm(ks[0], (BATCH, SEQ, D_MODEL), 1.0),
        "c": nrm(ks[1], (BATCH, D_MODEL), 1.0),
        "ctx": nrm(ks[2], (BATCH, CTX_LEN, D_MODEL), 1.0),
        "c_ctx": nrm(ks[3], (D_MODEL,), 1.0),
        "w_ada": nrm(ks[4], (DEPTH, D_MODEL, 3 * D_MODEL), 0.5 * D_MODEL ** -0.5),
        "b_ada": nrm(ks[5], (DEPTH, 3 * D_MODEL), 0.01),
        "w_in": nrm(ks[6], (DEPTH, D_MODEL, P_IN), D_MODEL ** -0.5),
        "s5_lambda_re": -0.5 + nrm(ks[7], lam_shape, 0.01),
        "s5_lambda_im": math.pi * n + nrm(ks[8], lam_shape, 0.01),
        "s5_log_dt": jax.random.uniform(ks[9], (DEPTH, N_DIR, S5_GROUPS), f32, math.log(1e-3), math.log(1e-1)),
        "s5_b_re": nrm(ks[10], (DEPTH, N_DIR, S5_GROUPS, S5_STATE, S5_GROUP), (2 * S5_GROUP) ** -0.5),
        "s5_b_im": nrm(ks[11], (DEPTH, N_DIR, S5_GROUPS, S5_STATE, S5_GROUP), (2 * S5_GROUP) ** -0.5),
        "s5_c_re": nrm(ks[12], (DEPTH, N_DIR, S5_GROUPS, S5_GROUP, S5_STATE), (2 * S5_STATE) ** -0.5),
        "s5_c_im": nrm(ks[13], (DEPTH, N_DIR, S5_GROUPS, S5_GROUP, S5_STATE), (2 * S5_STATE) ** -0.5),
        "s5_d": nrm(ks[14], (DEPTH, D_S5), 1.0),
        "w_glu": nrm(ks[15], (DEPTH, D_S5, D_S5), D_S5 ** -0.5),
        "b_glu": nrm(ks[16], (DEPTH, D_S5), 0.01),
        "conv_w": nrm(ks[17], (DEPTH, CONV_K, CONV_K, 3 * D_GDN), 1.0 / CONV_K),
        "gdn_a_log": jnp.log(jax.random.uniform(ks[18], (DEPTH, N_DIR, GDN_HEADS), f32, 1.0, 16.0)),
        "gdn_dt_bias": dt + jnp.log(-jnp.expm1(-dt)),
        "gdn_norm_w": 1.0 + nrm(ks[20], (DEPTH, GDN_HEAD), 0.01),
        "w_out": nrm(ks[21], (DEPTH, D_MIX, D_MODEL), DEEPNORM_BETA * D_MIX ** -0.5),
        "ln_g": 1.0 + nrm(ks[22], (DEPTH, D_MODEL), 0.01),
        "ln_b": nrm(ks[23], (DEPTH, D_MODEL), 0.01),
    }


def reference(x, c, ctx, c_ctx, w_ada, b_ada, w_in, s5_lambda_re, s5_lambda_im, s5_log_dt,
              s5_b_re, s5_b_im, s5_c_re, s5_c_im, s5_d, w_glu, b_glu, conv_w, gdn_a_log,
              gdn_dt_bias, gdn_norm_w, w_out, ln_g, ln_b):
    for l in range(DEPTH):
        p = {
            'w_ada': w_ada[l], 'b_ada': b_ada[l], 'w_in': w_in[l],
            's5_lambda_re': s5_lambda_re[l], 's5_lambda_im': s5_lambda_im[l], 's5_log_dt': s5_log_dt[l],
            's5_b_re': s5_b_re[l], 's5_b_im': s5_b_im[l], 's5_c_re': s5_c_re[l], 's5_c_im': s5_c_im[l],
            's5_d': s5_d[l], 'w_glu': w_glu[l], 'b_glu': b_glu[l], 'conv_w': conv_w[l],
            'gdn_a_log': gdn_a_log[l], 'gdn_dt_bias': gdn_dt_bias[l], 'gdn_norm_w': gdn_norm_w[l],
            'w_out': w_out[l], 'ln_g': ln_g[l], 'ln_b': ln_b[l],
        }
        x, ctx = _layer(x, ctx, c, c_ctx, p, l < DEPTH - 1)
    return x
```

```python
import functools
import math

import numpy as np
import jax
import jax.numpy as jnp
from jax import lax
from jax.experimental import pallas as pl
from jax.experimental.pallas import tpu as pltpu

f32 = jnp.float32
bf16 = jnp.bfloat16
HI = lax.Precision.HIGHEST

LANES = 128
S5_GROUP = 16
S5_STATE = 64
S5_T = 8
PACK_G = LANES // S5_GROUP
GDN_HEAD = 128
GDN_CHUNK = 64
GRID_W = 64
N_DIR = 2
LN_EPS = 1e-5
NORM_EPS = 1e-6
VMEM_LIMIT = 56 * 1024 * 1024


def _cparams(sem):
    return pltpu.CompilerParams(dimension_semantics=sem, vmem_limit_bytes=VMEM_LIMIT)


def _silu(x):
    return x * jax.nn.sigmoid(x)


def _ada_kernel(c_ref, w_ref, b_ref, o_ref):
    o_ref[...] = jnp.dot(_silu(c_ref[...]), w_ref[...], precision=HI,
                         preferred_element_type=f32) + b_ref[...]


def _ada(cond, w_ada, b_ada):
    rows, d = cond.shape
    n = w_ada.shape[1]
    tn = 512
    return pl.pallas_call(
        _ada_kernel,
        grid=(n // tn,),
        in_specs=[pl.BlockSpec((rows, d), lambda i: (0, 0)),
                  pl.BlockSpec((d, tn), lambda i: (0, i)),
                  pl.BlockSpec((1, tn), lambda i: (0, i))],
        out_specs=pl.BlockSpec((rows, tn), lambda i: (0, i)),
        out_shape=jax.ShapeDtypeStruct((rows, n), f32),
        compiler_params=_cparams(("parallel",)),
        name="ada",
    )(cond, w_ada, b_ada.reshape(1, n))


def _inproj_kernel(x_ref, scale_ref, shift_ref, wu_ref, wm_ref, wg_ref, gp_ref,
                   u_ref, zs_ref, qkv_ref, zg_ref, gate_ref, *, tt, d_s5, d_gdn, n_beta):
    nb, _, d = x_ref.shape
    sc = 1.0 + scale_ref[...]
    sh = shift_ref[...]
    h = (x_ref[...] * sc + sh).reshape(nb * tt, d).astype(bf16)
    r = jnp.dot(h, wm_ref[...], preferred_element_type=f32)
    zs_ref[...] = r[:, :d_s5].reshape(nb, tt, d_s5)
    qkv_ref[...] = r[:, d_s5:d_s5 + 3 * d_gdn].reshape(nb, tt, 3 * d_gdn)
    zg_ref[...] = r[:, d_s5 + 3 * d_gdn:].reshape(nb, tt, d_gdn)
    lg = jnp.dot(h, wg_ref[...], preferred_element_type=f32)
    lane = lax.broadcasted_iota(jnp.int32, lg.shape, 1)
    a = lg + gp_ref[1:2, :]
    softplus = jnp.maximum(a, 0.0) + jnp.log1p(jnp.exp(-jnp.abs(a)))
    gates = jnp.where(lane < n_beta, jax.nn.sigmoid(lg), -jnp.exp(gp_ref[0:1, :]) * softplus)
    gate_ref[...] = gates.reshape(nb, tt, LANES)
    sc2 = sc[:, 0, :]
    sh2 = sh[:, 0, :]
    ht = jnp.concatenate([x_ref[:, t, :] * sc2 + sh2 for t in range(tt)], axis=0).astype(bf16)
    ru = jnp.dot(ht, wu_ref[...], preferred_element_type=f32)
    for j in range(d_s5 // LANES):
        u_ref[j] = ru[:, j * LANES:(j + 1) * LANES].reshape(tt // S5_T, S5_T, nb, LANES)


def _inproj(x, scale, shift, wu, wm, wg, gp, d_s5, d_gdn):
    nb, L, d = x.shape
    tt = 32
    npk = d_s5 // LANES
    kern = functools.partial(_inproj_kernel, tt=tt, d_s5=d_s5, d_gdn=d_gdn,
                             n_beta=N_DIR * (d_gdn // GDN_HEAD))
    const2 = lambda i: (0, 0)
    const3 = lambda i: (0, 0, 0)
    return pl.pallas_call(
        kern,
        grid=(L // tt,),
        in_specs=[pl.BlockSpec((nb, tt, d), lambda i: (0, i, 0)),
                  pl.BlockSpec((nb, 1, d), const3),
                  pl.BlockSpec((nb, 1, d), const3),
                  pl.BlockSpec(wu.shape, const2),
                  pl.BlockSpec(wm.shape, const2),
                  pl.BlockSpec(wg.shape, const2),
                  pl.BlockSpec(gp.shape, const2)],
        out_specs=[pl.BlockSpec((npk, tt // S5_T, S5_T, nb, LANES), lambda i: (0, i, 0, 0, 0)),
                   pl.BlockSpec((nb, tt, d_s5), lambda i: (0, i, 0)),
                   pl.BlockSpec((nb, tt, 3 * d_gdn), lambda i: (0, i, 0)),
                   pl.BlockSpec((nb, tt, d_gdn), lambda i: (0, i, 0)),
                   pl.BlockSpec((nb, tt, LANES), lambda i: (0, i, 0))],
        out_shape=[jax.ShapeDtypeStruct((npk, L // S5_T, S5_T, nb, LANES), f32),
                   jax.ShapeDtypeStruct((nb, L, d_s5), f32),
                   jax.ShapeDtypeStruct((nb, L, 3 * d_gdn), f32),
                   jax.ShapeDtypeStruct((nb, L, d_gdn), f32),
                   jax.ShapeDtypeStruct((nb, L, LANES), f32)],
        compiler_params=_cparams(("parallel",)),
        name="inproj",
    )(x, scale, shift, wu, wm, wg, gp)


def _s5_kernel(*refs, ct, rev, emit_y, intra):
    it = iter(refs)
    u_ref, min_ref, at_ref, h0_ref = next(it), next(it), next(it), next(it)
    mout_ref = next(it) if emit_y else None
    mintra_ref = next(it) if intra else None
    d_ref = next(it) if intra else None
    y_ref = next(it) if emit_y else None
    hfin_ref = next(it)
    h_sc, v_sc, hall_sc = next(it), next(it), next(it)

    t = pl.program_id(1)
    nt = pl.num_programs(1)
    nb = u_ref.shape[3]
    half = h_sc.shape[1] // 2

    @pl.when(t == 0)
    def _():
        h_sc[...] = h0_ref[0]

    x = jnp.concatenate([u_ref[0, :, i, :, :].reshape(ct * nb, LANES) for i in range(S5_T)], axis=1)
    xb = x.astype(bf16)
    v_sc[...] = jnp.dot(xb, min_ref[0], preferred_element_type=f32)
    are = at_ref[0, :, :half]
    aim = at_ref[0, :, half:]

    def body(s, carry):
        k = (ct - 1 - s) if rev else s
        off = pl.multiple_of(k * nb, nb)
        hre, him = carry
        hall_sc[pl.ds(off, nb), :half] = hre
        hall_sc[pl.ds(off, nb), half:] = him
        vre = v_sc[pl.ds(off, nb), :half]
        vim = v_sc[pl.ds(off, nb), half:]
        return are * hre - aim * him + vre, are * him + aim * hre + vim

    hre, him = lax.fori_loop(0, ct, body, (h_sc[:, :half], h_sc[:, half:]))
    h_sc[:, :half] = hre
    h_sc[:, half:] = him

    @pl.when(t == nt - 1)
    def _():
        hfin_ref[0] = h_sc[...]

    if emit_y:
        y = jnp.dot(hall_sc[...].astype(bf16), mout_ref[0], preferred_element_type=f32)
        if intra:
            y = y + jnp.dot(xb, mintra_ref[0], preferred_element_type=f32) + x * d_ref[0]
        for i in range(S5_T):
            y_ref[0, :, i, :, :] = y[:, i * LANES:(i + 1) * LANES].reshape(ct, nb, LANES)


def _s5_pass(u5, m_in, a_t, h0, m_out=None, m_intra=None, d_skip=None, *, rev, emit_y):
    npk, nc, _, nb, _ = u5.shape
    ct = min(32, nc)
    nt = nc // ct
    intra = m_intra is not None
    kdim = S5_T * LANES
    sdim = a_t.shape[-1]
    tmap = (lambda j, t: (j, nt - 1 - t, 0, 0, 0)) if rev else (lambda j, t: (j, t, 0, 0, 0))
    pmap = lambda j, t: (j, 0, 0)
    args = [u5, m_in, a_t, h0]
    in_specs = [pl.BlockSpec((1, ct, S5_T, nb, LANES), tmap),
                pl.BlockSpec((1, kdim, sdim), pmap),
                pl.BlockSpec((1, nb, sdim), pmap),
                pl.BlockSpec((1, nb, sdim), pmap)]
    if emit_y:
        args.append(m_out)
        in_specs.append(pl.BlockSpec((1, sdim, kdim), pmap))
    if intra:
        args += [m_intra, d_skip]
        in_specs += [pl.BlockSpec((1, kdim, kdim), pmap), pl.BlockSpec((1, 1, kdim), pmap)]
    out_specs, out_shape = [], []
    if emit_y:
        out_specs.append(pl.BlockSpec((1, ct, S5_T, nb, LANES), tmap))
        out_shape.append(jax.ShapeDtypeStruct(u5.shape, f32))
    out_specs.append(pl.BlockSpec((1, nb, sdim), pmap))
    out_shape.append(jax.ShapeDtypeStruct((npk, nb, sdim), f32))
    kern = functools.partial(_s5_kernel, ct=ct, rev=rev, emit_y=emit_y, intra=intra)
    return pl.pallas_call(
        kern,
        grid=(npk, nt),
        in_specs=in_specs,
        out_specs=out_specs,
        out_shape=out_shape,
        scratch_shapes=[pltpu.VMEM((nb, sdim), f32),
                        pltpu.VMEM((ct * nb, sdim), f32),
                        pltpu.VMEM((ct * nb, sdim), f32)],
        compiler_params=_cparams(("parallel", "arbitrary")),
        name="s5_rev" if rev else "s5_fwd",
    )(*args)


def _s5_weights(lam_re, lam_im, log_dt, b_re, b_im, c_re, c_im):
    T = S5_T
    ndir, G, P = lam_re.shape
    npk = G // PACK_G
    dt = jnp.exp(log_dt.astype(f32))[..., None]
    lr, li = lam_re.astype(f32), lam_im.astype(f32)
    zr, zi = lr * dt, li * dt

    def apow(n):
        m = jnp.exp(n * zr)
        return m * jnp.cos(n * zi), m * jnp.sin(n * zi)

    nr = jnp.expm1(zr) * jnp.cos(zi) - 2.0 * jnp.square(jnp.sin(0.5 * zi))
    ni = jnp.exp(zr) * jnp.sin(zi)
    den = lr * lr + li * li
    qr = (nr * lr + ni * li) / den
    qi = (ni * lr - nr * li) / den
    bre, bim = b_re.astype(f32), b_im.astype(f32)
    bbr = qr[..., None] * bre - qi[..., None] * bim
    bbi = qr[..., None] * bim + qi[..., None] * bre
    cr, ci = c_re.astype(f32), c_im.astype(f32)

    pw = [apow(n) for n in range(T + 1)]

    def a_times_b(n, d):
        pr, pi = pw[n][0][d][..., None], pw[n][1][d][..., None]
        return pr * bbr[d] - pi * bbi[d], pr * bbi[d] + pi * bbr[d]

    def c_times_a(n, d):
        pr, pi = pw[n][0][d][:, None, :], pw[n][1][d][:, None, :]
        return cr[d] * pr - ci[d] * pi, cr[d] * pi + ci[d] * pr

    eye = jnp.eye(PACK_G, dtype=f32)

    def pack_in(mats):
        re = jnp.stack([m[0] for m in mats])
        im = jnp.stack([m[1] for m in mats])
        base = jnp.stack([re, im])
        base = base.reshape(2, T, npk, PACK_G, P, S5_GROUP)
        base = base.transpose(2, 1, 3, 5, 0, 4)
        full = base[:, :, :, :, :, None, :] * eye[None, None, :, None, None, :, None]
        return full.reshape(npk, T * LANES, 2 * PACK_G * P)

    def pack_out(mats):
        re = jnp.stack([m[0] for m in mats])
        im = jnp.stack([-m[1] for m in mats])
        base = jnp.stack([re, im])
        base = base.reshape(2, T, npk, PACK_G, S5_GROUP, P)
        base = base.transpose(2, 0, 3, 5, 1, 4)
        full = base[:, :, :, :, :, None, :] * eye[None, None, :, None, None, :, None]
        return full.reshape(npk, 2 * PACK_G * P, T * LANES)

    m_in_f = pack_in([a_times_b(T - 1 - s, 0) for s in range(T)])
    m_in_r = pack_in([a_times_b(s, 1) for s in range(T)])
    m_out_f = pack_out([c_times_a(i + 1, 0) for i in range(T)])
    m_out_r = pack_out([c_times_a(T - i, 1) for i in range(T)])

    def lag_kernel(d):
        ks = []
        for tau in range(T):
            car, cai = c_times_a(tau, d)
            ks.append(jnp.einsum('gop,gpi->gio', car, bbr[d]) - jnp.einsum('gop,gpi->gio', cai, bbi[d]))
        return jnp.stack(ks)

    kf, kr = lag_kernel(0), lag_kernel(1)
    lag = np.arange(T)[None, :] - np.arange(T)[:, None]
    toe = (kf[np.maximum(lag, 0)] * jnp.asarray(lag >= 0, f32)[:, :, None, None, None]
           + kr[np.maximum(-lag, 0)] * jnp.asarray(lag <= 0, f32)[:, :, None, None, None])
    toe = toe.reshape(T, T, npk, PACK_G, S5_GROUP, S5_GROUP).transpose(2, 0, 3, 4, 1, 5)
    m_intra = (toe[:, :, :, :, :, None, :] * eye[None, None, :, None, None, :, None]).reshape(
        npk, T * LANES, T * LANES)

    def a_chunk(d):
        pr, pi = pw[T][0][d].reshape(npk, PACK_G * P), pw[T][1][d].reshape(npk, PACK_G * P)
        return jnp.concatenate([pr, pi], axis=-1)

    return dict(m_in_f=m_in_f.astype(bf16), m_in_r=m_in_r.astype(bf16),
                m_out_f=m_out_f.astype(bf16), m_out_r=m_out_r.astype(bf16),
                m_intra=m_intra.astype(bf16), a_f=a_chunk(0), a_r=a_chunk(1))


def _gdn_prep_kernel(*refs, tile, rows_conv, d_gdn):
    if rows_conv:
        main_ref, up_ref, dn_ref, w_ref, q_ref, k_ref, v_ref = refs
    else:
        main_ref, w_ref, q_ref, k_ref, v_ref = refs
    w = w_ref[...]
    if rows_conv:
        i = pl.program_id(1)
        n = pl.num_programs(1)
        up = up_ref[0] * (i > 0).astype(f32)
        dn = dn_ref[0] * (i < n - 1).astype(f32)
        ext = jnp.concatenate([up, main_ref[0], dn], axis=0)
        base = GRID_W
        drs = (-1, 0, 1)
        col = lax.broadcasted_iota(jnp.int32, (tile, 1), 0) % GRID_W
        mask_m = col != 0
        mask_p = col != GRID_W - 1
    else:
        ext = main_ref[0]
        base = 0
        drs = (0,)
        col = lax.broadcasted_iota(jnp.int32, (tile, 1), 0)
        mask_m = col != 0
        mask_p = col != tile - 1
    n_ext = ext.shape[0]
    shifted = {-1: pltpu.roll(ext, 1, 0), 0: ext, 1: pltpu.roll(ext, n_ext - 1, 0)}
    acc = {}
    for dc in (-1, 0, 1):
        a = None
        for dr in drs:
            tap = w[(dr + 1) * 3 + (dc + 1)][None, :]
            term = shifted[dc][base + GRID_W * dr: base + GRID_W * dr + tile] * tap
            a = term if a is None else a + term
        acc[dc] = a
    conv = acc[0] + jnp.where(mask_m, acc[-1], 0.0) + jnp.where(mask_p, acc[1], 0.0)
    s = _silu(conv)
    nh = d_gdn // GDN_HEAD
    for h in range(nh):
        qh = s[:, h * GDN_HEAD:(h + 1) * GDN_HEAD]
        kh = s[:, d_gdn + h * GDN_HEAD: d_gdn + (h + 1) * GDN_HEAD]
        qn = qh * lax.rsqrt(jnp.sum(qh * qh, axis=-1, keepdims=True) + NORM_EPS) * (GDN_HEAD ** -0.5)
        kn = kh * lax.rsqrt(jnp.sum(kh * kh, axis=-1, keepdims=True) + NORM_EPS)
        q_ref[0, :, h * GDN_HEAD:(h + 1) * GDN_HEAD] = qn.astype(bf16)
        k_ref[0, :, h * GDN_HEAD:(h + 1) * GDN_HEAD] = kn.astype(bf16)
    v_ref[0] = s[:, 2 * d_gdn:].astype(bf16)


def _gdn_prep(qkv, conv_w, rows_conv):
    nb, L, c3 = qkv.shape
    d_gdn = c3 // 3
    w9 = conv_w.reshape(9, c3).astype(f32)
    if rows_conv:
        tile = min(512, L)
        nt = L // tile
        r = tile // GRID_W
        nrows = L // GRID_W
        in_specs = [pl.BlockSpec((1, tile, c3), lambda b, i: (b, i, 0)),
                    pl.BlockSpec((1, GRID_W, c3), lambda b, i: (b, jnp.maximum(i * r - 1, 0), 0)),
                    pl.BlockSpec((1, GRID_W, c3), lambda b, i: (b, jnp.minimum((i + 1) * r, nrows - 1), 0)),
                    pl.BlockSpec((9, c3), lambda b, i: (0, 0))]
        args = (qkv, qkv, qkv, w9)
    else:
        tile, nt = L, 1
        in_specs = [pl.BlockSpec((1, tile, c3), lambda b, i: (b, i, 0)),
                    pl.BlockSpec((9, c3), lambda b, i: (0, 0))]
        args = (qkv, w9)
    kern = functools.partial(_gdn_prep_kernel, tile=tile, rows_conv=rows_conv, d_gdn=d_gdn)
    o_spec = pl.BlockSpec((1, tile, d_gdn), lambda b, i: (b, i, 0))
    o_shape = jax.ShapeDtypeStruct((nb, L, d_gdn), bf16)
    return pl.pallas_call(
        kern,
        grid=(nb, nt),
        in_specs=in_specs,
        out_specs=[o_spec, o_spec, o_spec],
        out_shape=[o_shape, o_shape, o_shape],
        compiler_params=_cparams(("parallel", "parallel")),
        name="gdn_prep",
    )(*args)


def _unit_tri_inverse(a):
    n = a.shape[0]
    row = lax.broadcasted_iota(jnp.int32, (n, n), 0)
    col = lax.broadcasted_iota(jnp.int32, (n, n), 1)
    x = -a
    p = jnp.where(row == col, 1.0, 0.0) + x
    steps = int(math.log2(n)) - 1
    for _ in range(steps):
        x = jnp.dot(x, x, precision=HI, preferred_element_type=f32)
        p = p + jnp.dot(p, x, precision=HI, preferred_element_type=f32)
    return p


def _gdn_scan_kernel(q_ref, k_ref, v_ref, g_ref, s0_ref, o_ref, sfin_ref, s_sc, *, nchunks, rev, d, nh):
    C = GDN_CHUNK
    s_sc[...] = s0_ref[0]
    row = lax.broadcasted_iota(jnp.int32, (C, C), 0)
    col = lax.broadcasted_iota(jnp.int32, (C, C), 1)
    if rev:
        tri, strict = row <= col, row < col
    else:
        tri, strict = row >= col, row > col
    tri_f = tri.astype(f32)
    tri_t = jnp.where(tri, 0.0, 1.0) + jnp.where(row == col, 1.0, 0.0)

    def body(i, carry):
        c = (nchunks - 1 - i) if rev else i
        off = pl.multiple_of(c * C, C)
        gts = g_ref[0, pl.ds(off, C), :]
        g_col_all = jnp.dot(tri_f, gts, precision=HI, preferred_element_type=f32)
        g_row_all = lax.dot_general(gts, tri_t, (((0,), (0,)), ((), ())), precision=HI,
                                    preferred_element_type=f32)
        g_tot_all = jnp.sum(gts, axis=0, keepdims=True)
        for h in range(nh):
            cb = d * nh + h
            cg = N_DIR * nh + d * nh + h
            hs = slice(h * GDN_HEAD, (h + 1) * GDN_HEAD)
            b_col = gts[:, cb:cb + 1]
            g_col = g_col_all[:, cg:cg + 1]
            g_row = g_row_all[cg:cg + 1, :]
            g_tot = g_tot_all[:, cg:cg + 1]
            decay = jnp.exp(jnp.where(tri, g_col - g_row, -1e30))
            kb = k_ref[0, pl.ds(off, C), hs]
            qb = q_ref[0, pl.ds(off, C), hs]
            vb = v_ref[0, pl.ds(off, C), hs]
            kf = kb.astype(f32)
            kk = lax.dot_general(kb, kb, (((1,), (1,)), ((), ())), preferred_element_type=f32)
            qk = lax.dot_general(qb, kb, (((1,), (1,)), ((), ())), preferred_element_type=f32)
            a_mat = jnp.where(strict, b_col * kk * decay, 0.0)
            t_inv = _unit_tri_inverse(a_mat)
            gam = jnp.exp(g_col)
            rhs = jnp.concatenate([b_col * vb.astype(f32), (b_col * gam) * kf], axis=1).astype(bf16)
            sol = jnp.dot(t_inv.astype(bf16), rhs, preferred_element_type=f32)
            u0, w = sol[:, :GDN_HEAD], sol[:, GDN_HEAD:]
            s_old = s_sc[h]
            wq = jnp.concatenate([w.astype(bf16), qb], axis=0)
            r = jnp.dot(wq, s_old.astype(bf16), preferred_element_type=f32)
            u = u0 - r[:C]
            ub = u.astype(bf16)
            o = gam * r[C:] + jnp.dot((qk * decay).astype(bf16), ub, preferred_element_type=f32)
            k_out = (kf * jnp.exp(g_tot - g_col)).astype(bf16)
            s_sc[h] = jnp.exp(g_tot) * s_old + lax.dot_general(
                k_out, ub, (((0,), (0,)), ((), ())), preferred_element_type=f32)
            o_ref[0, pl.ds(off, C), hs] = o
        return carry

    lax.fori_loop(0, nchunks, body, 0)
    sfin_ref[0] = s_sc[...]


def _gdn_scan(q, k, v, gates, s0, *, rev):
    nb, L, d_gdn = q.shape
    nh = d_gdn // GDN_HEAD
    d = 1 if rev else 0
    kern = functools.partial(_gdn_scan_kernel, nchunks=L // GDN_CHUNK, rev=rev, d=d, nh=nh)
    seq = lambda b: (b, 0, 0)
    st = lambda b: (b, 0, 0, 0)
    return pl.pallas_call(
        kern,
        grid=(nb,),
        in_specs=[pl.BlockSpec((1, L, d_gdn), seq)] * 3 + [
            pl.BlockSpec((1, L, LANES), seq),
            pl.BlockSpec((1, nh, GDN_HEAD, GDN_HEAD), st)],
        out_specs=[pl.BlockSpec((1, L, d_gdn), seq),
                   pl.BlockSpec((1, nh, GDN_HEAD, GDN_HEAD), st)],
        out_shape=[jax.ShapeDtypeStruct((nb, L, d_gdn), f32),
                   jax.ShapeDtypeStruct((nb, nh, GDN_HEAD, GDN_HEAD), f32)],
        scratch_shapes=[pltpu.VMEM((nh, GDN_HEAD, GDN_HEAD), f32)],
        compiler_params=_cparams(("parallel",)),
        name="gdn_scan_rev" if rev else "gdn_scan_fwd",
    )(q, k, v, gates, s0)


def _post_kernel(x_ref, gate_ref, yf_ref, yr_ref, zs_ref, of_ref, or_ref, zg_ref,
                 wglu_ref, bglu_ref, nw_ref, wout_ref, lng_ref, lnb_ref, o_ref, *, tt, alpha):
    nb, _, d = x_ref.shape
    rows = nb * tt
    npk = yf_ref.shape[0]
    d_s5 = npk * LANES
    d_gdn = of_ref.shape[2]
    parts = []
    for j in range(npk):
        per_b = [(yf_ref[j, :, :, b, :] + yr_ref[j, :, :, b, :]).reshape(tt, LANES) for b in range(nb)]
        parts.append(jnp.concatenate(per_b, axis=0))
    y = jnp.concatenate(parts, axis=1)
    g = 0.5 * y * (1.0 + lax.erf(y * (2.0 ** -0.5)))
    glu = jnp.dot(g.astype(bf16), wglu_ref[...], preferred_element_type=f32) + bglu_ref[...]
    s5 = g * jax.nn.sigmoid(glu) * _silu(zs_ref[...].reshape(rows, d_s5))
    o = (of_ref[...] + or_ref[...]).reshape(rows, d_gdn)
    zg = zg_ref[...].reshape(rows, d_gdn)
    gd = []
    for h in range(d_gdn // GDN_HEAD):
        oh = o[:, h * GDN_HEAD:(h + 1) * GDN_HEAD]
        gd.append(oh * lax.rsqrt(jnp.mean(oh * oh, axis=-1, keepdims=True) + NORM_EPS) * nw_ref[...])
    gdn = jnp.concatenate(gd, axis=1) * _silu(zg)
    mix = jnp.concatenate([s5, gdn], axis=1).astype(bf16)
    yv = jnp.dot(mix, wout_ref[...], preferred_element_type=f32).reshape(nb, tt, d)
    r = alpha * x_ref[...] + gate_ref[...] * yv
    mu = jnp.mean(r, axis=-1, keepdims=True)
    var = jnp.mean(jnp.square(r - mu), axis=-1, keepdims=True)
    o_ref[...] = (r - mu) * lax.rsqrt(var + LN_EPS) * lng_ref[...] + lnb_ref[...]


def _post(x, gate, yf, yr, zs, o_f, o_r, zg, wglu, bglu, nw, wout, lng, lnb, alpha):
    nb, L, d = x.shape
    tt = 16
    npk = yf.shape[0]
    d_s5 = npk * LANES
    d_gdn = o_f.shape[2]
    kern = functools.partial(_post_kernel, tt=tt, alpha=alpha)
    tok = lambda i: (0, i, 0)
    c2 = lambda i: (0, 0)
    c3 = lambda i: (0, 0, 0)
    y_spec = pl.BlockSpec((npk, tt // S5_T, S5_T, nb, LANES), lambda i: (0, i, 0, 0, 0))
    return pl.pallas_call(
        kern,
        grid=(L // tt,),
        in_specs=[pl.BlockSpec((nb, tt, d), tok),
                  pl.BlockSpec((nb, 1, d), c3),
                  y_spec, y_spec,
                  pl.BlockSpec((nb, tt, d_s5), tok),
                  pl.BlockSpec((nb, tt, d_gdn), tok),
                  pl.BlockSpec((nb, tt, d_gdn), tok),
                  pl.BlockSpec((nb, tt, d_gdn), tok),
                  pl.BlockSpec(wglu.shape, c2),
                  pl.BlockSpec(bglu.shape, c2),
                  pl.BlockSpec(nw.shape, c2),
                  pl.BlockSpec(wout.shape, c2),
                  pl.BlockSpec(lng.shape, c3),
                  pl.BlockSpec(lnb.shape, c3)],
        out_specs=pl.BlockSpec((nb, tt, d), tok),
        out_shape=jax.ShapeDtypeStruct((nb, L, d), f32),
        compiler_params=_cparams(("parallel",)),
        name="post",
    )(x, gate, yf, yr, zs, o_f, o_r, zg, wglu, bglu, nw, wout, lng, lnb)


def _layer(x, c, ctx, c_ctx, p, alpha):
    nb, L, d = x.shape
    d_s5 = p['s5_d'].shape[0]
    d_gdn = p['conv_w'].shape[-1] // 3
    nh = d_gdn // GDN_HEAD
    npk = d_s5 // LANES

    pad = (-(nb + 1)) % 8
    cond = jnp.concatenate([c, c_ctx[None, :], jnp.zeros((pad, d), f32)], axis=0)
    m = _ada(cond, p['w_ada'], p['b_ada'])
    shift, scale, gate = m[:nb, :d], m[:nb, d:2 * d], m[:nb, 2 * d:]
    shift_c = jnp.broadcast_to(m[nb:nb + 1, :d], (nb, d))
    scale_c = jnp.broadcast_to(m[nb:nb + 1, d:2 * d], (nb, d))

    w_in = p['w_in']
    wu = w_in[:, :d_s5].astype(bf16)
    wm = w_in[:, d_s5:2 * d_s5 + 4 * d_gdn].astype(bf16)
    ng = 2 * N_DIR * nh
    wg = jnp.pad(w_in[:, 2 * d_s5 + 4 * d_gdn:], ((0, 0), (0, LANES - ng))).astype(bf16)
    gp = jnp.zeros((2, LANES), f32)
    gp = gp.at[0, ng // 2:ng].set(p['gdn_a_log'].reshape(-1).astype(f32))
    gp = gp.at[1, ng // 2:ng].set(p['gdn_dt_bias'].reshape(-1).astype(f32))

    u5, zs, qkv, zg, gates = _inproj(x, scale[:, None, :], shift[:, None, :], wu, wm, wg, gp, d_s5, d_gdn)
    u5c, _, qkvc, _, gatesc = _inproj(ctx, scale_c[:, None, :], shift_c[:, None, :], wu, wm, wg, gp, d_s5, d_gdn)

    sw = _s5_weights(p['s5_lambda_re'], p['s5_lambda_im'], p['s5_log_dt'],
                     p['s5_b_re'], p['s5_b_im'], p['s5_c_re'], p['s5_c_im'])
    sdim = sw['a_f'].shape[-1]
    a_f = jnp.broadcast_to(sw['a_f'][:, None, :], (npk, nb, sdim))
    a_r = jnp.broadcast_to(sw['a_r'][:, None, :], (npk, nb, sdim))
    h_zero = jnp.zeros((npk, nb, sdim), f32)
    d_skip = jnp.tile(p['s5_d'].astype(f32).reshape(npk, 1, LANES), (1, 1, S5_T))
    (hc_f,) = _s5_pass(u5c, sw['m_in_f'], a_f, h_zero, rev=False, emit_y=False)
    (hc_r,) = _s5_pass(u5c, sw['m_in_r'], a_r, h_zero, rev=True, emit_y=False)
    y_f, _ = _s5_pass(u5, sw['m_in_f'], a_f, hc_f, sw['m_out_f'], sw['m_intra'], d_skip, rev=False, emit_y=True)
    y_r, _ = _s5_pass(u5, sw['m_in_r'], a_r, hc_r, sw['m_out_r'], rev=True, emit_y=True)

    ql, kl, vl = _gdn_prep(qkv, p['conv_w'], True)
    qc, kc, vc = _gdn_prep(qkvc, p['conv_w'], False)
    s_zero = jnp.zeros((nb, nh, GDN_HEAD, GDN_HEAD), f32)
    _, sc_f = _gdn_scan(qc, kc, vc, gatesc, s_zero, rev=False)
    _, sc_r = _gdn_scan(qc, kc, vc, gatesc, s_zero, rev=True)
    o_f, _ = _gdn_scan(ql, kl, vl, gates, sc_f, rev=False)
    o_r, _ = _gdn_scan(ql, kl, vl, gates, sc_r, rev=True)

    return _post(x, gate[:, None, :], y_f, y_r, zs, o_f, o_r, zg,
                 p['w_glu'].astype(bf16), p['b_glu'].astype(f32).reshape(1, d_s5),
                 p['gdn_norm_w'].astype(f32).reshape(1, GDN_HEAD),
                 p['w_out'].astype(bf16), p['ln_g'].astype(f32).reshape(1, 1, d),
                 p['ln_b'].astype(f32).reshape(1, 1, d), alpha)


def kernel(x, c, ctx, c_ctx, w_ada, b_ada, w_in, s5_lambda_re, s5_lambda_im, s5_log_dt, s5_b_re, s5_b_im, s5_c_re, s5_c_im, s5_d, w_glu, b_glu, conv_w, gdn_a_log, gdn_dt_bias, gdn_norm_w, w_out, ln_g, ln_b):
    depth = w_ada.shape[0]
    assert depth == 1, "context-token outputs are only skipped for the last layer"
    alpha = (2.0 * depth) ** 0.25
    p = {
        'w_ada': w_ada[0], 'b_ada': b_ada[0], 'w_in': w_in[0],
        's5_lambda_re': s5_lambda_re[0], 's5_lambda_im': s5_lambda_im[0], 's5_log_dt': s5_log_dt[0],
        's5_b_re': s5_b_re[0], 's5_b_im': s5_b_im[0], 's5_c_re': s5_c_re[0], 's5_c_im': s5_c_im[0],
        's5_d': s5_d[0], 'w_glu': w_glu[0], 'b_glu': b_glu[0], 'conv_w': conv_w[0],
        'gdn_a_log': gdn_a_log[0], 'gdn_dt_bias': gdn_dt_bias[0], 'gdn_norm_w': gdn_norm_w[0],
        'w_out': w_out[0], 'ln_g': ln_g[0], 'ln_b': ln_b[0],
    }
    return _layer(x, c, ctx, c_ctx, p, alpha)
```

```python
import functools
import math

import numpy as np
import jax
import jax.numpy as jnp
from jax import lax
from jax.experimental import pallas as pl
from jax.experimental.pallas import tpu as pltpu

f32 = jnp.float32
bf16 = jnp.bfloat16
HI = lax.Precision.HIGHEST

LANES = 128
S5_GROUP = 16
S5_STATE = 64
S5_T = 8
PACK_G = LANES // S5_GROUP
GDN_HEAD = 128
GDN_CHUNK = 64
GRID_W = 64
N_DIR = 2
LN_EPS = 1e-5
NORM_EPS = 1e-6
VMEM_LIMIT = 56 * 1024 * 1024


def _cparams(sem):
    return pltpu.CompilerParams(dimension_semantics=sem, vmem_limit_bytes=VMEM_LIMIT)


def _silu(x):
    return x * jax.nn.sigmoid(x)


def _ada_kernel(c_ref, w_ref, b_ref, o_ref):
    o_ref[...] = jnp.dot(_silu(c_ref[...]), w_ref[...], precision=HI,
                         preferred_element_type=f32) + b_ref[...]


def _ada(cond, w_ada, b_ada):
    rows, d = cond.shape
    n = w_ada.shape[1]
    tn = 512
    return pl.pallas_call(
        _ada_kernel,
        grid=(n // tn,),
        in_specs=[pl.BlockSpec((rows, d), lambda i: (0, 0)),
                  pl.BlockSpec((d, tn), lambda i: (0, i)),
                  pl.BlockSpec((1, tn), lambda i: (0, i))],
        out_specs=pl.BlockSpec((rows, tn), lambda i: (0, i)),
        out_shape=jax.ShapeDtypeStruct((rows, n), f32),
        compiler_params=_cparams(("parallel",)),
        name="ada",
    )(cond, w_ada, b_ada.reshape(1, n))


def _inproj_kernel(x_ref, scale_ref, shift_ref, wu_ref, wm_ref, wg_ref, gp_ref,
                   u_ref, zs_ref, qkv_ref, zg_ref, gate_ref, *, tt, d_s5, d_gdn, n_beta):
    nb, _, d = x_ref.shape
    sc = 1.0 + scale_ref[...]
    sh = shift_ref[...]
    h = (x_ref[...] * sc + sh).reshape(nb * tt, d).astype(bf16)
    r = jnp.dot(h, wm_ref[...], preferred_element_type=f32)
    zs_ref[...] = r[:, :d_s5].reshape(nb, tt, d_s5)
    qkv_ref[...] = r[:, d_s5:d_s5 + 3 * d_gdn].reshape(nb, tt, 3 * d_gdn)
    zg_ref[...] = r[:, d_s5 + 3 * d_gdn:].reshape(nb, tt, d_gdn)
    lg = jnp.dot(h, wg_ref[...], preferred_element_type=f32)
    lane = lax.broadcasted_iota(jnp.int32, lg.shape, 1)
    a = lg + gp_ref[1:2, :]
    softplus = jnp.maximum(a, 0.0) + jnp.log1p(jnp.exp(-jnp.abs(a)))
    gates = jnp.where(lane < n_beta, jax.nn.sigmoid(lg), -jnp.exp(gp_ref[0:1, :]) * softplus)
    gate_ref[...] = gates.reshape(nb, tt, LANES)
    sc2 = sc[:, 0, :]
    sh2 = sh[:, 0, :]
    ht = jnp.concatenate([x_ref[:, t, :] * sc2 + sh2 for t in range(tt)], axis=0).astype(bf16)
    ru = jnp.dot(ht, wu_ref[...], preferred_element_type=f32)
    for j in range(d_s5 // LANES):
        u_ref[j] = ru[:, j * LANES:(j + 1) * LANES].reshape(tt // S5_T, S5_T, nb, LANES)


def _inproj(x, scale, shift, wu, wm, wg, gp, d_s5, d_gdn):
    nb, L, d = x.shape
    tt = 32
    npk = d_s5 // LANES
    kern = functools.partial(_inproj_kernel, tt=tt, d_s5=d_s5, d_gdn=d_gdn,
                             n_beta=N_DIR * (d_gdn // GDN_HEAD))
    const2 = lambda i: (0, 0)
    const3 = lambda i: (0, 0, 0)
    return pl.pallas_call(
        kern,
        grid=(L // tt,),
        in_specs=[pl.BlockSpec((nb, tt, d), lambda i: (0, i, 0)),
                  pl.BlockSpec((nb, 1, d), const3),
                  pl.BlockSpec((nb, 1, d), const3),
                  pl.BlockSpec(wu.shape, const2),
                  pl.BlockSpec(wm.shape, const2),
                  pl.BlockSpec(wg.shape, const2),
                  pl.BlockSpec(gp.shape, const2)],
        out_specs=[pl.BlockSpec((npk, tt // S5_T, S5_T, nb, LANES), lambda i: (0, i, 0, 0, 0)),
                   pl.BlockSpec((nb, tt, d_s5), lambda i: (0, i, 0)),
                   pl.BlockSpec((nb, tt, 3 * d_gdn), lambda i: (0, i, 0)),
                   pl.BlockSpec((nb, tt, d_gdn), lambda i: (0, i, 0)),
                   pl.BlockSpec((nb, tt, LANES), lambda i: (0, i, 0))],
        out_shape=[jax.ShapeDtypeStruct((npk, L // S5_T, S5_T, nb, LANES), f32),
                   jax.ShapeDtypeStruct((nb, L, d_s5), f32),
                   jax.ShapeDtypeStruct((nb, L, 3 * d_gdn), f32),
                   jax.ShapeDtypeStruct((nb, L, d_gdn), f32),
                   jax.ShapeDtypeStruct((nb, L, LANES), f32)],
        compiler_params=_cparams(("parallel",)),
        name="inproj",
    )(x, scale, shift, wu, wm, wg, gp)


def _s5_kernel(*refs, ct, rev, emit_y, intra):
    it = iter(refs)
    u_ref, min_ref, at_ref, h0_ref = next(it), next(it), next(it), next(it)
    mout_ref = next(it) if emit_y else None
    mintra_ref = next(it) if intra else None
    d_ref = next(it) if intra else None
    y_ref = next(it) if emit_y else None
    hfin_ref = next(it)
    h_sc, v_sc, hall_sc = next(it), next(it), next(it)

    t = pl.program_id(1)
    nt = pl.num_programs(1)
    nb = u_ref.shape[3]
    half = h_sc.shape[1] // 2

    @pl.when(t == 0)
    def _():
        h_sc[...] = h0_ref[0]

    x = jnp.concatenate([u_ref[0, :, i, :, :].reshape(ct * nb, LANES) for i in range(S5_T)], axis=1)
    xb = x.astype(bf16)
    v_sc[...] = jnp.dot(xb, min_ref[0], preferred_element_type=f32)
    are = at_ref[0, :, :half]
    aim = at_ref[0, :, half:]

    def body(s, carry):
        k = (ct - 1 - s) if rev else s
        off = pl.multiple_of(k * nb, nb)
        hre, him = carry
        hall_sc[pl.ds(off, nb), :half] = hre
        hall_sc[pl.ds(off, nb), half:] = him
        vre = v_sc[pl.ds(off, nb), :half]
        vim = v_sc[pl.ds(off, nb), half:]
        return are * hre - aim * him + vre, are * him + aim * hre + vim

    hre, him = lax.fori_loop(0, ct, body, (h_sc[:, :half], h_sc[:, half:]))
    h_sc[:, :half] = hre
    h_sc[:, half:] = him

    @pl.when(t == nt - 1)
    def _():
        hfin_ref[0] = h_sc[...]

    if emit_y:
        y = jnp.dot(hall_sc[...].astype(bf16), mout_ref[0], preferred_element_type=f32)
        if intra:
            y = y + jnp.dot(xb, mintra_ref[0], preferred_element_type=f32) + x * d_ref[0]
        for i in range(S5_T):
            y_ref[0, :, i, :, :] = y[:, i * LANES:(i + 1) * LANES].reshape(ct, nb, LANES)


def _s5_pass(u5, m_in, a_t, h0, m_out=None, m_intra=None, d_skip=None, *, rev, emit_y):
    npk, nc, _, nb, _ = u5.shape
    ct = min(32, nc)
    nt = nc // ct
    intra = m_intra is not None
    kdim = S5_T * LANES
    sdim = a_t.shape[-1]
    tmap = (lambda j, t: (j, nt - 1 - t, 0, 0, 0)) if rev else (lambda j, t: (j, t, 0, 0, 0))
    pmap = lambda j, t: (j, 0, 0)
    args = [u5, m_in, a_t, h0]
    in_specs = [pl.BlockSpec((1, ct, S5_T, nb, LANES), tmap),
                pl.BlockSpec((1, kdim, sdim), pmap),
                pl.BlockSpec((1, nb, sdim), pmap),
                pl.BlockSpec((1, nb, sdim), pmap)]
    if emit_y:
        args.append(m_out)
        in_specs.append(pl.BlockSpec((1, sdim, kdim), pmap))
    if intra:
        args += [m_intra, d_skip]
        in_specs += [pl.BlockSpec((1, kdim, kdim), pmap), pl.BlockSpec((1, 1, kdim), pmap)]
    out_specs, out_shape = [], []
    if emit_y:
        out_specs.append(pl.BlockSpec((1, ct, S5_T, nb, LANES), tmap))
        out_shape.append(jax.ShapeDtypeStruct(u5.shape, f32))
    out_specs.append(pl.BlockSpec((1, nb, sdim), pmap))
    out_shape.append(jax.ShapeDtypeStruct((npk, nb, sdim), f32))
    kern = functools.partial(_s5_kernel, ct=ct, rev=rev, emit_y=emit_y, intra=intra)
    return pl.pallas_call(
        kern,
        grid=(npk, nt),
        in_specs=in_specs,
        out_specs=out_specs,
        out_shape=out_shape,
        scratch_shapes=[pltpu.VMEM((nb, sdim), f32),
                        pltpu.VMEM((ct * nb, sdim), f32),
                        pltpu.VMEM((ct * nb, sdim), f32)],
        compiler_params=_cparams(("parallel", "arbitrary")),
        name="s5_rev" if rev else "s5_fwd",
    )(*args)


def _s5_weights(lam_re, lam_im, log_dt, b_re, b_im, c_re, c_im):
    T = S5_T
    ndir, G, P = lam_re.shape
    npk = G // PACK_G
    dt = jnp.exp(log_dt.astype(f32))[..., None]
    lr, li = lam_re.astype(f32), lam_im.astype(f32)
    zr, zi = lr * dt, li * dt

    def apow(n):
        m = jnp.exp(n * zr)
        return m * jnp.cos(n * zi), m * jnp.sin(n * zi)

    nr = jnp.expm1(zr) * jnp.cos(zi) - 2.0 * jnp.square(jnp.sin(0.5 * zi))
    ni = jnp.exp(zr) * jnp.sin(zi)
    den = lr * lr + li * li
    qr = (nr * lr + ni * li) / den
    qi = (ni * lr - nr * li) / den
    bre, bim = b_re.astype(f32), b_im.astype(f32)
    bbr = qr[..., None] * bre - qi[..., None] * bim
    bbi = qr[..., None] * bim + qi[..., None] * bre
    cr, ci = c_re.astype(f32), c_im.astype(f32)

    pw = [apow(n) for n in range(T + 1)]

    def a_times_b(n, d):
        pr, pi = pw[n][0][d][..., None], pw[n][1][d][..., None]
        return pr * bbr[d] - pi * bbi[d], pr * bbi[d] + pi * bbr[d]

    def c_times_a(n, d):
        pr, pi = pw[n][0][d][:, None, :], pw[n][1][d][:, None, :]
        return cr[d] * pr - ci[d] * pi, cr[d] * pi + ci[d] * pr

    eye = jnp.eye(PACK_G, dtype=f32)

    def pack_in(mats):
        re = jnp.stack([m[0] for m in mats])
        im = jnp.stack([m[1] for m in mats])
        base = jnp.stack([re, im])
        base = base.reshape(2, T, npk, PACK_G, P, S5_GROUP)
        base = base.transpose(2, 1, 3, 5, 0, 4)
        full = base[:, :, :, :, :, None, :] * eye[None, None, :, None, None, :, None]
        return full.reshape(npk, T * LANES, 2 * PACK_G * P)

    def pack_out(mats):
        re = jnp.stack([m[0] for m in mats])
        im = jnp.stack([-m[1] for m in mats])
        base = jnp.stack([re, im])
        base = base.reshape(2, T, npk, PACK_G, S5_GROUP, P)
        base = base.transpose(2, 0, 3, 5, 1, 4)
        full = base[:, :, :, :, :, None, :] * eye[None, None, :, None, None, :, None]
        return full.reshape(npk, 2 * PACK_G * P, T * LANES)

    m_in_f = pack_in([a_times_b(T - 1 - s, 0) for s in range(T)])
    m_in_r = pack_in([a_times_b(s, 1) for s in range(T)])
    m_out_f = pack_out([c_times_a(i + 1, 0) for i in range(T)])
    m_out_r = pack_out([c_times_a(T - i, 1) for i in range(T)])

    def lag_kernel(d):
        ks = []
        for tau in range(T):
            car, cai = c_times_a(tau, d)
            ks.append(jnp.einsum('gop,gpi->gio', car, bbr[d]) - jnp.einsum('gop,gpi->gio', cai, bbi[d]))
        return jnp.stack(ks)

    kf, kr = lag_kernel(0), lag_kernel(1)
    lag = np.arange(T)[None, :] - np.arange(T)[:, None]
    toe = (kf[np.maximum(lag, 0)] * jnp.asarray(lag >= 0, f32)[:, :, None, None, None]
           + kr[np.maximum(-lag, 0)] * jnp.asarray(lag <= 0, f32)[:, :, None, None, None])
    toe = toe.reshape(T, T, npk, PACK_G, S5_GROUP, S5_GROUP).transpose(2, 0, 3, 4, 1, 5)
    m_intra = (toe[:, :, :, :, :, None, :] * eye[None, None, :, None, None, :, None]).reshape(
        npk, T * LANES, T * LANES)

    def a_chunk(d):
        pr, pi = pw[T][0][d].reshape(npk, PACK_G * P), pw[T][1][d].reshape(npk, PACK_G * P)
        return jnp.concatenate([pr, pi], axis=-1)

    return dict(m_in_f=m_in_f.astype(bf16), m_in_r=m_in_r.astype(bf16),
                m_out_f=m_out_f.astype(bf16), m_out_r=m_out_r.astype(bf16),
                m_intra=m_intra.astype(bf16), a_f=a_chunk(0), a_r=a_chunk(1))


def _gdn_prep_kernel(*refs, tile, rows_conv, d_gdn):
    if rows_conv:
        main_ref, up_ref, dn_ref, w_ref, q_ref, k_ref, v_ref = refs
    else:
        main_ref, w_ref, q_ref, k_ref, v_ref = refs
    w = w_ref[...]
    if rows_conv:
        i = pl.program_id(1)
        n = pl.num_programs(1)
        up = up_ref[0] * (i > 0).astype(f32)
        dn = dn_ref[0] * (i < n - 1).astype(f32)
        ext = jnp.concatenate([up, main_ref[0], dn], axis=0)
        base = GRID_W
        drs = (-1, 0, 1)
        col = lax.broadcasted_iota(jnp.int32, (tile, 1), 0) % GRID_W
        mask_m = col != 0
        mask_p = col != GRID_W - 1
    else:
        ext = main_ref[0]
        base = 0
        drs = (0,)
        col = lax.broadcasted_iota(jnp.int32, (tile, 1), 0)
        mask_m = col != 0
        mask_p = col != tile - 1
    n_ext = ext.shape[0]
    shifted = {-1: pltpu.roll(ext, 1, 0), 0: ext, 1: pltpu.roll(ext, n_ext - 1, 0)}
    acc = {}
    for dc in (-1, 0, 1):
        a = None
        for dr in drs:
            tap = w[(dr + 1) * 3 + (dc + 1)][None, :]
            term = shifted[dc][base + GRID_W * dr: base + GRID_W * dr + tile] * tap
            a = term if a is None else a + term
        acc[dc] = a
    conv = acc[0] + jnp.where(mask_m, acc[-1], 0.0) + jnp.where(mask_p, acc[1], 0.0)
    s = _silu(conv)
    nh = d_gdn // GDN_HEAD
    for h in range(nh):
        qh = s[:, h * GDN_HEAD:(h + 1) * GDN_HEAD]
        kh = s[:, d_gdn + h * GDN_HEAD: d_gdn + (h + 1) * GDN_HEAD]
        qn = qh * lax.rsqrt(jnp.sum(qh * qh, axis=-1, keepdims=True) + NORM_EPS) * (GDN_HEAD ** -0.5)
        kn = kh * lax.rsqrt(jnp.sum(kh * kh, axis=-1, keepdims=True) + NORM_EPS)
        q_ref[0, :, h * GDN_HEAD:(h + 1) * GDN_HEAD] = qn.astype(bf16)
        k_ref[0, :, h * GDN_HEAD:(h + 1) * GDN_HEAD] = kn.astype(bf16)
    v_ref[0] = s[:, 2 * d_gdn:].astype(bf16)


def _gdn_prep(qkv, conv_w, rows_conv):
    nb, L, c3 = qkv.shape
    d_gdn = c3 // 3
    w9 = conv_w.reshape(9, c3).astype(f32)
    if rows_conv:
        tile = min(512, L)
        nt = L // tile
        r = tile // GRID_W
        nrows = L // GRID_W
        in_specs = [pl.BlockSpec((1, tile, c3), lambda b, i: (b, i, 0)),
                    pl.BlockSpec((1, GRID_W, c3), lambda b, i: (b, jnp.maximum(i * r - 1, 0), 0)),
                    pl.BlockSpec((1, GRID_W, c3), lambda b, i: (b, jnp.minimum((i + 1) * r, nrows - 1), 0)),
                    pl.BlockSpec((9, c3), lambda b, i: (0, 0))]
        args = (qkv, qkv, qkv, w9)
    else:
        tile, nt = L, 1
        in_specs = [pl.BlockSpec((1, tile, c3), lambda b, i: (b, i, 0)),
                    pl.BlockSpec((9, c3), lambda b, i: (0, 0))]
        args = (qkv, w9)
    kern = functools.partial(_gdn_prep_kernel, tile=tile, rows_conv=rows_conv, d_gdn=d_gdn)
    o_spec = pl.BlockSpec((1, tile, d_gdn), lambda b, i: (b, i, 0))
    o_shape = jax.ShapeDtypeStruct((nb, L, d_gdn), bf16)
    return pl.pallas_call(
        kern,
        grid=(nb, nt),
        in_specs=in_specs,
        out_specs=[o_spec, o_spec, o_spec],
        out_shape=[o_shape, o_shape, o_shape],
        compiler_params=_cparams(("parallel", "parallel")),
        name="gdn_prep",
    )(*args)


def _chunk_cumsum(g, rev):
    n = g.shape[0]
    r = lax.broadcasted_iota(jnp.int32, g.shape, 0)
    s = 1
    while s < n:
        if rev:
            g = g + jnp.where(r < n - s, pltpu.roll(g, n - s, 0), 0.0)
        else:
            g = g + jnp.where(r >= s, pltpu.roll(g, s, 0), 0.0)
        s *= 2
    return g


def _gdn_scan_kernel(*refs, nchunks, rev, d, nh, emit_o):
    if emit_o:
        (q_ref, k_ref, v_ref, g_ref, s0_ref, o_ref, sfin_ref,
         s_sc, a4_sc, apl_sc, tl_sc, qkd_sc, gcol_sc, grow_sc) = refs
    else:
        (q_ref, k_ref, v_ref, g_ref, s0_ref, sfin_ref,
         s_sc, a4_sc, apl_sc, tl_sc, qkd_sc, gcol_sc, grow_sc) = refs
        o_ref = None
    C = GDN_CHUNK
    bb = q_ref.shape[0]
    slots = 2 * nh
    npairs = bb * nchunks // 2
    ppb = nchunks // 2
    nprob = npairs * slots
    s_sc[...] = s0_ref[...]
    row = lax.broadcasted_iota(jnp.int32, (C, C), 0)
    col = lax.broadcasted_iota(jnp.int32, (C, C), 1)
    if rev:
        tri, strict = row <= col, row < col
    else:
        tri, strict = row >= col, row > col

    def unit_of(pair, u):
        bl = pair // ppb
        c = (pair % ppb) * 2 + u
        return bl, c, pl.multiple_of(c * C, C)

    def phase_a(pair, carry):
        for u in range(2):
            bl, c, off = unit_of(pair, u)
            unit = pair * 2 + u
            gts = g_ref[bl, pl.ds(off, C), :]
            g_col_all = _chunk_cumsum(gts, rev)
            g_row_all = g_col_all.T
            gcol_sc[unit] = g_col_all
            grow_sc[unit] = g_row_all
            for h in range(nh):
                cb = d * nh + h
                cg = N_DIR * nh + d * nh + h
                hs = slice(h * GDN_HEAD, (h + 1) * GDN_HEAD)
                decay = jnp.exp(jnp.where(tri, g_col_all[:, cg:cg + 1] - g_row_all[cg:cg + 1, :], -1e30))
                kb = k_ref[bl, pl.ds(off, C), hs]
                if emit_o:
                    kq = jnp.concatenate([kb, q_ref[bl, pl.ds(off, C), hs]], axis=0)
                else:
                    kq = kb
                kkqk = lax.dot_general(kq, kb, (((1,), (1,)), ((), ())), preferred_element_type=f32)
                a_mat = jnp.where(strict, gts[:, cb:cb + 1] * kkqk[:C] * decay, 0.0)
                a4_sc[:, pair, u * nh + h, :] = a_mat
                if emit_o:
                    qkd_sc[pair, u * nh + h] = (kkqk[C:] * decay).astype(bf16)
        return carry

    lax.fori_loop(0, npairs, phase_a, 0)

    def to_lanes(i, carry):
        apl_sc[i] = a4_sc[i].reshape(nprob, C).T
        return carry

    lax.fori_loop(0, C, to_lanes, 0)
    tl_sc[...] = jnp.zeros(tl_sc.shape, f32)
    jj = lax.broadcasted_iota(jnp.int32, (C, nprob), 0)

    def solve_row(s, carry):
        i = (C - 1 - s) if rev else s

        def mblk(mb, accs):
            a0, a1 = accs
            arow = apl_sc[i, pl.ds(pl.multiple_of(mb * 8, 8), 8), :]
            for mm in range(8):
                term = arow[mm:mm + 1, :] * tl_sc[mb * 8 + mm]
                if mm % 2 == 0:
                    a0 = a0 - term
                else:
                    a1 = a1 - term
            return a0, a1

        lo, hi = (i // 8, C // 8) if rev else (0, i // 8 + 1)
        a0, a1 = lax.fori_loop(lo, hi, mblk, (jnp.where(jj == i, 1.0, 0.0), jnp.zeros((C, nprob), f32)))
        tl_sc[i] = a0 + a1
        return carry

    lax.fori_loop(0, C, solve_row, 0)

    def from_lanes(i, carry):
        a4_sc[i] = tl_sc[i].T.reshape(npairs, slots, C)
        return carry

    lax.fori_loop(0, C, from_lanes, 0)

    def phase_c(step, carry):
        pair = (npairs - 1 - step) if rev else step
        for u in ((1, 0) if rev else (0, 1)):
            bl, c, off = unit_of(pair, u)
            unit = pair * 2 + u
            gts = g_ref[bl, pl.ds(off, C), :]
            g_col_all = gcol_sc[unit]
            g_tot_all = jnp.sum(gts, axis=0, keepdims=True)
            for h in range(nh):
                cb = d * nh + h
                cg = N_DIR * nh + d * nh + h
                hs = slice(h * GDN_HEAD, (h + 1) * GDN_HEAD)
                b_col = gts[:, cb:cb + 1]
                g_col = g_col_all[:, cg:cg + 1]
                g_tot = g_tot_all[:, cg:cg + 1]
                gam = jnp.exp(g_col)
                t_inv = a4_sc[:, pair, u * nh + h, :]
                kf = k_ref[bl, pl.ds(off, C), hs].astype(f32)
                vf = v_ref[bl, pl.ds(off, C), hs].astype(f32)
                rhs = jnp.concatenate([b_col * vf, (b_col * gam) * kf], axis=1).astype(bf16)
                sol = jnp.dot(t_inv.astype(bf16), rhs, preferred_element_type=f32)
                u0, w = sol[:, :GDN_HEAD], sol[:, GDN_HEAD:]
                s_old = s_sc[bl, h]
                if emit_o:
                    qg = (gam * q_ref[bl, pl.ds(off, C), hs].astype(f32)).astype(bf16)
                    lhs = jnp.concatenate([w.astype(bf16), qg], axis=0)
                else:
                    lhs = w.astype(bf16)
                r = jnp.dot(lhs, s_old.astype(bf16), preferred_element_type=f32)
                ub = (u0 - r[:C]).astype(bf16)
                if emit_o:
                    o_ref[bl, pl.ds(off, C), hs] = r[C:] + jnp.dot(
                        qkd_sc[pair, u * nh + h], ub, preferred_element_type=f32)
                k_out = (kf * jnp.exp(g_tot - g_col)).astype(bf16)
                s_sc[bl, h] = jnp.exp(g_tot) * s_old + lax.dot_general(
                    k_out, ub, (((0,), (0,)), ((), ())), preferred_element_type=f32)
        return carry

    lax.fori_loop(0, npairs, phase_c, 0)
    sfin_ref[...] = s_sc[...]


def _gdn_scan(q, k, v, gates, s0, *, rev, emit_o):
    nb, L, d_gdn = q.shape
    nh = d_gdn // GDN_HEAD
    nchunks = L // GDN_CHUNK
    C = GDN_CHUNK
    bb = max(1, min(nb, LANES // (nchunks * nh)))
    npairs = bb * nchunks // 2
    nprob = npairs * 2 * nh
    d = 1 if rev else 0
    kern = functools.partial(_gdn_scan_kernel, nchunks=nchunks, rev=rev, d=d, nh=nh, emit_o=emit_o)
    seq = lambda b: (b, 0, 0)
    st = lambda b: (b, 0, 0, 0)
    out_specs, out_shape = [], []
    if emit_o:
        out_specs.append(pl.BlockSpec((bb, L, d_gdn), seq))
        out_shape.append(jax.ShapeDtypeStruct((nb, L, d_gdn), f32))
    out_specs.append(pl.BlockSpec((bb, nh, GDN_HEAD, GDN_HEAD), st))
    out_shape.append(jax.ShapeDtypeStruct((nb, nh, GDN_HEAD, GDN_HEAD), f32))
    return pl.pallas_call(
        kern,
        grid=(nb // bb,),
        in_specs=[pl.BlockSpec((bb, L, d_gdn), seq)] * 3 + [
            pl.BlockSpec((bb, L, LANES), seq),
            pl.BlockSpec((bb, nh, GDN_HEAD, GDN_HEAD), st)],
        out_specs=out_specs,
        out_shape=out_shape,
        scratch_shapes=[pltpu.VMEM((bb, nh, GDN_HEAD, GDN_HEAD), f32),
                        pltpu.VMEM((C, npairs, 2 * nh, C), f32),
                        pltpu.VMEM((C, C, nprob), f32),
                        pltpu.VMEM((C, C, nprob), f32),
                        pltpu.VMEM((npairs, 2 * nh, C, C), bf16),
                        pltpu.VMEM((2 * npairs, C, LANES), f32),
                        pltpu.VMEM((2 * npairs, LANES, C), f32)],
        compiler_params=_cparams(("parallel",)),
        name=("gdn_scan_rev" if rev else "gdn_scan_fwd") + ("" if emit_o else "_state"),
    )(q, k, v, gates, s0)


def _post_kernel(x_ref, gate_ref, yf_ref, yr_ref, zs_ref, of_ref, or_ref, zg_ref,
                 wglu_ref, bglu_ref, nw_ref, wout_ref, lng_ref, lnb_ref, o_ref, *, tt, alpha):
    nb, _, d = x_ref.shape
    rows = nb * tt
    npk = yf_ref.shape[0]
    d_s5 = npk * LANES
    d_gdn = of_ref.shape[2]
    parts = []
    for j in range(npk):
        per_b = [(yf_ref[j, :, :, b, :] + yr_ref[j, :, :, b, :]).reshape(tt, LANES) for b in range(nb)]
        parts.append(jnp.concatenate(per_b, axis=0))
    y = jnp.concatenate(parts, axis=1)
    g = 0.5 * y * (1.0 + lax.erf(y * (2.0 ** -0.5)))
    glu = jnp.dot(g.astype(bf16), wglu_ref[...], preferred_element_type=f32) + bglu_ref[...]
    s5 = g * jax.nn.sigmoid(glu) * _silu(zs_ref[...].reshape(rows, d_s5))
    o = (of_ref[...] + or_ref[...]).reshape(rows, d_gdn)
    zg = zg_ref[...].reshape(rows, d_gdn)
    gd = []
    for h in range(d_gdn // GDN_HEAD):
        oh = o[:, h * GDN_HEAD:(h + 1) * GDN_HEAD]
        gd.append(oh * lax.rsqrt(jnp.mean(oh * oh, axis=-1, keepdims=True) + NORM_EPS) * nw_ref[...])
    gdn = jnp.concatenate(gd, axis=1) * _silu(zg)
    mix = jnp.concatenate([s5, gdn], axis=1).astype(bf16)
    yv = jnp.dot(mix, wout_ref[...], preferred_element_type=f32).reshape(nb, tt, d)
    r = alpha * x_ref[...] + gate_ref[...] * yv
    mu = jnp.mean(r, axis=-1, keepdims=True)
    var = jnp.mean(jnp.square(r - mu), axis=-1, keepdims=True)
    o_ref[...] = (r - mu) * lax.rsqrt(var + LN_EPS) * lng_ref[...] + lnb_ref[...]


def _post(x, gate, yf, yr, zs, o_f, o_r, zg, wglu, bglu, nw, wout, lng, lnb, alpha):
    nb, L, d = x.shape
    tt = 16
    npk = yf.shape[0]
    d_s5 = npk * LANES
    d_gdn = o_f.shape[2]
    kern = functools.partial(_post_kernel, tt=tt, alpha=alpha)
    tok = lambda i: (0, i, 0)
    c2 = lambda i: (0, 0)
    c3 = lambda i: (0, 0, 0)
    y_spec = pl.BlockSpec((npk, tt // S5_T, S5_T, nb, LANES), lambda i: (0, i, 0, 0, 0))
    return pl.pallas_call(
        kern,
        grid=(L // tt,),
        in_specs=[pl.BlockSpec((nb, tt, d), tok),
                  pl.BlockSpec((nb, 1, d), c3),
                  y_spec, y_spec,
                  pl.BlockSpec((nb, tt, d_s5), tok),
                  pl.BlockSpec((nb, tt, d_gdn), tok),
                  pl.BlockSpec((nb, tt, d_gdn), tok),
                  pl.BlockSpec((nb, tt, d_gdn), tok),
                  pl.BlockSpec(wglu.shape, c2),
                  pl.BlockSpec(bglu.shape, c2),
                  pl.BlockSpec(nw.shape, c2),
                  pl.BlockSpec(wout.shape, c2),
                  pl.BlockSpec(lng.shape, c3),
                  pl.BlockSpec(lnb.shape, c3)],
        out_specs=pl.BlockSpec((nb, tt, d), tok),
        out_shape=jax.ShapeDtypeStruct((nb, L, d), f32),
        compiler_params=_cparams(("parallel",)),
        name="post",
    )(x, gate, yf, yr, zs, o_f, o_r, zg, wglu, bglu, nw, wout, lng, lnb)


def _layer(x, c, ctx, c_ctx, p, alpha):
    nb, L, d = x.shape
    d_s5 = p['s5_d'].shape[0]
    d_gdn = p['conv_w'].shape[-1] // 3
    nh = d_gdn // GDN_HEAD
    npk = d_s5 // LANES

    pad = (-(nb + 1)) % 8
    cond = jnp.concatenate([c, c_ctx[None, :], jnp.zeros((pad, d), f32)], axis=0)
    m = _ada(cond, p['w_ada'], p['b_ada'])
    shift, scale, gate = m[:nb, :d], m[:nb, d:2 * d], m[:nb, 2 * d:]
    shift_c = jnp.broadcast_to(m[nb:nb + 1, :d], (nb, d))
    scale_c = jnp.broadcast_to(m[nb:nb + 1, d:2 * d], (nb, d))

    w_in = p['w_in']
    wu = w_in[:, :d_s5].astype(bf16)
    wm = w_in[:, d_s5:2 * d_s5 + 4 * d_gdn].astype(bf16)
    ng = 2 * N_DIR * nh
    wg = jnp.pad(w_in[:, 2 * d_s5 + 4 * d_gdn:], ((0, 0), (0, LANES - ng))).astype(bf16)
    gp = jnp.zeros((2, LANES), f32)
    gp = gp.at[0, ng // 2:ng].set(p['gdn_a_log'].reshape(-1).astype(f32))
    gp = gp.at[1, ng // 2:ng].set(p['gdn_dt_bias'].reshape(-1).astype(f32))

    u5, zs, qkv, zg, gates = _inproj(x, scale[:, None, :], shift[:, None, :], wu, wm, wg, gp, d_s5, d_gdn)
    u5c, _, qkvc, _, gatesc = _inproj(ctx, scale_c[:, None, :], shift_c[:, None, :], wu, wm, wg, gp, d_s5, d_gdn)

    sw = _s5_weights(p['s5_lambda_re'], p['s5_lambda_im'], p['s5_log_dt'],
                     p['s5_b_re'], p['s5_b_im'], p['s5_c_re'], p['s5_c_im'])
    sdim = sw['a_f'].shape[-1]
    a_f = jnp.broadcast_to(sw['a_f'][:, None, :], (npk, nb, sdim))
    a_r = jnp.broadcast_to(sw['a_r'][:, None, :], (npk, nb, sdim))
    h_zero = jnp.zeros((npk, nb, sdim), f32)
    d_skip = jnp.tile(p['s5_d'].astype(f32).reshape(npk, 1, LANES), (1, 1, S5_T))
    (hc_f,) = _s5_pass(u5c, sw['m_in_f'], a_f, h_zero, rev=False, emit_y=False)
    (hc_r,) = _s5_pass(u5c, sw['m_in_r'], a_r, h_zero, rev=True, emit_y=False)
    y_f, _ = _s5_pass(u5, sw['m_in_f'], a_f, hc_f, sw['m_out_f'], sw['m_intra'], d_skip, rev=False, emit_y=True)
    y_r, _ = _s5_pass(u5, sw['m_in_r'], a_r, hc_r, sw['m_out_r'], rev=True, emit_y=True)

    ql, kl, vl = _gdn_prep(qkv, p['conv_w'], True)
    qc, kc, vc = _gdn_prep(qkvc, p['conv_w'], False)
    s_zero = jnp.zeros((nb, nh, GDN_HEAD, GDN_HEAD), f32)
    (sc_f,) = _gdn_scan(qc, kc, vc, gatesc, s_zero, rev=False, emit_o=False)
    (sc_r,) = _gdn_scan(qc, kc, vc, gatesc, s_zero, rev=True, emit_o=False)
    o_f, _ = _gdn_scan(ql, kl, vl, gates, sc_f, rev=False, emit_o=True)
    o_r, _ = _gdn_scan(ql, kl, vl, gates, sc_r, rev=True, emit_o=True)

    return _post(x, gate[:, None, :], y_f, y_r, zs, o_f, o_r, zg,
                 p['w_glu'].astype(bf16), p['b_glu'].astype(f32).reshape(1, d_s5),
                 p['gdn_norm_w'].astype(f32).reshape(1, GDN_HEAD),
                 p['w_out'].astype(bf16), p['ln_g'].astype(f32).reshape(1, 1, d),
                 p['ln_b'].astype(f32).reshape(1, 1, d), alpha)


def kernel(x, c, ctx, c_ctx, w_ada, b_ada, w_in, s5_lambda_re, s5_lambda_im, s5_log_dt, s5_b_re, s5_b_im, s5_c_re, s5_c_im, s5_d, w_glu, b_glu, conv_w, gdn_a_log, gdn_dt_bias, gdn_norm_w, w_out, ln_g, ln_b):
    depth = w_ada.shape[0]
    assert depth == 1, "context-token outputs are only skipped for the last layer"
    alpha = (2.0 * depth) ** 0.25
    p = {
        'w_ada': w_ada[0], 'b_ada': b_ada[0], 'w_in': w_in[0],
        's5_lambda_re': s5_lambda_re[0], 's5_lambda_im': s5_lambda_im[0], 's5_log_dt': s5_log_dt[0],
        's5_b_re': s5_b_re[0], 's5_b_im': s5_b_im[0], 's5_c_re': s5_c_re[0], 's5_c_im': s5_c_im[0],
        's5_d': s5_d[0], 'w_glu': w_glu[0], 'b_glu': b_glu[0], 'conv_w': conv_w[0],
        'gdn_a_log': gdn_a_log[0], 'gdn_dt_bias': gdn_dt_bias[0], 'gdn_norm_w': gdn_norm_w[0],
        'w_out': w_out[0], 'ln_g': ln_g[0], 'ln_b': ln_b[0],
    }
    return _layer(x, c, ctx, c_ctx, p, alpha)
```

```python
import functools
import math

import numpy as np
import jax
import jax.numpy as jnp
from jax import lax
from jax.experimental import pallas as pl
from jax.experimental.pallas import tpu as pltpu

f32 = jnp.float32
bf16 = jnp.bfloat16
HI = lax.Precision.HIGHEST

LANES = 128
S5_GROUP = 16
S5_STATE = 64
S5_T = 8
PACK_G = LANES // S5_GROUP
GDN_HEAD = 128
GDN_CHUNK = 64
GRID_W = 64
N_DIR = 2
LN_EPS = 1e-5
NORM_EPS = 1e-6
VMEM_LIMIT = 56 * 1024 * 1024


def _cparams(sem):
    return pltpu.CompilerParams(dimension_semantics=sem, vmem_limit_bytes=VMEM_LIMIT)


def _silu(x):
    return x * jax.nn.sigmoid(x)


def _ada_kernel(c_ref, w_ref, b_ref, o_ref):
    o_ref[...] = jnp.dot(_silu(c_ref[...]), w_ref[...], precision=HI,
                         preferred_element_type=f32) + b_ref[...]


def _ada(cond, w_ada, b_ada):
    rows, d = cond.shape
    n = w_ada.shape[1]
    tn = 512
    return pl.pallas_call(
        _ada_kernel,
        grid=(n // tn,),
        in_specs=[pl.BlockSpec((rows, d), lambda i: (0, 0)),
                  pl.BlockSpec((d, tn), lambda i: (0, i)),
                  pl.BlockSpec((1, tn), lambda i: (0, i))],
        out_specs=pl.BlockSpec((rows, tn), lambda i: (0, i)),
        out_shape=jax.ShapeDtypeStruct((rows, n), f32),
        compiler_params=_cparams(("parallel",)),
        name="ada",
    )(cond, w_ada, b_ada.reshape(1, n))


def _inproj_kernel(x_ref, scale_ref, shift_ref, wu_ref, wm_ref, wg_ref, gp_ref,
                   u_ref, zs_ref, qkv_ref, zg_ref, gate_ref, *, tt, d_s5, d_gdn, n_beta):
    nb, _, d = x_ref.shape
    sc = 1.0 + scale_ref[...]
    sh = shift_ref[...]
    h = (x_ref[...] * sc + sh).reshape(nb * tt, d).astype(bf16)
    r = jnp.dot(h, wm_ref[...], preferred_element_type=f32)
    zs_ref[...] = r[:, :d_s5].reshape(nb, tt, d_s5)
    qkv_ref[...] = r[:, d_s5:d_s5 + 3 * d_gdn].reshape(nb, tt, 3 * d_gdn)
    zg_ref[...] = r[:, d_s5 + 3 * d_gdn:].reshape(nb, tt, d_gdn)
    lg = jnp.dot(h, wg_ref[...], preferred_element_type=f32)
    lane = lax.broadcasted_iota(jnp.int32, lg.shape, 1)
    a = lg + gp_ref[1:2, :]
    softplus = jnp.maximum(a, 0.0) + jnp.log1p(jnp.exp(-jnp.abs(a)))
    gates = jnp.where(lane < n_beta, jax.nn.sigmoid(lg), -jnp.exp(gp_ref[0:1, :]) * softplus)
    gate_ref[...] = gates.reshape(nb, tt, LANES)
    sc2 = sc[:, 0, :]
    sh2 = sh[:, 0, :]
    ht = jnp.concatenate([x_ref[:, t, :] * sc2 + sh2 for t in range(tt)], axis=0).astype(bf16)
    ru = jnp.dot(ht, wu_ref[...], preferred_element_type=f32)
    for j in range(d_s5 // LANES):
        u_ref[j] = ru[:, j * LANES:(j + 1) * LANES].reshape(tt // S5_T, S5_T, nb, LANES)


def _inproj(x, scale, shift, wu, wm, wg, gp, d_s5, d_gdn):
    nb, L, d = x.shape
    tt = 32
    npk = d_s5 // LANES
    kern = functools.partial(_inproj_kernel, tt=tt, d_s5=d_s5, d_gdn=d_gdn,
                             n_beta=N_DIR * (d_gdn // GDN_HEAD))
    const2 = lambda i: (0, 0)
    const3 = lambda i: (0, 0, 0)
    return pl.pallas_call(
        kern,
        grid=(L // tt,),
        in_specs=[pl.BlockSpec((nb, tt, d), lambda i: (0, i, 0)),
                  pl.BlockSpec((nb, 1, d), const3),
                  pl.BlockSpec((nb, 1, d), const3),
                  pl.BlockSpec(wu.shape, const2),
                  pl.BlockSpec(wm.shape, const2),
                  pl.BlockSpec(wg.shape, const2),
                  pl.BlockSpec(gp.shape, const2)],
        out_specs=[pl.BlockSpec((npk, tt // S5_T, S5_T, nb, LANES), lambda i: (0, i, 0, 0, 0)),
                   pl.BlockSpec((nb, tt, d_s5), lambda i: (0, i, 0)),
                   pl.BlockSpec((nb, tt, 3 * d_gdn), lambda i: (0, i, 0)),
                   pl.BlockSpec((nb, tt, d_gdn), lambda i: (0, i, 0)),
                   pl.BlockSpec((nb, tt, LANES), lambda i: (0, i, 0))],
        out_shape=[jax.ShapeDtypeStruct((npk, L // S5_T, S5_T, nb, LANES), f32),
                   jax.ShapeDtypeStruct((nb, L, d_s5), f32),
                   jax.ShapeDtypeStruct((nb, L, 3 * d_gdn), f32),
                   jax.ShapeDtypeStruct((nb, L, d_gdn), f32),
                   jax.ShapeDtypeStruct((nb, L, LANES), f32)],
        compiler_params=_cparams(("parallel",)),
        name="inproj",
    )(x, scale, shift, wu, wm, wg, gp)


def _s5_kernel(*refs, ct, rev, emit_y, intra):
    it = iter(refs)
    u_ref, min_ref, at_ref, h0_ref = next(it), next(it), next(it), next(it)
    mout_ref = next(it) if emit_y else None
    mintra_ref = next(it) if intra else None
    d_ref = next(it) if intra else None
    y_ref = next(it) if emit_y else None
    hfin_ref = next(it)
    h_sc, v_sc, hall_sc = next(it), next(it), next(it)

    t = pl.program_id(1)
    nt = pl.num_programs(1)
    nb = u_ref.shape[3]
    half = h_sc.shape[1] // 2

    @pl.when(t == 0)
    def _():
        h_sc[...] = h0_ref[0]

    x = jnp.concatenate([u_ref[0, :, i, :, :].reshape(ct * nb, LANES) for i in range(S5_T)], axis=1)
    xb = x.astype(bf16)
    v_sc[...] = jnp.dot(xb, min_ref[0], preferred_element_type=f32)
    are = at_ref[0, :, :half]
    aim = at_ref[0, :, half:]

    def body(s, carry):
        k = (ct - 1 - s) if rev else s
        off = pl.multiple_of(k * nb, nb)
        hre, him = carry
        hall_sc[pl.ds(off, nb), :half] = hre
        hall_sc[pl.ds(off, nb), half:] = him
        vre = v_sc[pl.ds(off, nb), :half]
        vim = v_sc[pl.ds(off, nb), half:]
        return are * hre - aim * him + vre, are * him + aim * hre + vim

    hre, him = lax.fori_loop(0, ct, body, (h_sc[:, :half], h_sc[:, half:]))
    h_sc[:, :half] = hre
    h_sc[:, half:] = him

    @pl.when(t == nt - 1)
    def _():
        hfin_ref[0] = h_sc[...]

    if emit_y:
        y = jnp.dot(hall_sc[...].astype(bf16), mout_ref[0], preferred_element_type=f32)
        if intra:
            y = y + jnp.dot(xb, mintra_ref[0], preferred_element_type=f32) + x * d_ref[0]
        for i in range(S5_T):
            y_ref[0, :, i, :, :] = y[:, i * LANES:(i + 1) * LANES].reshape(ct, nb, LANES)


def _s5_pass(u5, m_in, a_t, h0, m_out=None, m_intra=None, d_skip=None, *, rev, emit_y):
    npk, nc, _, nb, _ = u5.shape
    ct = min(32, nc)
    nt = nc // ct
    intra = m_intra is not None
    kdim = S5_T * LANES
    sdim = a_t.shape[-1]
    tmap = (lambda j, t: (j, nt - 1 - t, 0, 0, 0)) if rev else (lambda j, t: (j, t, 0, 0, 0))
    pmap = lambda j, t: (j, 0, 0)
    args = [u5, m_in, a_t, h0]
    in_specs = [pl.BlockSpec((1, ct, S5_T, nb, LANES), tmap),
                pl.BlockSpec((1, kdim, sdim), pmap),
                pl.BlockSpec((1, nb, sdim), pmap),
                pl.BlockSpec((1, nb, sdim), pmap)]
    if emit_y:
        args.append(m_out)
        in_specs.append(pl.BlockSpec((1, sdim, kdim), pmap))
    if intra:
        args += [m_intra, d_skip]
        in_specs += [pl.BlockSpec((1, kdim, kdim), pmap), pl.BlockSpec((1, 1, kdim), pmap)]
    out_specs, out_shape = [], []
    if emit_y:
        out_specs.append(pl.BlockSpec((1, ct, S5_T, nb, LANES), tmap))
        out_shape.append(jax.ShapeDtypeStruct(u5.shape, f32))
    out_specs.append(pl.BlockSpec((1, nb, sdim), pmap))
    out_shape.append(jax.ShapeDtypeStruct((npk, nb, sdim), f32))
    kern = functools.partial(_s5_kernel, ct=ct, rev=rev, emit_y=emit_y, intra=intra)
    return pl.pallas_call(
        kern,
        grid=(npk, nt),
        in_specs=in_specs,
        out_specs=out_specs,
        out_shape=out_shape,
        scratch_shapes=[pltpu.VMEM((nb, sdim), f32),
                        pltpu.VMEM((ct * nb, sdim), f32),
                        pltpu.VMEM((ct * nb, sdim), f32)],
        compiler_params=_cparams(("parallel", "arbitrary")),
        name="s5_rev" if rev else "s5_fwd",
    )(*args)


def _s5_weights(lam_re, lam_im, log_dt, b_re, b_im, c_re, c_im):
    T = S5_T
    ndir, G, P = lam_re.shape
    npk = G // PACK_G
    dt = jnp.exp(log_dt.astype(f32))[..., None]
    lr, li = lam_re.astype(f32), lam_im.astype(f32)
    zr, zi = lr * dt, li * dt

    def apow(n):
        m = jnp.exp(n * zr)
        return m * jnp.cos(n * zi), m * jnp.sin(n * zi)

    nr = jnp.expm1(zr) * jnp.cos(zi) - 2.0 * jnp.square(jnp.sin(0.5 * zi))
    ni = jnp.exp(zr) * jnp.sin(zi)
    den = lr * lr + li * li
    qr = (nr * lr + ni * li) / den
    qi = (ni * lr - nr * li) / den
    bre, bim = b_re.astype(f32), b_im.astype(f32)
    bbr = qr[..., None] * bre - qi[..., None] * bim
    bbi = qr[..., None] * bim + qi[..., None] * bre
    cr, ci = c_re.astype(f32), c_im.astype(f32)

    pw = [apow(n) for n in range(T + 1)]

    def a_times_b(n, d):
        pr, pi = pw[n][0][d][..., None], pw[n][1][d][..., None]
        return pr * bbr[d] - pi * bbi[d], pr * bbi[d] + pi * bbr[d]

    def c_times_a(n, d):
        pr, pi = pw[n][0][d][:, None, :], pw[n][1][d][:, None, :]
        return cr[d] * pr - ci[d] * pi, cr[d] * pi + ci[d] * pr

    eye = jnp.eye(PACK_G, dtype=f32)

    def pack_in(mats):
        re = jnp.stack([m[0] for m in mats])
        im = jnp.stack([m[1] for m in mats])
        base = jnp.stack([re, im])
        base = base.reshape(2, T, npk, PACK_G, P, S5_GROUP)
        base = base.transpose(2, 1, 3, 5, 0, 4)
        full = base[:, :, :, :, :, None, :] * eye[None, None, :, None, None, :, None]
        return full.reshape(npk, T * LANES, 2 * PACK_G * P)

    def pack_out(mats):
        re = jnp.stack([m[0] for m in mats])
        im = jnp.stack([-m[1] for m in mats])
        base = jnp.stack([re, im])
        base = base.reshape(2, T, npk, PACK_G, S5_GROUP, P)
        base = base.transpose(2, 0, 3, 5, 1, 4)
        full = base[:, :, :, :, :, None, :] * eye[None, None, :, None, None, :, None]
        return full.reshape(npk, 2 * PACK_G * P, T * LANES)

    m_in_f = pack_in([a_times_b(T - 1 - s, 0) for s in range(T)])
    m_in_r = pack_in([a_times_b(s, 1) for s in range(T)])
    m_out_f = pack_out([c_times_a(i + 1, 0) for i in range(T)])
    m_out_r = pack_out([c_times_a(T - i, 1) for i in range(T)])

    def lag_kernel(d):
        ks = []
        for tau in range(T):
            car, cai = c_times_a(tau, d)
            ks.append(jnp.einsum('gop,gpi->gio', car, bbr[d]) - jnp.einsum('gop,gpi->gio', cai, bbi[d]))
        return jnp.stack(ks)

    kf, kr = lag_kernel(0), lag_kernel(1)
    lag = np.arange(T)[None, :] - np.arange(T)[:, None]
    toe = (kf[np.maximum(lag, 0)] * jnp.asarray(lag >= 0, f32)[:, :, None, None, None]
           + kr[np.maximum(-lag, 0)] * jnp.asarray(lag <= 0, f32)[:, :, None, None, None])
    toe = toe.reshape(T, T, npk, PACK_G, S5_GROUP, S5_GROUP).transpose(2, 0, 3, 4, 1, 5)
    m_intra = (toe[:, :, :, :, :, None, :] * eye[None, None, :, None, None, :, None]).reshape(
        npk, T * LANES, T * LANES)

    def a_chunk(d):
        pr, pi = pw[T][0][d].reshape(npk, PACK_G * P), pw[T][1][d].reshape(npk, PACK_G * P)
        return jnp.concatenate([pr, pi], axis=-1)

    return dict(m_in_f=m_in_f.astype(bf16), m_in_r=m_in_r.astype(bf16),
                m_out_f=m_out_f.astype(bf16), m_out_r=m_out_r.astype(bf16),
                m_intra=m_intra.astype(bf16), a_f=a_chunk(0), a_r=a_chunk(1))


def _gdn_prep_kernel(*refs, tile, rows_conv, d_gdn):
    if rows_conv:
        main_ref, up_ref, dn_ref, w_ref, q_ref, k_ref, v_ref = refs
    else:
        main_ref, w_ref, q_ref, k_ref, v_ref = refs
    w = w_ref[...]
    if rows_conv:
        i = pl.program_id(1)
        n = pl.num_programs(1)
        up = up_ref[0] * (i > 0).astype(f32)
        dn = dn_ref[0] * (i < n - 1).astype(f32)
        ext = jnp.concatenate([up, main_ref[0], dn], axis=0)
        base = GRID_W
        drs = (-1, 0, 1)
        col = lax.broadcasted_iota(jnp.int32, (tile, 1), 0) % GRID_W
        mask_m = col != 0
        mask_p = col != GRID_W - 1
    else:
        ext = main_ref[0]
        base = 0
        drs = (0,)
        col = lax.broadcasted_iota(jnp.int32, (tile, 1), 0)
        mask_m = col != 0
        mask_p = col != tile - 1
    n_ext = ext.shape[0]
    shifted = {-1: pltpu.roll(ext, 1, 0), 0: ext, 1: pltpu.roll(ext, n_ext - 1, 0)}
    acc = {}
    for dc in (-1, 0, 1):
        a = None
        for dr in drs:
            tap = w[(dr + 1) * 3 + (dc + 1)][None, :]
            term = shifted[dc][base + GRID_W * dr: base + GRID_W * dr + tile] * tap
            a = term if a is None else a + term
        acc[dc] = a
    conv = acc[0] + jnp.where(mask_m, acc[-1], 0.0) + jnp.where(mask_p, acc[1], 0.0)
    s = _silu(conv)
    nh = d_gdn // GDN_HEAD
    for h in range(nh):
        qh = s[:, h * GDN_HEAD:(h + 1) * GDN_HEAD]
        kh = s[:, d_gdn + h * GDN_HEAD: d_gdn + (h + 1) * GDN_HEAD]
        qn = qh * lax.rsqrt(jnp.sum(qh * qh, axis=-1, keepdims=True) + NORM_EPS) * (GDN_HEAD ** -0.5)
        kn = kh * lax.rsqrt(jnp.sum(kh * kh, axis=-1, keepdims=True) + NORM_EPS)
        q_ref[0, :, h * GDN_HEAD:(h + 1) * GDN_HEAD] = qn.astype(bf16)
        k_ref[0, :, h * GDN_HEAD:(h + 1) * GDN_HEAD] = kn.astype(bf16)
    v_ref[0] = s[:, 2 * d_gdn:].astype(bf16)


def _gdn_prep(qkv, conv_w, rows_conv):
    nb, L, c3 = qkv.shape
    d_gdn = c3 // 3
    w9 = conv_w.reshape(9, c3).astype(f32)
    if rows_conv:
        tile = min(512, L)
        nt = L // tile
        r = tile // GRID_W
        nrows = L // GRID_W
        in_specs = [pl.BlockSpec((1, tile, c3), lambda b, i: (b, i, 0)),
                    pl.BlockSpec((1, GRID_W, c3), lambda b, i: (b, jnp.maximum(i * r - 1, 0), 0)),
                    pl.BlockSpec((1, GRID_W, c3), lambda b, i: (b, jnp.minimum((i + 1) * r, nrows - 1), 0)),
                    pl.BlockSpec((9, c3), lambda b, i: (0, 0))]
        args = (qkv, qkv, qkv, w9)
    else:
        tile, nt = L, 1
        in_specs = [pl.BlockSpec((1, tile, c3), lambda b, i: (b, i, 0)),
                    pl.BlockSpec((9, c3), lambda b, i: (0, 0))]
        args = (qkv, w9)
    kern = functools.partial(_gdn_prep_kernel, tile=tile, rows_conv=rows_conv, d_gdn=d_gdn)
    o_spec = pl.BlockSpec((1, tile, d_gdn), lambda b, i: (b, i, 0))
    o_shape = jax.ShapeDtypeStruct((nb, L, d_gdn), bf16)
    return pl.pallas_call(
        kern,
        grid=(nb, nt),
        in_specs=in_specs,
        out_specs=[o_spec, o_spec, o_spec],
        out_shape=[o_shape, o_shape, o_shape],
        compiler_params=_cparams(("parallel", "parallel")),
        name="gdn_prep",
    )(*args)


def _chunk_cumsum(g, rev):
    n = g.shape[0]
    r = lax.broadcasted_iota(jnp.int32, g.shape, 0)
    s = 1
    while s < n:
        if rev:
            g = g + jnp.where(r < n - s, pltpu.roll(g, n - s, 0), 0.0)
        else:
            g = g + jnp.where(r >= s, pltpu.roll(g, s, 0), 0.0)
        s *= 2
    return g


def _tri_solve_rows(apl_sc, tl_sc, dirn, upper):
    nblk = GDN_CHUNK // 8
    sub = lax.broadcasted_iota(jnp.int32, (8, LANES), 0)
    for step_blk in range(nblk):
        blk = (nblk - 1 - step_blk) if upper else step_blk
        mblocks = range(blk, nblk) if upper else range(0, blk + 1)

        def solve_row(s, carry, blk=blk, mblocks=mblocks):
            ii = (7 - s) if upper else s
            i = blk * 8 + ii
            acc = {}
            for mb in mblocks:
                arow = apl_sc[dirn, i, mb * 8:(mb + 1) * 8, :]
                jblocks = range(mb, nblk) if upper else range(0, mb + 1)
                for mm in range(8):
                    ab = jnp.broadcast_to(arow[mm:mm + 1, :], (8, LANES))
                    for jb in jblocks:
                        term = ab * tl_sc[dirn, mb * 8 + mm, jb * 8:(jb + 1) * 8, :]
                        key = (jb, mm % 2)
                        acc[key] = term if key not in acc else acc[key] + term
            for jb in (range(blk, nblk) if upper else range(0, blk + 1)):
                tot = acc[(jb, 0)] + acc[(jb, 1)]
                unit_row = jnp.where(sub == ii, 1.0, 0.0) if jb == blk else 0.0
                tl_sc[dirn, i, jb * 8:(jb + 1) * 8, :] = unit_row - tot
            return carry

        lax.fori_loop(0, 8, solve_row, 0)


def _gdn_scan_kernel(*refs, nchunks, nh, emit_o, has_s0):
    it = iter(refs)
    q_ref, k_ref, v_ref, g_ref = next(it), next(it), next(it), next(it)
    s0_ref = next(it) if has_s0 else None
    o_ref = next(it) if emit_o else None
    sfin_ref = next(it)
    s_sc, a4_sc, apl_sc, tl_sc, qkd_sc, gcol_sc = (next(it) for _ in range(6))
    C = GDN_CHUNK
    bb = q_ref.shape[0]
    slots = 2 * nh
    npairs = bb * nchunks // 2
    ppb = nchunks // 2
    nprob = npairs * slots
    s_sc[...] = s0_ref[...] if has_s0 else jnp.zeros(s_sc.shape, f32)
    if emit_o:
        o_ref[...] = jnp.zeros(o_ref.shape, f32)
    row = lax.broadcasted_iota(jnp.int32, (C, C), 0)
    col = lax.broadcasted_iota(jnp.int32, (C, C), 1)
    tri = (row >= col, row <= col)
    strict = (row > col, row < col)

    def unit_of(pair, u):
        bl = pair // ppb
        c = (pair % ppb) * 2 + u
        return bl, c, pl.multiple_of(c * C, C)

    def phase_a(pair, carry):
        for u in range(2):
            bl, c, off = unit_of(pair, u)
            unit = pair * 2 + u
            gts = g_ref[bl, pl.ds(off, C), :]
            gcum = (_chunk_cumsum(gts, False), _chunk_cumsum(gts, True))
            gcum_t = (gcum[0].T, gcum[1].T)
            gcol_sc[0, unit] = gcum[0]
            gcol_sc[1, unit] = gcum[1]
            for h in range(nh):
                hs = slice(h * GDN_HEAD, (h + 1) * GDN_HEAD)
                kb = k_ref[bl, pl.ds(off, C), hs]
                if emit_o:
                    kq = jnp.concatenate([kb, q_ref[bl, pl.ds(off, C), hs]], axis=0)
                else:
                    kq = kb
                kkqk = lax.dot_general(kq, kb, (((1,), (1,)), ((), ())), preferred_element_type=f32)
                a_mats, qkds = [], []
                for dirn in range(N_DIR):
                    cb = dirn * nh + h
                    cg = N_DIR * nh + dirn * nh + h
                    decay = jnp.exp(jnp.where(tri[dirn], gcum[dirn][:, cg:cg + 1] - gcum_t[dirn][cg:cg + 1, :],
                                              -1e30))
                    a_mats.append(jnp.where(strict[dirn], gts[:, cb:cb + 1] * kkqk[:C] * decay, 0.0))
                    if emit_o:
                        qkds.append(kkqk[C:] * decay)
                a4_sc[:, pair, u * nh + h, :] = jnp.concatenate(a_mats, axis=1)
                if emit_o:
                    qkd_sc[pair, u * nh + h] = jnp.concatenate(qkds, axis=1).astype(bf16)
        return carry

    lax.fori_loop(0, npairs, phase_a, 0)

    def to_lanes(i, carry):
        t = a4_sc[i].reshape(nprob, 2 * C).T
        apl_sc[0, i] = t[:C]
        apl_sc[1, i] = t[C:]
        return carry

    lax.fori_loop(0, C, to_lanes, 0)
    tl_sc[...] = jnp.zeros(tl_sc.shape, f32)
    _tri_solve_rows(apl_sc, tl_sc, 0, False)
    _tri_solve_rows(apl_sc, tl_sc, 1, True)

    def from_lanes(i, carry):
        t = jnp.concatenate([tl_sc[0, i], tl_sc[1, i]], axis=0)
        a4_sc[i] = t.T.reshape(npairs, slots, 2 * C)
        return carry

    lax.fori_loop(0, C, from_lanes, 0)

    def phase_c(step, carry):
        for sub_step in range(2):
            chains = []
            for dirn, pair, u in ((0, step, sub_step), (1, npairs - 1 - step, 1 - sub_step)):
                bl, c, off = unit_of(pair, u)
                gts = g_ref[bl, pl.ds(off, C), :]
                g_col_all = gcol_sc[dirn, pair * 2 + u]
                g_tot_all = jnp.sum(gts, axis=0, keepdims=True)
                for h in range(nh):
                    cb = dirn * nh + h
                    cg = N_DIR * nh + dirn * nh + h
                    hs = slice(h * GDN_HEAD, (h + 1) * GDN_HEAD)
                    lanes = slice(dirn * C, (dirn + 1) * C)
                    b_col = gts[:, cb:cb + 1]
                    g_col = g_col_all[:, cg:cg + 1]
                    g_tot = g_tot_all[:, cg:cg + 1]
                    gam = jnp.exp(g_col)
                    t_inv = a4_sc[:, pair, u * nh + h, lanes]
                    kf = k_ref[bl, pl.ds(off, C), hs].astype(f32)
                    vf = v_ref[bl, pl.ds(off, C), hs].astype(f32)
                    rhs = jnp.concatenate([b_col * vf, (b_col * gam) * kf], axis=1).astype(bf16)
                    sol = jnp.dot(t_inv.astype(bf16), rhs, preferred_element_type=f32)
                    lhs = sol[:, GDN_HEAD:].astype(bf16)
                    ch = dict(idx=(bl, dirn, h), off=off, hs=hs, u0=sol[:, :GDN_HEAD],
                              k_out=(kf * jnp.exp(g_tot - g_col)).astype(bf16), g_last=jnp.exp(g_tot))
                    if emit_o:
                        qg = (gam * q_ref[bl, pl.ds(off, C), hs].astype(f32)).astype(bf16)
                        lhs = jnp.concatenate([lhs, qg], axis=0)
                        ch['qkd'] = qkd_sc[pair, u * nh + h][:, lanes]
                    ch['lhs'] = lhs
                    chains.append(ch)
            s_old = [s_sc[ch['idx']] for ch in chains]
            r = [jnp.dot(ch['lhs'], s.astype(bf16), preferred_element_type=f32) for ch, s in zip(chains, s_old)]
            ub = [(ch['u0'] - ri[:C]).astype(bf16) for ch, ri in zip(chains, r)]
            s_new = [ch['g_last'] * s + lax.dot_general(ch['k_out'], ubi, (((0,), (0,)), ((), ())),
                                                        preferred_element_type=f32)
                     for ch, s, ubi in zip(chains, s_old, ub)]
            if emit_o:
                o_new = [ri[C:] + jnp.dot(ch['qkd'], ubi, preferred_element_type=f32)
                         for ch, ri, ubi in zip(chains, r, ub)]
            for n, ch in enumerate(chains):
                s_sc[ch['idx']] = s_new[n]
                if emit_o:
                    bl = ch['idx'][0]
                    o_ref[bl, pl.ds(ch['off'], C), ch['hs']] += o_new[n]
        return carry

    lax.fori_loop(0, npairs, phase_c, 0)
    sfin_ref[...] = s_sc[...]


def _gdn_scan(q, k, v, gates, s0, *, emit_o):
    nb, L, d_gdn = q.shape
    nh = d_gdn // GDN_HEAD
    nchunks = L // GDN_CHUNK
    C = GDN_CHUNK
    bb = max(1, min(nb, LANES // (nchunks * nh)))
    npairs = bb * nchunks // 2
    nprob = npairs * 2 * nh
    assert nprob == LANES, (nb, L, nh)
    has_s0 = s0 is not None
    kern = functools.partial(_gdn_scan_kernel, nchunks=nchunks, nh=nh, emit_o=emit_o, has_s0=has_s0)
    seq = lambda b: (b, 0, 0)
    st = lambda b: (b, 0, 0, 0, 0)
    st_spec = pl.BlockSpec((bb, N_DIR, nh, GDN_HEAD, GDN_HEAD), st)
    st_shape = jax.ShapeDtypeStruct((nb, N_DIR, nh, GDN_HEAD, GDN_HEAD), f32)
    args = [q, k, v, gates]
    in_specs = [pl.BlockSpec((bb, L, d_gdn), seq)] * 3 + [pl.BlockSpec((bb, L, LANES), seq)]
    if has_s0:
        args.append(s0)
        in_specs.append(st_spec)
    out_specs, out_shape = [], []
    if emit_o:
        out_specs.append(pl.BlockSpec((bb, L, d_gdn), seq))
        out_shape.append(jax.ShapeDtypeStruct((nb, L, d_gdn), f32))
    out_specs.append(st_spec)
    out_shape.append(st_shape)
    return pl.pallas_call(
        kern,
        grid=(nb // bb,),
        in_specs=in_specs,
        out_specs=out_specs,
        out_shape=out_shape,
        scratch_shapes=[pltpu.VMEM((bb, N_DIR, nh, GDN_HEAD, GDN_HEAD), f32),
                        pltpu.VMEM((C, npairs, 2 * nh, N_DIR * C), f32),
                        pltpu.VMEM((N_DIR, C, C, nprob), f32),
                        pltpu.VMEM((N_DIR, C, C, nprob), f32),
                        pltpu.VMEM((npairs, 2 * nh, C, N_DIR * C), bf16),
                        pltpu.VMEM((N_DIR, 2 * npairs, C, LANES), f32)],
        compiler_params=_cparams(("parallel",)),
        name="gdn_scan" if emit_o else "gdn_scan_state",
    )(*args)


def _post_kernel(x_ref, gate_ref, yf_ref, yr_ref, zs_ref, og_ref, zg_ref,
                 wglu_ref, bglu_ref, nw_ref, wout_ref, lng_ref, lnb_ref, o_ref, *, tt, alpha):
    nb, _, d = x_ref.shape
    rows = nb * tt
    npk = yf_ref.shape[0]
    d_s5 = npk * LANES
    d_gdn = og_ref.shape[2]
    parts = []
    for j in range(npk):
        per_b = [(yf_ref[j, :, :, b, :] + yr_ref[j, :, :, b, :]).reshape(tt, LANES) for b in range(nb)]
        parts.append(jnp.concatenate(per_b, axis=0))
    y = jnp.concatenate(parts, axis=1)
    g = 0.5 * y * (1.0 + lax.erf(y * (2.0 ** -0.5)))
    glu = jnp.dot(g.astype(bf16), wglu_ref[...], preferred_element_type=f32) + bglu_ref[...]
    s5 = g * jax.nn.sigmoid(glu) * _silu(zs_ref[...].reshape(rows, d_s5))
    o = og_ref[...].reshape(rows, d_gdn)
    zg = zg_ref[...].reshape(rows, d_gdn)
    gd = []
    for h in range(d_gdn // GDN_HEAD):
        oh = o[:, h * GDN_HEAD:(h + 1) * GDN_HEAD]
        gd.append(oh * lax.rsqrt(jnp.mean(oh * oh, axis=-1, keepdims=True) + NORM_EPS) * nw_ref[...])
    gdn = jnp.concatenate(gd, axis=1) * _silu(zg)
    mix = jnp.concatenate([s5, gdn], axis=1).astype(bf16)
    yv = jnp.dot(mix, wout_ref[...], preferred_element_type=f32).reshape(nb, tt, d)
    r = alpha * x_ref[...] + gate_ref[...] * yv
    mu = jnp.mean(r, axis=-1, keepdims=True)
    var = jnp.mean(jnp.square(r - mu), axis=-1, keepdims=True)
    o_ref[...] = (r - mu) * lax.rsqrt(var + LN_EPS) * lng_ref[...] + lnb_ref[...]


def _post(x, gate, yf, yr, zs, o_gdn, zg, wglu, bglu, nw, wout, lng, lnb, alpha):
    nb, L, d = x.shape
    tt = 16
    npk = yf.shape[0]
    d_s5 = npk * LANES
    d_gdn = o_gdn.shape[2]
    kern = functools.partial(_post_kernel, tt=tt, alpha=alpha)
    tok = lambda i: (0, i, 0)
    c2 = lambda i: (0, 0)
    c3 = lambda i: (0, 0, 0)
    y_spec = pl.BlockSpec((npk, tt // S5_T, S5_T, nb, LANES), lambda i: (0, i, 0, 0, 0))
    return pl.pallas_call(
        kern,
        grid=(L // tt,),
        in_specs=[pl.BlockSpec((nb, tt, d), tok),
                  pl.BlockSpec((nb, 1, d), c3),
                  y_spec, y_spec,
                  pl.BlockSpec((nb, tt, d_s5), tok),
                  pl.BlockSpec((nb, tt, d_gdn), tok),
                  pl.BlockSpec((nb, tt, d_gdn), tok),
                  pl.BlockSpec(wglu.shape, c2),
                  pl.BlockSpec(bglu.shape, c2),
                  pl.BlockSpec(nw.shape, c2),
                  pl.BlockSpec(wout.shape, c2),
                  pl.BlockSpec(lng.shape, c3),
                  pl.BlockSpec(lnb.shape, c3)],
        out_specs=pl.BlockSpec((nb, tt, d), tok),
        out_shape=jax.ShapeDtypeStruct((nb, L, d), f32),
        compiler_params=_cparams(("parallel",)),
        name="post",
    )(x, gate, yf, yr, zs, o_gdn, zg, wglu, bglu, nw, wout, lng, lnb)


def _layer(x, c, ctx, c_ctx, p, alpha):
    nb, L, d = x.shape
    d_s5 = p['s5_d'].shape[0]
    d_gdn = p['conv_w'].shape[-1] // 3
    nh = d_gdn // GDN_HEAD
    npk = d_s5 // LANES

    pad = (-(nb + 1)) % 8
    cond = jnp.concatenate([c, c_ctx[None, :], jnp.zeros((pad, d), f32)], axis=0)
    m = _ada(cond, p['w_ada'], p['b_ada'])
    shift, scale, gate = m[:nb, :d], m[:nb, d:2 * d], m[:nb, 2 * d:]
    shift_c = jnp.broadcast_to(m[nb:nb + 1, :d], (nb, d))
    scale_c = jnp.broadcast_to(m[nb:nb + 1, d:2 * d], (nb, d))

    w_in = p['w_in']
    wu = w_in[:, :d_s5].astype(bf16)
    wm = w_in[:, d_s5:2 * d_s5 + 4 * d_gdn].astype(bf16)
    ng = 2 * N_DIR * nh
    wg = jnp.pad(w_in[:, 2 * d_s5 + 4 * d_gdn:], ((0, 0), (0, LANES - ng))).astype(bf16)
    gp = jnp.zeros((2, LANES), f32)
    gp = gp.at[0, ng // 2:ng].set(p['gdn_a_log'].reshape(-1).astype(f32))
    gp = gp.at[1, ng // 2:ng].set(p['gdn_dt_bias'].reshape(-1).astype(f32))

    u5, zs, qkv, zg, gates = _inproj(x, scale[:, None, :], shift[:, None, :], wu, wm, wg, gp, d_s5, d_gdn)
    u5c, _, qkvc, _, gatesc = _inproj(ctx, scale_c[:, None, :], shift_c[:, None, :], wu, wm, wg, gp, d_s5, d_gdn)

    sw = _s5_weights(p['s5_lambda_re'], p['s5_lambda_im'], p['s5_log_dt'],
                     p['s5_b_re'], p['s5_b_im'], p['s5_c_re'], p['s5_c_im'])
    sdim = sw['a_f'].shape[-1]
    a_f = jnp.broadcast_to(sw['a_f'][:, None, :], (npk, nb, sdim))
    a_r = jnp.broadcast_to(sw['a_r'][:, None, :], (npk, nb, sdim))
    h_zero = jnp.zeros((npk, nb, sdim), f32)
    d_skip = jnp.tile(p['s5_d'].astype(f32).reshape(npk, 1, LANES), (1, 1, S5_T))
    (hc_f,) = _s5_pass(u5c, sw['m_in_f'], a_f, h_zero, rev=False, emit_y=False)
    (hc_r,) = _s5_pass(u5c, sw['m_in_r'], a_r, h_zero, rev=True, emit_y=False)
    y_f, _ = _s5_pass(u5, sw['m_in_f'], a_f, hc_f, sw['m_out_f'], sw['m_intra'], d_skip, rev=False, emit_y=True)
    y_r, _ = _s5_pass(u5, sw['m_in_r'], a_r, hc_r, sw['m_out_r'], rev=True, emit_y=True)

    ql, kl, vl = _gdn_prep(qkv, p['conv_w'], True)
    qc, kc, vc = _gdn_prep(qkvc, p['conv_w'], False)
    (s_ctx,) = _gdn_scan(qc, kc, vc, gatesc, None, emit_o=False)
    o_gdn, _ = _gdn_scan(ql, kl, vl, gates, s_ctx, emit_o=True)

    return _post(x, gate[:, None, :], y_f, y_r, zs, o_gdn, zg,
                 p['w_glu'].astype(bf16), p['b_glu'].astype(f32).reshape(1, d_s5),
                 p['gdn_norm_w'].astype(f32).reshape(1, GDN_HEAD),
                 p['w_out'].astype(bf16), p['ln_g'].astype(f32).reshape(1, 1, d),
                 p['ln_b'].astype(f32).reshape(1, 1, d), alpha)


def kernel(x, c, ctx, c_ctx, w_ada, b_ada, w_in, s5_lambda_re, s5_lambda_im, s5_log_dt, s5_b_re, s5_b_im, s5_c_re, s5_c_im, s5_d, w_glu, b_glu, conv_w, gdn_a_log, gdn_dt_bias, gdn_norm_w, w_out, ln_g, ln_b):
    depth = w_ada.shape[0]
    assert depth == 1, "context-token outputs are only skipped for the last layer"
    alpha = (2.0 * depth) ** 0.25
    p = {
        'w_ada': w_ada[0], 'b_ada': b_ada[0], 'w_in': w_in[0],
        's5_lambda_re': s5_lambda_re[0], 's5_lambda_im': s5_lambda_im[0], 's5_log_dt': s5_log_dt[0],
        's5_b_re': s5_b_re[0], 's5_b_im': s5_b_im[0], 's5_c_re': s5_c_re[0], 's5_c_im': s5_c_im[0],
        's5_d': s5_d[0], 'w_glu': w_glu[0], 'b_glu': b_glu[0], 'conv_w': conv_w[0],
        'gdn_a_log': gdn_a_log[0], 'gdn_dt_bias': gdn_dt_bias[0], 'gdn_norm_w': gdn_norm_w[0],
        'w_out': w_out[0], 'ln_g': ln_g[0], 'ln_b': ln_b[0],
    }
    return _layer(x, c, ctx, c_ctx, p, alpha)
```

```python
import functools
import math

import numpy as np
import jax
import jax.numpy as jnp
from jax import lax
from jax.experimental import pallas as pl
from jax.experimental.pallas import tpu as pltpu

f32 = jnp.float32
bf16 = jnp.bfloat16
HI = lax.Precision.HIGHEST

LANES = 128
S5_GROUP = 16
S5_STATE = 64
S5_T = 8
PACK_G = LANES // S5_GROUP
GDN_HEAD = 128
GDN_CHUNK = 64
GRID_W = 64
N_DIR = 2
LN_EPS = 1e-5
NORM_EPS = 1e-6
VMEM_LIMIT = 56 * 1024 * 1024


def _cparams(sem):
    return pltpu.CompilerParams(dimension_semantics=sem, vmem_limit_bytes=VMEM_LIMIT)


def _silu(x):
    return x * jax.nn.sigmoid(x)


def _ada_kernel(c_ref, w_ref, b_ref, o_ref):
    o_ref[...] = jnp.dot(_silu(c_ref[...]), w_ref[...], precision=HI,
                         preferred_element_type=f32) + b_ref[...]


def _ada(cond, w_ada, b_ada):
    rows, d = cond.shape
    n = w_ada.shape[1]
    tn = 512
    return pl.pallas_call(
        _ada_kernel,
        grid=(n // tn,),
        in_specs=[pl.BlockSpec((rows, d), lambda i: (0, 0)),
                  pl.BlockSpec((d, tn), lambda i: (0, i)),
                  pl.BlockSpec((1, tn), lambda i: (0, i))],
        out_specs=pl.BlockSpec((rows, tn), lambda i: (0, i)),
        out_shape=jax.ShapeDtypeStruct((rows, n), f32),
        compiler_params=_cparams(("parallel",)),
        name="ada",
    )(cond, w_ada, b_ada.reshape(1, n))


def _inproj_kernel(x_ref, scale_ref, shift_ref, wu_ref, wm_ref, wg_ref, gp_ref,
                   u_ref, zs_ref, qkv_ref, zg_ref, gate_ref, *, tt, d_s5, d_gdn, n_beta):
    nb, _, d = x_ref.shape
    sc = 1.0 + scale_ref[...]
    sh = shift_ref[...]
    h = (x_ref[...] * sc + sh).reshape(nb * tt, d).astype(bf16)
    r = jnp.dot(h, wm_ref[...], preferred_element_type=f32)
    zs_ref[...] = r[:, :d_s5].reshape(nb, tt, d_s5)
    qkv_ref[...] = r[:, d_s5:d_s5 + 3 * d_gdn].reshape(nb, tt, 3 * d_gdn)
    zg_ref[...] = r[:, d_s5 + 3 * d_gdn:].reshape(nb, tt, d_gdn)
    lg = jnp.dot(h, wg_ref[...], preferred_element_type=f32)
    lane = lax.broadcasted_iota(jnp.int32, lg.shape, 1)
    a = lg + gp_ref[1:2, :]
    softplus = jnp.maximum(a, 0.0) + jnp.log1p(jnp.exp(-jnp.abs(a)))
    gates = jnp.where(lane < n_beta, jax.nn.sigmoid(lg), -jnp.exp(gp_ref[0:1, :]) * softplus)
    gate_ref[...] = gates.reshape(nb, tt, LANES)
    sc2 = sc[:, 0, :]
    sh2 = sh[:, 0, :]
    ht = jnp.concatenate([x_ref[:, t, :] * sc2 + sh2 for t in range(tt)], axis=0).astype(bf16)
    ru = jnp.dot(ht, wu_ref[...], preferred_element_type=f32)
    for j in range(d_s5 // LANES):
        u_ref[j] = ru[:, j * LANES:(j + 1) * LANES].reshape(tt // S5_T, S5_T, nb, LANES)


def _inproj(x, scale, shift, wu, wm, wg, gp, d_s5, d_gdn):
    nb, L, d = x.shape
    tt = 32
    npk = d_s5 // LANES
    kern = functools.partial(_inproj_kernel, tt=tt, d_s5=d_s5, d_gdn=d_gdn,
                             n_beta=N_DIR * (d_gdn // GDN_HEAD))
    const2 = lambda i: (0, 0)
    const3 = lambda i: (0, 0, 0)
    return pl.pallas_call(
        kern,
        grid=(L // tt,),
        in_specs=[pl.BlockSpec((nb, tt, d), lambda i: (0, i, 0)),
                  pl.BlockSpec((nb, 1, d), const3),
                  pl.BlockSpec((nb, 1, d), const3),
                  pl.BlockSpec(wu.shape, const2),
                  pl.BlockSpec(wm.shape, const2),
                  pl.BlockSpec(wg.shape, const2),
                  pl.BlockSpec(gp.shape, const2)],
        out_specs=[pl.BlockSpec((npk, tt // S5_T, S5_T, nb, LANES), lambda i: (0, i, 0, 0, 0)),
                   pl.BlockSpec((nb, tt, d_s5), lambda i: (0, i, 0)),
                   pl.BlockSpec((nb, tt, 3 * d_gdn), lambda i: (0, i, 0)),
                   pl.BlockSpec((nb, tt, d_gdn), lambda i: (0, i, 0)),
                   pl.BlockSpec((nb, tt, LANES), lambda i: (0, i, 0))],
        out_shape=[jax.ShapeDtypeStruct((npk, L // S5_T, S5_T, nb, LANES), f32),
                   jax.ShapeDtypeStruct((nb, L, d_s5), f32),
                   jax.ShapeDtypeStruct((nb, L, 3 * d_gdn), f32),
                   jax.ShapeDtypeStruct((nb, L, d_gdn), f32),
                   jax.ShapeDtypeStruct((nb, L, LANES), f32)],
        compiler_params=_cparams(("parallel",)),
        name="inproj",
    )(x, scale, shift, wu, wm, wg, gp)


def _s5_kernel(*refs, ct, rev, emit_y, intra):
    it = iter(refs)
    u_ref, min_ref, at_ref, h0_ref = next(it), next(it), next(it), next(it)
    mout_ref = next(it) if emit_y else None
    mintra_ref = next(it) if intra else None
    d_ref = next(it) if intra else None
    y_ref = next(it) if emit_y else None
    hfin_ref = next(it)
    h_sc, v_sc, hall_sc = next(it), next(it), next(it)

    t = pl.program_id(1)
    nt = pl.num_programs(1)
    nb = u_ref.shape[3]
    half = h_sc.shape[1] // 2

    @pl.when(t == 0)
    def _():
        h_sc[...] = h0_ref[0]

    x = jnp.concatenate([u_ref[0, :, i, :, :].reshape(ct * nb, LANES) for i in range(S5_T)], axis=1)
    xb = x.astype(bf16)
    v_sc[...] = jnp.dot(xb, min_ref[0], preferred_element_type=f32)
    are = at_ref[0, :, :half]
    aim = at_ref[0, :, half:]

    def body(s, carry):
        k = (ct - 1 - s) if rev else s
        off = pl.multiple_of(k * nb, nb)
        hre, him = carry
        hall_sc[pl.ds(off, nb), :half] = hre
        hall_sc[pl.ds(off, nb), half:] = him
        vre = v_sc[pl.ds(off, nb), :half]
        vim = v_sc[pl.ds(off, nb), half:]
        return are * hre - aim * him + vre, are * him + aim * hre + vim

    hre, him = lax.fori_loop(0, ct, body, (h_sc[:, :half], h_sc[:, half:]))
    h_sc[:, :half] = hre
    h_sc[:, half:] = him

    @pl.when(t == nt - 1)
    def _():
        hfin_ref[0] = h_sc[...]

    if emit_y:
        y = jnp.dot(hall_sc[...].astype(bf16), mout_ref[0], preferred_element_type=f32)
        if intra:
            y = y + jnp.dot(xb, mintra_ref[0], preferred_element_type=f32) + x * d_ref[0]
        for i in range(S5_T):
            y_ref[0, :, i, :, :] = y[:, i * LANES:(i + 1) * LANES].reshape(ct, nb, LANES)


def _s5_pass(u5, m_in, a_t, h0, m_out=None, m_intra=None, d_skip=None, *, rev, emit_y):
    npk, nc, _, nb, _ = u5.shape
    ct = min(32, nc)
    nt = nc // ct
    intra = m_intra is not None
    kdim = S5_T * LANES
    sdim = a_t.shape[-1]
    tmap = (lambda j, t: (j, nt - 1 - t, 0, 0, 0)) if rev else (lambda j, t: (j, t, 0, 0, 0))
    pmap = lambda j, t: (j, 0, 0)
    args = [u5, m_in, a_t, h0]
    in_specs = [pl.BlockSpec((1, ct, S5_T, nb, LANES), tmap),
                pl.BlockSpec((1, kdim, sdim), pmap),
                pl.BlockSpec((1, nb, sdim), pmap),
                pl.BlockSpec((1, nb, sdim), pmap)]
    if emit_y:
        args.append(m_out)
        in_specs.append(pl.BlockSpec((1, sdim, kdim), pmap))
    if intra:
        args += [m_intra, d_skip]
        in_specs += [pl.BlockSpec((1, kdim, kdim), pmap), pl.BlockSpec((1, 1, kdim), pmap)]
    out_specs, out_shape = [], []
    if emit_y:
        out_specs.append(pl.BlockSpec((1, ct, S5_T, nb, LANES), tmap))
        out_shape.append(jax.ShapeDtypeStruct(u5.shape, f32))
    out_specs.append(pl.BlockSpec((1, nb, sdim), pmap))
    out_shape.append(jax.ShapeDtypeStruct((npk, nb, sdim), f32))
    kern = functools.partial(_s5_kernel, ct=ct, rev=rev, emit_y=emit_y, intra=intra)
    return pl.pallas_call(
        kern,
        grid=(npk, nt),
        in_specs=in_specs,
        out_specs=out_specs,
        out_shape=out_shape,
        scratch_shapes=[pltpu.VMEM((nb, sdim), f32),
                        pltpu.VMEM((ct * nb, sdim), f32),
                        pltpu.VMEM((ct * nb, sdim), f32)],
        compiler_params=_cparams(("parallel", "arbitrary")),
        name="s5_rev" if rev else "s5_fwd",
    )(*args)


def _s5_weights(lam_re, lam_im, log_dt, b_re, b_im, c_re, c_im):
    T = S5_T
    ndir, G, P = lam_re.shape
    npk = G // PACK_G
    dt = jnp.exp(log_dt.astype(f32))[..., None]
    lr, li = lam_re.astype(f32), lam_im.astype(f32)
    zr, zi = lr * dt, li * dt

    def apow(n):
        m = jnp.exp(n * zr)
        return m * jnp.cos(n * zi), m * jnp.sin(n * zi)

    nr = jnp.expm1(zr) * jnp.cos(zi) - 2.0 * jnp.square(jnp.sin(0.5 * zi))
    ni = jnp.exp(zr) * jnp.sin(zi)
    den = lr * lr + li * li
    qr = (nr * lr + ni * li) / den
    qi = (ni * lr - nr * li) / den
    bre, bim = b_re.astype(f32), b_im.astype(f32)
    bbr = qr[..., None] * bre - qi[..., None] * bim
    bbi = qr[..., None] * bim + qi[..., None] * bre
    cr, ci = c_re.astype(f32), c_im.astype(f32)

    pw = [apow(n) for n in range(T + 1)]

    def a_times_b(n, d):
        pr, pi = pw[n][0][d][..., None], pw[n][1][d][..., None]
        return pr * bbr[d] - pi * bbi[d], pr * bbi[d] + pi * bbr[d]

    def c_times_a(n, d):
        pr, pi = pw[n][0][d][:, None, :], pw[n][1][d][:, None, :]
        return cr[d] * pr - ci[d] * pi, cr[d] * pi + ci[d] * pr

    eye = jnp.eye(PACK_G, dtype=f32)

    def pack_in(mats):
        re = jnp.stack([m[0] for m in mats])
        im = jnp.stack([m[1] for m in mats])
        base = jnp.stack([re, im])
        base = base.reshape(2, T, npk, PACK_G, P, S5_GROUP)
        base = base.transpose(2, 1, 3, 5, 0, 4)
        full = base[:, :, :, :, :, None, :] * eye[None, None, :, None, None, :, None]
        return full.reshape(npk, T * LANES, 2 * PACK_G * P)

    def pack_out(mats):
        re = jnp.stack([m[0] for m in mats])
        im = jnp.stack([-m[1] for m in mats])
        base = jnp.stack([re, im])
        base = base.reshape(2, T, npk, PACK_G, S5_GROUP, P)
        base = base.transpose(2, 0, 3, 5, 1, 4)
        full = base[:, :, :, :, :, None, :] * eye[None, None, :, None, None, :, None]
        return full.reshape(npk, 2 * PACK_G * P, T * LANES)

    m_in_f = pack_in([a_times_b(T - 1 - s, 0) for s in range(T)])
    m_in_r = pack_in([a_times_b(s, 1) for s in range(T)])
    m_out_f = pack_out([c_times_a(i + 1, 0) for i in range(T)])
    m_out_r = pack_out([c_times_a(T - i, 1) for i in range(T)])

    def lag_kernel(d):
        ks = []
        for tau in range(T):
            car, cai = c_times_a(tau, d)
            ks.append(jnp.einsum('gop,gpi->gio', car, bbr[d]) - jnp.einsum('gop,gpi->gio', cai, bbi[d]))
        return jnp.stack(ks)

    kf, kr = lag_kernel(0), lag_kernel(1)
    lag = np.arange(T)[None, :] - np.arange(T)[:, None]
    toe = (kf[np.maximum(lag, 0)] * jnp.asarray(lag >= 0, f32)[:, :, None, None, None]
           + kr[np.maximum(-lag, 0)] * jnp.asarray(lag <= 0, f32)[:, :, None, None, None])
    toe = toe.reshape(T, T, npk, PACK_G, S5_GROUP, S5_GROUP).transpose(2, 0, 3, 4, 1, 5)
    m_intra = (toe[:, :, :, :, :, None, :] * eye[None, None, :, None, None, :, None]).reshape(
        npk, T * LANES, T * LANES)

    def a_chunk(d):
        pr, pi = pw[T][0][d].reshape(npk, PACK_G * P), pw[T][1][d].reshape(npk, PACK_G * P)
        return jnp.concatenate([pr, pi], axis=-1)

    return dict(m_in_f=m_in_f.astype(bf16), m_in_r=m_in_r.astype(bf16),
                m_out_f=m_out_f.astype(bf16), m_out_r=m_out_r.astype(bf16),
                m_intra=m_intra.astype(bf16), a_f=a_chunk(0), a_r=a_chunk(1))


def _gdn_prep_kernel(*refs, tile, rows_conv, d_gdn):
    if rows_conv:
        main_ref, up_ref, dn_ref, w_ref, q_ref, k_ref, v_ref = refs
    else:
        main_ref, w_ref, q_ref, k_ref, v_ref = refs
    w = w_ref[...]
    if rows_conv:
        i = pl.program_id(1)
        n = pl.num_programs(1)
        up = up_ref[0] * (i > 0).astype(f32)
        dn = dn_ref[0] * (i < n - 1).astype(f32)
        ext = jnp.concatenate([up, main_ref[0], dn], axis=0)
        base = GRID_W
        drs = (-1, 0, 1)
        col = lax.broadcasted_iota(jnp.int32, (tile, 1), 0) % GRID_W
        mask_m = col != 0
        mask_p = col != GRID_W - 1
    else:
        ext = main_ref[0]
        base = 0
        drs = (0,)
        col = lax.broadcasted_iota(jnp.int32, (tile, 1), 0)
        mask_m = col != 0
        mask_p = col != tile - 1
    n_ext = ext.shape[0]
    shifted = {-1: pltpu.roll(ext, 1, 0), 0: ext, 1: pltpu.roll(ext, n_ext - 1, 0)}
    acc = {}
    for dc in (-1, 0, 1):
        a = None
        for dr in drs:
            tap = w[(dr + 1) * 3 + (dc + 1)][None, :]
            term = shifted[dc][base + GRID_W * dr: base + GRID_W * dr + tile] * tap
            a = term if a is None else a + term
        acc[dc] = a
    conv = acc[0] + jnp.where(mask_m, acc[-1], 0.0) + jnp.where(mask_p, acc[1], 0.0)
    s = _silu(conv)
    nh = d_gdn // GDN_HEAD
    for h in range(nh):
        qh = s[:, h * GDN_HEAD:(h + 1) * GDN_HEAD]
        kh = s[:, d_gdn + h * GDN_HEAD: d_gdn + (h + 1) * GDN_HEAD]
        qn = qh * lax.rsqrt(jnp.sum(qh * qh, axis=-1, keepdims=True) + NORM_EPS) * (GDN_HEAD ** -0.5)
        kn = kh * lax.rsqrt(jnp.sum(kh * kh, axis=-1, keepdims=True) + NORM_EPS)
        q_ref[0, :, h * GDN_HEAD:(h + 1) * GDN_HEAD] = qn.astype(bf16)
        k_ref[0, :, h * GDN_HEAD:(h + 1) * GDN_HEAD] = kn.astype(bf16)
    v_ref[0] = s[:, 2 * d_gdn:].astype(bf16)


def _gdn_prep(qkv, conv_w, rows_conv):
    nb, L, c3 = qkv.shape
    d_gdn = c3 // 3
    w9 = conv_w.reshape(9, c3).astype(f32)
    if rows_conv:
        tile = min(512, L)
        nt = L // tile
        r = tile // GRID_W
        nrows = L // GRID_W
        in_specs = [pl.BlockSpec((1, tile, c3), lambda b, i: (b, i, 0)),
                    pl.BlockSpec((1, GRID_W, c3), lambda b, i: (b, jnp.maximum(i * r - 1, 0), 0)),
                    pl.BlockSpec((1, GRID_W, c3), lambda b, i: (b, jnp.minimum((i + 1) * r, nrows - 1), 0)),
                    pl.BlockSpec((9, c3), lambda b, i: (0, 0))]
        args = (qkv, qkv, qkv, w9)
    else:
        tile, nt = L, 1
        in_specs = [pl.BlockSpec((1, tile, c3), lambda b, i: (b, i, 0)),
                    pl.BlockSpec((9, c3), lambda b, i: (0, 0))]
        args = (qkv, w9)
    kern = functools.partial(_gdn_prep_kernel, tile=tile, rows_conv=rows_conv, d_gdn=d_gdn)
    o_spec = pl.BlockSpec((1, tile, d_gdn), lambda b, i: (b, i, 0))
    o_shape = jax.ShapeDtypeStruct((nb, L, d_gdn), bf16)
    return pl.pallas_call(
        kern,
        grid=(nb, nt),
        in_specs=in_specs,
        out_specs=[o_spec, o_spec, o_spec],
        out_shape=[o_shape, o_shape, o_shape],
        compiler_params=_cparams(("parallel", "parallel")),
        name="gdn_prep",
    )(*args)


def _chunk_cumsum(g, rev):
    n = g.shape[0]
    r = lax.broadcasted_iota(jnp.int32, g.shape, 0)
    s = 1
    while s < n:
        if rev:
            g = g + jnp.where(r < n - s, pltpu.roll(g, n - s, 0), 0.0)
        else:
            g = g + jnp.where(r >= s, pltpu.roll(g, s, 0), 0.0)
        s *= 2
    return g


def _tri_solve_rows(apl_sc, tl_sc, dirn, upper):
    nblk = GDN_CHUNK // 8
    base = dirn * GDN_CHUNK
    sub = lax.broadcasted_iota(jnp.int32, (8, LANES), 0)
    for step_blk in range(nblk):
        blk = (nblk - 1 - step_blk) if upper else step_blk
        mblocks = range(blk, nblk) if upper else range(0, blk + 1)

        def solve_row(s, carry, blk=blk, mblocks=mblocks):
            ii = (7 - s) if upper else s
            i = blk * 8 + ii
            acc = {}
            for mb in mblocks:
                arow = apl_sc[i, base + mb * 8:base + (mb + 1) * 8, :]
                jblocks = range(mb, nblk) if upper else range(0, mb + 1)
                for mm in range(8):
                    ab = jnp.broadcast_to(arow[mm:mm + 1, :], (8, LANES))
                    for jb in jblocks:
                        term = ab * tl_sc[mb * 8 + mm, base + jb * 8:base + (jb + 1) * 8, :]
                        key = (jb, mm % 2)
                        acc[key] = term if key not in acc else acc[key] + term
            for jb in (range(blk, nblk) if upper else range(0, blk + 1)):
                tot = acc[(jb, 0)] + acc[(jb, 1)]
                unit_row = jnp.where(sub == ii, 1.0, 0.0) if jb == blk else 0.0
                tl_sc[i, base + jb * 8:base + (jb + 1) * 8, :] = unit_row - tot
            return carry

        lax.fori_loop(0, 8, solve_row, 0)


def _gdn_scan_kernel(*refs, nchunks, nh, emit_o, has_s0):
    it = iter(refs)
    q_ref, k_ref, v_ref, g_ref = next(it), next(it), next(it), next(it)
    s0_ref = next(it) if has_s0 else None
    o_ref = next(it) if emit_o else None
    sfin_ref = next(it)
    s_sc, at_sc, apl_sc, tl_sc, qkd_sc, w_sc, gcol_sc, grow_sc = (next(it) for _ in range(8))
    C = GDN_CHUNK
    bb = q_ref.shape[0]
    nunits = bb * nchunks
    s_sc[...] = s0_ref[...] if has_s0 else jnp.zeros(s_sc.shape, f32)
    if emit_o:
        o_ref[...] = jnp.zeros(o_ref.shape, f32)
    row = lax.broadcasted_iota(jnp.int32, (C, C), 0)
    col = lax.broadcasted_iota(jnp.int32, (C, C), 1)
    tri = (row >= col, row <= col)
    strict = (row > col, row < col)
    nbeta = N_DIR * nh

    def unit_of(unit):
        bl = unit // nchunks
        return bl, pl.multiple_of((unit % nchunks) * C, C)

    def phase_a(unit, carry):
        bl, off = unit_of(unit)
        gts = g_ref[bl, pl.ds(off, C), :]
        gcum = (_chunk_cumsum(gts, False), _chunk_cumsum(gts, True))
        gcum_t = (gcum[0].T, gcum[1].T)
        gcol_sc[0, unit] = gcum[0]
        gcol_sc[1, unit] = gcum[1]
        grow_sc[unit] = jnp.concatenate(
            [gts.T[:nbeta], gcum_t[0][nbeta:nbeta + nh], gcum_t[1][nbeta + nh:nbeta + 2 * nh]], axis=0)
        for h in range(nh):
            hs = slice(h * GDN_HEAD, (h + 1) * GDN_HEAD)
            kb = k_ref[bl, pl.ds(off, C), hs]
            if emit_o:
                kq = jnp.concatenate([kb, q_ref[bl, pl.ds(off, C), hs]], axis=0)
            else:
                kq = kb
            kkqk = lax.dot_general(kq, kb, (((1,), (1,)), ((), ())), preferred_element_type=f32)
            a_mats, qkds = [], []
            for dirn in range(N_DIR):
                cb = dirn * nh + h
                cg = nbeta + dirn * nh + h
                decay = jnp.exp(jnp.where(tri[dirn], gcum[dirn][:, cg:cg + 1] - gcum_t[dirn][cg:cg + 1, :],
                                          -1e30))
                a_mats.append(jnp.where(strict[dirn], gts[:, cb:cb + 1] * kkqk[:C] * decay, 0.0))
                if emit_o:
                    qkds.append(kkqk[C:] * decay)
            at_sc[unit * nh + h] = jnp.concatenate(a_mats, axis=1)
            if emit_o:
                qkd_sc[unit * nh + h] = jnp.concatenate(qkds, axis=1).astype(bf16)
        return carry

    lax.fori_loop(0, nunits, phase_a, 0, unroll=2)

    def swap_in(ib, carry):
        rows = pl.ds(pl.multiple_of(ib * 8, 8), 8)
        tl_sc[rows] = jnp.swapaxes(at_sc[:, rows, :], 0, 1)
        return carry

    def to_lanes(i, carry):
        apl_sc[i] = tl_sc[i].T
        return carry

    lax.fori_loop(0, C // 8, swap_in, 0)
    lax.fori_loop(0, C, to_lanes, 0, unroll=4)
    tl_sc[...] = jnp.zeros(tl_sc.shape, f32)
    _tri_solve_rows(apl_sc, tl_sc, 0, False)
    _tri_solve_rows(apl_sc, tl_sc, 1, True)

    def from_lanes(i, carry):
        apl_sc[i] = tl_sc[i].T
        return carry

    def swap_out(ib, carry):
        rows = pl.ds(pl.multiple_of(ib * 8, 8), 8)
        at_sc[:, rows, :] = jnp.swapaxes(apl_sc[rows], 0, 1)
        return carry

    lax.fori_loop(0, C, from_lanes, 0, unroll=4)
    lax.fori_loop(0, C // 8, swap_out, 0)

    u0_sc = (apl_sc, tl_sc)

    def u0_slot(p):
        return p // 2, pl.ds(pl.multiple_of((p % 2) * C, C), C)

    def phase_b2(unit, carry):
        bl, off = unit_of(unit)
        g_rows = grow_sc[unit]
        for h in range(nh):
            hs = slice(h * GDN_HEAD, (h + 1) * GDN_HEAD)
            p = unit * nh + h
            t_both = at_sc[p]
            kb = k_ref[bl, pl.ds(off, C), hs]
            vb = v_ref[bl, pl.ds(off, C), hs]
            for dirn in range(N_DIR):
                cb = dirn * nh + h
                cg = nbeta + dirn * nh + h
                t_inv = t_both[:, dirn * C:(dirn + 1) * C]
                b_row = g_rows[cb:cb + 1, :]
                bg_row = b_row * jnp.exp(g_rows[cg:cg + 1, :])
                u0 = jnp.dot((t_inv * b_row).astype(bf16), vb, preferred_element_type=f32)
                w = jnp.dot((t_inv * bg_row).astype(bf16), kb, preferred_element_type=f32)
                slab, rows = u0_slot(p)
                u0_sc[dirn][slab, rows, :] = u0
                w_sc[dirn, p] = w.astype(bf16)
        return carry

    lax.fori_loop(0, nunits, phase_b2, 0, unroll=2)

    def phase_c(step, carry):
        chains = []
        for dirn, unit in ((0, step), (1, nunits - 1 - step)):
            bl, off = unit_of(unit)
            g_col_all = gcol_sc[dirn, unit]
            g_tot_all = g_col_all[0:1, :] if dirn else g_col_all[C - 1:C, :]
            for h in range(nh):
                cg = nbeta + dirn * nh + h
                hs = slice(h * GDN_HEAD, (h + 1) * GDN_HEAD)
                p = unit * nh + h
                g_col = g_col_all[:, cg:cg + 1]
                g_tot = g_tot_all[:, cg:cg + 1]
                slab, rows = u0_slot(p)
                lhs = w_sc[dirn, p]
                ch = dict(idx=(bl, dirn, h), off=off, hs=hs, u0=u0_sc[dirn][slab, rows, :],
                          kb=k_ref[bl, pl.ds(off, C), hs],
                          e_col=jnp.exp(g_tot - g_col), g_last=jnp.exp(g_tot))
                if emit_o:
                    lhs = jnp.concatenate([lhs, q_ref[bl, pl.ds(off, C), hs]], axis=0)
                    ch['qkd'] = qkd_sc[p][:, dirn * C:(dirn + 1) * C]
                    ch['gam'] = jnp.exp(g_col)
                ch['lhs'] = lhs
                chains.append(ch)
        s_old = [s_sc[ch['idx']] for ch in chains]
        r = [jnp.dot(ch['lhs'], s.astype(bf16), preferred_element_type=f32) for ch, s in zip(chains, s_old)]
        u = [ch['u0'] - ri[:C] for ch, ri in zip(chains, r)]
        s_new = [ch['g_last'] * s + lax.dot_general(ch['kb'], (ch['e_col'] * ui).astype(bf16),
                                                    (((0,), (0,)), ((), ())), preferred_element_type=f32)
                 for ch, s, ui in zip(chains, s_old, u)]
        if emit_o:
            o_new = [ch['gam'] * ri[C:] + jnp.dot(ch['qkd'], ui.astype(bf16), preferred_element_type=f32)
                     for ch, ri, ui in zip(chains, r, u)]
        for n, ch in enumerate(chains):
            s_sc[ch['idx']] = s_new[n]
            if emit_o:
                bl = ch['idx'][0]
                o_ref[bl, pl.ds(ch['off'], C), ch['hs']] += o_new[n]
        return carry

    lax.fori_loop(0, nunits, phase_c, 0)
    sfin_ref[...] = s_sc[...]


def _gdn_scan(q, k, v, gates, s0, *, emit_o):
    nb, L, d_gdn = q.shape
    nh = d_gdn // GDN_HEAD
    nchunks = L // GDN_CHUNK
    C = GDN_CHUNK
    bb = max(1, min(nb, LANES // (nchunks * nh)))
    nunits = bb * nchunks
    nprob = nunits * nh
    assert nprob == LANES, (nb, L, nh)
    has_s0 = s0 is not None
    kern = functools.partial(_gdn_scan_kernel, nchunks=nchunks, nh=nh, emit_o=emit_o, has_s0=has_s0)
    seq = lambda b: (b, 0, 0)
    st = lambda b: (b, 0, 0, 0, 0)
    st_spec = pl.BlockSpec((bb, N_DIR, nh, GDN_HEAD, GDN_HEAD), st)
    st_shape = jax.ShapeDtypeStruct((nb, N_DIR, nh, GDN_HEAD, GDN_HEAD), f32)
    args = [q, k, v, gates]
    in_specs = [pl.BlockSpec((bb, L, d_gdn), seq)] * 3 + [pl.BlockSpec((bb, L, LANES), seq)]
    if has_s0:
        args.append(s0)
        in_specs.append(st_spec)
    out_specs, out_shape = [], []
    if emit_o:
        out_specs.append(pl.BlockSpec((bb, L, d_gdn), seq))
        out_shape.append(jax.ShapeDtypeStruct((nb, L, d_gdn), f32))
    out_specs.append(st_spec)
    out_shape.append(st_shape)
    return pl.pallas_call(
        kern,
        grid=(nb // bb,),
        in_specs=in_specs,
        out_specs=out_specs,
        out_shape=out_shape,
        scratch_shapes=[pltpu.VMEM((bb, N_DIR, nh, GDN_HEAD, GDN_HEAD), f32),
                        pltpu.VMEM((nprob, C, N_DIR * C), f32),
                        pltpu.VMEM((C, N_DIR * C, nprob), f32),
                        pltpu.VMEM((C, N_DIR * C, nprob), f32),
                        pltpu.VMEM((nprob, C, N_DIR * C), bf16),
                        pltpu.VMEM((N_DIR, nprob, C, GDN_HEAD), bf16),
                        pltpu.VMEM((N_DIR, nunits, C, LANES), f32),
                        pltpu.VMEM((nunits, 4 * nh, C), f32)],
        compiler_params=_cparams(("parallel",)),
        name="gdn_scan" if emit_o else "gdn_scan_state",
    )(*args)


def _post_kernel(x_ref, gate_ref, yf_ref, yr_ref, zs_ref, og_ref, zg_ref,
                 wglu_ref, bglu_ref, nw_ref, wout_ref, lng_ref, lnb_ref, o_ref, *, tt, alpha):
    nb, _, d = x_ref.shape
    rows = nb * tt
    npk = yf_ref.shape[0]
    d_s5 = npk * LANES
    d_gdn = og_ref.shape[2]
    parts = []
    for j in range(npk):
        per_b = [(yf_ref[j, :, :, b, :] + yr_ref[j, :, :, b, :]).reshape(tt, LANES) for b in range(nb)]
        parts.append(jnp.concatenate(per_b, axis=0))
    y = jnp.concatenate(parts, axis=1)
    g = 0.5 * y * (1.0 + lax.erf(y * (2.0 ** -0.5)))
    glu = jnp.dot(g.astype(bf16), wglu_ref[...], preferred_element_type=f32) + bglu_ref[...]
    s5 = g * jax.nn.sigmoid(glu) * _silu(zs_ref[...].reshape(rows, d_s5))
    o = og_ref[...].reshape(rows, d_gdn)
    zg = zg_ref[...].reshape(rows, d_gdn)
    gd = []
    for h in range(d_gdn // GDN_HEAD):
        oh = o[:, h * GDN_HEAD:(h + 1) * GDN_HEAD]
        gd.append(oh * lax.rsqrt(jnp.mean(oh * oh, axis=-1, keepdims=True) + NORM_EPS) * nw_ref[...])
    gdn = jnp.concatenate(gd, axis=1) * _silu(zg)
    mix = jnp.concatenate([s5, gdn], axis=1).astype(bf16)
    yv = jnp.dot(mix, wout_ref[...], preferred_element_type=f32).reshape(nb, tt, d)
    r = alpha * x_ref[...] + gate_ref[...] * yv
    mu = jnp.mean(r, axis=-1, keepdims=True)
    var = jnp.mean(jnp.square(r - mu), axis=-1, keepdims=True)
    o_ref[...] = (r - mu) * lax.rsqrt(var + LN_EPS) * lng_ref[...] + lnb_ref[...]


def _post(x, gate, yf, yr, zs, o_gdn, zg, wglu, bglu, nw, wout, lng, lnb, alpha):
    nb, L, d = x.shape
    tt = 16
    npk = yf.shape[0]
    d_s5 = npk * LANES
    d_gdn = o_gdn.shape[2]
    kern = functools.partial(_post_kernel, tt=tt, alpha=alpha)
    tok = lambda i: (0, i, 0)
    c2 = lambda i: (0, 0)
    c3 = lambda i: (0, 0, 0)
    y_spec = pl.BlockSpec((npk, tt // S5_T, S5_T, nb, LANES), lambda i: (0, i, 0, 0, 0))
    return pl.pallas_call(
        kern,
        grid=(L // tt,),
        in_specs=[pl.BlockSpec((nb, tt, d), tok),
                  pl.BlockSpec((nb, 1, d), c3),
                  y_spec, y_spec,
                  pl.BlockSpec((nb, tt, d_s5), tok),
                  pl.BlockSpec((nb, tt, d_gdn), tok),
                  pl.BlockSpec((nb, tt, d_gdn), tok),
                  pl.BlockSpec(wglu.shape, c2),
                  pl.BlockSpec(bglu.shape, c2),
                  pl.BlockSpec(nw.shape, c2),
                  pl.BlockSpec(wout.shape, c2),
                  pl.BlockSpec(lng.shape, c3),
                  pl.BlockSpec(lnb.shape, c3)],
        out_specs=pl.BlockSpec((nb, tt, d), tok),
        out_shape=jax.ShapeDtypeStruct((nb, L, d), f32),
        compiler_params=_cparams(("parallel",)),
        name="post",
    )(x, gate, yf, yr, zs, o_gdn, zg, wglu, bglu, nw, wout, lng, lnb)


def _layer(x, c, ctx, c_ctx, p, alpha):
    nb, L, d = x.shape
    d_s5 = p['s5_d'].shape[0]
    d_gdn = p['conv_w'].shape[-1] // 3
    nh = d_gdn // GDN_HEAD
    npk = d_s5 // LANES

    pad = (-(nb + 1)) % 8
    cond = jnp.concatenate([c, c_ctx[None, :], jnp.zeros((pad, d), f32)], axis=0)
    m = _ada(cond, p['w_ada'], p['b_ada'])
    shift, scale, gate = m[:nb, :d], m[:nb, d:2 * d], m[:nb, 2 * d:]
    shift_c = jnp.broadcast_to(m[nb:nb + 1, :d], (nb, d))
    scale_c = jnp.broadcast_to(m[nb:nb + 1, d:2 * d], (nb, d))

    w_in = p['w_in']
    wu = w_in[:, :d_s5].astype(bf16)
    wm = w_in[:, d_s5:2 * d_s5 + 4 * d_gdn].astype(bf16)
    ng = 2 * N_DIR * nh
    wg = jnp.pad(w_in[:, 2 * d_s5 + 4 * d_gdn:], ((0, 0), (0, LANES - ng))).astype(bf16)
    gp = jnp.zeros((2, LANES), f32)
    gp = gp.at[0, ng // 2:ng].set(p['gdn_a_log'].reshape(-1).astype(f32))
    gp = gp.at[1, ng // 2:ng].set(p['gdn_dt_bias'].reshape(-1).astype(f32))

    u5, zs, qkv, zg, gates = _inproj(x, scale[:, None, :], shift[:, None, :], wu, wm, wg, gp, d_s5, d_gdn)
    u5c, _, qkvc, _, gatesc = _inproj(ctx, scale_c[:, None, :], shift_c[:, None, :], wu, wm, wg, gp, d_s5, d_gdn)

    sw = _s5_weights(p['s5_lambda_re'], p['s5_lambda_im'], p['s5_log_dt'],
                     p['s5_b_re'], p['s5_b_im'], p['s5_c_re'], p['s5_c_im'])
    sdim = sw['a_f'].shape[-1]
    a_f = jnp.broadcast_to(sw['a_f'][:, None, :], (npk, nb, sdim))
    a_r = jnp.broadcast_to(sw['a_r'][:, None, :], (npk, nb, sdim))
    h_zero = jnp.zeros((npk, nb, sdim), f32)
    d_skip = jnp.tile(p['s5_d'].astype(f32).reshape(npk, 1, LANES), (1, 1, S5_T))
    (hc_f,) = _s5_pass(u5c, sw['m_in_f'], a_f, h_zero, rev=False, emit_y=False)
    (hc_r,) = _s5_pass(u5c, sw['m_in_r'], a_r, h_zero, rev=True, emit_y=False)
    y_f, _ = _s5_pass(u5, sw['m_in_f'], a_f, hc_f, sw['m_out_f'], sw['m_intra'], d_skip, rev=False, emit_y=True)
    y_r, _ = _s5_pass(u5, sw['m_in_r'], a_r, hc_r, sw['m_out_r'], rev=True, emit_y=True)

    ql, kl, vl = _gdn_prep(qkv, p['conv_w'], True)
    qc, kc, vc = _gdn_prep(qkvc, p['conv_w'], False)
    (s_ctx,) = _gdn_scan(qc, kc, vc, gatesc, None, emit_o=False)
    o_gdn, _ = _gdn_scan(ql, kl, vl, gates, s_ctx, emit_o=True)

    return _post(x, gate[:, None, :], y_f, y_r, zs, o_gdn, zg,
                 p['w_glu'].astype(bf16), p['b_glu'].astype(f32).reshape(1, d_s5),
                 p['gdn_norm_w'].astype(f32).reshape(1, GDN_HEAD),
                 p['w_out'].astype(bf16), p['ln_g'].astype(f32).reshape(1, 1, d),
                 p['ln_b'].astype(f32).reshape(1, 1, d), alpha)


def kernel(x, c, ctx, c_ctx, w_ada, b_ada, w_in, s5_lambda_re, s5_lambda_im, s5_log_dt, s5_b_re, s5_b_im, s5_c_re, s5_c_im, s5_d, w_glu, b_glu, conv_w, gdn_a_log, gdn_dt_bias, gdn_norm_w, w_out, ln_g, ln_b):
    depth = w_ada.shape[0]
    assert depth == 1, "context-token outputs are only skipped for the last layer"
    alpha = (2.0 * depth) ** 0.25
    p = {
        'w_ada': w_ada[0], 'b_ada': b_ada[0], 'w_in': w_in[0],
        's5_lambda_re': s5_lambda_re[0], 's5_lambda_im': s5_lambda_im[0], 's5_log_dt': s5_log_dt[0],
        's5_b_re': s5_b_re[0], 's5_b_im': s5_b_im[0], 's5_c_re': s5_c_re[0], 's5_c_im': s5_c_im[0],
        's5_d': s5_d[0], 'w_glu': w_glu[0], 'b_glu': b_glu[0], 'conv_w': conv_w[0],
        'gdn_a_log': gdn_a_log[0], 'gdn_dt_bias': gdn_dt_bias[0], 'gdn_norm_w': gdn_norm_w[0],
        'w_out': w_out[0], 'ln_g': ln_g[0], 'ln_b': ln_b[0],
    }
    return _layer(x, c, ctx, c_ctx, p, alpha)
```

```python
import functools
import math

import numpy as np
import jax
import jax.numpy as jnp
from jax import lax
from jax.experimental import pallas as pl
from jax.experimental.pallas import tpu as pltpu

f32 = jnp.float32
bf16 = jnp.bfloat16
HI = lax.Precision.HIGHEST

LANES = 128
S5_GROUP = 16
S5_STATE = 64
S5_T = 8
PACK_G = LANES // S5_GROUP
GDN_HEAD = 128
GDN_CHUNK = 64
GRID_W = 64
N_DIR = 2
LN_EPS = 1e-5
NORM_EPS = 1e-6
VMEM_LIMIT = 56 * 1024 * 1024


def _cparams(sem):
    return pltpu.CompilerParams(dimension_semantics=sem, vmem_limit_bytes=VMEM_LIMIT)


def _silu(x):
    return x * jax.nn.sigmoid(x)


def _ada_kernel(c_ref, w_ref, b_ref, o_ref):
    o_ref[...] = jnp.dot(_silu(c_ref[...]), w_ref[...], precision=HI,
                         preferred_element_type=f32) + b_ref[...]


def _ada(cond, w_ada, b_ada):
    rows, d = cond.shape
    n = w_ada.shape[1]
    tn = 512
    return pl.pallas_call(
        _ada_kernel,
        grid=(n // tn,),
        in_specs=[pl.BlockSpec((rows, d), lambda i: (0, 0)),
                  pl.BlockSpec((d, tn), lambda i: (0, i)),
                  pl.BlockSpec((1, tn), lambda i: (0, i))],
        out_specs=pl.BlockSpec((rows, tn), lambda i: (0, i)),
        out_shape=jax.ShapeDtypeStruct((rows, n), f32),
        compiler_params=_cparams(("parallel",)),
        name="ada",
    )(cond, w_ada, b_ada.reshape(1, n))


def _inproj_kernel(x_ref, scale_ref, shift_ref, wu_ref, wm_ref, wg_ref, gp_ref,
                   u_ref, zs_ref, qkv_ref, zg_ref, gate_ref, *, tt, d_s5, d_gdn, n_beta):
    nb, _, d = x_ref.shape
    sc = 1.0 + scale_ref[...]
    sh = shift_ref[...]
    h = (x_ref[...] * sc + sh).reshape(nb * tt, d).astype(bf16)
    r = jnp.dot(h, wm_ref[...], preferred_element_type=f32)
    zs_ref[...] = r[:, :d_s5].reshape(nb, tt, d_s5)
    qkv_ref[...] = r[:, d_s5:d_s5 + 3 * d_gdn].reshape(nb, tt, 3 * d_gdn)
    zg_ref[...] = r[:, d_s5 + 3 * d_gdn:].reshape(nb, tt, d_gdn)
    lg = jnp.dot(h, wg_ref[...], preferred_element_type=f32)
    lane = lax.broadcasted_iota(jnp.int32, lg.shape, 1)
    a = lg + gp_ref[1:2, :]
    softplus = jnp.maximum(a, 0.0) + jnp.log1p(jnp.exp(-jnp.abs(a)))
    gates = jnp.where(lane < n_beta, jax.nn.sigmoid(lg), -jnp.exp(gp_ref[0:1, :]) * softplus)
    gate_ref[...] = gates.reshape(nb, tt, LANES)
    sc2 = sc[:, 0, :]
    sh2 = sh[:, 0, :]
    ht = jnp.concatenate([x_ref[:, t, :] * sc2 + sh2 for t in range(tt)], axis=0).astype(bf16)
    ru = jnp.dot(ht, wu_ref[...], preferred_element_type=f32)
    for j in range(d_s5 // LANES):
        u_ref[j] = ru[:, j * LANES:(j + 1) * LANES].reshape(tt // S5_T, S5_T, nb, LANES)


def _inproj(x, scale, shift, wu, wm, wg, gp, d_s5, d_gdn):
    nb, L, d = x.shape
    tt = 32
    npk = d_s5 // LANES
    kern = functools.partial(_inproj_kernel, tt=tt, d_s5=d_s5, d_gdn=d_gdn,
                             n_beta=N_DIR * (d_gdn // GDN_HEAD))
    const2 = lambda i: (0, 0)
    const3 = lambda i: (0, 0, 0)
    return pl.pallas_call(
        kern,
        grid=(L // tt,),
        in_specs=[pl.BlockSpec((nb, tt, d), lambda i: (0, i, 0)),
                  pl.BlockSpec((nb, 1, d), const3),
                  pl.BlockSpec((nb, 1, d), const3),
                  pl.BlockSpec(wu.shape, const2),
                  pl.BlockSpec(wm.shape, const2),
                  pl.BlockSpec(wg.shape, const2),
                  pl.BlockSpec(gp.shape, const2)],
        out_specs=[pl.BlockSpec((npk, tt // S5_T, S5_T, nb, LANES), lambda i: (0, i, 0, 0, 0)),
                   pl.BlockSpec((nb, tt, d_s5), lambda i: (0, i, 0)),
                   pl.BlockSpec((nb, tt, 3 * d_gdn), lambda i: (0, i, 0)),
                   pl.BlockSpec((nb, tt, d_gdn), lambda i: (0, i, 0)),
                   pl.BlockSpec((nb, tt, LANES), lambda i: (0, i, 0))],
        out_shape=[jax.ShapeDtypeStruct((npk, L // S5_T, S5_T, nb, LANES), f32),
                   jax.ShapeDtypeStruct((nb, L, d_s5), f32),
                   jax.ShapeDtypeStruct((nb, L, 3 * d_gdn), f32),
                   jax.ShapeDtypeStruct((nb, L, d_gdn), f32),
                   jax.ShapeDtypeStruct((nb, L, LANES), f32)],
        compiler_params=_cparams(("parallel",)),
        name="inproj",
    )(x, scale, shift, wu, wm, wg, gp)


def _s5_kernel(*refs, ct, rev, emit_y, intra):
    it = iter(refs)
    u_ref, cin_ref, at_ref, h0_ref = next(it), next(it), next(it), next(it)
    cout_ref = next(it) if emit_y else None
    cintra_ref = next(it) if intra else None
    d_ref = next(it) if intra else None
    y_ref = next(it) if emit_y else None
    hfin_ref = next(it)
    h_sc, v_sc, hall_sc, min_sc = next(it), next(it), next(it), next(it)
    mout_sc = next(it) if emit_y else None
    mintra_sc = next(it) if intra else None

    t = pl.program_id(1)
    nt = pl.num_programs(1)
    nb = u_ref.shape[3]
    sdim = h_sc.shape[1]
    half = sdim // 2
    gstate = half // PACK_G

    @pl.when(t == 0)
    def _():
        h_sc[...] = h0_ref[0]
        r_in = lax.broadcasted_iota(jnp.int32, (LANES, sdim), 0) // S5_GROUP
        c_in = (lax.broadcasted_iota(jnp.int32, (LANES, sdim), 1) % half) // gstate
        mask_in = (r_in == c_in).astype(f32)
        for s in range(S5_T):
            min_sc[s * LANES:(s + 1) * LANES, :] = (
                jnp.tile(cin_ref[0, s], (PACK_G, 1)) * mask_in).astype(bf16)
        if emit_y:
            r_out = lax.broadcasted_iota(jnp.int32, (half, LANES), 0) // gstate
            c_out = lax.broadcasted_iota(jnp.int32, (half, LANES), 1) // S5_GROUP
            mask_out = (r_out == c_out).astype(f32)
            for i in range(S5_T):
                for ri in range(2):
                    blk = cout_ref[0, i, ri * gstate:(ri + 1) * gstate, :]
                    mout_sc[ri * half:(ri + 1) * half, i * LANES:(i + 1) * LANES] = (
                        jnp.tile(blk, (PACK_G, 1)) * mask_out).astype(bf16)
        if intra:
            r_x = lax.broadcasted_iota(jnp.int32, (LANES, LANES), 0) // S5_GROUP
            c_x = lax.broadcasted_iota(jnp.int32, (LANES, LANES), 1) // S5_GROUP
            mask_x = (r_x == c_x).astype(f32)
            for s in range(S5_T):
                for i in range(S5_T):
                    mintra_sc[s * LANES:(s + 1) * LANES, i * LANES:(i + 1) * LANES] = (
                        jnp.tile(cintra_ref[0, s, i], (PACK_G, 1)) * mask_x).astype(bf16)

    x = jnp.concatenate([u_ref[0, :, i, :, :].reshape(ct * nb, LANES) for i in range(S5_T)], axis=1)
    xb = x.astype(bf16)
    v_sc[...] = jnp.dot(xb, min_sc[...], preferred_element_type=f32)
    are = at_ref[0, :, :half]
    aim = at_ref[0, :, half:]

    def body(s, carry):
        k = (ct - 1 - s) if rev else s
        off = pl.multiple_of(k * nb, nb)
        hre, him = carry
        hall_sc[pl.ds(off, nb), :half] = hre
        hall_sc[pl.ds(off, nb), half:] = him
        vre = v_sc[pl.ds(off, nb), :half]
        vim = v_sc[pl.ds(off, nb), half:]
        return are * hre - aim * him + vre, are * him + aim * hre + vim

    hre, him = lax.fori_loop(0, ct, body, (h_sc[:, :half], h_sc[:, half:]))
    h_sc[:, :half] = hre
    h_sc[:, half:] = him

    @pl.when(t == nt - 1)
    def _():
        hfin_ref[0] = h_sc[...]

    if emit_y:
        y = jnp.dot(hall_sc[...].astype(bf16), mout_sc[...], preferred_element_type=f32)
        if intra:
            y = y + jnp.dot(xb, mintra_sc[...], preferred_element_type=f32) + x * d_ref[0]
        for i in range(S5_T):
            y_ref[0, :, i, :, :] = y[:, i * LANES:(i + 1) * LANES].reshape(ct, nb, LANES)


def _s5_pass(u5, c_in, a_t, h0, c_out=None, c_intra=None, d_skip=None, *, rev, emit_y):
    npk, nc, _, nb, _ = u5.shape
    ct = min(32, nc)
    nt = nc // ct
    intra = c_intra is not None
    kdim = S5_T * LANES
    sdim = a_t.shape[-1]
    tmap = (lambda j, t: (j, nt - 1 - t, 0, 0, 0)) if rev else (lambda j, t: (j, t, 0, 0, 0))
    pmap = lambda j, t: (j, 0, 0)
    pmap4 = lambda j, t: (j, 0, 0, 0)
    pmap5 = lambda j, t: (j, 0, 0, 0, 0)
    args = [u5, c_in, a_t, h0]
    in_specs = [pl.BlockSpec((1, ct, S5_T, nb, LANES), tmap),
                pl.BlockSpec((1,) + c_in.shape[1:], pmap4),
                pl.BlockSpec((1, nb, sdim), pmap),
                pl.BlockSpec((1, nb, sdim), pmap)]
    scratch = [pltpu.VMEM((nb, sdim), f32),
               pltpu.VMEM((ct * nb, sdim), f32),
               pltpu.VMEM((ct * nb, sdim), f32),
               pltpu.VMEM((kdim, sdim), bf16)]
    if emit_y:
        args.append(c_out)
        in_specs.append(pl.BlockSpec((1,) + c_out.shape[1:], pmap4))
        scratch.append(pltpu.VMEM((sdim, kdim), bf16))
    if intra:
        args += [c_intra, d_skip]
        in_specs += [pl.BlockSpec((1,) + c_intra.shape[1:], pmap5), pl.BlockSpec((1, 1, kdim), pmap)]
        scratch.append(pltpu.VMEM((kdim, kdim), bf16))
    out_specs, out_shape = [], []
    if emit_y:
        out_specs.append(pl.BlockSpec((1, ct, S5_T, nb, LANES), tmap))
        out_shape.append(jax.ShapeDtypeStruct(u5.shape, f32))
    out_specs.append(pl.BlockSpec((1, nb, sdim), pmap))
    out_shape.append(jax.ShapeDtypeStruct((npk, nb, sdim), f32))
    kern = functools.partial(_s5_kernel, ct=ct, rev=rev, emit_y=emit_y, intra=intra)
    return pl.pallas_call(
        kern,
        grid=(npk, nt),
        in_specs=in_specs,
        out_specs=out_specs,
        out_shape=out_shape,
        scratch_shapes=scratch,
        compiler_params=_cparams(("parallel", "arbitrary")),
        name="s5_rev" if rev else "s5_fwd",
    )(*args)


def _s5_weights(lam_re, lam_im, log_dt, b_re, b_im, c_re, c_im):
    T = S5_T
    ndir, G, P = lam_re.shape
    npk = G // PACK_G
    dt = jnp.exp(log_dt.astype(f32))[..., None]
    lr, li = lam_re.astype(f32), lam_im.astype(f32)
    zr, zi = lr * dt, li * dt

    def apow(n):
        m = jnp.exp(n * zr)
        return m * jnp.cos(n * zi), m * jnp.sin(n * zi)

    nr = jnp.expm1(zr) * jnp.cos(zi) - 2.0 * jnp.square(jnp.sin(0.5 * zi))
    ni = jnp.exp(zr) * jnp.sin(zi)
    den = lr * lr + li * li
    qr = (nr * lr + ni * li) / den
    qi = (ni * lr - nr * li) / den
    bre, bim = b_re.astype(f32), b_im.astype(f32)
    bbr = qr[..., None] * bre - qi[..., None] * bim
    bbi = qr[..., None] * bim + qi[..., None] * bre
    cr, ci = c_re.astype(f32), c_im.astype(f32)

    pw = [apow(n) for n in range(T + 1)]

    def a_times_b(n, d):
        pr, pi = pw[n][0][d][..., None], pw[n][1][d][..., None]
        return pr * bbr[d] - pi * bbi[d], pr * bbi[d] + pi * bbr[d]

    def c_times_a(n, d):
        pr, pi = pw[n][0][d][:, None, :], pw[n][1][d][:, None, :]
        return cr[d] * pr - ci[d] * pi, cr[d] * pi + ci[d] * pr

    def pack_in(mats):
        re = jnp.stack([m[0] for m in mats])
        im = jnp.stack([m[1] for m in mats])
        base = jnp.stack([re, im])
        base = base.reshape(2, T, npk, PACK_G, P, S5_GROUP)
        base = base.transpose(2, 1, 5, 0, 3, 4)
        return base.reshape(npk, T, S5_GROUP, 2 * PACK_G * P)

    def pack_out(mats):
        re = jnp.stack([m[0] for m in mats])
        im = jnp.stack([-m[1] for m in mats])
        base = jnp.stack([re, im])
        base = base.reshape(2, T, npk, PACK_G, S5_GROUP, P)
        base = base.transpose(2, 1, 0, 5, 3, 4)
        return base.reshape(npk, T, 2 * P, LANES)

    m_in_f = pack_in([a_times_b(T - 1 - s, 0) for s in range(T)])
    m_in_r = pack_in([a_times_b(s, 1) for s in range(T)])
    m_out_f = pack_out([c_times_a(i + 1, 0) for i in range(T)])
    m_out_r = pack_out([c_times_a(T - i, 1) for i in range(T)])

    def lag_kernel(d):
        ks = []
        for tau in range(T):
            car, cai = c_times_a(tau, d)
            ks.append(jnp.einsum('gop,gpi->gio', car, bbr[d]) - jnp.einsum('gop,gpi->gio', cai, bbi[d]))
        return jnp.stack(ks)

    kf, kr = lag_kernel(0), lag_kernel(1)
    lag = np.arange(T)[None, :] - np.arange(T)[:, None]
    toe = (kf[np.maximum(lag, 0)] * jnp.asarray(lag >= 0, f32)[:, :, None, None, None]
           + kr[np.maximum(-lag, 0)] * jnp.asarray(lag <= 0, f32)[:, :, None, None, None])
    toe = toe.reshape(T, T, npk, PACK_G, S5_GROUP, S5_GROUP).transpose(2, 0, 1, 4, 3, 5)
    m_intra = toe.reshape(npk, T, T, S5_GROUP, LANES)

    def a_chunk(d):
        pr, pi = pw[T][0][d].reshape(npk, PACK_G * P), pw[T][1][d].reshape(npk, PACK_G * P)
        return jnp.concatenate([pr, pi], axis=-1)

    return dict(m_in_f=m_in_f, m_in_r=m_in_r, m_out_f=m_out_f, m_out_r=m_out_r,
                m_intra=m_intra, a_f=a_chunk(0), a_r=a_chunk(1))


def _gdn_prep_kernel(*refs, tile, rows_conv, d_gdn):
    if rows_conv:
        main_ref, up_ref, dn_ref, w_ref, q_ref, k_ref, v_ref = refs
    else:
        main_ref, w_ref, q_ref, k_ref, v_ref = refs
    w = w_ref[...]
    if rows_conv:
        i = pl.program_id(1)
        n = pl.num_programs(1)
        up = up_ref[0] * (i > 0).astype(f32)
        dn = dn_ref[0] * (i < n - 1).astype(f32)
        ext = jnp.concatenate([up, main_ref[0], dn], axis=0)
        base = GRID_W
        drs = (-1, 0, 1)
        col = lax.broadcasted_iota(jnp.int32, (tile, 1), 0) % GRID_W
        mask_m = col != 0
        mask_p = col != GRID_W - 1
    else:
        ext = main_ref[0]
        base = 0
        drs = (0,)
        col = lax.broadcasted_iota(jnp.int32, (tile, 1), 0)
        mask_m = col != 0
        mask_p = col != tile - 1
    n_ext = ext.shape[0]
    shifted = {-1: pltpu.roll(ext, 1, 0), 0: ext, 1: pltpu.roll(ext, n_ext - 1, 0)}
    acc = {}
    for dc in (-1, 0, 1):
        a = None
        for dr in drs:
            tap = w[(dr + 1) * 3 + (dc + 1)][None, :]
            term = shifted[dc][base + GRID_W * dr: base + GRID_W * dr + tile] * tap
            a = term if a is None else a + term
        acc[dc] = a
    conv = acc[0] + jnp.where(mask_m, acc[-1], 0.0) + jnp.where(mask_p, acc[1], 0.0)
    s = _silu(conv)
    nh = d_gdn // GDN_HEAD
    for h in range(nh):
        qh = s[:, h * GDN_HEAD:(h + 1) * GDN_HEAD]
        kh = s[:, d_gdn + h * GDN_HEAD: d_gdn + (h + 1) * GDN_HEAD]
        qn = qh * lax.rsqrt(jnp.sum(qh * qh, axis=-1, keepdims=True) + NORM_EPS) * (GDN_HEAD ** -0.5)
        kn = kh * lax.rsqrt(jnp.sum(kh * kh, axis=-1, keepdims=True) + NORM_EPS)
        q_ref[0, :, h * GDN_HEAD:(h + 1) * GDN_HEAD] = qn.astype(bf16)
        k_ref[0, :, h * GDN_HEAD:(h + 1) * GDN_HEAD] = kn.astype(bf16)
    v_ref[0] = s[:, 2 * d_gdn:].astype(bf16)


def _gdn_prep(qkv, conv_w, rows_conv):
    nb, L, c3 = qkv.shape
    d_gdn = c3 // 3
    w9 = conv_w.reshape(9, c3).astype(f32)
    if rows_conv:
        tile = min(512, L)
        nt = L // tile
        r = tile // GRID_W
        nrows = L // GRID_W
        in_specs = [pl.BlockSpec((1, tile, c3), lambda b, i: (b, i, 0)),
                    pl.BlockSpec((1, GRID_W, c3), lambda b, i: (b, jnp.maximum(i * r - 1, 0), 0)),
                    pl.BlockSpec((1, GRID_W, c3), lambda b, i: (b, jnp.minimum((i + 1) * r, nrows - 1), 0)),
                    pl.BlockSpec((9, c3), lambda b, i: (0, 0))]
        args = (qkv, qkv, qkv, w9)
    else:
        tile, nt = L, 1
        in_specs = [pl.BlockSpec((1, tile, c3), lambda b, i: (b, i, 0)),
                    pl.BlockSpec((9, c3), lambda b, i: (0, 0))]
        args = (qkv, w9)
    kern = functools.partial(_gdn_prep_kernel, tile=tile, rows_conv=rows_conv, d_gdn=d_gdn)
    o_spec = pl.BlockSpec((1, tile, d_gdn), lambda b, i: (b, i, 0))
    o_shape = jax.ShapeDtypeStruct((nb, L, d_gdn), bf16)
    return pl.pallas_call(
        kern,
        grid=(nb, nt),
        in_specs=in_specs,
        out_specs=[o_spec, o_spec, o_spec],
        out_shape=[o_shape, o_shape, o_shape],
        compiler_params=_cparams(("parallel", "parallel")),
        name="gdn_prep",
    )(*args)


def _chunk_cumsum(g, rev):
    n = g.shape[0]
    r = lax.broadcasted_iota(jnp.int32, g.shape, 0)
    s = 1
    while s < n:
        if rev:
            g = g + jnp.where(r < n - s, pltpu.roll(g, n - s, 0), 0.0)
        else:
            g = g + jnp.where(r >= s, pltpu.roll(g, s, 0), 0.0)
        s *= 2
    return g


def _tri_solve_rows(apl_sc, tl_sc, dirn, upper):
    nblk = GDN_CHUNK // 8
    base = dirn * GDN_CHUNK
    sub = lax.broadcasted_iota(jnp.int32, (8, LANES), 0)
    for step_blk in range(nblk):
        blk = (nblk - 1 - step_blk) if upper else step_blk
        mblocks = range(blk, nblk) if upper else range(0, blk + 1)

        def solve_row(s, carry, blk=blk, mblocks=mblocks):
            ii = (7 - s) if upper else s
            i = blk * 8 + ii
            acc = {}
            for mb in mblocks:
                arow = apl_sc[i, base + mb * 8:base + (mb + 1) * 8, :]
                jblocks = range(mb, nblk) if upper else range(0, mb + 1)
                for mm in range(8):
                    ab = jnp.broadcast_to(arow[mm:mm + 1, :], (8, LANES))
                    for jb in jblocks:
                        term = ab * tl_sc[mb * 8 + mm, base + jb * 8:base + (jb + 1) * 8, :]
                        key = (jb, mm % 2)
                        acc[key] = term if key not in acc else acc[key] + term
            for jb in (range(blk, nblk) if upper else range(0, blk + 1)):
                tot = acc[(jb, 0)] + acc[(jb, 1)]
                unit_row = jnp.where(sub == ii, 1.0, 0.0) if jb == blk else 0.0
                tl_sc[i, base + jb * 8:base + (jb + 1) * 8, :] = unit_row - tot
            return carry

        lax.fori_loop(0, 8, solve_row, 0)


def _gdn_scan_kernel(*refs, nchunks, nh, emit_o, has_s0):
    it = iter(refs)
    q_ref, k_ref, v_ref, g_ref = next(it), next(it), next(it), next(it)
    s0_ref = next(it) if has_s0 else None
    o_ref = next(it) if emit_o else None
    sfin_ref = next(it)
    s_sc, at_sc, apl_sc, tl_sc, qkd_sc, w_sc, gcol_sc, grow_sc = (next(it) for _ in range(8))
    C = GDN_CHUNK
    bb = q_ref.shape[0]
    nunits = bb * nchunks
    s_sc[...] = s0_ref[...] if has_s0 else jnp.zeros(s_sc.shape, f32)
    if emit_o:
        o_ref[...] = jnp.zeros(o_ref.shape, f32)
    row = lax.broadcasted_iota(jnp.int32, (C, C), 0)
    col = lax.broadcasted_iota(jnp.int32, (C, C), 1)
    tri = (row >= col, row <= col)
    strict = (row > col, row < col)
    nbeta = N_DIR * nh

    def unit_of(unit):
        bl = unit // nchunks
        return bl, pl.multiple_of((unit % nchunks) * C, C)

    def phase_a(unit, carry):
        bl, off = unit_of(unit)
        gts = g_ref[bl, pl.ds(off, C), :]
        gcum = (_chunk_cumsum(gts, False), _chunk_cumsum(gts, True))
        gcum_t = (gcum[0].T, gcum[1].T)
        gcol_sc[0, unit] = gcum[0]
        gcol_sc[1, unit] = gcum[1]
        grow_sc[unit] = jnp.concatenate(
            [gts.T[:nbeta], gcum_t[0][nbeta:nbeta + nh], gcum_t[1][nbeta + nh:nbeta + 2 * nh]], axis=0)
        for h in range(nh):
            hs = slice(h * GDN_HEAD, (h + 1) * GDN_HEAD)
            kb = k_ref[bl, pl.ds(off, C), hs]
            if emit_o:
                kq = jnp.concatenate([kb, q_ref[bl, pl.ds(off, C), hs]], axis=0)
            else:
                kq = kb
            kkqk = lax.dot_general(kq, kb, (((1,), (1,)), ((), ())), preferred_element_type=f32)
            a_mats, qkds = [], []
            for dirn in range(N_DIR):
                cb = dirn * nh + h
                cg = nbeta + dirn * nh + h
                decay = jnp.exp(jnp.where(tri[dirn], gcum[dirn][:, cg:cg + 1] - gcum_t[dirn][cg:cg + 1, :],
                                          -1e30))
                a_mats.append(jnp.where(strict[dirn], gts[:, cb:cb + 1] * kkqk[:C] * decay, 0.0))
                if emit_o:
                    qkds.append(kkqk[C:] * decay)
            at_sc[unit * nh + h] = jnp.concatenate(a_mats, axis=1)
            if emit_o:
                qkd_sc[unit * nh + h] = jnp.concatenate(qkds, axis=1).astype(bf16)
        return carry

    lax.fori_loop(0, nunits, phase_a, 0, unroll=2)

    def swap_in(ib, carry):
        rows = pl.ds(pl.multiple_of(ib * 8, 8), 8)
        tl_sc[rows] = jnp.swapaxes(at_sc[:, rows, :], 0, 1)
        return carry

    def to_lanes(i, carry):
        apl_sc[i] = tl_sc[i].T
        return carry

    lax.fori_loop(0, C // 8, swap_in, 0)
    lax.fori_loop(0, C, to_lanes, 0, unroll=4)
    tl_sc[...] = jnp.zeros(tl_sc.shape, f32)
    _tri_solve_rows(apl_sc, tl_sc, 0, False)
    _tri_solve_rows(apl_sc, tl_sc, 1, True)

    def from_lanes(i, carry):
        apl_sc[i] = tl_sc[i].T
        return carry

    def swap_out(ib, carry):
        rows = pl.ds(pl.multiple_of(ib * 8, 8), 8)
        at_sc[:, rows, :] = jnp.swapaxes(apl_sc[rows], 0, 1)
        return carry

    lax.fori_loop(0, C, from_lanes, 0, unroll=4)
    lax.fori_loop(0, C // 8, swap_out, 0)

    u0_sc = (apl_sc, tl_sc)

    def u0_slot(p):
        return p // 2, pl.ds(pl.multiple_of((p % 2) * C, C), C)

    def phase_b2(unit, carry):
        bl, off = unit_of(unit)
        g_rows = grow_sc[unit]
        for h in range(nh):
            hs = slice(h * GDN_HEAD, (h + 1) * GDN_HEAD)
            p = unit * nh + h
            t_both = at_sc[p]
            kb = k_ref[bl, pl.ds(off, C), hs]
            vb = v_ref[bl, pl.ds(off, C), hs]
            for dirn in range(N_DIR):
                cb = dirn * nh + h
                cg = nbeta + dirn * nh + h
                t_inv = t_both[:, dirn * C:(dirn + 1) * C]
                b_row = g_rows[cb:cb + 1, :]
                bg_row = b_row * jnp.exp(g_rows[cg:cg + 1, :])
                u0 = jnp.dot((t_inv * b_row).astype(bf16), vb, preferred_element_type=f32)
                w = jnp.dot((t_inv * bg_row).astype(bf16), kb, preferred_element_type=f32)
                slab, rows = u0_slot(p)
                u0_sc[dirn][slab, rows, :] = u0
                w_sc[dirn, p] = w.astype(bf16)
        return carry

    lax.fori_loop(0, nunits, phase_b2, 0, unroll=2)

    def phase_c(step, carry):
        chains = []
        for dirn, unit in ((0, step), (1, nunits - 1 - step)):
            bl, off = unit_of(unit)
            g_col_all = gcol_sc[dirn, unit]
            g_tot_all = g_col_all[0:1, :] if dirn else g_col_all[C - 1:C, :]
            for h in range(nh):
                cg = nbeta + dirn * nh + h
                hs = slice(h * GDN_HEAD, (h + 1) * GDN_HEAD)
                p = unit * nh + h
                g_col = g_col_all[:, cg:cg + 1]
                g_tot = g_tot_all[:, cg:cg + 1]
                slab, rows = u0_slot(p)
                lhs = w_sc[dirn, p]
                ch = dict(idx=(bl, dirn, h), off=off, hs=hs, u0=u0_sc[dirn][slab, rows, :],
                          kb=k_ref[bl, pl.ds(off, C), hs],
                          e_col=jnp.exp(g_tot - g_col), g_last=jnp.exp(g_tot))
                if emit_o:
                    lhs = jnp.concatenate([lhs, q_ref[bl, pl.ds(off, C), hs]], axis=0)
                    ch['qkd'] = qkd_sc[p][:, dirn * C:(dirn + 1) * C]
                    ch['gam'] = jnp.exp(g_col)
                ch['lhs'] = lhs
                chains.append(ch)
        s_old = [s_sc[ch['idx']] for ch in chains]
        r = [jnp.dot(ch['lhs'], s.astype(bf16), preferred_element_type=f32) for ch, s in zip(chains, s_old)]
        u = [ch['u0'] - ri[:C] for ch, ri in zip(chains, r)]
        s_new = [ch['g_last'] * s + lax.dot_general(ch['kb'], (ch['e_col'] * ui).astype(bf16),
                                                    (((0,), (0,)), ((), ())), preferred_element_type=f32)
                 for ch, s, ui in zip(chains, s_old, u)]
        if emit_o:
            o_new = [ch['gam'] * ri[C:] + jnp.dot(ch['qkd'], ui.astype(bf16), preferred_element_type=f32)
                     for ch, ri, ui in zip(chains, r, u)]
        for n, ch in enumerate(chains):
            s_sc[ch['idx']] = s_new[n]
            if emit_o:
                bl = ch['idx'][0]
                o_ref[bl, pl.ds(ch['off'], C), ch['hs']] += o_new[n]
        return carry

    lax.fori_loop(0, nunits, phase_c, 0)
    sfin_ref[...] = s_sc[...]


def _gdn_scan(q, k, v, gates, s0, *, emit_o):
    nb, L, d_gdn = q.shape
    nh = d_gdn // GDN_HEAD
    nchunks = L // GDN_CHUNK
    C = GDN_CHUNK
    bb = max(1, min(nb, LANES // (nchunks * nh)))
    nunits = bb * nchunks
    nprob = nunits * nh
    assert nprob == LANES, (nb, L, nh)
    has_s0 = s0 is not None
    kern = functools.partial(_gdn_scan_kernel, nchunks=nchunks, nh=nh, emit_o=emit_o, has_s0=has_s0)
    seq = lambda b: (b, 0, 0)
    st = lambda b: (b, 0, 0, 0, 0)
    st_spec = pl.BlockSpec((bb, N_DIR, nh, GDN_HEAD, GDN_HEAD), st)
    st_shape = jax.ShapeDtypeStruct((nb, N_DIR, nh, GDN_HEAD, GDN_HEAD), f32)
    args = [q, k, v, gates]
    in_specs = [pl.BlockSpec((bb, L, d_gdn), seq)] * 3 + [pl.BlockSpec((bb, L, LANES), seq)]
    if has_s0:
        args.append(s0)
        in_specs.append(st_spec)
    out_specs, out_shape = [], []
    if emit_o:
        out_specs.append(pl.BlockSpec((bb, L, d_gdn), seq))
        out_shape.append(jax.ShapeDtypeStruct((nb, L, d_gdn), f32))
    out_specs.append(st_spec)
    out_shape.append(st_shape)
    return pl.pallas_call(
        kern,
        grid=(nb // bb,),
        in_specs=in_specs,
        out_specs=out_specs,
        out_shape=out_shape,
        scratch_shapes=[pltpu.VMEM((bb, N_DIR, nh, GDN_HEAD, GDN_HEAD), f32),
                        pltpu.VMEM((nprob, C, N_DIR * C), f32),
                        pltpu.VMEM((C, N_DIR * C, nprob), f32),
                        pltpu.VMEM((C, N_DIR * C, nprob), f32),
                        pltpu.VMEM((nprob, C, N_DIR * C), bf16),
                        pltpu.VMEM((N_DIR, nprob, C, GDN_HEAD), bf16),
                        pltpu.VMEM((N_DIR, nunits, C, LANES), f32),
                        pltpu.VMEM((nunits, 4 * nh, C), f32)],
        compiler_params=_cparams(("parallel",)),
        name="gdn_scan" if emit_o else "gdn_scan_state",
    )(*args)


def _post_kernel(x_ref, gate_ref, yf_ref, yr_ref, zs_ref, og_ref, zg_ref,
                 wglu_ref, bglu_ref, nw_ref, wout_ref, lng_ref, lnb_ref, o_ref, *, tt, alpha):
    nb, _, d = x_ref.shape
    rows = nb * tt
    npk = yf_ref.shape[0]
    d_s5 = npk * LANES
    d_gdn = og_ref.shape[2]
    parts = []
    for j in range(npk):
        per_b = [(yf_ref[j, :, :, b, :] + yr_ref[j, :, :, b, :]).reshape(tt, LANES) for b in range(nb)]
        parts.append(jnp.concatenate(per_b, axis=0))
    y = jnp.concatenate(parts, axis=1)
    g = 0.5 * y * (1.0 + lax.erf(y * (2.0 ** -0.5)))
    glu = jnp.dot(g.astype(bf16), wglu_ref[...], preferred_element_type=f32) + bglu_ref[...]
    s5 = g * jax.nn.sigmoid(glu) * _silu(zs_ref[...].reshape(rows, d_s5))
    o = og_ref[...].reshape(rows, d_gdn)
    zg = zg_ref[...].reshape(rows, d_gdn)
    gd = []
    for h in range(d_gdn // GDN_HEAD):
        oh = o[:, h * GDN_HEAD:(h + 1) * GDN_HEAD]
        gd.append(oh * lax.rsqrt(jnp.mean(oh * oh, axis=-1, keepdims=True) + NORM_EPS) * nw_ref[...])
    gdn = jnp.concatenate(gd, axis=1) * _silu(zg)
    mix = jnp.concatenate([s5, gdn], axis=1).astype(bf16)
    yv = jnp.dot(mix, wout_ref[...], preferred_element_type=f32).reshape(nb, tt, d)
    r = alpha * x_ref[...] + gate_ref[...] * yv
    mu = jnp.mean(r, axis=-1, keepdims=True)
    var = jnp.mean(jnp.square(r - mu), axis=-1, keepdims=True)
    o_ref[...] = (r - mu) * lax.rsqrt(var + LN_EPS) * lng_ref[...] + lnb_ref[...]


def _post(x, gate, yf, yr, zs, o_gdn, zg, wglu, bglu, nw, wout, lng, lnb, alpha):
    nb, L, d = x.shape
    tt = 16
    npk = yf.shape[0]
    d_s5 = npk * LANES
    d_gdn = o_gdn.shape[2]
    kern = functools.partial(_post_kernel, tt=tt, alpha=alpha)
    tok = lambda i: (0, i, 0)
    c2 = lambda i: (0, 0)
    c3 = lambda i: (0, 0, 0)
    y_spec = pl.BlockSpec((npk, tt // S5_T, S5_T, nb, LANES), lambda i: (0, i, 0, 0, 0))
    return pl.pallas_call(
        kern,
        grid=(L // tt,),
        in_specs=[pl.BlockSpec((nb, tt, d), tok),
                  pl.BlockSpec((nb, 1, d), c3),
                  y_spec, y_spec,
                  pl.BlockSpec((nb, tt, d_s5), tok),
                  pl.BlockSpec((nb, tt, d_gdn), tok),
                  pl.BlockSpec((nb, tt, d_gdn), tok),
                  pl.BlockSpec(wglu.shape, c2),
                  pl.BlockSpec(bglu.shape, c2),
                  pl.BlockSpec(nw.shape, c2),
                  pl.BlockSpec(wout.shape, c2),
                  pl.BlockSpec(lng.shape, c3),
                  pl.BlockSpec(lnb.shape, c3)],
        out_specs=pl.BlockSpec((nb, tt, d), tok),
        out_shape=jax.ShapeDtypeStruct((nb, L, d), f32),
        compiler_params=_cparams(("parallel",)),
        name="post",
    )(x, gate, yf, yr, zs, o_gdn, zg, wglu, bglu, nw, wout, lng, lnb)


def _layer(x, c, ctx, c_ctx, p, alpha):
    nb, L, d = x.shape
    d_s5 = p['s5_d'].shape[0]
    d_gdn = p['conv_w'].shape[-1] // 3
    nh = d_gdn // GDN_HEAD
    npk = d_s5 // LANES

    pad = (-(nb + 1)) % 8
    cond = jnp.concatenate([c, c_ctx[None, :], jnp.zeros((pad, d), f32)], axis=0)
    m = _ada(cond, p['w_ada'], p['b_ada'])
    shift, scale, gate = m[:nb, :d], m[:nb, d:2 * d], m[:nb, 2 * d:]
    shift_c = jnp.broadcast_to(m[nb:nb + 1, :d], (nb, d))
    scale_c = jnp.broadcast_to(m[nb:nb + 1, d:2 * d], (nb, d))

    w_in = p['w_in']
    wu = w_in[:, :d_s5].astype(bf16)
    wm = w_in[:, d_s5:2 * d_s5 + 4 * d_gdn].astype(bf16)
    ng = 2 * N_DIR * nh
    wg = jnp.pad(w_in[:, 2 * d_s5 + 4 * d_gdn:], ((0, 0), (0, LANES - ng))).astype(bf16)
    gp = jnp.zeros((2, LANES), f32)
    gp = gp.at[0, ng // 2:ng].set(p['gdn_a_log'].reshape(-1).astype(f32))
    gp = gp.at[1, ng // 2:ng].set(p['gdn_dt_bias'].reshape(-1).astype(f32))

    u5, zs, qkv, zg, gates = _inproj(x, scale[:, None, :], shift[:, None, :], wu, wm, wg, gp, d_s5, d_gdn)
    u5c, _, qkvc, _, gatesc = _inproj(ctx, scale_c[:, None, :], shift_c[:, None, :], wu, wm, wg, gp, d_s5, d_gdn)

    sw = _s5_weights(p['s5_lambda_re'], p['s5_lambda_im'], p['s5_log_dt'],
                     p['s5_b_re'], p['s5_b_im'], p['s5_c_re'], p['s5_c_im'])
    sdim = sw['a_f'].shape[-1]
    a_f = jnp.broadcast_to(sw['a_f'][:, None, :], (npk, nb, sdim))
    a_r = jnp.broadcast_to(sw['a_r'][:, None, :], (npk, nb, sdim))
    h_zero = jnp.zeros((npk, nb, sdim), f32)
    d_skip = jnp.tile(p['s5_d'].astype(f32).reshape(npk, 1, LANES), (1, 1, S5_T))
    (hc_f,) = _s5_pass(u5c, sw['m_in_f'], a_f, h_zero, rev=False, emit_y=False)
    (hc_r,) = _s5_pass(u5c, sw['m_in_r'], a_r, h_zero, rev=True, emit_y=False)
    y_f, _ = _s5_pass(u5, sw['m_in_f'], a_f, hc_f, sw['m_out_f'], sw['m_intra'], d_skip, rev=False, emit_y=True)
    y_r, _ = _s5_pass(u5, sw['m_in_r'], a_r, hc_r, sw['m_out_r'], rev=True, emit_y=True)

    ql, kl, vl = _gdn_prep(qkv, p['conv_w'], True)
    qc, kc, vc = _gdn_prep(qkvc, p['conv_w'], False)
    (s_ctx,) = _gdn_scan(qc, kc, vc, gatesc, None, emit_o=False)
    o_gdn, _ = _gdn_scan(ql, kl, vl, gates, s_ctx, emit_o=True)

    return _post(x, gate[:, None, :], y_f, y_r, zs, o_gdn, zg,
                 p['w_glu'].astype(bf16), p['b_glu'].astype(f32).reshape(1, d_s5),
                 p['gdn_norm_w'].astype(f32).reshape(1, GDN_HEAD),
                 p['w_out'].astype(bf16), p['ln_g'].astype(f32).reshape(1, 1, d),
                 p['ln_b'].astype(f32).reshape(1, 1, d), alpha)


def kernel(x, c, ctx, c_ctx, w_ada, b_ada, w_in, s5_lambda_re, s5_lambda_im, s5_log_dt, s5_b_re, s5_b_im, s5_c_re, s5_c_im, s5_d, w_glu, b_glu, conv_w, gdn_a_log, gdn_dt_bias, gdn_norm_w, w_out, ln_g, ln_b):
    depth = w_ada.shape[0]
    assert depth == 1, "context-token outputs are only skipped for the last layer"
    alpha = (2.0 * depth) ** 0.25
    p = {
        'w_ada': w_ada[0], 'b_ada': b_ada[0], 'w_in': w_in[0],
        's5_lambda_re': s5_lambda_re[0], 's5_lambda_im': s5_lambda_im[0], 's5_log_dt': s5_log_dt[0],
        's5_b_re': s5_b_re[0], 's5_b_im': s5_b_im[0], 's5_c_re': s5_c_re[0], 's5_c_im': s5_c_im[0],
        's5_d': s5_d[0], 'w_glu': w_glu[0], 'b_glu': b_glu[0], 'conv_w': conv_w[0],
        'gdn_a_log': gdn_a_log[0], 'gdn_dt_bias': gdn_dt_bias[0], 'gdn_norm_w': gdn_norm_w[0],
        'w_out': w_out[0], 'ln_g': ln_g[0], 'ln_b': ln_b[0],
    }
    return _layer(x, c, ctx, c_ctx, p, alpha)
```

```python
import functools
import math

import numpy as np
import jax
import jax.numpy as jnp
from jax import lax
from jax.experimental import pallas as pl
from jax.experimental.pallas import tpu as pltpu

f32 = jnp.float32
bf16 = jnp.bfloat16
HI = lax.Precision.HIGHEST

LANES = 128
S5_GROUP = 16
S5_STATE = 64
S5_T = 8
PACK_G = LANES // S5_GROUP
GDN_HEAD = 128
GDN_CHUNK = 64
GRID_W = 64
N_DIR = 2
LN_EPS = 1e-5
NORM_EPS = 1e-6
VMEM_LIMIT = 56 * 1024 * 1024


def _cparams(sem):
    return pltpu.CompilerParams(dimension_semantics=sem, vmem_limit_bytes=VMEM_LIMIT)


def _silu(x):
    return x * jax.nn.sigmoid(x)


def _ada_kernel(c_ref, w_ref, b_ref, o_ref):
    o_ref[...] = jnp.dot(_silu(c_ref[...]), w_ref[...], precision=HI,
                         preferred_element_type=f32) + b_ref[...]


def _ada(cond, w_ada, b_ada):
    rows, d = cond.shape
    n = w_ada.shape[1]
    tn = 512
    return pl.pallas_call(
        _ada_kernel,
        grid=(n // tn,),
        in_specs=[pl.BlockSpec((rows, d), lambda i: (0, 0)),
                  pl.BlockSpec((d, tn), lambda i: (0, i)),
                  pl.BlockSpec((1, tn), lambda i: (0, i))],
        out_specs=pl.BlockSpec((rows, tn), lambda i: (0, i)),
        out_shape=jax.ShapeDtypeStruct((rows, n), f32),
        compiler_params=_cparams(("parallel",)),
        name="ada",
    )(cond, w_ada, b_ada.reshape(1, n))


def _inproj_kernel(x_ref, scale_ref, shift_ref, wu_ref, wm_ref, wg_ref, gp_ref,
                   u_ref, zs_ref, qkv_ref, zg_ref, gate_ref, *, tt, d_s5, d_gdn, n_beta):
    nb, _, d = x_ref.shape
    sc = 1.0 + scale_ref[...]
    sh = shift_ref[...]
    h = (x_ref[...] * sc + sh).reshape(nb * tt, d).astype(bf16)
    r = jnp.dot(h, wm_ref[...], preferred_element_type=f32)
    zs_ref[...] = r[:, :d_s5].reshape(nb, tt, d_s5)
    qkv_ref[...] = r[:, d_s5:d_s5 + 3 * d_gdn].reshape(nb, tt, 3 * d_gdn)
    zg_ref[...] = r[:, d_s5 + 3 * d_gdn:].reshape(nb, tt, d_gdn)
    lg = jnp.dot(h, wg_ref[...], preferred_element_type=f32)
    lane = lax.broadcasted_iota(jnp.int32, lg.shape, 1)
    a = lg + gp_ref[1:2, :]
    softplus = jnp.maximum(a, 0.0) + jnp.log1p(jnp.exp(-jnp.abs(a)))
    gates = jnp.where(lane < n_beta, jax.nn.sigmoid(lg), -jnp.exp(gp_ref[0:1, :]) * softplus)
    gate_ref[...] = gates.reshape(nb, tt, LANES)
    sc2 = sc[:, 0, :]
    sh2 = sh[:, 0, :]
    ht = jnp.concatenate([x_ref[:, t, :] * sc2 + sh2 for t in range(tt)], axis=0).astype(bf16)
    ru = jnp.dot(ht, wu_ref[...], preferred_element_type=f32)
    for j in range(d_s5 // LANES):
        u_ref[j] = ru[:, j * LANES:(j + 1) * LANES].reshape(tt // S5_T, S5_T, nb, LANES)


def _inproj(x, scale, shift, wu, wm, wg, gp, d_s5, d_gdn):
    nb, L, d = x.shape
    tt = 32
    npk = d_s5 // LANES
    kern = functools.partial(_inproj_kernel, tt=tt, d_s5=d_s5, d_gdn=d_gdn,
                             n_beta=N_DIR * (d_gdn // GDN_HEAD))
    const2 = lambda i: (0, 0)
    const3 = lambda i: (0, 0, 0)
    return pl.pallas_call(
        kern,
        grid=(L // tt,),
        in_specs=[pl.BlockSpec((nb, tt, d), lambda i: (0, i, 0)),
                  pl.BlockSpec((nb, 1, d), const3),
                  pl.BlockSpec((nb, 1, d), const3),
                  pl.BlockSpec(wu.shape, const2),
                  pl.BlockSpec(wm.shape, const2),
                  pl.BlockSpec(wg.shape, const2),
                  pl.BlockSpec(gp.shape, const2)],
        out_specs=[pl.BlockSpec((npk, tt // S5_T, S5_T, nb, LANES), lambda i: (0, i, 0, 0, 0)),
                   pl.BlockSpec((nb, tt, d_s5), lambda i: (0, i, 0)),
                   pl.BlockSpec((nb, tt, 3 * d_gdn), lambda i: (0, i, 0)),
                   pl.BlockSpec((nb, tt, d_gdn), lambda i: (0, i, 0)),
                   pl.BlockSpec((nb, tt, LANES), lambda i: (0, i, 0))],
        out_shape=[jax.ShapeDtypeStruct((npk, L // S5_T, S5_T, nb, LANES), f32),
                   jax.ShapeDtypeStruct((nb, L, d_s5), f32),
                   jax.ShapeDtypeStruct((nb, L, 3 * d_gdn), f32),
                   jax.ShapeDtypeStruct((nb, L, d_gdn), f32),
                   jax.ShapeDtypeStruct((nb, L, LANES), f32)],
        compiler_params=_cparams(("parallel",)),
        name="inproj",
    )(x, scale, shift, wu, wm, wg, gp)


def _s5_kernel(*refs, ct, rev, emit_y, intra):
    it = iter(refs)
    u_ref, cin_ref, at_ref, h0_ref = next(it), next(it), next(it), next(it)
    cout_ref = next(it) if emit_y else None
    cintra_ref = next(it) if intra else None
    d_ref = next(it) if intra else None
    y_ref = next(it) if emit_y else None
    hfin_ref = next(it)
    h_sc, v_sc, hall_sc, min_sc = next(it), next(it), next(it), next(it)
    mout_sc = next(it) if emit_y else None
    mintra_sc = next(it) if intra else None

    t = pl.program_id(1)
    nt = pl.num_programs(1)
    nb = u_ref.shape[3]
    sdim = h_sc.shape[1]
    half = sdim // 2
    gstate = half // PACK_G

    @pl.when(t == 0)
    def _():
        h_sc[...] = h0_ref[0]
        r_in = lax.broadcasted_iota(jnp.int32, (LANES, sdim), 0) // S5_GROUP
        c_in = (lax.broadcasted_iota(jnp.int32, (LANES, sdim), 1) % half) // gstate
        mask_in = (r_in == c_in).astype(f32)
        for s in range(S5_T):
            min_sc[s * LANES:(s + 1) * LANES, :] = (
                jnp.tile(cin_ref[0, s], (PACK_G, 1)) * mask_in).astype(bf16)
        if emit_y:
            r_out = lax.broadcasted_iota(jnp.int32, (half, LANES), 0) // gstate
            c_out = lax.broadcasted_iota(jnp.int32, (half, LANES), 1) // S5_GROUP
            mask_out = (r_out == c_out).astype(f32)
            for i in range(S5_T):
                for ri in range(2):
                    blk = cout_ref[0, i, ri * gstate:(ri + 1) * gstate, :]
                    mout_sc[ri * half:(ri + 1) * half, i * LANES:(i + 1) * LANES] = (
                        jnp.tile(blk, (PACK_G, 1)) * mask_out).astype(bf16)
        if intra:
            r_x = lax.broadcasted_iota(jnp.int32, (LANES, LANES), 0) // S5_GROUP
            c_x = lax.broadcasted_iota(jnp.int32, (LANES, LANES), 1) // S5_GROUP
            mask_x = (r_x == c_x).astype(f32)
            for s in range(S5_T):
                for i in range(S5_T):
                    mintra_sc[s * LANES:(s + 1) * LANES, i * LANES:(i + 1) * LANES] = (
                        jnp.tile(cintra_ref[0, s, i], (PACK_G, 1)) * mask_x).astype(bf16)

    x = jnp.concatenate([u_ref[0, :, i, :, :].reshape(ct * nb, LANES) for i in range(S5_T)], axis=1)
    xb = x.astype(bf16)
    v_sc[...] = jnp.dot(xb, min_sc[...], preferred_element_type=f32)
    are = at_ref[0, :, :half]
    aim = at_ref[0, :, half:]

    def body(s, carry):
        k = (ct - 1 - s) if rev else s
        off = pl.multiple_of(k * nb, nb)
        hre, him = carry
        hall_sc[pl.ds(off, nb), :half] = hre
        hall_sc[pl.ds(off, nb), half:] = him
        vre = v_sc[pl.ds(off, nb), :half]
        vim = v_sc[pl.ds(off, nb), half:]
        return are * hre - aim * him + vre, are * him + aim * hre + vim

    hre, him = lax.fori_loop(0, ct, body, (h_sc[:, :half], h_sc[:, half:]))
    h_sc[:, :half] = hre
    h_sc[:, half:] = him

    @pl.when(t == nt - 1)
    def _():
        hfin_ref[0] = h_sc[...]

    if emit_y:
        y = jnp.dot(hall_sc[...].astype(bf16), mout_sc[...], preferred_element_type=f32)
        if intra:
            y = y + jnp.dot(xb, mintra_sc[...], preferred_element_type=f32) + x * d_ref[0]
        for i in range(S5_T):
            y_ref[0, :, i, :, :] = y[:, i * LANES:(i + 1) * LANES].reshape(ct, nb, LANES)


def _s5_pass(u5, c_in, a_t, h0, c_out=None, c_intra=None, d_skip=None, *, rev, emit_y):
    npk, nc, _, nb, _ = u5.shape
    ct = min(32, nc)
    nt = nc // ct
    intra = c_intra is not None
    kdim = S5_T * LANES
    sdim = a_t.shape[-1]
    tmap = (lambda j, t: (j, nt - 1 - t, 0, 0, 0)) if rev else (lambda j, t: (j, t, 0, 0, 0))
    pmap = lambda j, t: (j, 0, 0)
    pmap4 = lambda j, t: (j, 0, 0, 0)
    pmap5 = lambda j, t: (j, 0, 0, 0, 0)
    args = [u5, c_in, a_t, h0]
    in_specs = [pl.BlockSpec((1, ct, S5_T, nb, LANES), tmap),
                pl.BlockSpec((1,) + c_in.shape[1:], pmap4),
                pl.BlockSpec((1, nb, sdim), pmap),
                pl.BlockSpec((1, nb, sdim), pmap)]
    scratch = [pltpu.VMEM((nb, sdim), f32),
               pltpu.VMEM((ct * nb, sdim), f32),
               pltpu.VMEM((ct * nb, sdim), f32),
               pltpu.VMEM((kdim, sdim), bf16)]
    if emit_y:
        args.append(c_out)
        in_specs.append(pl.BlockSpec((1,) + c_out.shape[1:], pmap4))
        scratch.append(pltpu.VMEM((sdim, kdim), bf16))
    if intra:
        args += [c_intra, d_skip]
        in_specs += [pl.BlockSpec((1,) + c_intra.shape[1:], pmap5), pl.BlockSpec((1, 1, kdim), pmap)]
        scratch.append(pltpu.VMEM((kdim, kdim), bf16))
    out_specs, out_shape = [], []
    if emit_y:
        out_specs.append(pl.BlockSpec((1, ct, S5_T, nb, LANES), tmap))
        out_shape.append(jax.ShapeDtypeStruct(u5.shape, f32))
    out_specs.append(pl.BlockSpec((1, nb, sdim), pmap))
    out_shape.append(jax.ShapeDtypeStruct((npk, nb, sdim), f32))
    kern = functools.partial(_s5_kernel, ct=ct, rev=rev, emit_y=emit_y, intra=intra)
    return pl.pallas_call(
        kern,
        grid=(npk, nt),
        in_specs=in_specs,
        out_specs=out_specs,
        out_shape=out_shape,
        scratch_shapes=scratch,
        compiler_params=_cparams(("parallel", "arbitrary")),
        name="s5_rev" if rev else "s5_fwd",
    )(*args)


def _s5_weights(lam_re, lam_im, log_dt, b_re, b_im, c_re, c_im):
    T = S5_T
    ndir, G, P = lam_re.shape
    npk = G // PACK_G
    dt = jnp.exp(log_dt.astype(f32))[..., None]
    lr, li = lam_re.astype(f32), lam_im.astype(f32)
    zr, zi = lr * dt, li * dt

    def apow(n):
        m = jnp.exp(n * zr)
        return m * jnp.cos(n * zi), m * jnp.sin(n * zi)

    nr = jnp.expm1(zr) * jnp.cos(zi) - 2.0 * jnp.square(jnp.sin(0.5 * zi))
    ni = jnp.exp(zr) * jnp.sin(zi)
    den = lr * lr + li * li
    qr = (nr * lr + ni * li) / den
    qi = (ni * lr - nr * li) / den
    bre, bim = b_re.astype(f32), b_im.astype(f32)
    bbr = qr[..., None] * bre - qi[..., None] * bim
    bbi = qr[..., None] * bim + qi[..., None] * bre
    cr, ci = c_re.astype(f32), c_im.astype(f32)

    pw = [apow(n) for n in range(T + 1)]

    def a_times_b(n, d):
        pr, pi = pw[n][0][d][..., None], pw[n][1][d][..., None]
        return pr * bbr[d] - pi * bbi[d], pr * bbi[d] + pi * bbr[d]

    def c_times_a(n, d):
        pr, pi = pw[n][0][d][:, None, :], pw[n][1][d][:, None, :]
        return cr[d] * pr - ci[d] * pi, cr[d] * pi + ci[d] * pr

    def pack_in(mats):
        re = jnp.stack([m[0] for m in mats])
        im = jnp.stack([m[1] for m in mats])
        base = jnp.stack([re, im])
        base = base.reshape(2, T, npk, PACK_G, P, S5_GROUP)
        base = base.transpose(2, 1, 5, 0, 3, 4)
        return base.reshape(npk, T, S5_GROUP, 2 * PACK_G * P)

    def pack_out(mats):
        re = jnp.stack([m[0] for m in mats])
        im = jnp.stack([-m[1] for m in mats])
        base = jnp.stack([re, im])
        base = base.reshape(2, T, npk, PACK_G, S5_GROUP, P)
        base = base.transpose(2, 1, 0, 5, 3, 4)
        return base.reshape(npk, T, 2 * P, LANES)

    m_in_f = pack_in([a_times_b(T - 1 - s, 0) for s in range(T)])
    m_in_r = pack_in([a_times_b(s, 1) for s in range(T)])
    m_out_f = pack_out([c_times_a(i + 1, 0) for i in range(T)])
    m_out_r = pack_out([c_times_a(T - i, 1) for i in range(T)])

    def lag_kernel(d):
        ks = []
        for tau in range(T):
            car, cai = c_times_a(tau, d)
            ks.append(jnp.einsum('gop,gpi->gio', car, bbr[d]) - jnp.einsum('gop,gpi->gio', cai, bbi[d]))
        return jnp.stack(ks)

    kf, kr = lag_kernel(0), lag_kernel(1)
    lag = np.arange(T)[None, :] - np.arange(T)[:, None]
    toe = (kf[np.maximum(lag, 0)] * jnp.asarray(lag >= 0, f32)[:, :, None, None, None]
           + kr[np.maximum(-lag, 0)] * jnp.asarray(lag <= 0, f32)[:, :, None, None, None])
    toe = toe.reshape(T, T, npk, PACK_G, S5_GROUP, S5_GROUP).transpose(2, 0, 1, 4, 3, 5)
    m_intra = toe.reshape(npk, T, T, S5_GROUP, LANES)

    def a_chunk(d):
        pr, pi = pw[T][0][d].reshape(npk, PACK_G * P), pw[T][1][d].reshape(npk, PACK_G * P)
        return jnp.concatenate([pr, pi], axis=-1)

    return dict(m_in_f=m_in_f, m_in_r=m_in_r, m_out_f=m_out_f, m_out_r=m_out_r,
                m_intra=m_intra, a_f=a_chunk(0), a_r=a_chunk(1))


def _gdn_prep_kernel(*refs, tile, rows_conv, d_gdn):
    if rows_conv:
        main_ref, up_ref, dn_ref, w_ref, q_ref, k_ref, v_ref = refs
    else:
        main_ref, w_ref, q_ref, k_ref, v_ref = refs
    w = w_ref[...]
    if rows_conv:
        i = pl.program_id(1)
        n = pl.num_programs(1)
        up = up_ref[0] * (i > 0).astype(f32)
        dn = dn_ref[0] * (i < n - 1).astype(f32)
        ext = jnp.concatenate([up, main_ref[0], dn], axis=0)
        base = GRID_W
        drs = (-1, 0, 1)
        col = lax.broadcasted_iota(jnp.int32, (tile, 1), 0) % GRID_W
        mask_m = col != 0
        mask_p = col != GRID_W - 1
    else:
        ext = main_ref[0]
        base = 0
        drs = (0,)
        col = lax.broadcasted_iota(jnp.int32, (tile, 1), 0)
        mask_m = col != 0
        mask_p = col != tile - 1
    n_ext = ext.shape[0]
    shifted = {-1: pltpu.roll(ext, 1, 0), 0: ext, 1: pltpu.roll(ext, n_ext - 1, 0)}
    acc = {}
    for dc in (-1, 0, 1):
        a = None
        for dr in drs:
            tap = w[(dr + 1) * 3 + (dc + 1)][None, :]
            term = shifted[dc][base + GRID_W * dr: base + GRID_W * dr + tile] * tap
            a = term if a is None else a + term
        acc[dc] = a
    conv = acc[0] + jnp.where(mask_m, acc[-1], 0.0) + jnp.where(mask_p, acc[1], 0.0)
    s = _silu(conv)
    nh = d_gdn // GDN_HEAD
    for h in range(nh):
        qh = s[:, h * GDN_HEAD:(h + 1) * GDN_HEAD]
        kh = s[:, d_gdn + h * GDN_HEAD: d_gdn + (h + 1) * GDN_HEAD]
        qn = qh * lax.rsqrt(jnp.sum(qh * qh, axis=-1, keepdims=True) + NORM_EPS) * (GDN_HEAD ** -0.5)
        kn = kh * lax.rsqrt(jnp.sum(kh * kh, axis=-1, keepdims=True) + NORM_EPS)
        q_ref[0, :, h * GDN_HEAD:(h + 1) * GDN_HEAD] = qn.astype(bf16)
        k_ref[0, :, h * GDN_HEAD:(h + 1) * GDN_HEAD] = kn.astype(bf16)
    v_ref[0] = s[:, 2 * d_gdn:].astype(bf16)


def _gdn_prep(qkv, conv_w, rows_conv):
    nb, L, c3 = qkv.shape
    d_gdn = c3 // 3
    w9 = conv_w.reshape(9, c3).astype(f32)
    if rows_conv:
        tile = min(512, L)
        nt = L // tile
        r = tile // GRID_W
        nrows = L // GRID_W
        in_specs = [pl.BlockSpec((1, tile, c3), lambda b, i: (b, i, 0)),
                    pl.BlockSpec((1, GRID_W, c3), lambda b, i: (b, jnp.maximum(i * r - 1, 0), 0)),
                    pl.BlockSpec((1, GRID_W, c3), lambda b, i: (b, jnp.minimum((i + 1) * r, nrows - 1), 0)),
                    pl.BlockSpec((9, c3), lambda b, i: (0, 0))]
        args = (qkv, qkv, qkv, w9)
    else:
        tile, nt = L, 1
        in_specs = [pl.BlockSpec((1, tile, c3), lambda b, i: (b, i, 0)),
                    pl.BlockSpec((9, c3), lambda b, i: (0, 0))]
        args = (qkv, w9)
    kern = functools.partial(_gdn_prep_kernel, tile=tile, rows_conv=rows_conv, d_gdn=d_gdn)
    o_spec = pl.BlockSpec((1, tile, d_gdn), lambda b, i: (b, i, 0))
    o_shape = jax.ShapeDtypeStruct((nb, L, d_gdn), bf16)
    return pl.pallas_call(
        kern,
        grid=(nb, nt),
        in_specs=in_specs,
        out_specs=[o_spec, o_spec, o_spec],
        out_shape=[o_shape, o_shape, o_shape],
        compiler_params=_cparams(("parallel", "parallel")),
        name="gdn_prep",
    )(*args)


def _chunk_cumsum(g, rev):
    n = g.shape[0]
    r = lax.broadcasted_iota(jnp.int32, g.shape, 0)
    s = 1
    while s < n:
        if rev:
            g = g + jnp.where(r < n - s, pltpu.roll(g, n - s, 0), 0.0)
        else:
            g = g + jnp.where(r >= s, pltpu.roll(g, s, 0), 0.0)
        s *= 2
    return g


def _tri_solve_rows(apl_sc, tl_sc, dirn, upper):
    nblk = GDN_CHUNK // 8
    base = dirn * GDN_CHUNK
    sub = lax.broadcasted_iota(jnp.int32, (8, LANES), 0)
    for step_blk in range(nblk):
        blk = (nblk - 1 - step_blk) if upper else step_blk
        mblocks = range(blk, nblk) if upper else range(0, blk + 1)

        def solve_row(s, carry, blk=blk, mblocks=mblocks):
            ii = (7 - s) if upper else s
            i = blk * 8 + ii
            acc = {}
            for mb in mblocks:
                arow = apl_sc[i, base + mb * 8:base + (mb + 1) * 8, :]
                jblocks = range(mb, nblk) if upper else range(0, mb + 1)
                for mm in range(8):
                    ab = jnp.broadcast_to(arow[mm:mm + 1, :], (8, LANES))
                    for jb in jblocks:
                        term = ab * tl_sc[mb * 8 + mm, base + jb * 8:base + (jb + 1) * 8, :]
                        key = (jb, mm % 2)
                        acc[key] = term if key not in acc else acc[key] + term
            for jb in (range(blk, nblk) if upper else range(0, blk + 1)):
                tot = acc[(jb, 0)] + acc[(jb, 1)]
                unit_row = jnp.where(sub == ii, 1.0, 0.0) if jb == blk else 0.0
                tl_sc[i, base + jb * 8:base + (jb + 1) * 8, :] = unit_row - tot
            return carry

        lax.fori_loop(0, 8, solve_row, 0)


def _gdn_scan_kernel(*refs, nchunks, nh, emit_o, has_s0):
    it = iter(refs)
    q_ref, k_ref, v_ref, g_ref = next(it), next(it), next(it), next(it)
    s0_ref = next(it) if has_s0 else None
    o_ref = next(it) if emit_o else None
    sfin_ref = next(it)
    s_sc, at_sc, apl_sc, tl_sc, mk_sc, w_sc, gcol_sc, grow_sc = (next(it) for _ in range(8))
    C = GDN_CHUNK
    bb = q_ref.shape[0]
    nunits = bb * nchunks
    s_sc[...] = s0_ref[...] if has_s0 else jnp.zeros(s_sc.shape, f32)
    if emit_o:
        o_ref[...] = jnp.zeros(o_ref.shape, f32)
    row = lax.broadcasted_iota(jnp.int32, (C, N_DIR * C), 0)
    lane = lax.broadcasted_iota(jnp.int32, (C, N_DIR * C), 1)
    fwd_half = lane < C
    col = jnp.where(fwd_half, lane, lane - C)
    tri = (fwd_half & (row >= col)) | (~fwd_half & (row <= col))
    strict = (fwd_half & (row > col)) | (~fwd_half & (row < col))
    nbeta = N_DIR * nh
    gate_lane = lax.broadcasted_iota(jnp.int32, (C, LANES), 1)

    def unit_of(unit):
        bl = unit // nchunks
        return bl, pl.multiple_of((unit % nchunks) * C, C)

    def phase_a(unit, carry):
        bl, off = unit_of(unit)
        gts = g_ref[bl, pl.ds(off, C), :]
        gmix = jnp.where(gate_lane < nbeta, gts,
                         jnp.where(gate_lane < nbeta + nh, _chunk_cumsum(gts, False), _chunk_cumsum(gts, True)))
        g_rows = gmix.T[:2 * nbeta]
        gcol_sc[unit] = gmix
        grow_sc[unit] = g_rows
        for h in range(nh):
            hs = slice(h * GDN_HEAD, (h + 1) * GDN_HEAD)
            cbf, cbr = h, nh + h
            cgf, cgr = nbeta + h, nbeta + nh + h
            kb = k_ref[bl, pl.ds(off, C), hs]
            k2 = jnp.concatenate([kb, kb], axis=0)
            if emit_o:
                kq = jnp.concatenate([kb, q_ref[bl, pl.ds(off, C), hs]], axis=0)
            else:
                kq = kb
            kkqk = lax.dot_general(kq, k2, (((1,), (1,)), ((), ())), preferred_element_type=f32)
            g_col = jnp.where(fwd_half, gmix[:, cgf:cgf + 1], gmix[:, cgr:cgr + 1])
            g_row = jnp.concatenate([g_rows[cgf:cgf + 1, :], g_rows[cgr:cgr + 1, :]], axis=1)
            b_col = jnp.where(fwd_half, gmix[:, cbf:cbf + 1], gmix[:, cbr:cbr + 1])
            decay = jnp.exp(jnp.where(tri, g_col - g_row, -1e30))
            at_sc[unit * nh + h] = jnp.where(strict, b_col * kkqk[:C] * decay, 0.0)
            if emit_o:
                mk_sc[unit * nh + h, :C, :] = (kkqk[C:] * decay).astype(bf16)
        return carry

    lax.fori_loop(0, nunits, phase_a, 0, unroll=2)

    def swap_in(ib, carry):
        rows = pl.ds(pl.multiple_of(ib * 8, 8), 8)
        tl_sc[rows] = jnp.swapaxes(at_sc[:, rows, :], 0, 1)
        return carry

    def to_lanes(i, carry):
        apl_sc[i] = tl_sc[i].T
        return carry

    lax.fori_loop(0, C // 8, swap_in, 0)
    lax.fori_loop(0, C, to_lanes, 0, unroll=4)
    tl_sc[...] = jnp.zeros(tl_sc.shape, f32)
    _tri_solve_rows(apl_sc, tl_sc, 0, False)
    _tri_solve_rows(apl_sc, tl_sc, 1, True)

    def from_lanes(i, carry):
        apl_sc[i] = tl_sc[i].T
        return carry

    def swap_out(ib, carry):
        rows = pl.ds(pl.multiple_of(ib * 8, 8), 8)
        at_sc[:, rows, :] = jnp.swapaxes(apl_sc[rows], 0, 1)
        return carry

    lax.fori_loop(0, C, from_lanes, 0, unroll=4)
    lax.fori_loop(0, C // 8, swap_out, 0)

    u0_sc = (apl_sc, tl_sc)

    def u0_slot(p):
        return p // 2, pl.ds(pl.multiple_of((p % 2) * C, C), C)

    eye_bf = (lax.broadcasted_iota(jnp.int32, (GDN_HEAD, GDN_HEAD), 0)
              == lax.broadcasted_iota(jnp.int32, (GDN_HEAD, GDN_HEAD), 1)).astype(bf16)

    def phase_b2(unit, carry):
        bl, off = unit_of(unit)
        g_rows = grow_sc[unit]
        g_cols = gcol_sc[unit]
        for h in range(nh):
            hs = slice(h * GDN_HEAD, (h + 1) * GDN_HEAD)
            p = unit * nh + h
            t_both = at_sc[p]
            kb = k_ref[bl, pl.ds(off, C), hs]
            vb = v_ref[bl, pl.ds(off, C), hs]
            kf = kb.astype(f32)
            k_outs = []
            for dirn in range(N_DIR):
                cb = dirn * nh + h
                cg = nbeta + dirn * nh + h
                t_inv = t_both[:, dirn * C:(dirn + 1) * C]
                b_row = g_rows[cb:cb + 1, :]
                bg_row = b_row * jnp.exp(g_rows[cg:cg + 1, :])
                u0 = jnp.dot((t_inv * b_row).astype(bf16), vb, preferred_element_type=f32)
                w = jnp.dot((t_inv * bg_row).astype(bf16), kb, preferred_element_type=f32)
                slab, rows = u0_slot(p)
                u0_sc[dirn][slab, rows, :] = u0
                w_sc[dirn, p] = w.astype(bf16)
                g_last = g_cols[0:1, cg:cg + 1] if dirn else g_cols[C - 1:C, cg:cg + 1]
                k_outs.append(kf * jnp.exp(g_last - g_cols[:, cg:cg + 1]))
            k_out_both = jnp.concatenate(k_outs, axis=0).astype(bf16)
            mk_sc[p, C:, :] = lax.dot_general(eye_bf, k_out_both, (((1,), (1,)), ((), ())),
                                              preferred_element_type=f32).astype(bf16)
        return carry

    lax.fori_loop(0, nunits, phase_b2, 0, unroll=2)

    def phase_c(step, carry):
        chains = []
        for dirn, unit in ((0, step), (1, nunits - 1 - step)):
            bl, off = unit_of(unit)
            g_col_all = gcol_sc[unit]
            g_tot_all = g_col_all[0:1, :] if dirn else g_col_all[C - 1:C, :]
            for h in range(nh):
                cg = nbeta + dirn * nh + h
                hs = slice(h * GDN_HEAD, (h + 1) * GDN_HEAD)
                p = unit * nh + h
                slab, rows = u0_slot(p)
                lhs = w_sc[dirn, p]
                ch = dict(idx=(bl, dirn, h), off=off, hs=hs, dirn=dirn, u0=u0_sc[dirn][slab, rows, :],
                          g_last=jnp.exp(g_tot_all[:, cg:cg + 1]))
                if emit_o:
                    lhs = jnp.concatenate([lhs, q_ref[bl, pl.ds(off, C), hs]], axis=0)
                    ch['mk'] = mk_sc[p]
                    ch['gam'] = jnp.exp(g_col_all[:, cg:cg + 1])
                else:
                    ch['mk'] = mk_sc[p, C:, :]
                ch['lhs'] = lhs
                chains.append(ch)
        s_old = [s_sc[ch['idx']] for ch in chains]
        r = [jnp.dot(ch['lhs'], s.astype(bf16), preferred_element_type=f32) for ch, s in zip(chains, s_old)]
        zeros = jnp.zeros((C, GDN_HEAD), bf16)
        ub = [(ch['u0'] - ri[:C]).astype(bf16) for ch, ri in zip(chains, r)]
        m = [jnp.dot(ch['mk'], jnp.concatenate([zeros, ubi] if ch['dirn'] else [ubi, zeros], axis=0),
                     preferred_element_type=f32) for ch, ubi in zip(chains, ub)]
        for n, ch in enumerate(chains):
            if emit_o:
                bl = ch['idx'][0]
                o_ref[bl, pl.ds(ch['off'], C), ch['hs']] += ch['gam'] * r[n][C:] + m[n][:C]
                s_sc[ch['idx']] = ch['g_last'] * s_old[n] + m[n][C:]
            else:
                s_sc[ch['idx']] = ch['g_last'] * s_old[n] + m[n]
        return carry

    lax.fori_loop(0, nunits, phase_c, 0)
    sfin_ref[...] = s_sc[...]


def _gdn_scan(q, k, v, gates, s0, *, emit_o):
    nb, L, d_gdn = q.shape
    nh = d_gdn // GDN_HEAD
    nchunks = L // GDN_CHUNK
    C = GDN_CHUNK
    bb = max(1, min(nb, LANES // (nchunks * nh)))
    nunits = bb * nchunks
    nprob = nunits * nh
    assert nprob == LANES, (nb, L, nh)
    has_s0 = s0 is not None
    kern = functools.partial(_gdn_scan_kernel, nchunks=nchunks, nh=nh, emit_o=emit_o, has_s0=has_s0)
    seq = lambda b: (b, 0, 0)
    st = lambda b: (b, 0, 0, 0, 0)
    st_spec = pl.BlockSpec((bb, N_DIR, nh, GDN_HEAD, GDN_HEAD), st)
    st_shape = jax.ShapeDtypeStruct((nb, N_DIR, nh, GDN_HEAD, GDN_HEAD), f32)
    args = [q, k, v, gates]
    in_specs = [pl.BlockSpec((bb, L, d_gdn), seq)] * 3 + [pl.BlockSpec((bb, L, LANES), seq)]
    if has_s0:
        args.append(s0)
        in_specs.append(st_spec)
    out_specs, out_shape = [], []
    if emit_o:
        out_specs.append(pl.BlockSpec((bb, L, d_gdn), seq))
        out_shape.append(jax.ShapeDtypeStruct((nb, L, d_gdn), f32))
    out_specs.append(st_spec)
    out_shape.append(st_shape)
    return pl.pallas_call(
        kern,
        grid=(nb // bb,),
        in_specs=in_specs,
        out_specs=out_specs,
        out_shape=out_shape,
        scratch_shapes=[pltpu.VMEM((bb, N_DIR, nh, GDN_HEAD, GDN_HEAD), f32),
                        pltpu.VMEM((nprob, C, N_DIR * C), f32),
                        pltpu.VMEM((C, N_DIR * C, nprob), f32),
                        pltpu.VMEM((C, N_DIR * C, nprob), f32),
                        pltpu.VMEM((nprob, C + GDN_HEAD, N_DIR * C), bf16),
                        pltpu.VMEM((N_DIR, nprob, C, GDN_HEAD), bf16),
                        pltpu.VMEM((nunits, C, LANES), f32),
                        pltpu.VMEM((nunits, 4 * nh, C), f32)],
        compiler_params=_cparams(("parallel",)),
        name="gdn_scan" if emit_o else "gdn_scan_state",
    )(*args)


def _post_kernel(x_ref, gate_ref, yf_ref, yr_ref, zs_ref, og_ref, zg_ref,
                 wglu_ref, bglu_ref, nw_ref, wout_ref, lng_ref, lnb_ref, o_ref, *, tt, alpha):
    nb, _, d = x_ref.shape
    rows = nb * tt
    npk = yf_ref.shape[0]
    d_s5 = npk * LANES
    d_gdn = og_ref.shape[2]
    parts = []
    for j in range(npk):
        per_b = [(yf_ref[j, :, :, b, :] + yr_ref[j, :, :, b, :]).reshape(tt, LANES) for b in range(nb)]
        parts.append(jnp.concatenate(per_b, axis=0))
    y = jnp.concatenate(parts, axis=1)
    g = 0.5 * y * (1.0 + lax.erf(y * (2.0 ** -0.5)))
    glu = jnp.dot(g.astype(bf16), wglu_ref[...], preferred_element_type=f32) + bglu_ref[...]
    s5 = g * jax.nn.sigmoid(glu) * _silu(zs_ref[...].reshape(rows, d_s5))
    o = og_ref[...].reshape(rows, d_gdn)
    zg = zg_ref[...].reshape(rows, d_gdn)
    gd = []
    for h in range(d_gdn // GDN_HEAD):
        oh = o[:, h * GDN_HEAD:(h + 1) * GDN_HEAD]
        gd.append(oh * lax.rsqrt(jnp.mean(oh * oh, axis=-1, keepdims=True) + NORM_EPS) * nw_ref[...])
    gdn = jnp.concatenate(gd, axis=1) * _silu(zg)
    mix = jnp.concatenate([s5, gdn], axis=1).astype(bf16)
    yv = jnp.dot(mix, wout_ref[...], preferred_element_type=f32).reshape(nb, tt, d)
    r = alpha * x_ref[...] + gate_ref[...] * yv
    mu = jnp.mean(r, axis=-1, keepdims=True)
    var = jnp.mean(jnp.square(r - mu), axis=-1, keepdims=True)
    o_ref[...] = (r - mu) * lax.rsqrt(var + LN_EPS) * lng_ref[...] + lnb_ref[...]


def _post(x, gate, yf, yr, zs, o_gdn, zg, wglu, bglu, nw, wout, lng, lnb, alpha):
    nb, L, d = x.shape
    tt = 16
    npk = yf.shape[0]
    d_s5 = npk * LANES
    d_gdn = o_gdn.shape[2]
    kern = functools.partial(_post_kernel, tt=tt, alpha=alpha)
    tok = lambda i: (0, i, 0)
    c2 = lambda i: (0, 0)
    c3 = lambda i: (0, 0, 0)
    y_spec = pl.BlockSpec((npk, tt // S5_T, S5_T, nb, LANES), lambda i: (0, i, 0, 0, 0))
    return pl.pallas_call(
        kern,
        grid=(L // tt,),
        in_specs=[pl.BlockSpec((nb, tt, d), tok),
                  pl.BlockSpec((nb, 1, d), c3),
                  y_spec, y_spec,
                  pl.BlockSpec((nb, tt, d_s5), tok),
                  pl.BlockSpec((nb, tt, d_gdn), tok),
                  pl.BlockSpec((nb, tt, d_gdn), tok),
                  pl.BlockSpec(wglu.shape, c2),
                  pl.BlockSpec(bglu.shape, c2),
                  pl.BlockSpec(nw.shape, c2),
                  pl.BlockSpec(wout.shape, c2),
                  pl.BlockSpec(lng.shape, c3),
                  pl.BlockSpec(lnb.shape, c3)],
        out_specs=pl.BlockSpec((nb, tt, d), tok),
        out_shape=jax.ShapeDtypeStruct((nb, L, d), f32),
        compiler_params=_cparams(("parallel",)),
        name="post",
    )(x, gate, yf, yr, zs, o_gdn, zg, wglu, bglu, nw, wout, lng, lnb)


def _layer(x, c, ctx, c_ctx, p, alpha):
    nb, L, d = x.shape
    d_s5 = p['s5_d'].shape[0]
    d_gdn = p['conv_w'].shape[-1] // 3
    nh = d_gdn // GDN_HEAD
    npk = d_s5 // LANES

    pad = (-(nb + 1)) % 8
    cond = jnp.concatenate([c, c_ctx[None, :], jnp.zeros((pad, d), f32)], axis=0)
    m = _ada(cond, p['w_ada'], p['b_ada'])
    shift, scale, gate = m[:nb, :d], m[:nb, d:2 * d], m[:nb, 2 * d:]
    shift_c = jnp.broadcast_to(m[nb:nb + 1, :d], (nb, d))
    scale_c = jnp.broadcast_to(m[nb:nb + 1, d:2 * d], (nb, d))

    w_in = p['w_in']
    wu = w_in[:, :d_s5].astype(bf16)
    wm = w_in[:, d_s5:2 * d_s5 + 4 * d_gdn].astype(bf16)
    ng = 2 * N_DIR * nh
    wg = jnp.pad(w_in[:, 2 * d_s5 + 4 * d_gdn:], ((0, 0), (0, LANES - ng))).astype(bf16)
    gp = jnp.zeros((2, LANES), f32)
    gp = gp.at[0, ng // 2:ng].set(p['gdn_a_log'].reshape(-1).astype(f32))
    gp = gp.at[1, ng // 2:ng].set(p['gdn_dt_bias'].reshape(-1).astype(f32))

    u5, zs, qkv, zg, gates = _inproj(x, scale[:, None, :], shift[:, None, :], wu, wm, wg, gp, d_s5, d_gdn)
    u5c, _, qkvc, _, gatesc = _inproj(ctx, scale_c[:, None, :], shift_c[:, None, :], wu, wm, wg, gp, d_s5, d_gdn)

    sw = _s5_weights(p['s5_lambda_re'], p['s5_lambda_im'], p['s5_log_dt'],
                     p['s5_b_re'], p['s5_b_im'], p['s5_c_re'], p['s5_c_im'])
    sdim = sw['a_f'].shape[-1]
    a_f = jnp.broadcast_to(sw['a_f'][:, None, :], (npk, nb, sdim))
    a_r = jnp.broadcast_to(sw['a_r'][:, None, :], (npk, nb, sdim))
    h_zero = jnp.zeros((npk, nb, sdim), f32)
    d_skip = jnp.tile(p['s5_d'].astype(f32).reshape(npk, 1, LANES), (1, 1, S5_T))
    (hc_f,) = _s5_pass(u5c, sw['m_in_f'], a_f, h_zero, rev=False, emit_y=False)
    (hc_r,) = _s5_pass(u5c, sw['m_in_r'], a_r, h_zero, rev=True, emit_y=False)
    y_f, _ = _s5_pass(u5, sw['m_in_f'], a_f, hc_f, sw['m_out_f'], sw['m_intra'], d_skip, rev=False, emit_y=True)
    y_r, _ = _s5_pass(u5, sw['m_in_r'], a_r, hc_r, sw['m_out_r'], rev=True, emit_y=True)

    ql, kl, vl = _gdn_prep(qkv, p['conv_w'], True)
    qc, kc, vc = _gdn_prep(qkvc, p['conv_w'], False)
    (s_ctx,) = _gdn_scan(qc, kc, vc, gatesc, None, emit_o=False)
    o_gdn, _ = _gdn_scan(ql, kl, vl, gates, s_ctx, emit_o=True)

    return _post(x, gate[:, None, :], y_f, y_r, zs, o_gdn, zg,
                 p['w_glu'].astype(bf16), p['b_glu'].astype(f32).reshape(1, d_s5),
                 p['gdn_norm_w'].astype(f32).reshape(1, GDN_HEAD),
                 p['w_out'].astype(bf16), p['ln_g'].astype(f32).reshape(1, 1, d),
                 p['ln_b'].astype(f32).reshape(1, 1, d), alpha)


def kernel(x, c, ctx, c_ctx, w_ada, b_ada, w_in, s5_lambda_re, s5_lambda_im, s5_log_dt, s5_b_re, s5_b_im, s5_c_re, s5_c_im, s5_d, w_glu, b_glu, conv_w, gdn_a_log, gdn_dt_bias, gdn_norm_w, w_out, ln_g, ln_b):
    depth = w_ada.shape[0]
    assert depth == 1, "context-token outputs are only skipped for the last layer"
    alpha = (2.0 * depth) ** 0.25
    p = {
        'w_ada': w_ada[0], 'b_ada': b_ada[0], 'w_in': w_in[0],
        's5_lambda_re': s5_lambda_re[0], 's5_lambda_im': s5_lambda_im[0], 's5_log_dt': s5_log_dt[0],
        's5_b_re': s5_b_re[0], 's5_b_im': s5_b_im[0], 's5_c_re': s5_c_re[0], 's5_c_im': s5_c_im[0],
        's5_d': s5_d[0], 'w_glu': w_glu[0], 'b_glu': b_glu[0], 'conv_w': conv_w[0],
        'gdn_a_log': gdn_a_log[0], 'gdn_dt_bias': gdn_dt_bias[0], 'gdn_norm_w': gdn_norm_w[0],
        'w_out': w_out[0], 'ln_g': ln_g[0], 'ln_b': ln_b[0],
    }
    return _layer(x, c, ctx, c_ctx, p, alpha)
```

```python
import functools
import math

import numpy as np
import jax
import jax.numpy as jnp
from jax import lax
from jax.experimental import pallas as pl
from jax.experimental.pallas import tpu as pltpu

f32 = jnp.float32
bf16 = jnp.bfloat16
HI = lax.Precision.HIGHEST

LANES = 128
S5_GROUP = 16
S5_STATE = 64
S5_T = 8
PACK_G = LANES // S5_GROUP
GDN_HEAD = 128
GDN_CHUNK = 64
GRID_W = 64
N_DIR = 2
LN_EPS = 1e-5
NORM_EPS = 1e-6
VMEM_LIMIT = 56 * 1024 * 1024


def _cparams(sem):
    return pltpu.CompilerParams(dimension_semantics=sem, vmem_limit_bytes=VMEM_LIMIT)


def _silu(x):
    return x * jax.nn.sigmoid(x)


def _ada_kernel(c_ref, w_ref, b_ref, o_ref):
    o_ref[...] = jnp.dot(_silu(c_ref[...]), w_ref[...], precision=HI,
                         preferred_element_type=f32) + b_ref[...]


def _ada(cond, w_ada, b_ada):
    rows, d = cond.shape
    n = w_ada.shape[1]
    tn = 512
    return pl.pallas_call(
        _ada_kernel,
        grid=(n // tn,),
        in_specs=[pl.BlockSpec((rows, d), lambda i: (0, 0)),
                  pl.BlockSpec((d, tn), lambda i: (0, i)),
                  pl.BlockSpec((1, tn), lambda i: (0, i))],
        out_specs=pl.BlockSpec((rows, tn), lambda i: (0, i)),
        out_shape=jax.ShapeDtypeStruct((rows, n), f32),
        compiler_params=_cparams(("parallel",)),
        name="ada",
    )(cond, w_ada, b_ada.reshape(1, n))


def _inproj_kernel(x_ref, scale_ref, shift_ref, wu_ref, wm_ref, wg_ref, gp_ref,
                   u_ref, zs_ref, qkv_ref, zg_ref, gate_ref, *, tt, d_s5, d_gdn, n_beta):
    nb, _, d = x_ref.shape
    sc = 1.0 + scale_ref[...]
    sh = shift_ref[...]
    h = (x_ref[...] * sc + sh).reshape(nb * tt, d).astype(bf16)
    r = jnp.dot(h, wm_ref[...], preferred_element_type=f32)
    zs_ref[...] = r[:, :d_s5].reshape(nb, tt, d_s5)
    qkv_ref[...] = r[:, d_s5:d_s5 + 3 * d_gdn].reshape(nb, tt, 3 * d_gdn)
    zg_ref[...] = r[:, d_s5 + 3 * d_gdn:].reshape(nb, tt, d_gdn)
    lg = jnp.dot(h, wg_ref[...], preferred_element_type=f32)
    lane = lax.broadcasted_iota(jnp.int32, lg.shape, 1)
    a = lg + gp_ref[1:2, :]
    softplus = jnp.maximum(a, 0.0) + jnp.log1p(jnp.exp(-jnp.abs(a)))
    gates = jnp.where(lane < n_beta, jax.nn.sigmoid(lg), -jnp.exp(gp_ref[0:1, :]) * softplus)
    gate_ref[...] = gates.reshape(nb, tt, LANES)
    sc2 = sc[:, 0, :]
    sh2 = sh[:, 0, :]
    ht = (jnp.swapaxes(x_ref[...], 0, 1) * sc2[None] + sh2[None]).reshape(tt * nb, d).astype(bf16)
    ru = jnp.dot(ht, wu_ref[...], preferred_element_type=f32)
    for j in range(d_s5 // LANES):
        u_ref[j] = ru[:, j * LANES:(j + 1) * LANES].reshape(tt // S5_T, S5_T, nb, LANES)


def _inproj(x, scale, shift, wu, wm, wg, gp, d_s5, d_gdn):
    nb, L, d = x.shape
    tt = 32
    npk = d_s5 // LANES
    kern = functools.partial(_inproj_kernel, tt=tt, d_s5=d_s5, d_gdn=d_gdn,
                             n_beta=N_DIR * (d_gdn // GDN_HEAD))
    const2 = lambda i: (0, 0)
    const3 = lambda i: (0, 0, 0)
    return pl.pallas_call(
        kern,
        grid=(L // tt,),
        in_specs=[pl.BlockSpec((nb, tt, d), lambda i: (0, i, 0)),
                  pl.BlockSpec((nb, 1, d), const3),
                  pl.BlockSpec((nb, 1, d), const3),
                  pl.BlockSpec(wu.shape, const2),
                  pl.BlockSpec(wm.shape, const2),
                  pl.BlockSpec(wg.shape, const2),
                  pl.BlockSpec(gp.shape, const2)],
        out_specs=[pl.BlockSpec((npk, tt // S5_T, S5_T, nb, LANES), lambda i: (0, i, 0, 0, 0)),
                   pl.BlockSpec((nb, tt, d_s5), lambda i: (0, i, 0)),
                   pl.BlockSpec((nb, tt, 3 * d_gdn), lambda i: (0, i, 0)),
                   pl.BlockSpec((nb, tt, d_gdn), lambda i: (0, i, 0)),
                   pl.BlockSpec((nb, tt, LANES), lambda i: (0, i, 0))],
        out_shape=[jax.ShapeDtypeStruct((npk, L // S5_T, S5_T, nb, LANES), f32),
                   jax.ShapeDtypeStruct((nb, L, d_s5), f32),
                   jax.ShapeDtypeStruct((nb, L, 3 * d_gdn), f32),
                   jax.ShapeDtypeStruct((nb, L, d_gdn), f32),
                   jax.ShapeDtypeStruct((nb, L, LANES), f32)],
        compiler_params=_cparams(("parallel",)),
        name="inproj",
    )(x, scale, shift, wu, wm, wg, gp)


def _s5_kernel(*refs, ct, rev, emit_y, intra):
    it = iter(refs)
    u_ref, cin_ref, at_ref, h0_ref = next(it), next(it), next(it), next(it)
    cout_ref = next(it) if emit_y else None
    cintra_ref = next(it) if intra else None
    d_ref = next(it) if intra else None
    y_ref = next(it) if emit_y else None
    hfin_ref = next(it)
    h_sc, v_sc, hall_sc, min_sc = next(it), next(it), next(it), next(it)
    mout_sc = next(it) if emit_y else None
    mintra_sc = next(it) if intra else None

    t = pl.program_id(1)
    nt = pl.num_programs(1)
    nb = u_ref.shape[3]
    sdim = h_sc.shape[1]
    half = sdim // 2
    gstate = half // PACK_G

    @pl.when(t == 0)
    def _():
        h_sc[...] = h0_ref[0]
        r_in = lax.broadcasted_iota(jnp.int32, (LANES, sdim), 0) // S5_GROUP
        c_in = (lax.broadcasted_iota(jnp.int32, (LANES, sdim), 1) % half) // gstate
        mask_in = (r_in == c_in).astype(f32)
        for s in range(S5_T):
            min_sc[s * LANES:(s + 1) * LANES, :] = (
                jnp.tile(cin_ref[0, s], (PACK_G, 1)) * mask_in).astype(bf16)
        if emit_y:
            r_out = lax.broadcasted_iota(jnp.int32, (half, LANES), 0) // gstate
            c_out = lax.broadcasted_iota(jnp.int32, (half, LANES), 1) // S5_GROUP
            mask_out = (r_out == c_out).astype(f32)
            for i in range(S5_T):
                for ri in range(2):
                    blk = cout_ref[0, i, ri * gstate:(ri + 1) * gstate, :]
                    mout_sc[ri * half:(ri + 1) * half, i * LANES:(i + 1) * LANES] = (
                        jnp.tile(blk, (PACK_G, 1)) * mask_out).astype(bf16)
        if intra:
            r_x = lax.broadcasted_iota(jnp.int32, (LANES, LANES), 0) // S5_GROUP
            c_x = lax.broadcasted_iota(jnp.int32, (LANES, LANES), 1) // S5_GROUP
            mask_x = (r_x == c_x).astype(f32)
            for s in range(S5_T):
                for i in range(S5_T):
                    mintra_sc[s * LANES:(s + 1) * LANES, i * LANES:(i + 1) * LANES] = (
                        jnp.tile(cintra_ref[0, s, i], (PACK_G, 1)) * mask_x).astype(bf16)

    x = jnp.concatenate([u_ref[0, :, i, :, :].reshape(ct * nb, LANES) for i in range(S5_T)], axis=1)
    xb = x.astype(bf16)
    v_sc[...] = jnp.dot(xb, min_sc[...], preferred_element_type=f32)
    are = at_ref[0, :, :half]
    aim = at_ref[0, :, half:]

    def body(s, carry):
        k = (ct - 1 - s) if rev else s
        off = pl.multiple_of(k * nb, nb)
        hre, him = carry
        hall_sc[pl.ds(off, nb), :half] = hre
        hall_sc[pl.ds(off, nb), half:] = him
        vre = v_sc[pl.ds(off, nb), :half]
        vim = v_sc[pl.ds(off, nb), half:]
        return are * hre - aim * him + vre, are * him + aim * hre + vim

    hre, him = lax.fori_loop(0, ct, body, (h_sc[:, :half], h_sc[:, half:]))
    h_sc[:, :half] = hre
    h_sc[:, half:] = him

    @pl.when(t == nt - 1)
    def _():
        hfin_ref[0] = h_sc[...]

    if emit_y:
        y = jnp.dot(hall_sc[...].astype(bf16), mout_sc[...], preferred_element_type=f32)
        if intra:
            y = y + jnp.dot(xb, mintra_sc[...], preferred_element_type=f32) + x * d_ref[0]
        for i in range(S5_T):
            y_ref[0, :, i, :, :] = y[:, i * LANES:(i + 1) * LANES].reshape(ct, nb, LANES)


def _s5_pass(u5, c_in, a_t, h0, c_out=None, c_intra=None, d_skip=None, *, rev, emit_y):
    npk, nc, _, nb, _ = u5.shape
    ct = min(32, nc)
    nt = nc // ct
    intra = c_intra is not None
    kdim = S5_T * LANES
    sdim = a_t.shape[-1]
    tmap = (lambda j, t: (j, nt - 1 - t, 0, 0, 0)) if rev else (lambda j, t: (j, t, 0, 0, 0))
    pmap = lambda j, t: (j, 0, 0)
    pmap4 = lambda j, t: (j, 0, 0, 0)
    pmap5 = lambda j, t: (j, 0, 0, 0, 0)
    args = [u5, c_in, a_t, h0]
    in_specs = [pl.BlockSpec((1, ct, S5_T, nb, LANES), tmap),
                pl.BlockSpec((1,) + c_in.shape[1:], pmap4),
                pl.BlockSpec((1, nb, sdim), pmap),
                pl.BlockSpec((1, nb, sdim), pmap)]
    scratch = [pltpu.VMEM((nb, sdim), f32),
               pltpu.VMEM((ct * nb, sdim), f32),
               pltpu.VMEM((ct * nb, sdim), f32),
               pltpu.VMEM((kdim, sdim), bf16)]
    if emit_y:
        args.append(c_out)
        in_specs.append(pl.BlockSpec((1,) + c_out.shape[1:], pmap4))
        scratch.append(pltpu.VMEM((sdim, kdim), bf16))
    if intra:
        args += [c_intra, d_skip]
        in_specs += [pl.BlockSpec((1,) + c_intra.shape[1:], pmap5), pl.BlockSpec((1, 1, kdim), pmap)]
        scratch.append(pltpu.VMEM((kdim, kdim), bf16))
    out_specs, out_shape = [], []
    if emit_y:
        out_specs.append(pl.BlockSpec((1, ct, S5_T, nb, LANES), tmap))
        out_shape.append(jax.ShapeDtypeStruct(u5.shape, f32))
    out_specs.append(pl.BlockSpec((1, nb, sdim), pmap))
    out_shape.append(jax.ShapeDtypeStruct((npk, nb, sdim), f32))
    kern = functools.partial(_s5_kernel, ct=ct, rev=rev, emit_y=emit_y, intra=intra)
    return pl.pallas_call(
        kern,
        grid=(npk, nt),
        in_specs=in_specs,
        out_specs=out_specs,
        out_shape=out_shape,
        scratch_shapes=scratch,
        compiler_params=_cparams(("parallel", "arbitrary")),
        name="s5_rev" if rev else "s5_fwd",
    )(*args)


def _s5_weights(lam_re, lam_im, log_dt, b_re, b_im, c_re, c_im):
    T = S5_T
    ndir, G, P = lam_re.shape
    npk = G // PACK_G
    dt = jnp.exp(log_dt.astype(f32))[..., None]
    lr, li = lam_re.astype(f32), lam_im.astype(f32)
    zr, zi = lr * dt, li * dt

    def apow(n):
        m = jnp.exp(n * zr)
        return m * jnp.cos(n * zi), m * jnp.sin(n * zi)

    nr = jnp.expm1(zr) * jnp.cos(zi) - 2.0 * jnp.square(jnp.sin(0.5 * zi))
    ni = jnp.exp(zr) * jnp.sin(zi)
    den = lr * lr + li * li
    qr = (nr * lr + ni * li) / den
    qi = (ni * lr - nr * li) / den
    bre, bim = b_re.astype(f32), b_im.astype(f32)
    bbr = qr[..., None] * bre - qi[..., None] * bim
    bbi = qr[..., None] * bim + qi[..., None] * bre
    cr, ci = c_re.astype(f32), c_im.astype(f32)

    pw = [apow(n) for n in range(T + 1)]

    def a_times_b(n, d):
        pr, pi = pw[n][0][d][..., None], pw[n][1][d][..., None]
        return pr * bbr[d] - pi * bbi[d], pr * bbi[d] + pi * bbr[d]

    def c_times_a(n, d):
        pr, pi = pw[n][0][d][:, None, :], pw[n][1][d][:, None, :]
        return cr[d] * pr - ci[d] * pi, cr[d] * pi + ci[d] * pr

    def pack_in(mats):
        re = jnp.stack([m[0] for m in mats])
        im = jnp.stack([m[1] for m in mats])
        base = jnp.stack([re, im])
        base = base.reshape(2, T, npk, PACK_G, P, S5_GROUP)
        base = base.transpose(2, 1, 5, 0, 3, 4)
        return base.reshape(npk, T, S5_GROUP, 2 * PACK_G * P)

    def pack_out(mats):
        re = jnp.stack([m[0] for m in mats])
        im = jnp.stack([-m[1] for m in mats])
        base = jnp.stack([re, im])
        base = base.reshape(2, T, npk, PACK_G, S5_GROUP, P)
        base = base.transpose(2, 1, 0, 5, 3, 4)
        return base.reshape(npk, T, 2 * P, LANES)

    m_in_f = pack_in([a_times_b(T - 1 - s, 0) for s in range(T)])
    m_in_r = pack_in([a_times_b(s, 1) for s in range(T)])
    m_out_f = pack_out([c_times_a(i + 1, 0) for i in range(T)])
    m_out_r = pack_out([c_times_a(T - i, 1) for i in range(T)])

    def lag_kernel(d):
        ks = []
        for tau in range(T):
            car, cai = c_times_a(tau, d)
            ks.append(jnp.einsum('gop,gpi->gio', car, bbr[d]) - jnp.einsum('gop,gpi->gio', cai, bbi[d]))
        return jnp.stack(ks)

    kf, kr = lag_kernel(0), lag_kernel(1)
    lag = np.arange(T)[None, :] - np.arange(T)[:, None]
    toe = (kf[np.maximum(lag, 0)] * jnp.asarray(lag >= 0, f32)[:, :, None, None, None]
           + kr[np.maximum(-lag, 0)] * jnp.asarray(lag <= 0, f32)[:, :, None, None, None])
    toe = toe.reshape(T, T, npk, PACK_G, S5_GROUP, S5_GROUP).transpose(2, 0, 1, 4, 3, 5)
    m_intra = toe.reshape(npk, T, T, S5_GROUP, LANES)

    def a_chunk(d):
        pr, pi = pw[T][0][d].reshape(npk, PACK_G * P), pw[T][1][d].reshape(npk, PACK_G * P)
        return jnp.concatenate([pr, pi], axis=-1)

    return dict(m_in_f=m_in_f, m_in_r=m_in_r, m_out_f=m_out_f, m_out_r=m_out_r,
                m_intra=m_intra, a_f=a_chunk(0), a_r=a_chunk(1))


def _gdn_prep_kernel(*refs, tile, rows_conv, d_gdn):
    if rows_conv:
        main_ref, up_ref, dn_ref, w_ref, q_ref, k_ref, v_ref = refs
    else:
        main_ref, w_ref, q_ref, k_ref, v_ref = refs
    w = w_ref[...]
    if rows_conv:
        i = pl.program_id(1)
        n = pl.num_programs(1)
        up = up_ref[0] * (i > 0).astype(f32)
        dn = dn_ref[0] * (i < n - 1).astype(f32)
        ext = jnp.concatenate([up, main_ref[0], dn], axis=0)
        base = GRID_W
        drs = (-1, 0, 1)
        col = lax.broadcasted_iota(jnp.int32, (tile, 1), 0) % GRID_W
        mask_m = col != 0
        mask_p = col != GRID_W - 1
    else:
        ext = main_ref[0]
        base = 0
        drs = (0,)
        col = lax.broadcasted_iota(jnp.int32, (tile, 1), 0)
        mask_m = col != 0
        mask_p = col != tile - 1
    n_ext = ext.shape[0]
    shifted = {-1: pltpu.roll(ext, 1, 0), 0: ext, 1: pltpu.roll(ext, n_ext - 1, 0)}
    acc = {}
    for dc in (-1, 0, 1):
        a = None
        for dr in drs:
            tap = w[(dr + 1) * 3 + (dc + 1)][None, :]
            term = shifted[dc][base + GRID_W * dr: base + GRID_W * dr + tile] * tap
            a = term if a is None else a + term
        acc[dc] = a
    conv = acc[0] + jnp.where(mask_m, acc[-1], 0.0) + jnp.where(mask_p, acc[1], 0.0)
    s = _silu(conv)
    nh = d_gdn // GDN_HEAD
    for h in range(nh):
        qh = s[:, h * GDN_HEAD:(h + 1) * GDN_HEAD]
        kh = s[:, d_gdn + h * GDN_HEAD: d_gdn + (h + 1) * GDN_HEAD]
        qn = qh * lax.rsqrt(jnp.sum(qh * qh, axis=-1, keepdims=True) + NORM_EPS) * (GDN_HEAD ** -0.5)
        kn = kh * lax.rsqrt(jnp.sum(kh * kh, axis=-1, keepdims=True) + NORM_EPS)
        q_ref[0, :, h * GDN_HEAD:(h + 1) * GDN_HEAD] = qn.astype(bf16)
        k_ref[0, :, h * GDN_HEAD:(h + 1) * GDN_HEAD] = kn.astype(bf16)
    v_ref[0] = s[:, 2 * d_gdn:].astype(bf16)


def _gdn_prep(qkv, conv_w, rows_conv):
    nb, L, c3 = qkv.shape
    d_gdn = c3 // 3
    w9 = conv_w.reshape(9, c3).astype(f32)
    if rows_conv:
        tile = min(512, L)
        nt = L // tile
        r = tile // GRID_W
        nrows = L // GRID_W
        in_specs = [pl.BlockSpec((1, tile, c3), lambda b, i: (b, i, 0)),
                    pl.BlockSpec((1, GRID_W, c3), lambda b, i: (b, jnp.maximum(i * r - 1, 0), 0)),
                    pl.BlockSpec((1, GRID_W, c3), lambda b, i: (b, jnp.minimum((i + 1) * r, nrows - 1), 0)),
                    pl.BlockSpec((9, c3), lambda b, i: (0, 0))]
        args = (qkv, qkv, qkv, w9)
    else:
        tile, nt = L, 1
        in_specs = [pl.BlockSpec((1, tile, c3), lambda b, i: (b, i, 0)),
                    pl.BlockSpec((9, c3), lambda b, i: (0, 0))]
        args = (qkv, w9)
    kern = functools.partial(_gdn_prep_kernel, tile=tile, rows_conv=rows_conv, d_gdn=d_gdn)
    o_spec = pl.BlockSpec((1, tile, d_gdn), lambda b, i: (b, i, 0))
    o_shape = jax.ShapeDtypeStruct((nb, L, d_gdn), bf16)
    return pl.pallas_call(
        kern,
        grid=(nb, nt),
        in_specs=in_specs,
        out_specs=[o_spec, o_spec, o_spec],
        out_shape=[o_shape, o_shape, o_shape],
        compiler_params=_cparams(("parallel", "parallel")),
        name="gdn_prep",
    )(*args)


def _chunk_cumsum(g, rev):
    n = g.shape[0]
    r = lax.broadcasted_iota(jnp.int32, g.shape, 0)
    s = 1
    while s < n:
        if rev:
            g = g + jnp.where(r < n - s, pltpu.roll(g, n - s, 0), 0.0)
        else:
            g = g + jnp.where(r >= s, pltpu.roll(g, s, 0), 0.0)
        s *= 2
    return g


def _tri_solve_rows(apl_sc, tl_sc, dirn, upper):
    nblk = GDN_CHUNK // 8
    base = dirn * GDN_CHUNK
    sub = lax.broadcasted_iota(jnp.int32, (8, LANES), 0)
    for step_blk in range(nblk):
        blk = (nblk - 1 - step_blk) if upper else step_blk
        mblocks = range(blk, nblk) if upper else range(0, blk + 1)

        def solve_row(s, carry, blk=blk, mblocks=mblocks):
            ii = (7 - s) if upper else s
            i = blk * 8 + ii
            acc = {}
            for mb in mblocks:
                arow = apl_sc[i, base + mb * 8:base + (mb + 1) * 8, :]
                jblocks = range(mb, nblk) if upper else range(0, mb + 1)
                for mm in range(8):
                    ab = jnp.broadcast_to(arow[mm:mm + 1, :], (8, LANES))
                    for jb in jblocks:
                        term = ab * tl_sc[mb * 8 + mm, base + jb * 8:base + (jb + 1) * 8, :]
                        key = (jb, mm % 2)
                        acc[key] = term if key not in acc else acc[key] + term
            for jb in (range(blk, nblk) if upper else range(0, blk + 1)):
                tot = acc[(jb, 0)] + acc[(jb, 1)]
                unit_row = jnp.where(sub == ii, 1.0, 0.0) if jb == blk else 0.0
                tl_sc[i, base + jb * 8:base + (jb + 1) * 8, :] = unit_row - tot
            return carry

        lax.fori_loop(0, 8, solve_row, 0)


def _gdn_scan_kernel(*refs, nchunks, nh, emit_o, has_s0):
    it = iter(refs)
    q_ref, k_ref, v_ref, g_ref = next(it), next(it), next(it), next(it)
    s0_ref = next(it) if has_s0 else None
    o_ref = next(it) if emit_o else None
    sfin_ref = next(it)
    s_sc, at_sc, apl_sc, tl_sc, mk_sc, w_sc, gcol_sc, grow_sc = (next(it) for _ in range(8))
    C = GDN_CHUNK
    bb = q_ref.shape[0]
    nunits = bb * nchunks
    s_sc[...] = s0_ref[...] if has_s0 else jnp.zeros(s_sc.shape, f32)
    if emit_o:
        o_ref[...] = jnp.zeros(o_ref.shape, f32)
    row = lax.broadcasted_iota(jnp.int32, (C, N_DIR * C), 0)
    lane = lax.broadcasted_iota(jnp.int32, (C, N_DIR * C), 1)
    fwd_half = lane < C
    col = jnp.where(fwd_half, lane, lane - C)
    tri = (fwd_half & (row >= col)) | (~fwd_half & (row <= col))
    strict = (fwd_half & (row > col)) | (~fwd_half & (row < col))
    nbeta = N_DIR * nh
    gate_lane = lax.broadcasted_iota(jnp.int32, (C, LANES), 1)

    def unit_of(unit):
        bl = unit // nchunks
        return bl, pl.multiple_of((unit % nchunks) * C, C)

    def phase_a(unit, carry):
        bl, off = unit_of(unit)
        gts = g_ref[bl, pl.ds(off, C), :]
        gmix = jnp.where(gate_lane < nbeta, gts,
                         jnp.where(gate_lane < nbeta + nh, _chunk_cumsum(gts, False), _chunk_cumsum(gts, True)))
        g_rows = gmix.T[:2 * nbeta]
        gcol_sc[unit] = gmix
        grow_sc[unit] = g_rows
        for h in range(nh):
            hs = slice(h * GDN_HEAD, (h + 1) * GDN_HEAD)
            cbf, cbr = h, nh + h
            cgf, cgr = nbeta + h, nbeta + nh + h
            kb = k_ref[bl, pl.ds(off, C), hs]
            k2 = jnp.concatenate([kb, kb], axis=0)
            if emit_o:
                kq = jnp.concatenate([kb, q_ref[bl, pl.ds(off, C), hs]], axis=0)
            else:
                kq = kb
            kkqk = lax.dot_general(kq, k2, (((1,), (1,)), ((), ())), preferred_element_type=f32)
            g_col = jnp.where(fwd_half, gmix[:, cgf:cgf + 1], gmix[:, cgr:cgr + 1])
            g_row = jnp.concatenate([g_rows[cgf:cgf + 1, :], g_rows[cgr:cgr + 1, :]], axis=1)
            b_row = jnp.concatenate([g_rows[cbf:cbf + 1, :], g_rows[cbr:cbr + 1, :]], axis=1)
            decay_b = jnp.exp(jnp.where(tri, g_col - g_row, -1e30)) * b_row
            at_sc[unit * nh + h] = jnp.where(strict, kkqk[:C] * decay_b, 0.0)
            if emit_o:
                mk_sc[unit * nh + h, :C, :] = (kkqk[C:] * decay_b).astype(bf16)
        return carry

    lax.fori_loop(0, nunits, phase_a, 0, unroll=2)

    def swap_in(ib, carry):
        rows = pl.ds(pl.multiple_of(ib * 8, 8), 8)
        tl_sc[rows] = jnp.swapaxes(at_sc[:, rows, :], 0, 1)
        return carry

    def to_lanes(i, carry):
        apl_sc[i] = tl_sc[i].T
        return carry

    lax.fori_loop(0, C // 8, swap_in, 0)
    lax.fori_loop(0, C, to_lanes, 0, unroll=4)
    tl_sc[...] = jnp.zeros(tl_sc.shape, f32)
    _tri_solve_rows(apl_sc, tl_sc, 0, False)
    _tri_solve_rows(apl_sc, tl_sc, 1, True)

    def from_lanes(i, carry):
        apl_sc[i] = tl_sc[i].T
        return carry

    def swap_out(ib, carry):
        rows = pl.ds(pl.multiple_of(ib * 8, 8), 8)
        at_sc[:, rows, :] = jnp.swapaxes(apl_sc[rows], 0, 1)
        return carry

    lax.fori_loop(0, C, from_lanes, 0, unroll=4)
    lax.fori_loop(0, C // 8, swap_out, 0)

    u0_sc = (apl_sc, tl_sc)

    def u0_slot(p):
        return p // 2, pl.ds(pl.multiple_of((p % 2) * C, C), C)

    eye_bf = (lax.broadcasted_iota(jnp.int32, (GDN_HEAD, GDN_HEAD), 0)
              == lax.broadcasted_iota(jnp.int32, (GDN_HEAD, GDN_HEAD), 1)).astype(bf16)

    def phase_b2(unit, carry):
        bl, off = unit_of(unit)
        g_rows = grow_sc[unit]
        g_cols = gcol_sc[unit]
        zero_kv = jnp.zeros((C, GDN_HEAD), bf16)
        for h in range(nh):
            hs = slice(h * GDN_HEAD, (h + 1) * GDN_HEAD)
            p = unit * nh + h
            cbf, cbr = h, nh + h
            cgf, cgr = nbeta + h, nbeta + nh + h
            t_both = at_sc[p]
            kb = k_ref[bl, pl.ds(off, C), hs]
            vb = v_ref[bl, pl.ds(off, C), hs]
            g_row = jnp.concatenate([g_rows[cgf:cgf + 1, :], g_rows[cgr:cgr + 1, :]], axis=1)
            b_row = jnp.concatenate([g_rows[cbf:cbf + 1, :], g_rows[cbr:cbr + 1, :]], axis=1)
            g_last = jnp.concatenate([jnp.broadcast_to(g_cols[C - 1:C, cgf:cgf + 1], (1, C)),
                                      jnp.broadcast_to(g_cols[0:1, cgr:cgr + 1], (1, C))], axis=1)
            v_bd = jnp.concatenate([jnp.concatenate([vb, zero_kv], axis=1),
                                    jnp.concatenate([zero_kv, vb], axis=1)], axis=0)
            k_bd = jnp.concatenate([jnp.concatenate([kb, zero_kv], axis=1),
                                    jnp.concatenate([zero_kv, kb], axis=1)], axis=0)
            u0 = jnp.dot(t_both.astype(bf16), v_bd, preferred_element_type=f32)
            w = jnp.dot((t_both * jnp.exp(g_row)).astype(bf16), k_bd, preferred_element_type=f32)
            slab, rows = u0_slot(p)
            for dirn in range(N_DIR):
                u0_sc[dirn][slab, rows, :] = u0[:, dirn * GDN_HEAD:(dirn + 1) * GDN_HEAD]
                w_sc[dirn, p] = w[:, dirn * GDN_HEAD:(dirn + 1) * GDN_HEAD].astype(bf16)
            k_t = lax.dot_general(eye_bf, jnp.concatenate([kb, kb], axis=0), (((1,), (1,)), ((), ())),
                                  preferred_element_type=f32)
            mk_sc[p, C:, :] = (k_t * (b_row * jnp.exp(g_last - g_row))).astype(bf16)
        return carry

    lax.fori_loop(0, nunits, phase_b2, 0, unroll=4)

    def phase_c(step, carry):
        chains = []
        for dirn, unit in ((0, step), (1, nunits - 1 - step)):
            bl, off = unit_of(unit)
            g_col_all = gcol_sc[unit]
            g_tot_all = g_col_all[0:1, :] if dirn else g_col_all[C - 1:C, :]
            for h in range(nh):
                cg = nbeta + dirn * nh + h
                hs = slice(h * GDN_HEAD, (h + 1) * GDN_HEAD)
                p = unit * nh + h
                slab, rows = u0_slot(p)
                lhs = w_sc[dirn, p]
                ch = dict(idx=(bl, dirn, h), off=off, hs=hs, dirn=dirn, u0=u0_sc[dirn][slab, rows, :],
                          g_last=jnp.exp(g_tot_all[:, cg:cg + 1]))
                if emit_o:
                    lhs = jnp.concatenate([lhs, q_ref[bl, pl.ds(off, C), hs]], axis=0)
                    ch['mk'] = mk_sc[p]
                    ch['gam'] = jnp.exp(g_col_all[:, cg:cg + 1])
                else:
                    ch['mk'] = mk_sc[p, C:, :]
                ch['lhs'] = lhs
                chains.append(ch)
        s_old = [s_sc[ch['idx']] for ch in chains]
        r = [jnp.dot(ch['lhs'], s.astype(bf16), preferred_element_type=f32) for ch, s in zip(chains, s_old)]
        zeros = jnp.zeros((C, GDN_HEAD), bf16)
        ub = [(ch['u0'] - ri[:C]).astype(bf16) for ch, ri in zip(chains, r)]
        m = [jnp.dot(ch['mk'], jnp.concatenate([zeros, ubi] if ch['dirn'] else [ubi, zeros], axis=0),
                     preferred_element_type=f32) for ch, ubi in zip(chains, ub)]
        for n, ch in enumerate(chains):
            if emit_o:
                bl = ch['idx'][0]
                o_ref[bl, pl.ds(ch['off'], C), ch['hs']] += ch['gam'] * r[n][C:] + m[n][:C]
                s_sc[ch['idx']] = ch['g_last'] * s_old[n] + m[n][C:]
            else:
                s_sc[ch['idx']] = ch['g_last'] * s_old[n] + m[n]
        return carry

    lax.fori_loop(0, nunits, phase_c, 0)
    sfin_ref[...] = s_sc[...]


def _gdn_scan(q, k, v, gates, s0, *, emit_o):
    nb, L, d_gdn = q.shape
    nh = d_gdn // GDN_HEAD
    nchunks = L // GDN_CHUNK
    C = GDN_CHUNK
    bb = max(1, min(nb, LANES // (nchunks * nh)))
    nunits = bb * nchunks
    nprob = nunits * nh
    assert nprob == LANES, (nb, L, nh)
    has_s0 = s0 is not None
    kern = functools.partial(_gdn_scan_kernel, nchunks=nchunks, nh=nh, emit_o=emit_o, has_s0=has_s0)
    seq = lambda b: (b, 0, 0)
    st = lambda b: (b, 0, 0, 0, 0)
    st_spec = pl.BlockSpec((bb, N_DIR, nh, GDN_HEAD, GDN_HEAD), st)
    st_shape = jax.ShapeDtypeStruct((nb, N_DIR, nh, GDN_HEAD, GDN_HEAD), f32)
    args = [q, k, v, gates]
    in_specs = [pl.BlockSpec((bb, L, d_gdn), seq)] * 3 + [pl.BlockSpec((bb, L, LANES), seq)]
    if has_s0:
        args.append(s0)
        in_specs.append(st_spec)
    out_specs, out_shape = [], []
    if emit_o:
        out_specs.append(pl.BlockSpec((bb, L, d_gdn), seq))
        out_shape.append(jax.ShapeDtypeStruct((nb, L, d_gdn), f32))
    out_specs.append(st_spec)
    out_shape.append(st_shape)
    return pl.pallas_call(
        kern,
        grid=(nb // bb,),
        in_specs=in_specs,
        out_specs=out_specs,
        out_shape=out_shape,
        scratch_shapes=[pltpu.VMEM((bb, N_DIR, nh, GDN_HEAD, GDN_HEAD), f32),
                        pltpu.VMEM((nprob, C, N_DIR * C), f32),
                        pltpu.VMEM((C, N_DIR * C, nprob), f32),
                        pltpu.VMEM((C, N_DIR * C, nprob), f32),
                        pltpu.VMEM((nprob, C + GDN_HEAD, N_DIR * C), bf16),
                        pltpu.VMEM((N_DIR, nprob, C, GDN_HEAD), bf16),
                        pltpu.VMEM((nunits, C, LANES), f32),
                        pltpu.VMEM((nunits, 4 * nh, C), f32)],
        compiler_params=_cparams(("parallel",)),
        name="gdn_scan" if emit_o else "gdn_scan_state",
    )(*args)


def _post_kernel(x_ref, gate_ref, yf_ref, yr_ref, zs_ref, og_ref, zg_ref,
                 wglu_ref, bglu_ref, nw_ref, wout_ref, lng_ref, lnb_ref, o_ref, *, tt, alpha):
    nb, _, d = x_ref.shape
    rows = nb * tt
    npk = yf_ref.shape[0]
    d_s5 = npk * LANES
    d_gdn = og_ref.shape[2]
    parts = []
    for j in range(npk):
        ysum = (yf_ref[j] + yr_ref[j]).reshape(tt, nb, LANES)
        parts.append(jnp.swapaxes(ysum, 0, 1).reshape(rows, LANES))
    y = jnp.concatenate(parts, axis=1)
    g = 0.5 * y * (1.0 + lax.erf(y * (2.0 ** -0.5)))
    glu = jnp.dot(g.astype(bf16), wglu_ref[...], preferred_element_type=f32) + bglu_ref[...]
    s5 = g * jax.nn.sigmoid(glu) * _silu(zs_ref[...].reshape(rows, d_s5))
    o = og_ref[...].reshape(rows, d_gdn)
    zg = zg_ref[...].reshape(rows, d_gdn)
    gd = []
    for h in range(d_gdn // GDN_HEAD):
        oh = o[:, h * GDN_HEAD:(h + 1) * GDN_HEAD]
        gd.append(oh * lax.rsqrt(jnp.mean(oh * oh, axis=-1, keepdims=True) + NORM_EPS) * nw_ref[...])
    gdn = jnp.concatenate(gd, axis=1) * _silu(zg)
    mix = jnp.concatenate([s5, gdn], axis=1).astype(bf16)
    yv = jnp.dot(mix, wout_ref[...], preferred_element_type=f32).reshape(nb, tt, d)
    r = alpha * x_ref[...] + gate_ref[...] * yv
    mu = jnp.mean(r, axis=-1, keepdims=True)
    var = jnp.mean(jnp.square(r - mu), axis=-1, keepdims=True)
    o_ref[...] = (r - mu) * lax.rsqrt(var + LN_EPS) * lng_ref[...] + lnb_ref[...]


def _post(x, gate, yf, yr, zs, o_gdn, zg, wglu, bglu, nw, wout, lng, lnb, alpha):
    nb, L, d = x.shape
    tt = 16
    npk = yf.shape[0]
    d_s5 = npk * LANES
    d_gdn = o_gdn.shape[2]
    kern = functools.partial(_post_kernel, tt=tt, alpha=alpha)
    tok = lambda i: (0, i, 0)
    c2 = lambda i: (0, 0)
    c3 = lambda i: (0, 0, 0)
    y_spec = pl.BlockSpec((npk, tt // S5_T, S5_T, nb, LANES), lambda i: (0, i, 0, 0, 0))
    return pl.pallas_call(
        kern,
        grid=(L // tt,),
        in_specs=[pl.BlockSpec((nb, tt, d), tok),
                  pl.BlockSpec((nb, 1, d), c3),
                  y_spec, y_spec,
                  pl.BlockSpec((nb, tt, d_s5), tok),
                  pl.BlockSpec((nb, tt, d_gdn), tok),
                  pl.BlockSpec((nb, tt, d_gdn), tok),
                  pl.BlockSpec(wglu.shape, c2),
                  pl.BlockSpec(bglu.shape, c2),
                  pl.BlockSpec(nw.shape, c2),
                  pl.BlockSpec(wout.shape, c2),
                  pl.BlockSpec(lng.shape, c3),
                  pl.BlockSpec(lnb.shape, c3)],
        out_specs=pl.BlockSpec((nb, tt, d), tok),
        out_shape=jax.ShapeDtypeStruct((nb, L, d), f32),
        compiler_params=_cparams(("parallel",)),
        name="post",
    )(x, gate, yf, yr, zs, o_gdn, zg, wglu, bglu, nw, wout, lng, lnb)


def _layer(x, c, ctx, c_ctx, p, alpha):
    nb, L, d = x.shape
    d_s5 = p['s5_d'].shape[0]
    d_gdn = p['conv_w'].shape[-1] // 3
    nh = d_gdn // GDN_HEAD
    npk = d_s5 // LANES

    pad = (-(nb + 1)) % 8
    cond = jnp.concatenate([c, c_ctx[None, :], jnp.zeros((pad, d), f32)], axis=0)
    m = _ada(cond, p['w_ada'], p['b_ada'])
    shift, scale, gate = m[:nb, :d], m[:nb, d:2 * d], m[:nb, 2 * d:]
    shift_c = jnp.broadcast_to(m[nb:nb + 1, :d], (nb, d))
    scale_c = jnp.broadcast_to(m[nb:nb + 1, d:2 * d], (nb, d))

    w_in = p['w_in']
    wu = w_in[:, :d_s5].astype(bf16)
    wm = w_in[:, d_s5:2 * d_s5 + 4 * d_gdn].astype(bf16)
    ng = 2 * N_DIR * nh
    wg = jnp.pad(w_in[:, 2 * d_s5 + 4 * d_gdn:], ((0, 0), (0, LANES - ng))).astype(bf16)
    gp = jnp.zeros((2, LANES), f32)
    gp = gp.at[0, ng // 2:ng].set(p['gdn_a_log'].reshape(-1).astype(f32))
    gp = gp.at[1, ng // 2:ng].set(p['gdn_dt_bias'].reshape(-1).astype(f32))

    u5, zs, qkv, zg, gates = _inproj(x, scale[:, None, :], shift[:, None, :], wu, wm, wg, gp, d_s5, d_gdn)
    u5c, _, qkvc, _, gatesc = _inproj(ctx, scale_c[:, None, :], shift_c[:, None, :], wu, wm, wg, gp, d_s5, d_gdn)

    sw = _s5_weights(p['s5_lambda_re'], p['s5_lambda_im'], p['s5_log_dt'],
                     p['s5_b_re'], p['s5_b_im'], p['s5_c_re'], p['s5_c_im'])
    sdim = sw['a_f'].shape[-1]
    a_f = jnp.broadcast_to(sw['a_f'][:, None, :], (npk, nb, sdim))
    a_r = jnp.broadcast_to(sw['a_r'][:, None, :], (npk, nb, sdim))
    h_zero = jnp.zeros((npk, nb, sdim), f32)
    d_skip = jnp.tile(p['s5_d'].astype(f32).reshape(npk, 1, LANES), (1, 1, S5_T))
    (hc_f,) = _s5_pass(u5c, sw['m_in_f'], a_f, h_zero, rev=False, emit_y=False)
    (hc_r,) = _s5_pass(u5c, sw['m_in_r'], a_r, h_zero, rev=True, emit_y=False)
    y_f, _ = _s5_pass(u5, sw['m_in_f'], a_f, hc_f, sw['m_out_f'], sw['m_intra'], d_skip, rev=False, emit_y=True)
    y_r, _ = _s5_pass(u5, sw['m_in_r'], a_r, hc_r, sw['m_out_r'], rev=True, emit_y=True)

    ql, kl, vl = _gdn_prep(qkv, p['conv_w'], True)
    qc, kc, vc = _gdn_prep(qkvc, p['conv_w'], False)
    (s_ctx,) = _gdn_scan(qc, kc, vc, gatesc, None, emit_o=False)
    o_gdn, _ = _gdn_scan(ql, kl, vl, gates, s_ctx, emit_o=True)

    return _post(x, gate[:, None, :], y_f, y_r, zs, o_gdn, zg,
                 p['w_glu'].astype(bf16), p['b_glu'].astype(f32).reshape(1, d_s5),
                 p['gdn_norm_w'].astype(f32).reshape(1, GDN_HEAD),
                 p['w_out'].astype(bf16), p['ln_g'].astype(f32).reshape(1, 1, d),
                 p['ln_b'].astype(f32).reshape(1, 1, d), alpha)


def kernel(x, c, ctx, c_ctx, w_ada, b_ada, w_in, s5_lambda_re, s5_lambda_im, s5_log_dt, s5_b_re, s5_b_im, s5_c_re, s5_c_im, s5_d, w_glu, b_glu, conv_w, gdn_a_log, gdn_dt_bias, gdn_norm_w, w_out, ln_g, ln_b):
    depth = w_ada.shape[0]
    assert depth == 1, "context-token outputs are only skipped for the last layer"
    alpha = (2.0 * depth) ** 0.25
    p = {
        'w_ada': w_ada[0], 'b_ada': b_ada[0], 'w_in': w_in[0],
        's5_lambda_re': s5_lambda_re[0], 's5_lambda_im': s5_lambda_im[0], 's5_log_dt': s5_log_dt[0],
        's5_b_re': s5_b_re[0], 's5_b_im': s5_b_im[0], 's5_c_re': s5_c_re[0], 's5_c_im': s5_c_im[0],
        's5_d': s5_d[0], 'w_glu': w_glu[0], 'b_glu': b_glu[0], 'conv_w': conv_w[0],
        'gdn_a_log': gdn_a_log[0], 'gdn_dt_bias': gdn_dt_bias[0], 'gdn_norm_w': gdn_norm_w[0],
        'w_out': w_out[0], 'ln_g': ln_g[0], 'ln_b': ln_b[0],
    }
    return _layer(x, c, ctx, c_ctx, p, alpha)
```

```python
import functools
import math

import numpy as np
import jax
import jax.numpy as jnp
from jax import lax
from jax.experimental import pallas as pl
from jax.experimental.pallas import tpu as pltpu

f32 = jnp.float32
bf16 = jnp.bfloat16
HI = lax.Precision.HIGHEST

LANES = 128
S5_GROUP = 16
S5_STATE = 64
S5_T = 8
PACK_G = LANES // S5_GROUP
GDN_HEAD = 128
GDN_CHUNK = 64
GRID_W = 64
N_DIR = 2
LN_EPS = 1e-5
NORM_EPS = 1e-6
VMEM_LIMIT = 56 * 1024 * 1024


def _cparams(sem):
    return pltpu.CompilerParams(dimension_semantics=sem, vmem_limit_bytes=VMEM_LIMIT)


def _silu(x):
    return x * jax.nn.sigmoid(x)


def _ada_kernel(c_ref, w_ref, b_ref, o_ref):
    o_ref[...] = jnp.dot(_silu(c_ref[...]), w_ref[...], precision=HI,
                         preferred_element_type=f32) + b_ref[...]


def _ada(cond, w_ada, b_ada):
    rows, d = cond.shape
    n = w_ada.shape[1]
    tn = 512
    return pl.pallas_call(
        _ada_kernel,
        grid=(n // tn,),
        in_specs=[pl.BlockSpec((rows, d), lambda i: (0, 0)),
                  pl.BlockSpec((d, tn), lambda i: (0, i)),
                  pl.BlockSpec((1, tn), lambda i: (0, i))],
        out_specs=pl.BlockSpec((rows, tn), lambda i: (0, i)),
        out_shape=jax.ShapeDtypeStruct((rows, n), f32),
        compiler_params=_cparams(("parallel",)),
        name="ada",
    )(cond, w_ada, b_ada.reshape(1, n))


def _inproj_kernel(x_ref, scale_ref, shift_ref, wu_ref, wm_ref, wg_ref, gp_ref,
                   u_ref, zs_ref, qkv_ref, zg_ref, gate_ref, *, tt, d_s5, d_gdn, n_beta):
    nb, _, d = x_ref.shape
    sc = 1.0 + scale_ref[...]
    sh = shift_ref[...]
    h = (x_ref[...] * sc + sh).reshape(nb * tt, d).astype(bf16)
    r = jnp.dot(h, wm_ref[...], preferred_element_type=f32)
    zs_ref[...] = r[:, :d_s5].reshape(nb, tt, d_s5)
    qkv_ref[...] = r[:, d_s5:d_s5 + 3 * d_gdn].reshape(nb, tt, 3 * d_gdn)
    zg_ref[...] = r[:, d_s5 + 3 * d_gdn:].reshape(nb, tt, d_gdn)
    lg = jnp.dot(h, wg_ref[...], preferred_element_type=f32)
    lane = lax.broadcasted_iota(jnp.int32, lg.shape, 1)
    a = lg + gp_ref[1:2, :]
    softplus = jnp.maximum(a, 0.0) + jnp.log1p(jnp.exp(-jnp.abs(a)))
    gates = jnp.where(lane < n_beta, jax.nn.sigmoid(lg), -jnp.exp(gp_ref[0:1, :]) * softplus)
    gate_ref[...] = gates.reshape(nb, tt, LANES)
    sc2 = sc[:, 0, :]
    sh2 = sh[:, 0, :]
    ht = (jnp.swapaxes(x_ref[...], 0, 1) * sc2[None] + sh2[None]).reshape(tt * nb, d).astype(bf16)
    ru = jnp.dot(ht, wu_ref[...], preferred_element_type=f32)
    for j in range(d_s5 // LANES):
        u_ref[j] = ru[:, j * LANES:(j + 1) * LANES].reshape(tt // S5_T, S5_T, nb, LANES)


def _inproj(x, scale, shift, wu, wm, wg, gp, d_s5, d_gdn):
    nb, L, d = x.shape
    tt = 32
    npk = d_s5 // LANES
    kern = functools.partial(_inproj_kernel, tt=tt, d_s5=d_s5, d_gdn=d_gdn,
                             n_beta=N_DIR * (d_gdn // GDN_HEAD))
    const2 = lambda i: (0, 0)
    const3 = lambda i: (0, 0, 0)
    return pl.pallas_call(
        kern,
        grid=(L // tt,),
        in_specs=[pl.BlockSpec((nb, tt, d), lambda i: (0, i, 0)),
                  pl.BlockSpec((nb, 1, d), const3),
                  pl.BlockSpec((nb, 1, d), const3),
                  pl.BlockSpec(wu.shape, const2),
                  pl.BlockSpec(wm.shape, const2),
                  pl.BlockSpec(wg.shape, const2),
                  pl.BlockSpec(gp.shape, const2)],
        out_specs=[pl.BlockSpec((npk, tt // S5_T, S5_T, nb, LANES), lambda i: (0, i, 0, 0, 0)),
                   pl.BlockSpec((nb, tt, d_s5), lambda i: (0, i, 0)),
                   pl.BlockSpec((nb, tt, 3 * d_gdn), lambda i: (0, i, 0)),
                   pl.BlockSpec((nb, tt, d_gdn), lambda i: (0, i, 0)),
                   pl.BlockSpec((nb, tt, LANES), lambda i: (0, i, 0))],
        out_shape=[jax.ShapeDtypeStruct((npk, L // S5_T, S5_T, nb, LANES), f32),
                   jax.ShapeDtypeStruct((nb, L, d_s5), f32),
                   jax.ShapeDtypeStruct((nb, L, 3 * d_gdn), f32),
                   jax.ShapeDtypeStruct((nb, L, d_gdn), f32),
                   jax.ShapeDtypeStruct((nb, L, LANES), f32)],
        compiler_params=_cparams(("parallel",)),
        name="inproj",
    )(x, scale, shift, wu, wm, wg, gp)


def _s5_kernel(*refs, ct, rev, emit_y, intra):
    it = iter(refs)
    u_ref, cin_ref, at_ref, h0_ref = next(it), next(it), next(it), next(it)
    cout_ref = next(it) if emit_y else None
    cintra_ref = next(it) if intra else None
    d_ref = next(it) if intra else None
    y_ref = next(it) if emit_y else None
    hfin_ref = next(it)
    h_sc, v_sc, hall_sc, min_sc = next(it), next(it), next(it), next(it)
    mout_sc = next(it) if emit_y else None
    mintra_sc = next(it) if intra else None

    t = pl.program_id(1)
    nt = pl.num_programs(1)
    nb = u_ref.shape[3]
    sdim = h_sc.shape[1]
    half = sdim // 2
    gstate = half // PACK_G

    @pl.when(t == 0)
    def _():
        h_sc[...] = h0_ref[0]
        r_in = lax.broadcasted_iota(jnp.int32, (LANES, sdim), 0) // S5_GROUP
        c_in = (lax.broadcasted_iota(jnp.int32, (LANES, sdim), 1) % half) // gstate
        mask_in = (r_in == c_in).astype(f32)
        for s in range(S5_T):
            min_sc[s * LANES:(s + 1) * LANES, :] = (
                jnp.tile(cin_ref[0, s], (PACK_G, 1)) * mask_in).astype(bf16)
        if emit_y:
            r_out = lax.broadcasted_iota(jnp.int32, (half, LANES), 0) // gstate
            c_out = lax.broadcasted_iota(jnp.int32, (half, LANES), 1) // S5_GROUP
            mask_out = (r_out == c_out).astype(f32)
            for i in range(S5_T):
                for ri in range(2):
                    blk = cout_ref[0, i, ri * gstate:(ri + 1) * gstate, :]
                    mout_sc[ri * half:(ri + 1) * half, i * LANES:(i + 1) * LANES] = (
                        jnp.tile(blk, (PACK_G, 1)) * mask_out).astype(bf16)
        if intra:
            r_x = lax.broadcasted_iota(jnp.int32, (LANES, LANES), 0) // S5_GROUP
            c_x = lax.broadcasted_iota(jnp.int32, (LANES, LANES), 1) // S5_GROUP
            mask_x = (r_x == c_x).astype(f32)
            for s in range(S5_T):
                for i in range(S5_T):
                    mintra_sc[s * LANES:(s + 1) * LANES, i * LANES:(i + 1) * LANES] = (
                        jnp.tile(cintra_ref[0, s, i], (PACK_G, 1)) * mask_x).astype(bf16)

    x = jnp.concatenate([u_ref[0, :, i, :, :].reshape(ct * nb, LANES) for i in range(S5_T)], axis=1)
    xb = x.astype(bf16)
    v_sc[...] = jnp.dot(xb, min_sc[...], preferred_element_type=f32)
    are = at_ref[0, :, :half]
    aim = at_ref[0, :, half:]

    def body(s, carry):
        k = (ct - 1 - s) if rev else s
        off = pl.multiple_of(k * nb, nb)
        hre, him = carry
        hall_sc[pl.ds(off, nb), :half] = hre
        hall_sc[pl.ds(off, nb), half:] = him
        vre = v_sc[pl.ds(off, nb), :half]
        vim = v_sc[pl.ds(off, nb), half:]
        return are * hre - aim * him + vre, are * him + aim * hre + vim

    hre, him = lax.fori_loop(0, ct, body, (h_sc[:, :half], h_sc[:, half:]))
    h_sc[:, :half] = hre
    h_sc[:, half:] = him

    @pl.when(t == nt - 1)
    def _():
        hfin_ref[0] = h_sc[...]

    if emit_y:
        y = jnp.dot(hall_sc[...].astype(bf16), mout_sc[...], preferred_element_type=f32)
        if intra:
            y = y + jnp.dot(xb, mintra_sc[...], preferred_element_type=f32) + x * d_ref[0]
        for i in range(S5_T):
            y_ref[0, :, i, :, :] = y[:, i * LANES:(i + 1) * LANES].reshape(ct, nb, LANES)


def _s5_pass(u5, c_in, a_t, h0, c_out=None, c_intra=None, d_skip=None, *, rev, emit_y):
    npk, nc, _, nb, _ = u5.shape
    ct = min(32, nc)
    nt = nc // ct
    intra = c_intra is not None
    kdim = S5_T * LANES
    sdim = a_t.shape[-1]
    tmap = (lambda j, t: (j, nt - 1 - t, 0, 0, 0)) if rev else (lambda j, t: (j, t, 0, 0, 0))
    pmap = lambda j, t: (j, 0, 0)
    pmap4 = lambda j, t: (j, 0, 0, 0)
    pmap5 = lambda j, t: (j, 0, 0, 0, 0)
    args = [u5, c_in, a_t, h0]
    in_specs = [pl.BlockSpec((1, ct, S5_T, nb, LANES), tmap),
                pl.BlockSpec((1,) + c_in.shape[1:], pmap4),
                pl.BlockSpec((1, nb, sdim), pmap),
                pl.BlockSpec((1, nb, sdim), pmap)]
    scratch = [pltpu.VMEM((nb, sdim), f32),
               pltpu.VMEM((ct * nb, sdim), f32),
               pltpu.VMEM((ct * nb, sdim), f32),
               pltpu.VMEM((kdim, sdim), bf16)]
    if emit_y:
        args.append(c_out)
        in_specs.append(pl.BlockSpec((1,) + c_out.shape[1:], pmap4))
        scratch.append(pltpu.VMEM((sdim, kdim), bf16))
    if intra:
        args += [c_intra, d_skip]
        in_specs += [pl.BlockSpec((1,) + c_intra.shape[1:], pmap5), pl.BlockSpec((1, 1, kdim), pmap)]
        scratch.append(pltpu.VMEM((kdim, kdim), bf16))
    out_specs, out_shape = [], []
    if emit_y:
        out_specs.append(pl.BlockSpec((1, ct, S5_T, nb, LANES), tmap))
        out_shape.append(jax.ShapeDtypeStruct(u5.shape, f32))
    out_specs.append(pl.BlockSpec((1, nb, sdim), pmap))
    out_shape.append(jax.ShapeDtypeStruct((npk, nb, sdim), f32))
    kern = functools.partial(_s5_kernel, ct=ct, rev=rev, emit_y=emit_y, intra=intra)
    return pl.pallas_call(
        kern,
        grid=(npk, nt),
        in_specs=in_specs,
        out_specs=out_specs,
        out_shape=out_shape,
        scratch_shapes=scratch,
        compiler_params=_cparams(("parallel", "arbitrary")),
        name="s5_rev" if rev else "s5_fwd",
    )(*args)


def _s5_weights(lam_re, lam_im, log_dt, b_re, b_im, c_re, c_im):
    T = S5_T
    ndir, G, P = lam_re.shape
    npk = G // PACK_G
    dt = jnp.exp(log_dt.astype(f32))[..., None]
    lr, li = lam_re.astype(f32), lam_im.astype(f32)
    zr, zi = lr * dt, li * dt

    nr = jnp.expm1(zr) * jnp.cos(zi) - 2.0 * jnp.square(jnp.sin(0.5 * zi))
    ni = jnp.exp(zr) * jnp.sin(zi)
    den = lr * lr + li * li
    qr = (nr * lr + ni * li) / den
    qi = (ni * lr - nr * li) / den
    bre, bim = b_re.astype(f32), b_im.astype(f32)
    bbr = qr[..., None] * bre - qi[..., None] * bim
    bbi = qr[..., None] * bim + qi[..., None] * bre
    cr, ci = c_re.astype(f32), c_im.astype(f32)

    def powers(e, d):
        ev = jnp.asarray(np.asarray(e, np.float32))[:, None, None]
        m = jnp.exp(ev * zr[d])
        return m * jnp.cos(ev * zi[d]), m * jnp.sin(ev * zi[d])

    def c_times_a(e, d):
        pr, pi = powers(e, d)
        pr, pi = pr[:, :, None, :], pi[:, :, None, :]
        return cr[d] * pr - ci[d] * pi, cr[d] * pi + ci[d] * pr

    steps = np.arange(T)

    def pack_in(e, d):
        pr, pi = powers(e, d)
        pr, pi = pr[..., None], pi[..., None]
        re = pr * bbr[d] - pi * bbi[d]
        im = pr * bbi[d] + pi * bbr[d]
        base = jnp.stack([re, im]).reshape(2, T, npk, PACK_G, P, S5_GROUP)
        return base.transpose(2, 1, 5, 0, 3, 4).reshape(npk, T, S5_GROUP, 2 * PACK_G * P)

    def pack_out(e, d):
        car, cai = c_times_a(e, d)
        base = jnp.stack([car, -cai]).reshape(2, T, npk, PACK_G, S5_GROUP, P)
        return base.transpose(2, 1, 0, 5, 3, 4).reshape(npk, T, 2 * P, LANES)

    m_in_f = pack_in(T - 1 - steps, 0)
    m_in_r = pack_in(steps, 1)
    m_out_f = pack_out(steps + 1, 0)
    m_out_r = pack_out(T - steps, 1)

    lag = steps[None, :] - steps[:, None]

    def lag_term(d, sign):
        car, cai = c_times_a(np.abs(lag).reshape(-1), d)
        mask = jnp.asarray((sign * lag >= 0).reshape(-1, 1, 1, 1), f32)
        k = (jnp.einsum('tgop,gpi->tgio', car * mask, bbr[d])
             - jnp.einsum('tgop,gpi->tgio', cai * mask, bbi[d]))
        return k.reshape(T, T, G, S5_GROUP, S5_GROUP)

    toe = lag_term(0, 1) + lag_term(1, -1)
    toe = toe.reshape(T, T, npk, PACK_G, S5_GROUP, S5_GROUP).transpose(2, 0, 1, 4, 3, 5)
    m_intra = toe.reshape(npk, T, T, S5_GROUP, LANES)

    def a_chunk(d):
        pr, pi = powers([T], d)
        return jnp.concatenate([pr.reshape(npk, PACK_G * P), pi.reshape(npk, PACK_G * P)], axis=-1)

    return dict(m_in_f=m_in_f, m_in_r=m_in_r, m_out_f=m_out_f, m_out_r=m_out_r,
                m_intra=m_intra, a_f=a_chunk(0), a_r=a_chunk(1))


def _gdn_prep_kernel(*refs, tile, rows_conv, d_gdn):
    if rows_conv:
        main_ref, up_ref, dn_ref, w_ref, q_ref, k_ref, v_ref = refs
    else:
        main_ref, w_ref, q_ref, k_ref, v_ref = refs
    w = w_ref[...]
    if rows_conv:
        i = pl.program_id(1)
        n = pl.num_programs(1)
        up = up_ref[0] * (i > 0).astype(f32)
        dn = dn_ref[0] * (i < n - 1).astype(f32)
        ext = jnp.concatenate([up, main_ref[0], dn], axis=0)
        base = GRID_W
        drs = (-1, 0, 1)
        col = lax.broadcasted_iota(jnp.int32, (tile, 1), 0) % GRID_W
        mask_m = col != 0
        mask_p = col != GRID_W - 1
    else:
        ext = main_ref[0]
        base = 0
        drs = (0,)
        col = lax.broadcasted_iota(jnp.int32, (tile, 1), 0)
        mask_m = col != 0
        mask_p = col != tile - 1
    n_ext = ext.shape[0]
    shifted = {-1: pltpu.roll(ext, 1, 0), 0: ext, 1: pltpu.roll(ext, n_ext - 1, 0)}
    acc = {}
    for dc in (-1, 0, 1):
        a = None
        for dr in drs:
            tap = w[(dr + 1) * 3 + (dc + 1)][None, :]
            term = shifted[dc][base + GRID_W * dr: base + GRID_W * dr + tile] * tap
            a = term if a is None else a + term
        acc[dc] = a
    conv = acc[0] + jnp.where(mask_m, acc[-1], 0.0) + jnp.where(mask_p, acc[1], 0.0)
    s = _silu(conv)
    nh = d_gdn // GDN_HEAD
    for h in range(nh):
        qh = s[:, h * GDN_HEAD:(h + 1) * GDN_HEAD]
        kh = s[:, d_gdn + h * GDN_HEAD: d_gdn + (h + 1) * GDN_HEAD]
        qn = qh * lax.rsqrt(jnp.sum(qh * qh, axis=-1, keepdims=True) + NORM_EPS) * (GDN_HEAD ** -0.5)
        kn = kh * lax.rsqrt(jnp.sum(kh * kh, axis=-1, keepdims=True) + NORM_EPS)
        q_ref[0, :, h * GDN_HEAD:(h + 1) * GDN_HEAD] = qn.astype(bf16)
        k_ref[0, :, h * GDN_HEAD:(h + 1) * GDN_HEAD] = kn.astype(bf16)
    v_ref[0] = s[:, 2 * d_gdn:].astype(bf16)


def _gdn_prep(qkv, conv_w, rows_conv):
    nb, L, c3 = qkv.shape
    d_gdn = c3 // 3
    w9 = conv_w.reshape(9, c3).astype(f32)
    if rows_conv:
        tile = min(512, L)
        nt = L // tile
        r = tile // GRID_W
        nrows = L // GRID_W
        in_specs = [pl.BlockSpec((1, tile, c3), lambda b, i: (b, i, 0)),
                    pl.BlockSpec((1, GRID_W, c3), lambda b, i: (b, jnp.maximum(i * r - 1, 0), 0)),
                    pl.BlockSpec((1, GRID_W, c3), lambda b, i: (b, jnp.minimum((i + 1) * r, nrows - 1), 0)),
                    pl.BlockSpec((9, c3), lambda b, i: (0, 0))]
        args = (qkv, qkv, qkv, w9)
    else:
        tile, nt = L, 1
        in_specs = [pl.BlockSpec((1, tile, c3), lambda b, i: (b, i, 0)),
                    pl.BlockSpec((9, c3), lambda b, i: (0, 0))]
        args = (qkv, w9)
    kern = functools.partial(_gdn_prep_kernel, tile=tile, rows_conv=rows_conv, d_gdn=d_gdn)
    o_spec = pl.BlockSpec((1, tile, d_gdn), lambda b, i: (b, i, 0))
    o_shape = jax.ShapeDtypeStruct((nb, L, d_gdn), bf16)
    return pl.pallas_call(
        kern,
        grid=(nb, nt),
        in_specs=in_specs,
        out_specs=[o_spec, o_spec, o_spec],
        out_shape=[o_shape, o_shape, o_shape],
        compiler_params=_cparams(("parallel", "parallel")),
        name="gdn_prep",
    )(*args)


def _chunk_cumsum(g, rev):
    n = g.shape[0]
    r = lax.broadcasted_iota(jnp.int32, g.shape, 0)
    s = 1
    while s < n:
        if rev:
            g = g + jnp.where(r < n - s, pltpu.roll(g, n - s, 0), 0.0)
        else:
            g = g + jnp.where(r >= s, pltpu.roll(g, s, 0), 0.0)
        s *= 2
    return g


def _tri_solve_rows(apl_sc, tl_sc, dirn, upper):
    nblk = GDN_CHUNK // 8
    base = dirn * GDN_CHUNK
    sub = lax.broadcasted_iota(jnp.int32, (8, LANES), 0)
    for step_blk in range(nblk):
        blk = (nblk - 1 - step_blk) if upper else step_blk
        mblocks = range(blk, nblk) if upper else range(0, blk + 1)

        def solve_row(s, carry, blk=blk, mblocks=mblocks):
            ii = (7 - s) if upper else s
            i = blk * 8 + ii
            acc = {}
            for mb in mblocks:
                arow = apl_sc[i, base + mb * 8:base + (mb + 1) * 8, :]
                jblocks = range(mb, nblk) if upper else range(0, mb + 1)
                for mm in range(8):
                    ab = jnp.broadcast_to(arow[mm:mm + 1, :], (8, LANES))
                    for jb in jblocks:
                        term = ab * tl_sc[mb * 8 + mm, base + jb * 8:base + (jb + 1) * 8, :]
                        key = (jb, mm % 2)
                        acc[key] = term if key not in acc else acc[key] + term
            for jb in (range(blk, nblk) if upper else range(0, blk + 1)):
                tot = acc[(jb, 0)] + acc[(jb, 1)]
                unit_row = jnp.where(sub == ii, 1.0, 0.0) if jb == blk else 0.0
                tl_sc[i, base + jb * 8:base + (jb + 1) * 8, :] = unit_row - tot
            return carry

        lax.fori_loop(0, 8, solve_row, 0)


def _gdn_scan_kernel(*refs, nchunks, nh, emit_o, has_s0):
    it = iter(refs)
    q_ref, k_ref, v_ref, g_ref = next(it), next(it), next(it), next(it)
    s0_ref = next(it) if has_s0 else None
    o_ref = next(it) if emit_o else None
    sfin_ref = next(it)
    s_sc, at_sc, apl_sc, tl_sc, mk_sc, w_sc, gcol_sc, grow_sc = (next(it) for _ in range(8))
    C = GDN_CHUNK
    bb = q_ref.shape[0]
    nunits = bb * nchunks
    s_sc[...] = s0_ref[...] if has_s0 else jnp.zeros(s_sc.shape, f32)
    if emit_o:
        o_ref[...] = jnp.zeros(o_ref.shape, f32)
    row = lax.broadcasted_iota(jnp.int32, (C, N_DIR * C), 0)
    lane = lax.broadcasted_iota(jnp.int32, (C, N_DIR * C), 1)
    fwd_half = lane < C
    col = jnp.where(fwd_half, lane, lane - C)
    tri = (fwd_half & (row >= col)) | (~fwd_half & (row <= col))
    strict = (fwd_half & (row > col)) | (~fwd_half & (row < col))
    nbeta = N_DIR * nh
    gate_lane = lax.broadcasted_iota(jnp.int32, (C, LANES), 1)

    def unit_of(unit):
        bl = unit // nchunks
        return bl, pl.multiple_of((unit % nchunks) * C, C)

    def phase_a(unit, carry):
        bl, off = unit_of(unit)
        gts = g_ref[bl, pl.ds(off, C), :]
        gmix = jnp.where(gate_lane < nbeta, gts,
                         jnp.where(gate_lane < nbeta + nh, _chunk_cumsum(gts, False), _chunk_cumsum(gts, True)))
        g_rows = gmix.T[:2 * nbeta]
        gcol_sc[unit] = gmix
        grow_sc[unit] = g_rows
        for h in range(nh):
            hs = slice(h * GDN_HEAD, (h + 1) * GDN_HEAD)
            cbf, cbr = h, nh + h
            cgf, cgr = nbeta + h, nbeta + nh + h
            kb = k_ref[bl, pl.ds(off, C), hs]
            k2 = jnp.concatenate([kb, kb], axis=0)
            if emit_o:
                kq = jnp.concatenate([kb, q_ref[bl, pl.ds(off, C), hs]], axis=0)
            else:
                kq = kb
            kkqk = lax.dot_general(kq, k2, (((1,), (1,)), ((), ())), preferred_element_type=f32)
            g_col = jnp.where(fwd_half, gmix[:, cgf:cgf + 1], gmix[:, cgr:cgr + 1])
            g_row = jnp.concatenate([g_rows[cgf:cgf + 1, :], g_rows[cgr:cgr + 1, :]], axis=1)
            b_row = jnp.concatenate([g_rows[cbf:cbf + 1, :], g_rows[cbr:cbr + 1, :]], axis=1)
            decay_b = jnp.exp(jnp.where(tri, g_col - g_row, -1e30)) * b_row
            at_sc[unit * nh + h] = jnp.where(strict, kkqk[:C] * decay_b, 0.0)
            if emit_o:
                mk_sc[unit * nh + h, :C, :] = (kkqk[C:] * decay_b).astype(bf16)
        return carry

    lax.fori_loop(0, nunits, phase_a, 0, unroll=2)

    def swap_in(ib, carry):
        rows = pl.ds(pl.multiple_of(ib * 8, 8), 8)
        tl_sc[rows] = jnp.swapaxes(at_sc[:, rows, :], 0, 1)
        return carry

    def to_lanes(i, carry):
        apl_sc[i] = tl_sc[i].T
        return carry

    lax.fori_loop(0, C // 8, swap_in, 0)
    lax.fori_loop(0, C, to_lanes, 0, unroll=4)
    tl_sc[...] = jnp.zeros(tl_sc.shape, f32)
    _tri_solve_rows(apl_sc, tl_sc, 0, False)
    _tri_solve_rows(apl_sc, tl_sc, 1, True)

    def from_lanes(i, carry):
        apl_sc[i] = tl_sc[i].T
        return carry

    def swap_out(ib, carry):
        rows = pl.ds(pl.multiple_of(ib * 8, 8), 8)
        at_sc[:, rows, :] = jnp.swapaxes(apl_sc[rows], 0, 1)
        return carry

    lax.fori_loop(0, C, from_lanes, 0, unroll=4)
    lax.fori_loop(0, C // 8, swap_out, 0)

    u0_sc = (apl_sc, tl_sc)

    def u0_slot(p):
        return p // 2, pl.ds(pl.multiple_of((p % 2) * C, C), C)

    eye_bf = (lax.broadcasted_iota(jnp.int32, (GDN_HEAD, GDN_HEAD), 0)
              == lax.broadcasted_iota(jnp.int32, (GDN_HEAD, GDN_HEAD), 1)).astype(bf16)

    def phase_b2(unit, carry):
        bl, off = unit_of(unit)
        g_rows = grow_sc[unit]
        g_cols = gcol_sc[unit]
        zero_kv = jnp.zeros((C, GDN_HEAD), bf16)
        for h in range(nh):
            hs = slice(h * GDN_HEAD, (h + 1) * GDN_HEAD)
            p = unit * nh + h
            cbf, cbr = h, nh + h
            cgf, cgr = nbeta + h, nbeta + nh + h
            t_both = at_sc[p]
            kb = k_ref[bl, pl.ds(off, C), hs]
            vb = v_ref[bl, pl.ds(off, C), hs]
            g_row = jnp.concatenate([g_rows[cgf:cgf + 1, :], g_rows[cgr:cgr + 1, :]], axis=1)
            b_row = jnp.concatenate([g_rows[cbf:cbf + 1, :], g_rows[cbr:cbr + 1, :]], axis=1)
            g_last = jnp.concatenate([jnp.broadcast_to(g_cols[C - 1:C, cgf:cgf + 1], (1, C)),
                                      jnp.broadcast_to(g_cols[0:1, cgr:cgr + 1], (1, C))], axis=1)
            v_bd = jnp.concatenate([jnp.concatenate([vb, zero_kv], axis=1),
                                    jnp.concatenate([zero_kv, vb], axis=1)], axis=0)
            k_bd = jnp.concatenate([jnp.concatenate([kb, zero_kv], axis=1),
                                    jnp.concatenate([zero_kv, kb], axis=1)], axis=0)
            u0 = jnp.dot(t_both.astype(bf16), v_bd, preferred_element_type=f32)
            w = jnp.dot((t_both * jnp.exp(g_row)).astype(bf16), k_bd, preferred_element_type=f32)
            slab, rows = u0_slot(p)
            for dirn in range(N_DIR):
                u0_sc[dirn][slab, rows, :] = u0[:, dirn * GDN_HEAD:(dirn + 1) * GDN_HEAD]
                w_sc[dirn, p] = w[:, dirn * GDN_HEAD:(dirn + 1) * GDN_HEAD].astype(bf16)
            k_t = lax.dot_general(eye_bf, jnp.concatenate([kb, kb], axis=0), (((1,), (1,)), ((), ())),
                                  preferred_element_type=f32)
            mk_sc[p, C:, :] = (k_t * (b_row * jnp.exp(g_last - g_row))).astype(bf16)
        return carry

    lax.fori_loop(0, nunits, phase_b2, 0, unroll=4)

    def phase_c(step, carry):
        chains = []
        for dirn, unit in ((0, step), (1, nunits - 1 - step)):
            bl, off = unit_of(unit)
            g_col_all = gcol_sc[unit]
            g_tot_all = g_col_all[0:1, :] if dirn else g_col_all[C - 1:C, :]
            for h in range(nh):
                cg = nbeta + dirn * nh + h
                hs = slice(h * GDN_HEAD, (h + 1) * GDN_HEAD)
                p = unit * nh + h
                slab, rows = u0_slot(p)
                lhs = w_sc[dirn, p]
                ch = dict(idx=(bl, dirn, h), off=off, hs=hs, dirn=dirn, u0=u0_sc[dirn][slab, rows, :],
                          g_last=jnp.exp(g_tot_all[:, cg:cg + 1]))
                if emit_o:
                    lhs = jnp.concatenate([lhs, q_ref[bl, pl.ds(off, C), hs]], axis=0)
                    ch['mk'] = mk_sc[p]
                    ch['gam'] = jnp.exp(g_col_all[:, cg:cg + 1])
                else:
                    ch['mk'] = mk_sc[p, C:, :]
                ch['lhs'] = lhs
                chains.append(ch)
        s_old = [s_sc[ch['idx']] for ch in chains]
        r = [jnp.dot(ch['lhs'], s.astype(bf16), preferred_element_type=f32) for ch, s in zip(chains, s_old)]
        zeros = jnp.zeros((C, GDN_HEAD), bf16)
        ub = [(ch['u0'] - ri[:C]).astype(bf16) for ch, ri in zip(chains, r)]
        m = [jnp.dot(ch['mk'], jnp.concatenate([zeros, ubi] if ch['dirn'] else [ubi, zeros], axis=0),
                     preferred_element_type=f32) for ch, ubi in zip(chains, ub)]
        for n, ch in enumerate(chains):
            if emit_o:
                bl = ch['idx'][0]
                o_ref[bl, pl.ds(ch['off'], C), ch['hs']] += ch['gam'] * r[n][C:] + m[n][:C]
                s_sc[ch['idx']] = ch['g_last'] * s_old[n] + m[n][C:]
            else:
                s_sc[ch['idx']] = ch['g_last'] * s_old[n] + m[n]
        return carry

    lax.fori_loop(0, nunits, phase_c, 0)
    sfin_ref[...] = s_sc[...]


def _gdn_scan(q, k, v, gates, s0, *, emit_o):
    nb, L, d_gdn = q.shape
    nh = d_gdn // GDN_HEAD
    nchunks = L // GDN_CHUNK
    C = GDN_CHUNK
    bb = max(1, min(nb, LANES // (nchunks * nh)))
    nunits = bb * nchunks
    nprob = nunits * nh
    assert nprob == LANES, (nb, L, nh)
    has_s0 = s0 is not None
    kern = functools.partial(_gdn_scan_kernel, nchunks=nchunks, nh=nh, emit_o=emit_o, has_s0=has_s0)
    seq = lambda b: (b, 0, 0)
    st = lambda b: (b, 0, 0, 0, 0)
    st_spec = pl.BlockSpec((bb, N_DIR, nh, GDN_HEAD, GDN_HEAD), st)
    st_shape = jax.ShapeDtypeStruct((nb, N_DIR, nh, GDN_HEAD, GDN_HEAD), f32)
    args = [q, k, v, gates]
    in_specs = [pl.BlockSpec((bb, L, d_gdn), seq)] * 3 + [pl.BlockSpec((bb, L, LANES), seq)]
    if has_s0:
        args.append(s0)
        in_specs.append(st_spec)
    out_specs, out_shape = [], []
    if emit_o:
        out_specs.append(pl.BlockSpec((bb, L, d_gdn), seq))
        out_shape.append(jax.ShapeDtypeStruct((nb, L, d_gdn), f32))
    out_specs.append(st_spec)
    out_shape.append(st_shape)
    return pl.pallas_call(
        kern,
        grid=(nb // bb,),
        in_specs=in_specs,
        out_specs=out_specs,
        out_shape=out_shape,
        scratch_shapes=[pltpu.VMEM((bb, N_DIR, nh, GDN_HEAD, GDN_HEAD), f32),
                        pltpu.VMEM((nprob, C, N_DIR * C), f32),
                        pltpu.VMEM((C, N_DIR * C, nprob), f32),
                        pltpu.VMEM((C, N_DIR * C, nprob), f32),
                        pltpu.VMEM((nprob, C + GDN_HEAD, N_DIR * C), bf16),
                        pltpu.VMEM((N_DIR, nprob, C, GDN_HEAD), bf16),
                        pltpu.VMEM((nunits, C, LANES), f32),
                        pltpu.VMEM((nunits, 4 * nh, C), f32)],
        compiler_params=_cparams(("parallel",)),
        name="gdn_scan" if emit_o else "gdn_scan_state",
    )(*args)


def _post_kernel(x_ref, gate_ref, yf_ref, yr_ref, zs_ref, og_ref, zg_ref,
                 wglu_ref, bglu_ref, nw_ref, wout_ref, lng_ref, lnb_ref, o_ref, *, tt, alpha):
    nb, _, d = x_ref.shape
    rows = nb * tt
    npk = yf_ref.shape[0]
    d_s5 = npk * LANES
    d_gdn = og_ref.shape[2]
    parts = []
    for j in range(npk):
        ysum = (yf_ref[j] + yr_ref[j]).reshape(tt, nb, LANES)
        parts.append(jnp.swapaxes(ysum, 0, 1).reshape(rows, LANES))
    y = jnp.concatenate(parts, axis=1)
    g = 0.5 * y * (1.0 + lax.erf(y * (2.0 ** -0.5)))
    glu = jnp.dot(g.astype(bf16), wglu_ref[...], preferred_element_type=f32) + bglu_ref[...]
    s5 = g * jax.nn.sigmoid(glu) * _silu(zs_ref[...].reshape(rows, d_s5))
    o = og_ref[...].reshape(rows, d_gdn)
    zg = zg_ref[...].reshape(rows, d_gdn)
    gd = []
    for h in range(d_gdn // GDN_HEAD):
        oh = o[:, h * GDN_HEAD:(h + 1) * GDN_HEAD]
        gd.append(oh * lax.rsqrt(jnp.mean(oh * oh, axis=-1, keepdims=True) + NORM_EPS) * nw_ref[...])
    gdn = jnp.concatenate(gd, axis=1) * _silu(zg)
    mix = jnp.concatenate([s5, gdn], axis=1).astype(bf16)
    yv = jnp.dot(mix, wout_ref[...], preferred_element_type=f32).reshape(nb, tt, d)
    r = alpha * x_ref[...] + gate_ref[...] * yv
    mu = jnp.mean(r, axis=-1, keepdims=True)
    var = jnp.mean(jnp.square(r - mu), axis=-1, keepdims=True)
    o_ref[...] = (r - mu) * lax.rsqrt(var + LN_EPS) * lng_ref[...] + lnb_ref[...]


def _post(x, gate, yf, yr, zs, o_gdn, zg, wglu, bglu, nw, wout, lng, lnb, alpha):
    nb, L, d = x.shape
    tt = 32
    npk = yf.shape[0]
    d_s5 = npk * LANES
    d_gdn = o_gdn.shape[2]
    kern = functools.partial(_post_kernel, tt=tt, alpha=alpha)
    tok = lambda i: (0, i, 0)
    c2 = lambda i: (0, 0)
    c3 = lambda i: (0, 0, 0)
    y_spec = pl.BlockSpec((npk, tt // S5_T, S5_T, nb, LANES), lambda i: (0, i, 0, 0, 0))
    return pl.pallas_call(
        kern,
        grid=(L // tt,),
        in_specs=[pl.BlockSpec((nb, tt, d), tok),
                  pl.BlockSpec((nb, 1, d), c3),
                  y_spec, y_spec,
                  pl.BlockSpec((nb, tt, d_s5), tok),
                  pl.BlockSpec((nb, tt, d_gdn), tok),
                  pl.BlockSpec((nb, tt, d_gdn), tok),
                  pl.BlockSpec(wglu.shape, c2),
                  pl.BlockSpec(bglu.shape, c2),
                  pl.BlockSpec(nw.shape, c2),
                  pl.BlockSpec(wout.shape, c2),
                  pl.BlockSpec(lng.shape, c3),
                  pl.BlockSpec(lnb.shape, c3)],
        out_specs=pl.BlockSpec((nb, tt, d), tok),
        out_shape=jax.ShapeDtypeStruct((nb, L, d), f32),
        compiler_params=_cparams(("parallel",)),
        name="post",
    )(x, gate, yf, yr, zs, o_gdn, zg, wglu, bglu, nw, wout, lng, lnb)


def _layer(x, c, ctx, c_ctx, p, alpha):
    nb, L, d = x.shape
    d_s5 = p['s5_d'].shape[0]
    d_gdn = p['conv_w'].shape[-1] // 3
    nh = d_gdn // GDN_HEAD
    npk = d_s5 // LANES

    pad = (-(nb + 1)) % 8
    cond = jnp.concatenate([c, c_ctx[None, :], jnp.zeros((pad, d), f32)], axis=0)
    m = _ada(cond, p['w_ada'], p['b_ada'])
    shift, scale, gate = m[:nb, :d], m[:nb, d:2 * d], m[:nb, 2 * d:]
    shift_c = jnp.broadcast_to(m[nb:nb + 1, :d], (nb, d))
    scale_c = jnp.broadcast_to(m[nb:nb + 1, d:2 * d], (nb, d))

    w_in = p['w_in']
    wu = w_in[:, :d_s5].astype(bf16)
    wm = w_in[:, d_s5:2 * d_s5 + 4 * d_gdn].astype(bf16)
    ng = 2 * N_DIR * nh
    wg = jnp.pad(w_in[:, 2 * d_s5 + 4 * d_gdn:], ((0, 0), (0, LANES - ng))).astype(bf16)
    gp = jnp.pad(jnp.stack([p['gdn_a_log'].reshape(-1), p['gdn_dt_bias'].reshape(-1)]).astype(f32),
                 ((0, 0), (ng // 2, LANES - ng)))

    u5, zs, qkv, zg, gates = _inproj(x, scale[:, None, :], shift[:, None, :], wu, wm, wg, gp, d_s5, d_gdn)
    u5c, _, qkvc, _, gatesc = _inproj(ctx, scale_c[:, None, :], shift_c[:, None, :], wu, wm, wg, gp, d_s5, d_gdn)

    sw = _s5_weights(p['s5_lambda_re'], p['s5_lambda_im'], p['s5_log_dt'],
                     p['s5_b_re'], p['s5_b_im'], p['s5_c_re'], p['s5_c_im'])
    sdim = sw['a_f'].shape[-1]
    a_f = jnp.broadcast_to(sw['a_f'][:, None, :], (npk, nb, sdim))
    a_r = jnp.broadcast_to(sw['a_r'][:, None, :], (npk, nb, sdim))
    h_zero = jnp.zeros((npk, nb, sdim), f32)
    d_skip = jnp.tile(p['s5_d'].astype(f32).reshape(npk, 1, LANES), (1, 1, S5_T))
    (hc_f,) = _s5_pass(u5c, sw['m_in_f'], a_f, h_zero, rev=False, emit_y=False)
    (hc_r,) = _s5_pass(u5c, sw['m_in_r'], a_r, h_zero, rev=True, emit_y=False)
    y_f, _ = _s5_pass(u5, sw['m_in_f'], a_f, hc_f, sw['m_out_f'], sw['m_intra'], d_skip, rev=False, emit_y=True)
    y_r, _ = _s5_pass(u5, sw['m_in_r'], a_r, hc_r, sw['m_out_r'], rev=True, emit_y=True)

    ql, kl, vl = _gdn_prep(qkv, p['conv_w'], True)
    qc, kc, vc = _gdn_prep(qkvc, p['conv_w'], False)
    (s_ctx,) = _gdn_scan(qc, kc, vc, gatesc, None, emit_o=False)
    o_gdn, _ = _gdn_scan(ql, kl, vl, gates, s_ctx, emit_o=True)

    return _post(x, gate[:, None, :], y_f, y_r, zs, o_gdn, zg,
                 p['w_glu'].astype(bf16), p['b_glu'].astype(f32).reshape(1, d_s5),
                 p['gdn_norm_w'].astype(f32).reshape(1, GDN_HEAD),
                 p['w_out'].astype(bf16), p['ln_g'].astype(f32).reshape(1, 1, d),
                 p['ln_b'].astype(f32).reshape(1, 1, d), alpha)


def kernel(x, c, ctx, c_ctx, w_ada, b_ada, w_in, s5_lambda_re, s5_lambda_im, s5_log_dt, s5_b_re, s5_b_im, s5_c_re, s5_c_im, s5_d, w_glu, b_glu, conv_w, gdn_a_log, gdn_dt_bias, gdn_norm_w, w_out, ln_g, ln_b):
    depth = w_ada.shape[0]
    assert depth == 1, "context-token outputs are only skipped for the last layer"
    alpha = (2.0 * depth) ** 0.25
    p = {
        'w_ada': w_ada[0], 'b_ada': b_ada[0], 'w_in': w_in[0],
        's5_lambda_re': s5_lambda_re[0], 's5_lambda_im': s5_lambda_im[0], 's5_log_dt': s5_log_dt[0],
        's5_b_re': s5_b_re[0], 's5_b_im': s5_b_im[0], 's5_c_re': s5_c_re[0], 's5_c_im': s5_c_im[0],
        's5_d': s5_d[0], 'w_glu': w_glu[0], 'b_glu': b_glu[0], 'conv_w': conv_w[0],
        'gdn_a_log': gdn_a_log[0], 'gdn_dt_bias': gdn_dt_bias[0], 'gdn_norm_w': gdn_norm_w[0],
        'w_out': w_out[0], 'ln_g': ln_g[0], 'ln_b': ln_b[0],
    }
    return _layer(x, c, ctx, c_ctx, p, alpha)
```

```python
import functools
import math

import numpy as np
import jax
import jax.numpy as jnp
from jax import lax
from jax.experimental import pallas as pl
from jax.experimental.pallas import tpu as pltpu

f32 = jnp.float32
bf16 = jnp.bfloat16
HI = lax.Precision.HIGHEST

LANES = 128
S5_GROUP = 16
S5_STATE = 64
S5_T = 8
PACK_G = LANES // S5_GROUP
GDN_HEAD = 128
GDN_CHUNK = 64
GRID_W = 64
N_DIR = 2
LN_EPS = 1e-5
NORM_EPS = 1e-6
VMEM_LIMIT = 56 * 1024 * 1024


def _cparams(sem):
    return pltpu.CompilerParams(dimension_semantics=sem, vmem_limit_bytes=VMEM_LIMIT)


def _sigmoid(x):
    return 0.5 + 0.5 * jnp.tanh(0.5 * x)


def _silu(x):
    h = 0.5 * x
    return h + h * jnp.tanh(h)


def _ada_kernel(c_ref, w_ref, b_ref, o_ref):
    o_ref[...] = jnp.dot(_silu(c_ref[...]), w_ref[...], precision=HI,
                         preferred_element_type=f32) + b_ref[...]


def _ada(cond, w_ada, b_ada):
    rows, d = cond.shape
    n = w_ada.shape[1]
    tn = 512
    return pl.pallas_call(
        _ada_kernel,
        grid=(n // tn,),
        in_specs=[pl.BlockSpec((rows, d), lambda i: (0, 0)),
                  pl.BlockSpec((d, tn), lambda i: (0, i)),
                  pl.BlockSpec((1, tn), lambda i: (0, i))],
        out_specs=pl.BlockSpec((rows, tn), lambda i: (0, i)),
        out_shape=jax.ShapeDtypeStruct((rows, n), f32),
        compiler_params=_cparams(("parallel",)),
        name="ada",
    )(cond, w_ada, b_ada.reshape(1, n))


def _inproj_kernel(x_ref, scale_ref, shift_ref, wu_ref, wm_ref, wg_ref, gp_ref,
                   u_ref, zs_ref, qkv_ref, zg_ref, gate_ref, *, tt, d_s5, d_gdn, n_beta):
    nb, _, d = x_ref.shape
    sc = 1.0 + scale_ref[...]
    sh = shift_ref[...]
    h = (x_ref[...] * sc + sh).reshape(nb * tt, d).astype(bf16)
    r = jnp.dot(h, wm_ref[...], preferred_element_type=f32)
    zs_ref[...] = r[:, :d_s5].reshape(nb, tt, d_s5)
    qkv_ref[...] = r[:, d_s5:d_s5 + 3 * d_gdn].reshape(nb, tt, 3 * d_gdn)
    zg_ref[...] = r[:, d_s5 + 3 * d_gdn:].reshape(nb, tt, d_gdn)
    lg = jnp.dot(h, wg_ref[...], preferred_element_type=f32)
    lane = lax.broadcasted_iota(jnp.int32, lg.shape, 1)
    a = lg + gp_ref[1:2, :]
    softplus = jnp.maximum(a, 0.0) + jnp.log1p(jnp.exp(-jnp.abs(a)))
    gates = jnp.where(lane < n_beta, _sigmoid(lg), -jnp.exp(gp_ref[0:1, :]) * softplus)
    gate_ref[...] = gates.reshape(nb, tt, LANES)
    sc2 = sc[:, 0, :]
    sh2 = sh[:, 0, :]
    ht = (jnp.swapaxes(x_ref[...], 0, 1) * sc2[None] + sh2[None]).reshape(tt * nb, d).astype(bf16)
    ru = jnp.dot(ht, wu_ref[...], preferred_element_type=f32)
    for j in range(d_s5 // LANES):
        u_ref[j] = ru[:, j * LANES:(j + 1) * LANES].reshape(tt // S5_T, S5_T, nb, LANES)


def _inproj(x, scale, shift, wu, wm, wg, gp, d_s5, d_gdn):
    nb, L, d = x.shape
    tt = 32
    npk = d_s5 // LANES
    kern = functools.partial(_inproj_kernel, tt=tt, d_s5=d_s5, d_gdn=d_gdn,
                             n_beta=N_DIR * (d_gdn // GDN_HEAD))
    const2 = lambda i: (0, 0)
    const3 = lambda i: (0, 0, 0)
    return pl.pallas_call(
        kern,
        grid=(L // tt,),
        in_specs=[pl.BlockSpec((nb, tt, d), lambda i: (0, i, 0)),
                  pl.BlockSpec((nb, 1, d), const3),
                  pl.BlockSpec((nb, 1, d), const3),
                  pl.BlockSpec(wu.shape, const2),
                  pl.BlockSpec(wm.shape, const2),
                  pl.BlockSpec(wg.shape, const2),
                  pl.BlockSpec(gp.shape, const2)],
        out_specs=[pl.BlockSpec((npk, tt // S5_T, S5_T, nb, LANES), lambda i: (0, i, 0, 0, 0)),
                   pl.BlockSpec((nb, tt, d_s5), lambda i: (0, i, 0)),
                   pl.BlockSpec((nb, tt, 3 * d_gdn), lambda i: (0, i, 0)),
                   pl.BlockSpec((nb, tt, d_gdn), lambda i: (0, i, 0)),
                   pl.BlockSpec((nb, tt, LANES), lambda i: (0, i, 0))],
        out_shape=[jax.ShapeDtypeStruct((npk, L // S5_T, S5_T, nb, LANES), f32),
                   jax.ShapeDtypeStruct((nb, L, d_s5), f32),
                   jax.ShapeDtypeStruct((nb, L, 3 * d_gdn), f32),
                   jax.ShapeDtypeStruct((nb, L, d_gdn), f32),
                   jax.ShapeDtypeStruct((nb, L, LANES), f32)],
        compiler_params=_cparams(("parallel",)),
        name="inproj",
    )(x, scale, shift, wu, wm, wg, gp)


def _s5_kernel(*refs, ct, rev, emit_y, intra):
    it = iter(refs)
    u_ref, cin_ref, at_ref, h0_ref = next(it), next(it), next(it), next(it)
    cout_ref = next(it) if emit_y else None
    cintra_ref = next(it) if intra else None
    d_ref = next(it) if intra else None
    y_ref = next(it) if emit_y else None
    hfin_ref = next(it)
    h_sc, v_sc, hall_sc, min_sc = next(it), next(it), next(it), next(it)
    mout_sc = next(it) if emit_y else None
    mintra_sc = next(it) if intra else None

    t = pl.program_id(1)
    nt = pl.num_programs(1)
    nb = u_ref.shape[3]
    sdim = h_sc.shape[1]
    half = sdim // 2
    gstate = half // PACK_G

    @pl.when(t == 0)
    def _():
        h_sc[...] = h0_ref[0]
        r_in = lax.broadcasted_iota(jnp.int32, (LANES, sdim), 0) // S5_GROUP
        c_in = (lax.broadcasted_iota(jnp.int32, (LANES, sdim), 1) % half) // gstate
        mask_in = (r_in == c_in).astype(f32)
        for s in range(S5_T):
            min_sc[s * LANES:(s + 1) * LANES, :] = (
                jnp.tile(cin_ref[0, s], (PACK_G, 1)) * mask_in).astype(bf16)
        if emit_y:
            r_out = lax.broadcasted_iota(jnp.int32, (half, LANES), 0) // gstate
            c_out = lax.broadcasted_iota(jnp.int32, (half, LANES), 1) // S5_GROUP
            mask_out = (r_out == c_out).astype(f32)
            for i in range(S5_T):
                for ri in range(2):
                    blk = cout_ref[0, i, ri * gstate:(ri + 1) * gstate, :]
                    mout_sc[ri * half:(ri + 1) * half, i * LANES:(i + 1) * LANES] = (
                        jnp.tile(blk, (PACK_G, 1)) * mask_out).astype(bf16)
        if intra:
            r_x = lax.broadcasted_iota(jnp.int32, (LANES, LANES), 0) // S5_GROUP
            c_x = lax.broadcasted_iota(jnp.int32, (LANES, LANES), 1) // S5_GROUP
            mask_x = (r_x == c_x).astype(f32)
            for s in range(S5_T):
                for i in range(S5_T):
                    mintra_sc[s * LANES:(s + 1) * LANES, i * LANES:(i + 1) * LANES] = (
                        jnp.tile(cintra_ref[0, s, i], (PACK_G, 1)) * mask_x).astype(bf16)

    x = jnp.concatenate([u_ref[0, :, i, :, :].reshape(ct * nb, LANES) for i in range(S5_T)], axis=1)
    xb = x.astype(bf16)
    v_sc[...] = jnp.dot(xb, min_sc[...], preferred_element_type=f32)
    are = at_ref[0, :, :half]
    aim = at_ref[0, :, half:]

    def body(s, carry):
        k = (ct - 1 - s) if rev else s
        off = pl.multiple_of(k * nb, nb)
        hre, him = carry
        hall_sc[pl.ds(off, nb), :half] = hre
        hall_sc[pl.ds(off, nb), half:] = him
        vre = v_sc[pl.ds(off, nb), :half]
        vim = v_sc[pl.ds(off, nb), half:]
        return are * hre - aim * him + vre, are * him + aim * hre + vim

    hre, him = lax.fori_loop(0, ct, body, (h_sc[:, :half], h_sc[:, half:]))
    h_sc[:, :half] = hre
    h_sc[:, half:] = him

    @pl.when(t == nt - 1)
    def _():
        hfin_ref[0] = h_sc[...]

    if emit_y:
        y = jnp.dot(hall_sc[...].astype(bf16), mout_sc[...], preferred_element_type=f32)
        if intra:
            y = y + jnp.dot(xb, mintra_sc[...], preferred_element_type=f32) + x * d_ref[0]
        for i in range(S5_T):
            y_ref[0, :, i, :, :] = y[:, i * LANES:(i + 1) * LANES].reshape(ct, nb, LANES)


def _s5_pass(u5, c_in, a_t, h0, c_out=None, c_intra=None, d_skip=None, *, rev, emit_y):
    npk, nc, _, nb, _ = u5.shape
    ct = min(32, nc)
    nt = nc // ct
    intra = c_intra is not None
    kdim = S5_T * LANES
    sdim = a_t.shape[-1]
    tmap = (lambda j, t: (j, nt - 1 - t, 0, 0, 0)) if rev else (lambda j, t: (j, t, 0, 0, 0))
    pmap = lambda j, t: (j, 0, 0)
    pmap4 = lambda j, t: (j, 0, 0, 0)
    pmap5 = lambda j, t: (j, 0, 0, 0, 0)
    args = [u5, c_in, a_t, h0]
    in_specs = [pl.BlockSpec((1, ct, S5_T, nb, LANES), tmap),
                pl.BlockSpec((1,) + c_in.shape[1:], pmap4),
                pl.BlockSpec((1, nb, sdim), pmap),
                pl.BlockSpec((1, nb, sdim), pmap)]
    scratch = [pltpu.VMEM((nb, sdim), f32),
               pltpu.VMEM((ct * nb, sdim), f32),
               pltpu.VMEM((ct * nb, sdim), f32),
               pltpu.VMEM((kdim, sdim), bf16)]
    if emit_y:
        args.append(c_out)
        in_specs.append(pl.BlockSpec((1,) + c_out.shape[1:], pmap4))
        scratch.append(pltpu.VMEM((sdim, kdim), bf16))
    if intra:
        args += [c_intra, d_skip]
        in_specs += [pl.BlockSpec((1,) + c_intra.shape[1:], pmap5), pl.BlockSpec((1, 1, kdim), pmap)]
        scratch.append(pltpu.VMEM((kdim, kdim), bf16))
    out_specs, out_shape = [], []
    if emit_y:
        out_specs.append(pl.BlockSpec((1, ct, S5_T, nb, LANES), tmap))
        out_shape.append(jax.ShapeDtypeStruct(u5.shape, f32))
    out_specs.append(pl.BlockSpec((1, nb, sdim), pmap))
    out_shape.append(jax.ShapeDtypeStruct((npk, nb, sdim), f32))
    kern = functools.partial(_s5_kernel, ct=ct, rev=rev, emit_y=emit_y, intra=intra)
    return pl.pallas_call(
        kern,
        grid=(npk, nt),
        in_specs=in_specs,
        out_specs=out_specs,
        out_shape=out_shape,
        scratch_shapes=scratch,
        compiler_params=_cparams(("parallel", "arbitrary")),
        name="s5_rev" if rev else "s5_fwd",
    )(*args)


def _s5_weights(lam_re, lam_im, log_dt, b_re, b_im, c_re, c_im):
    T = S5_T
    ndir, G, P = lam_re.shape
    npk = G // PACK_G
    dt = jnp.exp(log_dt.astype(f32))[..., None]
    lr, li = lam_re.astype(f32), lam_im.astype(f32)
    zr, zi = lr * dt, li * dt

    nr = jnp.expm1(zr) * jnp.cos(zi) - 2.0 * jnp.square(jnp.sin(0.5 * zi))
    ni = jnp.exp(zr) * jnp.sin(zi)
    den = lr * lr + li * li
    qr = (nr * lr + ni * li) / den
    qi = (ni * lr - nr * li) / den
    bre, bim = b_re.astype(f32), b_im.astype(f32)
    bbr = qr[..., None] * bre - qi[..., None] * bim
    bbi = qr[..., None] * bim + qi[..., None] * bre
    cr, ci = c_re.astype(f32), c_im.astype(f32)

    def powers(e, d):
        ev = jnp.asarray(np.asarray(e, np.float32))[:, None, None]
        m = jnp.exp(ev * zr[d])
        return m * jnp.cos(ev * zi[d]), m * jnp.sin(ev * zi[d])

    def c_times_a(e, d):
        pr, pi = powers(e, d)
        pr, pi = pr[:, :, None, :], pi[:, :, None, :]
        return cr[d] * pr - ci[d] * pi, cr[d] * pi + ci[d] * pr

    steps = np.arange(T)

    def pack_in(e, d):
        pr, pi = powers(e, d)
        pr, pi = pr[..., None], pi[..., None]
        re = pr * bbr[d] - pi * bbi[d]
        im = pr * bbi[d] + pi * bbr[d]
        base = jnp.stack([re, im]).reshape(2, T, npk, PACK_G, P, S5_GROUP)
        return base.transpose(2, 1, 5, 0, 3, 4).reshape(npk, T, S5_GROUP, 2 * PACK_G * P)

    def pack_out(e, d):
        car, cai = c_times_a(e, d)
        base = jnp.stack([car, -cai]).reshape(2, T, npk, PACK_G, S5_GROUP, P)
        return base.transpose(2, 1, 0, 5, 3, 4).reshape(npk, T, 2 * P, LANES)

    m_in_f = pack_in(T - 1 - steps, 0)
    m_in_r = pack_in(steps, 1)
    m_out_f = pack_out(steps + 1, 0)
    m_out_r = pack_out(T - steps, 1)

    lag = steps[None, :] - steps[:, None]

    def lag_term(d, sign):
        car, cai = c_times_a(np.abs(lag).reshape(-1), d)
        mask = jnp.asarray((sign * lag >= 0).reshape(-1, 1, 1, 1), f32)
        k = (jnp.einsum('tgop,gpi->tgio', car * mask, bbr[d])
             - jnp.einsum('tgop,gpi->tgio', cai * mask, bbi[d]))
        return k.reshape(T, T, G, S5_GROUP, S5_GROUP)

    toe = lag_term(0, 1) + lag_term(1, -1)
    toe = toe.reshape(T, T, npk, PACK_G, S5_GROUP, S5_GROUP).transpose(2, 0, 1, 4, 3, 5)
    m_intra = toe.reshape(npk, T, T, S5_GROUP, LANES)

    def a_chunk(d):
        pr, pi = powers([T], d)
        return jnp.concatenate([pr.reshape(npk, PACK_G * P), pi.reshape(npk, PACK_G * P)], axis=-1)

    return dict(m_in_f=m_in_f, m_in_r=m_in_r, m_out_f=m_out_f, m_out_r=m_out_r,
                m_intra=m_intra, a_f=a_chunk(0), a_r=a_chunk(1))


def _gdn_prep_kernel(*refs, tile, rows_conv, d_gdn):
    if rows_conv:
        main_ref, up_ref, dn_ref, w_ref, q_ref, k_ref, v_ref = refs
    else:
        main_ref, w_ref, q_ref, k_ref, v_ref = refs
    w = w_ref[...]
    if rows_conv:
        i = pl.program_id(1)
        n = pl.num_programs(1)
        up = up_ref[0] * (i > 0).astype(f32)
        dn = dn_ref[0] * (i < n - 1).astype(f32)
        ext = jnp.concatenate([up, main_ref[0], dn], axis=0)
        base = GRID_W
        drs = (-1, 0, 1)
        col = lax.broadcasted_iota(jnp.int32, (tile, 1), 0) % GRID_W
        mask_m = col != 0
        mask_p = col != GRID_W - 1
    else:
        ext = main_ref[0]
        base = 0
        drs = (0,)
        col = lax.broadcasted_iota(jnp.int32, (tile, 1), 0)
        mask_m = col != 0
        mask_p = col != tile - 1
    acc = {}
    for dc in (-1, 0, 1):
        a = None
        for dr in drs:
            tap = w[(dr + 1) * 3 + (dc + 1)][None, :]
            term = ext[base + GRID_W * dr: base + GRID_W * dr + tile] * tap
            a = term if a is None else a + term
        acc[dc] = a
    conv = (acc[0] + jnp.where(mask_m, pltpu.roll(acc[-1], 1, 0), 0.0)
            + jnp.where(mask_p, pltpu.roll(acc[1], tile - 1, 0), 0.0))
    s = _silu(conv)
    nh = d_gdn // GDN_HEAD
    for h in range(nh):
        qh = s[:, h * GDN_HEAD:(h + 1) * GDN_HEAD]
        kh = s[:, d_gdn + h * GDN_HEAD: d_gdn + (h + 1) * GDN_HEAD]
        qn = qh * (lax.rsqrt(jnp.sum(qh * qh, axis=-1, keepdims=True) + NORM_EPS) * (GDN_HEAD ** -0.5))
        kn = kh * lax.rsqrt(jnp.sum(kh * kh, axis=-1, keepdims=True) + NORM_EPS)
        q_ref[0, :, h * GDN_HEAD:(h + 1) * GDN_HEAD] = qn.astype(bf16)
        k_ref[0, :, h * GDN_HEAD:(h + 1) * GDN_HEAD] = kn.astype(bf16)
    v_ref[0] = s[:, 2 * d_gdn:].astype(bf16)


def _gdn_prep(qkv, conv_w, rows_conv):
    nb, L, c3 = qkv.shape
    d_gdn = c3 // 3
    w9 = conv_w.reshape(9, c3).astype(f32)
    if rows_conv:
        tile = min(512, L)
        nt = L // tile
        r = tile // GRID_W
        nrows = L // GRID_W
        in_specs = [pl.BlockSpec((1, tile, c3), lambda b, i: (b, i, 0)),
                    pl.BlockSpec((1, GRID_W, c3), lambda b, i: (b, jnp.maximum(i * r - 1, 0), 0)),
                    pl.BlockSpec((1, GRID_W, c3), lambda b, i: (b, jnp.minimum((i + 1) * r, nrows - 1), 0)),
                    pl.BlockSpec((9, c3), lambda b, i: (0, 0))]
        args = (qkv, qkv, qkv, w9)
    else:
        tile, nt = L, 1
        in_specs = [pl.BlockSpec((1, tile, c3), lambda b, i: (b, i, 0)),
                    pl.BlockSpec((9, c3), lambda b, i: (0, 0))]
        args = (qkv, w9)
    kern = functools.partial(_gdn_prep_kernel, tile=tile, rows_conv=rows_conv, d_gdn=d_gdn)
    o_spec = pl.BlockSpec((1, tile, d_gdn), lambda b, i: (b, i, 0))
    o_shape = jax.ShapeDtypeStruct((nb, L, d_gdn), bf16)
    return pl.pallas_call(
        kern,
        grid=(nb, nt),
        in_specs=in_specs,
        out_specs=[o_spec, o_spec, o_spec],
        out_shape=[o_shape, o_shape, o_shape],
        compiler_params=_cparams(("parallel", "parallel")),
        name="gdn_prep",
    )(*args)


def _chunk_cumsum(g, rev):
    n = g.shape[0]
    r = lax.broadcasted_iota(jnp.int32, g.shape, 0)
    s = 1
    while s < n:
        if rev:
            g = g + jnp.where(r < n - s, pltpu.roll(g, n - s, 0), 0.0)
        else:
            g = g + jnp.where(r >= s, pltpu.roll(g, s, 0), 0.0)
        s *= 2
    return g


def _tri_solve_rows(apl_sc, tl_sc, dirn, upper):
    nblk = GDN_CHUNK // 8
    base = dirn * GDN_CHUNK
    sub = lax.broadcasted_iota(jnp.int32, (8, LANES), 0)
    for step_blk in range(nblk):
        blk = (nblk - 1 - step_blk) if upper else step_blk
        mblocks = range(blk, nblk) if upper else range(0, blk + 1)

        def solve_row(s, carry, blk=blk, mblocks=mblocks):
            ii = (7 - s) if upper else s
            i = blk * 8 + ii
            acc = {}
            for mb in mblocks:
                arow = apl_sc[i, base + mb * 8:base + (mb + 1) * 8, :]
                jblocks = range(mb, nblk) if upper else range(0, mb + 1)
                for mm in range(8):
                    ab = jnp.broadcast_to(arow[mm:mm + 1, :], (8, LANES))
                    for jb in jblocks:
                        term = ab * tl_sc[mb * 8 + mm, base + jb * 8:base + (jb + 1) * 8, :]
                        key = (jb, mm % 2)
                        acc[key] = term if key not in acc else acc[key] + term
            for jb in (range(blk, nblk) if upper else range(0, blk + 1)):
                tot = acc[(jb, 0)] + acc[(jb, 1)]
                unit_row = jnp.where(sub == ii, 1.0, 0.0) if jb == blk else 0.0
                tl_sc[i, base + jb * 8:base + (jb + 1) * 8, :] = unit_row - tot
            return carry

        lax.fori_loop(0, 8, solve_row, 0)


def _gdn_scan_kernel(*refs, nchunks, nh, emit_o, has_s0):
    it = iter(refs)
    q_ref, k_ref, v_ref, g_ref = next(it), next(it), next(it), next(it)
    s0_ref = next(it) if has_s0 else None
    o_ref = next(it) if emit_o else None
    sfin_ref = next(it)
    s_sc, at_sc, apl_sc, tl_sc, mk_sc, w_sc, gcol_sc, grow_sc = (next(it) for _ in range(8))
    C = GDN_CHUNK
    bb = q_ref.shape[0]
    nunits = bb * nchunks
    s_sc[...] = s0_ref[...] if has_s0 else jnp.zeros(s_sc.shape, f32)
    if emit_o:
        o_ref[...] = jnp.zeros(o_ref.shape, f32)
    row = lax.broadcasted_iota(jnp.int32, (C, N_DIR * C), 0)
    lane = lax.broadcasted_iota(jnp.int32, (C, N_DIR * C), 1)
    fwd_half = lane < C
    col = jnp.where(fwd_half, lane, lane - C)
    tri = (fwd_half & (row >= col)) | (~fwd_half & (row <= col))
    strict = (fwd_half & (row > col)) | (~fwd_half & (row < col))
    nbeta = N_DIR * nh
    gate_lane = lax.broadcasted_iota(jnp.int32, (C, LANES), 1)

    def unit_of(unit):
        bl = unit // nchunks
        return bl, pl.multiple_of((unit % nchunks) * C, C)

    def phase_a(unit, carry):
        bl, off = unit_of(unit)
        gts = g_ref[bl, pl.ds(off, C), :]
        gmix = jnp.where(gate_lane < nbeta, gts,
                         jnp.where(gate_lane < nbeta + nh, _chunk_cumsum(gts, False), _chunk_cumsum(gts, True)))
        g_rows = gmix.T[:2 * nbeta]
        gcol_sc[unit] = gmix
        grow_sc[unit] = g_rows
        for h in range(nh):
            hs = slice(h * GDN_HEAD, (h + 1) * GDN_HEAD)
            cbf, cbr = h, nh + h
            cgf, cgr = nbeta + h, nbeta + nh + h
            kb = k_ref[bl, pl.ds(off, C), hs]
            k2 = jnp.concatenate([kb, kb], axis=0)
            if emit_o:
                kq = jnp.concatenate([kb, q_ref[bl, pl.ds(off, C), hs]], axis=0)
            else:
                kq = kb
            kkqk = lax.dot_general(kq, k2, (((1,), (1,)), ((), ())), preferred_element_type=f32)
            g_col = jnp.where(fwd_half, gmix[:, cgf:cgf + 1], gmix[:, cgr:cgr + 1])
            g_row = jnp.concatenate([g_rows[cgf:cgf + 1, :], g_rows[cgr:cgr + 1, :]], axis=1)
            b_row = jnp.concatenate([g_rows[cbf:cbf + 1, :], g_rows[cbr:cbr + 1, :]], axis=1)
            decay_b = jnp.exp(jnp.where(tri, g_col - g_row, -1e30)) * b_row
            at_sc[unit * nh + h] = jnp.where(strict, kkqk[:C] * decay_b, 0.0)
            if emit_o:
                mk_sc[unit * nh + h, :C, :] = (kkqk[C:] * decay_b).astype(bf16)
        return carry

    lax.fori_loop(0, nunits, phase_a, 0, unroll=2)

    def swap_in(ib, carry):
        rows = pl.ds(pl.multiple_of(ib * 8, 8), 8)
        tl_sc[rows] = jnp.swapaxes(at_sc[:, rows, :], 0, 1)
        return carry

    def to_lanes(i, carry):
        apl_sc[i] = tl_sc[i].T
        return carry

    lax.fori_loop(0, C // 8, swap_in, 0)
    lax.fori_loop(0, C, to_lanes, 0, unroll=4)
    tl_sc[...] = jnp.zeros(tl_sc.shape, f32)
    _tri_solve_rows(apl_sc, tl_sc, 0, False)
    _tri_solve_rows(apl_sc, tl_sc, 1, True)

    def from_lanes(i, carry):
        apl_sc[i] = tl_sc[i].T
        return carry

    def swap_out(ib, carry):
        rows = pl.ds(pl.multiple_of(ib * 8, 8), 8)
        at_sc[:, rows, :] = jnp.swapaxes(apl_sc[rows], 0, 1)
        return carry

    lax.fori_loop(0, C, from_lanes, 0, unroll=4)
    lax.fori_loop(0, C // 8, swap_out, 0)

    u0_sc = (apl_sc, tl_sc)

    def u0_slot(p):
        return p // 2, pl.ds(pl.multiple_of((p % 2) * C, C), C)

    eye_bf = (lax.broadcasted_iota(jnp.int32, (GDN_HEAD, GDN_HEAD), 0)
              == lax.broadcasted_iota(jnp.int32, (GDN_HEAD, GDN_HEAD), 1)).astype(bf16)

    def phase_b2(unit, carry):
        bl, off = unit_of(unit)
        g_rows = grow_sc[unit]
        g_cols = gcol_sc[unit]
        zero_kv = jnp.zeros((C, GDN_HEAD), bf16)
        for h in range(nh):
            hs = slice(h * GDN_HEAD, (h + 1) * GDN_HEAD)
            p = unit * nh + h
            cbf, cbr = h, nh + h
            cgf, cgr = nbeta + h, nbeta + nh + h
            t_both = at_sc[p]
            kb = k_ref[bl, pl.ds(off, C), hs]
            vb = v_ref[bl, pl.ds(off, C), hs]
            g_row = jnp.concatenate([g_rows[cgf:cgf + 1, :], g_rows[cgr:cgr + 1, :]], axis=1)
            b_row = jnp.concatenate([g_rows[cbf:cbf + 1, :], g_rows[cbr:cbr + 1, :]], axis=1)
            g_last = jnp.concatenate([jnp.broadcast_to(g_cols[C - 1:C, cgf:cgf + 1], (1, C)),
                                      jnp.broadcast_to(g_cols[0:1, cgr:cgr + 1], (1, C))], axis=1)
            v_bd = jnp.concatenate([jnp.concatenate([vb, zero_kv], axis=1),
                                    jnp.concatenate([zero_kv, vb], axis=1)], axis=0)
            k_bd = jnp.concatenate([jnp.concatenate([kb, zero_kv], axis=1),
                                    jnp.concatenate([zero_kv, kb], axis=1)], axis=0)
            u0 = jnp.dot(t_both.astype(bf16), v_bd, preferred_element_type=f32)
            w = jnp.dot((t_both * jnp.exp(g_row)).astype(bf16), k_bd, preferred_element_type=f32)
            slab, rows = u0_slot(p)
            for dirn in range(N_DIR):
                u0_sc[dirn][slab, rows, :] = u0[:, dirn * GDN_HEAD:(dirn + 1) * GDN_HEAD]
                w_sc[dirn, p] = w[:, dirn * GDN_HEAD:(dirn + 1) * GDN_HEAD].astype(bf16)
            k_t = lax.dot_general(eye_bf, jnp.concatenate([kb, kb], axis=0), (((1,), (1,)), ((), ())),
                                  preferred_element_type=f32)
            mk_sc[p, C:, :] = (k_t * (b_row * jnp.exp(g_last - g_row))).astype(bf16)
        return carry

    lax.fori_loop(0, nunits, phase_b2, 0, unroll=4)

    def phase_c(step, carry):
        chains = []
        for dirn, unit in ((0, step), (1, nunits - 1 - step)):
            bl, off = unit_of(unit)
            g_col_all = gcol_sc[unit]
            g_tot_all = g_col_all[0:1, :] if dirn else g_col_all[C - 1:C, :]
            for h in range(nh):
                cg = nbeta + dirn * nh + h
                hs = slice(h * GDN_HEAD, (h + 1) * GDN_HEAD)
                p = unit * nh + h
                slab, rows = u0_slot(p)
                lhs = w_sc[dirn, p]
                ch = dict(idx=(bl, dirn, h), off=off, hs=hs, dirn=dirn, u0=u0_sc[dirn][slab, rows, :],
                          g_last=jnp.exp(g_tot_all[:, cg:cg + 1]))
                if emit_o:
                    lhs = jnp.concatenate([lhs, q_ref[bl, pl.ds(off, C), hs]], axis=0)
                    ch['mk'] = mk_sc[p]
                    ch['gam'] = jnp.exp(g_col_all[:, cg:cg + 1])
                else:
                    ch['mk'] = mk_sc[p, C:, :]
                ch['lhs'] = lhs
                chains.append(ch)
        s_old = [s_sc[ch['idx']] for ch in chains]
        r = [jnp.dot(ch['lhs'], s.astype(bf16), preferred_element_type=f32) for ch, s in zip(chains, s_old)]
        zeros = jnp.zeros((C, GDN_HEAD), bf16)
        ub = [(ch['u0'] - ri[:C]).astype(bf16) for ch, ri in zip(chains, r)]
        m = [jnp.dot(ch['mk'], jnp.concatenate([zeros, ubi] if ch['dirn'] else [ubi, zeros], axis=0),
                     preferred_element_type=f32) for ch, ubi in zip(chains, ub)]
        for n, ch in enumerate(chains):
            if emit_o:
                bl = ch['idx'][0]
                o_ref[bl, pl.ds(ch['off'], C), ch['hs']] += ch['gam'] * r[n][C:] + m[n][:C]
                s_sc[ch['idx']] = ch['g_last'] * s_old[n] + m[n][C:]
            else:
                s_sc[ch['idx']] = ch['g_last'] * s_old[n] + m[n]
        return carry

    lax.fori_loop(0, nunits, phase_c, 0)
    sfin_ref[...] = s_sc[...]


def _gdn_scan(q, k, v, gates, s0, *, emit_o):
    nb, L, d_gdn = q.shape
    nh = d_gdn // GDN_HEAD
    nchunks = L // GDN_CHUNK
    C = GDN_CHUNK
    bb = max(1, min(nb, LANES // (nchunks * nh)))
    nunits = bb * nchunks
    nprob = nunits * nh
    assert nprob == LANES, (nb, L, nh)
    has_s0 = s0 is not None
    kern = functools.partial(_gdn_scan_kernel, nchunks=nchunks, nh=nh, emit_o=emit_o, has_s0=has_s0)
    seq = lambda b: (b, 0, 0)
    st = lambda b: (b, 0, 0, 0, 0)
    st_spec = pl.BlockSpec((bb, N_DIR, nh, GDN_HEAD, GDN_HEAD), st)
    st_shape = jax.ShapeDtypeStruct((nb, N_DIR, nh, GDN_HEAD, GDN_HEAD), f32)
    args = [q, k, v, gates]
    in_specs = [pl.BlockSpec((bb, L, d_gdn), seq)] * 3 + [pl.BlockSpec((bb, L, LANES), seq)]
    if has_s0:
        args.append(s0)
        in_specs.append(st_spec)
    out_specs, out_shape = [], []
    if emit_o:
        out_specs.append(pl.BlockSpec((bb, L, d_gdn), seq))
        out_shape.append(jax.ShapeDtypeStruct((nb, L, d_gdn), f32))
    out_specs.append(st_spec)
    out_shape.append(st_shape)
    return pl.pallas_call(
        kern,
        grid=(nb // bb,),
        in_specs=in_specs,
        out_specs=out_specs,
        out_shape=out_shape,
        scratch_shapes=[pltpu.VMEM((bb, N_DIR, nh, GDN_HEAD, GDN_HEAD), f32),
                        pltpu.VMEM((nprob, C, N_DIR * C), f32),
                        pltpu.VMEM((C, N_DIR * C, nprob), f32),
                        pltpu.VMEM((C, N_DIR * C, nprob), f32),
                        pltpu.VMEM((nprob, C + GDN_HEAD, N_DIR * C), bf16),
                        pltpu.VMEM((N_DIR, nprob, C, GDN_HEAD), bf16),
                        pltpu.VMEM((nunits, C, LANES), f32),
                        pltpu.VMEM((nunits, 4 * nh, C), f32)],
        compiler_params=_cparams(("parallel",)),
        name="gdn_scan" if emit_o else "gdn_scan_state",
    )(*args)


def _post_kernel(x_ref, gate_ref, yf_ref, yr_ref, zs_ref, og_ref, zg_ref,
                 wglu_ref, bglu_ref, nw_ref, wout_ref, lng_ref, lnb_ref, o_ref, *, tt, alpha):
    nb, _, d = x_ref.shape
    rows = nb * tt
    npk = yf_ref.shape[0]
    d_s5 = npk * LANES
    d_gdn = og_ref.shape[2]
    parts = []
    for j in range(npk):
        ysum = (yf_ref[j] + yr_ref[j]).reshape(tt, nb, LANES)
        parts.append(jnp.swapaxes(ysum, 0, 1).reshape(rows, LANES))
    y = jnp.concatenate(parts, axis=1)
    g = 0.5 * y * (1.0 + lax.erf(y * (2.0 ** -0.5)))
    glu = jnp.dot(g.astype(bf16), wglu_ref[...], preferred_element_type=f32) + bglu_ref[...]
    s5 = g * _sigmoid(glu) * _silu(zs_ref[...].reshape(rows, d_s5))
    o = og_ref[...].reshape(rows, d_gdn)
    zg = zg_ref[...].reshape(rows, d_gdn)
    gd = []
    for h in range(d_gdn // GDN_HEAD):
        oh = o[:, h * GDN_HEAD:(h + 1) * GDN_HEAD]
        gd.append(oh * lax.rsqrt(jnp.mean(oh * oh, axis=-1, keepdims=True) + NORM_EPS) * nw_ref[...])
    gdn = jnp.concatenate(gd, axis=1) * _silu(zg)
    mix = jnp.concatenate([s5, gdn], axis=1).astype(bf16)
    yv = jnp.dot(mix, wout_ref[...], preferred_element_type=f32).reshape(nb, tt, d)
    r = alpha * x_ref[...] + gate_ref[...] * yv
    mu = jnp.mean(r, axis=-1, keepdims=True)
    var = jnp.mean(jnp.square(r - mu), axis=-1, keepdims=True)
    o_ref[...] = (r - mu) * lax.rsqrt(var + LN_EPS) * lng_ref[...] + lnb_ref[...]


def _post(x, gate, yf, yr, zs, o_gdn, zg, wglu, bglu, nw, wout, lng, lnb, alpha):
    nb, L, d = x.shape
    tt = 32
    npk = yf.shape[0]
    d_s5 = npk * LANES
    d_gdn = o_gdn.shape[2]
    kern = functools.partial(_post_kernel, tt=tt, alpha=alpha)
    tok = lambda i: (0, i, 0)
    c2 = lambda i: (0, 0)
    c3 = lambda i: (0, 0, 0)
    y_spec = pl.BlockSpec((npk, tt // S5_T, S5_T, nb, LANES), lambda i: (0, i, 0, 0, 0))
    return pl.pallas_call(
        kern,
        grid=(L // tt,),
        in_specs=[pl.BlockSpec((nb, tt, d), tok),
                  pl.BlockSpec((nb, 1, d), c3),
                  y_spec, y_spec,
                  pl.BlockSpec((nb, tt, d_s5), tok),
                  pl.BlockSpec((nb, tt, d_gdn), tok),
                  pl.BlockSpec((nb, tt, d_gdn), tok),
                  pl.BlockSpec(wglu.shape, c2),
                  pl.BlockSpec(bglu.shape, c2),
                  pl.BlockSpec(nw.shape, c2),
                  pl.BlockSpec(wout.shape, c2),
                  pl.BlockSpec(lng.shape, c3),
                  pl.BlockSpec(lnb.shape, c3)],
        out_specs=pl.BlockSpec((nb, tt, d), tok),
        out_shape=jax.ShapeDtypeStruct((nb, L, d), f32),
        compiler_params=_cparams(("parallel",)),
        name="post",
    )(x, gate, yf, yr, zs, o_gdn, zg, wglu, bglu, nw, wout, lng, lnb)


def _layer(x, c, ctx, c_ctx, p, alpha):
    nb, L, d = x.shape
    d_s5 = p['s5_d'].shape[0]
    d_gdn = p['conv_w'].shape[-1] // 3
    nh = d_gdn // GDN_HEAD
    npk = d_s5 // LANES

    pad = (-(nb + 1)) % 8
    cond = jnp.concatenate([c, c_ctx[None, :], jnp.zeros((pad, d), f32)], axis=0)
    m = _ada(cond, p['w_ada'], p['b_ada'])
    shift, scale, gate = m[:nb, :d], m[:nb, d:2 * d], m[:nb, 2 * d:]
    shift_c = jnp.broadcast_to(m[nb:nb + 1, :d], (nb, d))
    scale_c = jnp.broadcast_to(m[nb:nb + 1, d:2 * d], (nb, d))

    w_in = p['w_in']
    wu = w_in[:, :d_s5].astype(bf16)
    wm = w_in[:, d_s5:2 * d_s5 + 4 * d_gdn].astype(bf16)
    ng = 2 * N_DIR * nh
    wg = jnp.pad(w_in[:, 2 * d_s5 + 4 * d_gdn:], ((0, 0), (0, LANES - ng))).astype(bf16)
    gp = jnp.pad(jnp.stack([p['gdn_a_log'].reshape(-1), p['gdn_dt_bias'].reshape(-1)]).astype(f32),
                 ((0, 0), (ng // 2, LANES - ng)))

    u5, zs, qkv, zg, gates = _inproj(x, scale[:, None, :], shift[:, None, :], wu, wm, wg, gp, d_s5, d_gdn)
    u5c, _, qkvc, _, gatesc = _inproj(ctx, scale_c[:, None, :], shift_c[:, None, :], wu, wm, wg, gp, d_s5, d_gdn)

    sw = _s5_weights(p['s5_lambda_re'], p['s5_lambda_im'], p['s5_log_dt'],
                     p['s5_b_re'], p['s5_b_im'], p['s5_c_re'], p['s5_c_im'])
    sdim = sw['a_f'].shape[-1]
    a_f = jnp.broadcast_to(sw['a_f'][:, None, :], (npk, nb, sdim))
    a_r = jnp.broadcast_to(sw['a_r'][:, None, :], (npk, nb, sdim))
    h_zero = jnp.zeros((npk, nb, sdim), f32)
    d_skip = jnp.tile(p['s5_d'].astype(f32).reshape(npk, 1, LANES), (1, 1, S5_T))
    (hc_f,) = _s5_pass(u5c, sw['m_in_f'], a_f, h_zero, rev=False, emit_y=False)
    (hc_r,) = _s5_pass(u5c, sw['m_in_r'], a_r, h_zero, rev=True, emit_y=False)
    y_f, _ = _s5_pass(u5, sw['m_in_f'], a_f, hc_f, sw['m_out_f'], sw['m_intra'], d_skip, rev=False, emit_y=True)
    y_r, _ = _s5_pass(u5, sw['m_in_r'], a_r, hc_r, sw['m_out_r'], rev=True, emit_y=True)

    ql, kl, vl = _gdn_prep(qkv, p['conv_w'], True)
    qc, kc, vc = _gdn_prep(qkvc, p['conv_w'], False)
    (s_ctx,) = _gdn_scan(qc, kc, vc, gatesc, None, emit_o=False)
    o_gdn, _ = _gdn_scan(ql, kl, vl, gates, s_ctx, emit_o=True)

    return _post(x, gate[:, None, :], y_f, y_r, zs, o_gdn, zg,
                 p['w_glu'].astype(bf16), p['b_glu'].astype(f32).reshape(1, d_s5),
                 p['gdn_norm_w'].astype(f32).reshape(1, GDN_HEAD),
                 p['w_out'].astype(bf16), p['ln_g'].astype(f32).reshape(1, 1, d),
                 p['ln_b'].astype(f32).reshape(1, 1, d), alpha)


def kernel(x, c, ctx, c_ctx, w_ada, b_ada, w_in, s5_lambda_re, s5_lambda_im, s5_log_dt, s5_b_re, s5_b_im, s5_c_re, s5_c_im, s5_d, w_glu, b_glu, conv_w, gdn_a_log, gdn_dt_bias, gdn_norm_w, w_out, ln_g, ln_b):
    depth = w_ada.shape[0]
    assert depth == 1, "context-token outputs are only skipped for the last layer"
    alpha = (2.0 * depth) ** 0.25
    p = {
        'w_ada': w_ada[0], 'b_ada': b_ada[0], 'w_in': w_in[0],
        's5_lambda_re': s5_lambda_re[0], 's5_lambda_im': s5_lambda_im[0], 's5_log_dt': s5_log_dt[0],
        's5_b_re': s5_b_re[0], 's5_b_im': s5_b_im[0], 's5_c_re': s5_c_re[0], 's5_c_im': s5_c_im[0],
        's5_d': s5_d[0], 'w_glu': w_glu[0], 'b_glu': b_glu[0], 'conv_w': conv_w[0],
        'gdn_a_log': gdn_a_log[0], 'gdn_dt_bias': gdn_dt_bias[0], 'gdn_norm_w': gdn_norm_w[0],
        'w_out': w_out[0], 'ln_g': ln_g[0], 'ln_b': ln_b[0],
    }
    return _layer(x, c, ctx, c_ctx, p, alpha)
```

```python
import functools
import math

import numpy as np
import jax
import jax.numpy as jnp
from jax import lax
from jax.experimental import pallas as pl
from jax.experimental.pallas import tpu as pltpu

f32 = jnp.float32
bf16 = jnp.bfloat16
HI = lax.Precision.HIGHEST

LANES = 128
S5_GROUP = 16
S5_STATE = 64
S5_T = 8
PACK_G = LANES // S5_GROUP
GDN_HEAD = 128
GDN_CHUNK = 64
GRID_W = 64
N_DIR = 2
LN_EPS = 1e-5
NORM_EPS = 1e-6
VMEM_LIMIT = 56 * 1024 * 1024


def _cparams(sem):
    return pltpu.CompilerParams(dimension_semantics=sem, vmem_limit_bytes=VMEM_LIMIT)


def _sigmoid(x):
    return 0.5 + 0.5 * jnp.tanh(0.5 * x)


def _silu(x):
    h = 0.5 * x
    return h + h * jnp.tanh(h)


def _ada_kernel(c_ref, w_ref, b_ref, o_ref):
    o_ref[...] = jnp.dot(_silu(c_ref[...]), w_ref[...], precision=HI,
                         preferred_element_type=f32) + b_ref[...]


def _ada(cond, w_ada, b_ada):
    rows, d = cond.shape
    n = w_ada.shape[1]
    tn = 512
    return pl.pallas_call(
        _ada_kernel,
        grid=(n // tn,),
        in_specs=[pl.BlockSpec((rows, d), lambda i: (0, 0)),
                  pl.BlockSpec((d, tn), lambda i: (0, i)),
                  pl.BlockSpec((1, tn), lambda i: (0, i))],
        out_specs=pl.BlockSpec((rows, tn), lambda i: (0, i)),
        out_shape=jax.ShapeDtypeStruct((rows, n), f32),
        compiler_params=_cparams(("parallel",)),
        name="ada",
    )(cond, w_ada, b_ada.reshape(1, n))


def _inproj_kernel(x_ref, scale_ref, shift_ref, wu_ref, wm_ref, wg_ref, gp_ref,
                   u_ref, zs_ref, qkv_ref, zg_ref, gate_ref, *, tt, d_s5, d_gdn, n_beta):
    nb, _, d = x_ref.shape
    sc = 1.0 + scale_ref[...]
    sh = shift_ref[...]
    h = (x_ref[...] * sc + sh).reshape(nb * tt, d).astype(bf16)
    r = jnp.dot(h, wm_ref[...], preferred_element_type=f32)
    zs_ref[...] = r[:, :d_s5].reshape(nb, tt, d_s5).astype(bf16)
    qkv_ref[...] = r[:, d_s5:d_s5 + 3 * d_gdn].reshape(nb, tt, 3 * d_gdn)
    zg_ref[...] = r[:, d_s5 + 3 * d_gdn:].reshape(nb, tt, d_gdn).astype(bf16)
    lg = jnp.dot(h, wg_ref[...], preferred_element_type=f32)
    lane = lax.broadcasted_iota(jnp.int32, lg.shape, 1)
    a = lg + gp_ref[1:2, :]
    softplus = jnp.maximum(a, 0.0) + jnp.log1p(jnp.exp(-jnp.abs(a)))
    gates = jnp.where(lane < n_beta, _sigmoid(lg), -jnp.exp(gp_ref[0:1, :]) * softplus)
    gate_ref[...] = gates.reshape(nb, tt, LANES)
    sc2 = sc[:, 0, :]
    sh2 = sh[:, 0, :]
    ht = (jnp.swapaxes(x_ref[...], 0, 1) * sc2[None] + sh2[None]).reshape(tt * nb, d).astype(bf16)
    ru = jnp.dot(ht, wu_ref[...], preferred_element_type=f32)
    for j in range(d_s5 // LANES):
        u_ref[j] = ru[:, j * LANES:(j + 1) * LANES].reshape(tt // S5_T, S5_T, nb, LANES)


def _inproj(x, scale, shift, wu, wm, wg, gp, d_s5, d_gdn):
    nb, L, d = x.shape
    tt = 32
    npk = d_s5 // LANES
    kern = functools.partial(_inproj_kernel, tt=tt, d_s5=d_s5, d_gdn=d_gdn,
                             n_beta=N_DIR * (d_gdn // GDN_HEAD))
    const2 = lambda i: (0, 0)
    const3 = lambda i: (0, 0, 0)
    return pl.pallas_call(
        kern,
        grid=(L // tt,),
        in_specs=[pl.BlockSpec((nb, tt, d), lambda i: (0, i, 0)),
                  pl.BlockSpec((nb, 1, d), const3),
                  pl.BlockSpec((nb, 1, d), const3),
                  pl.BlockSpec(wu.shape, const2),
                  pl.BlockSpec(wm.shape, const2),
                  pl.BlockSpec(wg.shape, const2),
                  pl.BlockSpec(gp.shape, const2)],
        out_specs=[pl.BlockSpec((npk, tt // S5_T, S5_T, nb, LANES), lambda i: (0, i, 0, 0, 0)),
                   pl.BlockSpec((nb, tt, d_s5), lambda i: (0, i, 0)),
                   pl.BlockSpec((nb, tt, 3 * d_gdn), lambda i: (0, i, 0)),
                   pl.BlockSpec((nb, tt, d_gdn), lambda i: (0, i, 0)),
                   pl.BlockSpec((nb, tt, LANES), lambda i: (0, i, 0))],
        out_shape=[jax.ShapeDtypeStruct((npk, L // S5_T, S5_T, nb, LANES), f32),
                   jax.ShapeDtypeStruct((nb, L, d_s5), bf16),
                   jax.ShapeDtypeStruct((nb, L, 3 * d_gdn), f32),
                   jax.ShapeDtypeStruct((nb, L, d_gdn), bf16),
                   jax.ShapeDtypeStruct((nb, L, LANES), f32)],
        compiler_params=_cparams(("parallel",)),
        name="inproj",
    )(x, scale, shift, wu, wm, wg, gp)


def _s5_kernel(*refs, ct, rev, emit_y, intra, add_y):
    it = iter(refs)
    u_ref, cin_ref, at_ref, h0_ref = next(it), next(it), next(it), next(it)
    cout_ref = next(it) if emit_y else None
    cintra_ref = next(it) if intra else None
    d_ref = next(it) if intra else None
    yadd_ref = next(it) if add_y else None
    y_ref = next(it) if emit_y else None
    hfin_ref = next(it)
    h_sc, v_sc, hall_sc, min_sc = next(it), next(it), next(it), next(it)
    mout_sc = next(it) if emit_y else None
    mintra_sc = next(it) if intra else None

    t = pl.program_id(1)
    nt = pl.num_programs(1)
    nb = u_ref.shape[3]
    sdim = h_sc.shape[1]
    half = sdim // 2
    gstate = half // PACK_G

    @pl.when(t == 0)
    def _():
        h_sc[...] = h0_ref[0]
        r_in = lax.broadcasted_iota(jnp.int32, (LANES, sdim), 0) // S5_GROUP
        c_in = (lax.broadcasted_iota(jnp.int32, (LANES, sdim), 1) % half) // gstate
        mask_in = (r_in == c_in).astype(f32)
        for s in range(S5_T):
            min_sc[s * LANES:(s + 1) * LANES, :] = (
                jnp.tile(cin_ref[0, s], (PACK_G, 1)) * mask_in).astype(bf16)
        if emit_y:
            r_out = lax.broadcasted_iota(jnp.int32, (half, LANES), 0) // gstate
            c_out = lax.broadcasted_iota(jnp.int32, (half, LANES), 1) // S5_GROUP
            mask_out = (r_out == c_out).astype(f32)
            for i in range(S5_T):
                for ri in range(2):
                    blk = cout_ref[0, i, ri * gstate:(ri + 1) * gstate, :]
                    mout_sc[ri * half:(ri + 1) * half, i * LANES:(i + 1) * LANES] = (
                        jnp.tile(blk, (PACK_G, 1)) * mask_out).astype(bf16)
        if intra:
            r_x = lax.broadcasted_iota(jnp.int32, (LANES, LANES), 0) // S5_GROUP
            c_x = lax.broadcasted_iota(jnp.int32, (LANES, LANES), 1) // S5_GROUP
            mask_x = (r_x == c_x).astype(f32)
            for s in range(S5_T):
                for i in range(S5_T):
                    mintra_sc[s * LANES:(s + 1) * LANES, i * LANES:(i + 1) * LANES] = (
                        jnp.tile(cintra_ref[0, s, i], (PACK_G, 1)) * mask_x).astype(bf16)

    x = jnp.concatenate([u_ref[0, :, i, :, :].reshape(ct * nb, LANES) for i in range(S5_T)], axis=1)
    xb = x.astype(bf16)
    v_sc[...] = jnp.dot(xb, min_sc[...], preferred_element_type=f32)
    are = at_ref[0, :, :half]
    aim = at_ref[0, :, half:]

    def body(s, carry):
        k = (ct - 1 - s) if rev else s
        off = pl.multiple_of(k * nb, nb)
        hre, him = carry
        hall_sc[pl.ds(off, nb), :half] = hre
        hall_sc[pl.ds(off, nb), half:] = him
        vre = v_sc[pl.ds(off, nb), :half]
        vim = v_sc[pl.ds(off, nb), half:]
        return are * hre - aim * him + vre, are * him + aim * hre + vim

    hre, him = lax.fori_loop(0, ct, body, (h_sc[:, :half], h_sc[:, half:]))
    h_sc[:, :half] = hre
    h_sc[:, half:] = him

    @pl.when(t == nt - 1)
    def _():
        hfin_ref[0] = h_sc[...]

    if emit_y:
        y = jnp.dot(hall_sc[...].astype(bf16), mout_sc[...], preferred_element_type=f32)
        if intra:
            y = y + jnp.dot(xb, mintra_sc[...], preferred_element_type=f32) + x * d_ref[0]
        for i in range(S5_T):
            yi = y[:, i * LANES:(i + 1) * LANES].reshape(ct, nb, LANES)
            y_ref[0, :, i, :, :] = yi + yadd_ref[0, :, i, :, :] if add_y else yi


def _s5_pass(u5, c_in, a_t, h0, c_out=None, c_intra=None, d_skip=None, y_add=None, *, rev, emit_y):
    npk, nc, _, nb, _ = u5.shape
    ct = min(32, nc)
    nt = nc // ct
    intra = c_intra is not None
    add_y = y_add is not None
    kdim = S5_T * LANES
    sdim = a_t.shape[-1]
    tmap = (lambda j, t: (j, nt - 1 - t, 0, 0, 0)) if rev else (lambda j, t: (j, t, 0, 0, 0))
    pmap = lambda j, t: (j, 0, 0)
    pmap4 = lambda j, t: (j, 0, 0, 0)
    pmap5 = lambda j, t: (j, 0, 0, 0, 0)
    args = [u5, c_in, a_t, h0]
    in_specs = [pl.BlockSpec((1, ct, S5_T, nb, LANES), tmap),
                pl.BlockSpec((1,) + c_in.shape[1:], pmap4),
                pl.BlockSpec((1, nb, sdim), pmap),
                pl.BlockSpec((1, nb, sdim), pmap)]
    scratch = [pltpu.VMEM((nb, sdim), f32),
               pltpu.VMEM((ct * nb, sdim), f32),
               pltpu.VMEM((ct * nb, sdim), f32),
               pltpu.VMEM((kdim, sdim), bf16)]
    if emit_y:
        args.append(c_out)
        in_specs.append(pl.BlockSpec((1,) + c_out.shape[1:], pmap4))
        scratch.append(pltpu.VMEM((sdim, kdim), bf16))
    if intra:
        args += [c_intra, d_skip]
        in_specs += [pl.BlockSpec((1,) + c_intra.shape[1:], pmap5), pl.BlockSpec((1, 1, kdim), pmap)]
        scratch.append(pltpu.VMEM((kdim, kdim), bf16))
    if add_y:
        args.append(y_add)
        in_specs.append(pl.BlockSpec((1, ct, S5_T, nb, LANES), tmap))
    out_specs, out_shape = [], []
    if emit_y:
        out_specs.append(pl.BlockSpec((1, ct, S5_T, nb, LANES), tmap))
        out_shape.append(jax.ShapeDtypeStruct(u5.shape, f32))
    out_specs.append(pl.BlockSpec((1, nb, sdim), pmap))
    out_shape.append(jax.ShapeDtypeStruct((npk, nb, sdim), f32))
    kern = functools.partial(_s5_kernel, ct=ct, rev=rev, emit_y=emit_y, intra=intra, add_y=add_y)
    return pl.pallas_call(
        kern,
        grid=(npk, nt),
        in_specs=in_specs,
        out_specs=out_specs,
        out_shape=out_shape,
        scratch_shapes=scratch,
        compiler_params=_cparams(("parallel", "arbitrary")),
        name="s5_rev" if rev else "s5_fwd",
    )(*args)


def _s5_weights(lam_re, lam_im, log_dt, b_re, b_im, c_re, c_im):
    T = S5_T
    ndir, G, P = lam_re.shape
    npk = G // PACK_G
    dt = jnp.exp(log_dt.astype(f32))[..., None]
    lr, li = lam_re.astype(f32), lam_im.astype(f32)
    zr, zi = lr * dt, li * dt

    nr = jnp.expm1(zr) * jnp.cos(zi) - 2.0 * jnp.square(jnp.sin(0.5 * zi))
    ni = jnp.exp(zr) * jnp.sin(zi)
    den = lr * lr + li * li
    qr = (nr * lr + ni * li) / den
    qi = (ni * lr - nr * li) / den
    bre, bim = b_re.astype(f32), b_im.astype(f32)
    bbr = qr[..., None] * bre - qi[..., None] * bim
    bbi = qr[..., None] * bim + qi[..., None] * bre
    cr, ci = c_re.astype(f32), c_im.astype(f32)

    def powers(e, d):
        ev = jnp.asarray(np.asarray(e, np.float32))[:, None, None]
        m = jnp.exp(ev * zr[d])
        return m * jnp.cos(ev * zi[d]), m * jnp.sin(ev * zi[d])

    def c_times_a(e, d):
        pr, pi = powers(e, d)
        pr, pi = pr[:, :, None, :], pi[:, :, None, :]
        return cr[d] * pr - ci[d] * pi, cr[d] * pi + ci[d] * pr

    steps = np.arange(T)

    def pack_in(e, d):
        pr, pi = powers(e, d)
        pr, pi = pr[..., None], pi[..., None]
        re = pr * bbr[d] - pi * bbi[d]
        im = pr * bbi[d] + pi * bbr[d]
        base = jnp.stack([re, im]).reshape(2, T, npk, PACK_G, P, S5_GROUP)
        return base.transpose(2, 1, 5, 0, 3, 4).reshape(npk, T, S5_GROUP, 2 * PACK_G * P)

    def pack_out(e, d):
        car, cai = c_times_a(e, d)
        base = jnp.stack([car, -cai]).reshape(2, T, npk, PACK_G, S5_GROUP, P)
        return base.transpose(2, 1, 0, 5, 3, 4).reshape(npk, T, 2 * P, LANES)

    m_in_f = pack_in(T - 1 - steps, 0)
    m_in_r = pack_in(steps, 1)
    m_out_f = pack_out(steps + 1, 0)
    m_out_r = pack_out(T - steps, 1)

    lag = steps[None, :] - steps[:, None]

    def lag_term(d, sign):
        car, cai = c_times_a(np.abs(lag).reshape(-1), d)
        mask = jnp.asarray((sign * lag >= 0).reshape(-1, 1, 1, 1), f32)
        k = (jnp.einsum('tgop,gpi->tgio', car * mask, bbr[d])
             - jnp.einsum('tgop,gpi->tgio', cai * mask, bbi[d]))
        return k.reshape(T, T, G, S5_GROUP, S5_GROUP)

    toe = lag_term(0, 1) + lag_term(1, -1)
    toe = toe.reshape(T, T, npk, PACK_G, S5_GROUP, S5_GROUP).transpose(2, 0, 1, 4, 3, 5)
    m_intra = toe.reshape(npk, T, T, S5_GROUP, LANES)

    def a_chunk(d):
        pr, pi = powers([T], d)
        return jnp.concatenate([pr.reshape(npk, PACK_G * P), pi.reshape(npk, PACK_G * P)], axis=-1)

    return dict(m_in_f=m_in_f, m_in_r=m_in_r, m_out_f=m_out_f, m_out_r=m_out_r,
                m_intra=m_intra, a_f=a_chunk(0), a_r=a_chunk(1))


def _gdn_prep_kernel(*refs, tile, rows_conv, d_gdn):
    if rows_conv:
        main_ref, up_ref, dn_ref, w_ref, q_ref, k_ref, v_ref = refs
    else:
        main_ref, w_ref, q_ref, k_ref, v_ref = refs
    w = w_ref[...]
    if rows_conv:
        i = pl.program_id(1)
        n = pl.num_programs(1)
        up = up_ref[0] * (i > 0).astype(f32)
        dn = dn_ref[0] * (i < n - 1).astype(f32)
        ext = jnp.concatenate([up, main_ref[0], dn], axis=0)
        base = GRID_W
        drs = (-1, 0, 1)
        col = lax.broadcasted_iota(jnp.int32, (tile, 1), 0) % GRID_W
        mask_m = col != 0
        mask_p = col != GRID_W - 1
    else:
        ext = main_ref[0]
        base = 0
        drs = (0,)
        col = lax.broadcasted_iota(jnp.int32, (tile, 1), 0)
        mask_m = col != 0
        mask_p = col != tile - 1
    acc = {}
    for dc in (-1, 0, 1):
        a = None
        for dr in drs:
            tap = w[(dr + 1) * 3 + (dc + 1)][None, :]
            term = ext[base + GRID_W * dr: base + GRID_W * dr + tile] * tap
            a = term if a is None else a + term
        acc[dc] = a
    conv = (acc[0] + jnp.where(mask_m, pltpu.roll(acc[-1], 1, 0), 0.0)
            + jnp.where(mask_p, pltpu.roll(acc[1], tile - 1, 0), 0.0))
    s = _silu(conv)
    nh = d_gdn // GDN_HEAD
    for h in range(nh):
        qh = s[:, h * GDN_HEAD:(h + 1) * GDN_HEAD]
        kh = s[:, d_gdn + h * GDN_HEAD: d_gdn + (h + 1) * GDN_HEAD]
        qn = qh * (lax.rsqrt(jnp.sum(qh * qh, axis=-1, keepdims=True) + NORM_EPS) * (GDN_HEAD ** -0.5))
        kn = kh * lax.rsqrt(jnp.sum(kh * kh, axis=-1, keepdims=True) + NORM_EPS)
        q_ref[0, :, h * GDN_HEAD:(h + 1) * GDN_HEAD] = qn.astype(bf16)
        k_ref[0, :, h * GDN_HEAD:(h + 1) * GDN_HEAD] = kn.astype(bf16)
    v_ref[0] = s[:, 2 * d_gdn:].astype(bf16)


def _gdn_prep(qkv, conv_w, rows_conv):
    nb, L, c3 = qkv.shape
    d_gdn = c3 // 3
    w9 = conv_w.reshape(9, c3).astype(f32)
    if rows_conv:
        tile = min(512, L)
        nt = L // tile
        r = tile // GRID_W
        nrows = L // GRID_W
        in_specs = [pl.BlockSpec((1, tile, c3), lambda b, i: (b, i, 0)),
                    pl.BlockSpec((1, GRID_W, c3), lambda b, i: (b, jnp.maximum(i * r - 1, 0), 0)),
                    pl.BlockSpec((1, GRID_W, c3), lambda b, i: (b, jnp.minimum((i + 1) * r, nrows - 1), 0)),
                    pl.BlockSpec((9, c3), lambda b, i: (0, 0))]
        args = (qkv, qkv, qkv, w9)
    else:
        tile, nt = L, 1
        in_specs = [pl.BlockSpec((1, tile, c3), lambda b, i: (b, i, 0)),
                    pl.BlockSpec((9, c3), lambda b, i: (0, 0))]
        args = (qkv, w9)
    kern = functools.partial(_gdn_prep_kernel, tile=tile, rows_conv=rows_conv, d_gdn=d_gdn)
    o_spec = pl.BlockSpec((1, tile, d_gdn), lambda b, i: (b, i, 0))
    o_shape = jax.ShapeDtypeStruct((nb, L, d_gdn), bf16)
    return pl.pallas_call(
        kern,
        grid=(nb, nt),
        in_specs=in_specs,
        out_specs=[o_spec, o_spec, o_spec],
        out_shape=[o_shape, o_shape, o_shape],
        compiler_params=_cparams(("parallel", "parallel")),
        name="gdn_prep",
    )(*args)


def _chunk_cumsum(g, rev):
    n = g.shape[0]
    r = lax.broadcasted_iota(jnp.int32, g.shape, 0)
    s = 1
    while s < n:
        if rev:
            g = g + jnp.where(r < n - s, pltpu.roll(g, n - s, 0), 0.0)
        else:
            g = g + jnp.where(r >= s, pltpu.roll(g, s, 0), 0.0)
        s *= 2
    return g


def _tri_solve_rows(apl_sc, tl_sc, dirn, upper):
    nblk = GDN_CHUNK // 8
    base = dirn * GDN_CHUNK
    sub = lax.broadcasted_iota(jnp.int32, (8, LANES), 0)
    for step_blk in range(nblk):
        blk = (nblk - 1 - step_blk) if upper else step_blk
        mblocks = range(blk, nblk) if upper else range(0, blk + 1)

        def solve_row(s, carry, blk=blk, mblocks=mblocks):
            ii = (7 - s) if upper else s
            i = blk * 8 + ii
            acc = {}
            for mb in mblocks:
                arow = apl_sc[i, base + mb * 8:base + (mb + 1) * 8, :]
                jblocks = range(mb, nblk) if upper else range(0, mb + 1)
                for mm in range(8):
                    ab = jnp.broadcast_to(arow[mm:mm + 1, :], (8, LANES))
                    for jb in jblocks:
                        term = ab * tl_sc[mb * 8 + mm, base + jb * 8:base + (jb + 1) * 8, :]
                        key = (jb, mm % 2)
                        acc[key] = term if key not in acc else acc[key] + term
            for jb in (range(blk, nblk) if upper else range(0, blk + 1)):
                tot = acc[(jb, 0)] + acc[(jb, 1)]
                unit_row = jnp.where(sub == ii, 1.0, 0.0) if jb == blk else 0.0
                tl_sc[i, base + jb * 8:base + (jb + 1) * 8, :] = unit_row - tot
            return carry

        lax.fori_loop(0, 8, solve_row, 0)


def _gdn_scan_kernel(*refs, nchunks, nh, emit_o, has_s0):
    it = iter(refs)
    q_ref, k_ref, v_ref, g_ref = next(it), next(it), next(it), next(it)
    s0_ref = next(it) if has_s0 else None
    o_ref = next(it) if emit_o else None
    sfin_ref = next(it)
    s_sc, at_sc, apl_sc, tl_sc, mk_sc, w_sc, gcol_sc, grow_sc = (next(it) for _ in range(8))
    C = GDN_CHUNK
    bb = q_ref.shape[0]
    nunits = bb * nchunks
    s_sc[...] = s0_ref[...] if has_s0 else jnp.zeros(s_sc.shape, f32)
    if emit_o:
        o_ref[...] = jnp.zeros(o_ref.shape, f32)
    row = lax.broadcasted_iota(jnp.int32, (C, N_DIR * C), 0)
    lane = lax.broadcasted_iota(jnp.int32, (C, N_DIR * C), 1)
    fwd_half = lane < C
    col = jnp.where(fwd_half, lane, lane - C)
    tri = (fwd_half & (row >= col)) | (~fwd_half & (row <= col))
    strict = (fwd_half & (row > col)) | (~fwd_half & (row < col))
    nbeta = N_DIR * nh
    gate_lane = lax.broadcasted_iota(jnp.int32, (C, LANES), 1)

    def unit_of(unit):
        bl = unit // nchunks
        return bl, pl.multiple_of((unit % nchunks) * C, C)

    def phase_a(unit, carry):
        bl, off = unit_of(unit)
        gts = g_ref[bl, pl.ds(off, C), :]
        gmix = jnp.where(gate_lane < nbeta, gts,
                         jnp.where(gate_lane < nbeta + nh, _chunk_cumsum(gts, False), _chunk_cumsum(gts, True)))
        g_rows = gmix.T[:2 * nbeta]
        gcol_sc[unit] = gmix
        grow_sc[unit] = g_rows
        for h in range(nh):
            hs = slice(h * GDN_HEAD, (h + 1) * GDN_HEAD)
            cbf, cbr = h, nh + h
            cgf, cgr = nbeta + h, nbeta + nh + h
            kb = k_ref[bl, pl.ds(off, C), hs]
            k2 = jnp.concatenate([kb, kb], axis=0)
            if emit_o:
                kq = jnp.concatenate([kb, q_ref[bl, pl.ds(off, C), hs]], axis=0)
            else:
                kq = kb
            kkqk = lax.dot_general(kq, k2, (((1,), (1,)), ((), ())), preferred_element_type=f32)
            g_col = jnp.where(fwd_half, gmix[:, cgf:cgf + 1], gmix[:, cgr:cgr + 1])
            g_row = jnp.concatenate([g_rows[cgf:cgf + 1, :], g_rows[cgr:cgr + 1, :]], axis=1)
            b_row = jnp.concatenate([g_rows[cbf:cbf + 1, :], g_rows[cbr:cbr + 1, :]], axis=1)
            decay_b = jnp.exp(jnp.where(tri, g_col - g_row, -1e30)) * b_row
            at_sc[unit * nh + h] = jnp.where(strict, kkqk[:C] * decay_b, 0.0)
            if emit_o:
                mk_sc[unit * nh + h, :C, :] = (kkqk[C:] * decay_b).astype(bf16)
        return carry

    lax.fori_loop(0, nunits, phase_a, 0, unroll=2)

    def swap_in(ib, carry):
        rows = pl.ds(pl.multiple_of(ib * 8, 8), 8)
        tl_sc[rows] = jnp.swapaxes(at_sc[:, rows, :], 0, 1)
        return carry

    def to_lanes(i, carry):
        apl_sc[i] = tl_sc[i].T
        return carry

    lax.fori_loop(0, C // 8, swap_in, 0)
    lax.fori_loop(0, C, to_lanes, 0, unroll=4)
    tl_sc[...] = jnp.zeros(tl_sc.shape, f32)
    _tri_solve_rows(apl_sc, tl_sc, 0, False)
    _tri_solve_rows(apl_sc, tl_sc, 1, True)

    def from_lanes(i, carry):
        apl_sc[i] = tl_sc[i].T
        return carry

    def swap_out(ib, carry):
        rows = pl.ds(pl.multiple_of(ib * 8, 8), 8)
        at_sc[:, rows, :] = jnp.swapaxes(apl_sc[rows], 0, 1)
        return carry

    lax.fori_loop(0, C, from_lanes, 0, unroll=4)
    lax.fori_loop(0, C // 8, swap_out, 0)

    u0_sc = (apl_sc, tl_sc)

    def u0_slot(p):
        return p // 2, pl.ds(pl.multiple_of((p % 2) * C, C), C)

    eye_bf = (lax.broadcasted_iota(jnp.int32, (GDN_HEAD, GDN_HEAD), 0)
              == lax.broadcasted_iota(jnp.int32, (GDN_HEAD, GDN_HEAD), 1)).astype(bf16)

    def phase_b2(unit, carry):
        bl, off = unit_of(unit)
        g_rows = grow_sc[unit]
        g_cols = gcol_sc[unit]
        zero_kv = jnp.zeros((C, GDN_HEAD), bf16)
        for h in range(nh):
            hs = slice(h * GDN_HEAD, (h + 1) * GDN_HEAD)
            p = unit * nh + h
            cbf, cbr = h, nh + h
            cgf, cgr = nbeta + h, nbeta + nh + h
            t_both = at_sc[p]
            kb = k_ref[bl, pl.ds(off, C), hs]
            vb = v_ref[bl, pl.ds(off, C), hs]
            g_row = jnp.concatenate([g_rows[cgf:cgf + 1, :], g_rows[cgr:cgr + 1, :]], axis=1)
            b_row = jnp.concatenate([g_rows[cbf:cbf + 1, :], g_rows[cbr:cbr + 1, :]], axis=1)
            g_last = jnp.concatenate([jnp.broadcast_to(g_cols[C - 1:C, cgf:cgf + 1], (1, C)),
                                      jnp.broadcast_to(g_cols[0:1, cgr:cgr + 1], (1, C))], axis=1)
            v_bd = jnp.concatenate([jnp.concatenate([vb, zero_kv], axis=1),
                                    jnp.concatenate([zero_kv, vb], axis=1)], axis=0)
            k_bd = jnp.concatenate([jnp.concatenate([kb, zero_kv], axis=1),
                                    jnp.concatenate([zero_kv, kb], axis=1)], axis=0)
            u0 = jnp.dot(t_both.astype(bf16), v_bd, preferred_element_type=f32)
            w = jnp.dot((t_both * jnp.exp(g_row)).astype(bf16), k_bd, preferred_element_type=f32)
            slab, rows = u0_slot(p)
            for dirn in range(N_DIR):
                u0_sc[dirn][slab, rows, :] = u0[:, dirn * GDN_HEAD:(dirn + 1) * GDN_HEAD]
                w_sc[dirn, p] = w[:, dirn * GDN_HEAD:(dirn + 1) * GDN_HEAD].astype(bf16)
            k_t = lax.dot_general(eye_bf, jnp.concatenate([kb, kb], axis=0), (((1,), (1,)), ((), ())),
                                  preferred_element_type=f32)
            mk_sc[p, C:, :] = (k_t * (b_row * jnp.exp(g_last - g_row))).astype(bf16)
        return carry

    lax.fori_loop(0, nunits, phase_b2, 0, unroll=4)

    def phase_c(step, carry):
        chains = []
        for dirn, unit in ((0, step), (1, nunits - 1 - step)):
            bl, off = unit_of(unit)
            g_col_all = gcol_sc[unit]
            g_tot_all = g_col_all[0:1, :] if dirn else g_col_all[C - 1:C, :]
            for h in range(nh):
                cg = nbeta + dirn * nh + h
                hs = slice(h * GDN_HEAD, (h + 1) * GDN_HEAD)
                p = unit * nh + h
                slab, rows = u0_slot(p)
                lhs = w_sc[dirn, p]
                ch = dict(idx=(bl, dirn, h), off=off, hs=hs, dirn=dirn, u0=u0_sc[dirn][slab, rows, :],
                          g_last=jnp.exp(g_tot_all[:, cg:cg + 1]))
                if emit_o:
                    lhs = jnp.concatenate([lhs, q_ref[bl, pl.ds(off, C), hs]], axis=0)
                    ch['mk'] = mk_sc[p]
                    ch['gam'] = jnp.exp(g_col_all[:, cg:cg + 1])
                else:
                    ch['mk'] = mk_sc[p, C:, :]
                ch['lhs'] = lhs
                chains.append(ch)
        s_old = [s_sc[ch['idx']] for ch in chains]
        r = [jnp.dot(ch['lhs'], s.astype(bf16), preferred_element_type=f32) for ch, s in zip(chains, s_old)]
        zeros = jnp.zeros((C, GDN_HEAD), bf16)
        ub = [(ch['u0'] - ri[:C]).astype(bf16) for ch, ri in zip(chains, r)]
        m = [jnp.dot(ch['mk'], jnp.concatenate([zeros, ubi] if ch['dirn'] else [ubi, zeros], axis=0),
                     preferred_element_type=f32) for ch, ubi in zip(chains, ub)]
        for n, ch in enumerate(chains):
            if emit_o:
                bl = ch['idx'][0]
                o_ref[bl, pl.ds(ch['off'], C), ch['hs']] += ch['gam'] * r[n][C:] + m[n][:C]
                s_sc[ch['idx']] = ch['g_last'] * s_old[n] + m[n][C:]
            else:
                s_sc[ch['idx']] = ch['g_last'] * s_old[n] + m[n]
        return carry

    lax.fori_loop(0, nunits, phase_c, 0)
    sfin_ref[...] = s_sc[...]


def _gdn_scan(q, k, v, gates, s0, *, emit_o):
    nb, L, d_gdn = q.shape
    nh = d_gdn // GDN_HEAD
    nchunks = L // GDN_CHUNK
    C = GDN_CHUNK
    bb = max(1, min(nb, LANES // (nchunks * nh)))
    nunits = bb * nchunks
    nprob = nunits * nh
    assert nprob == LANES, (nb, L, nh)
    has_s0 = s0 is not None
    kern = functools.partial(_gdn_scan_kernel, nchunks=nchunks, nh=nh, emit_o=emit_o, has_s0=has_s0)
    seq = lambda b: (b, 0, 0)
    st = lambda b: (b, 0, 0, 0, 0)
    st_spec = pl.BlockSpec((bb, N_DIR, nh, GDN_HEAD, GDN_HEAD), st)
    st_shape = jax.ShapeDtypeStruct((nb, N_DIR, nh, GDN_HEAD, GDN_HEAD), f32)
    args = [q, k, v, gates]
    in_specs = [pl.BlockSpec((bb, L, d_gdn), seq)] * 3 + [pl.BlockSpec((bb, L, LANES), seq)]
    if has_s0:
        args.append(s0)
        in_specs.append(st_spec)
    out_specs, out_shape = [], []
    if emit_o:
        out_specs.append(pl.BlockSpec((bb, L, d_gdn), seq))
        out_shape.append(jax.ShapeDtypeStruct((nb, L, d_gdn), f32))
    out_specs.append(st_spec)
    out_shape.append(st_shape)
    return pl.pallas_call(
        kern,
        grid=(nb // bb,),
        in_specs=in_specs,
        out_specs=out_specs,
        out_shape=out_shape,
        scratch_shapes=[pltpu.VMEM((bb, N_DIR, nh, GDN_HEAD, GDN_HEAD), f32),
                        pltpu.VMEM((nprob, C, N_DIR * C), f32),
                        pltpu.VMEM((C, N_DIR * C, nprob), f32),
                        pltpu.VMEM((C, N_DIR * C, nprob), f32),
                        pltpu.VMEM((nprob, C + GDN_HEAD, N_DIR * C), bf16),
                        pltpu.VMEM((N_DIR, nprob, C, GDN_HEAD), bf16),
                        pltpu.VMEM((nunits, C, LANES), f32),
                        pltpu.VMEM((nunits, 4 * nh, C), f32)],
        compiler_params=_cparams(("parallel",)),
        name="gdn_scan" if emit_o else "gdn_scan_state",
    )(*args)


def _post_kernel(x_ref, gate_ref, y5_ref, zs_ref, og_ref, zg_ref,
                 wglu_ref, bglu_ref, nw_ref, wout_ref, lng_ref, lnb_ref, o_ref, *, tt, alpha):
    nb, _, d = x_ref.shape
    rows = nb * tt
    npk = y5_ref.shape[0]
    d_s5 = npk * LANES
    d_gdn = og_ref.shape[2]
    parts = []
    for j in range(npk):
        parts.append(jnp.swapaxes(y5_ref[j].reshape(tt, nb, LANES), 0, 1).reshape(rows, LANES))
    y = jnp.concatenate(parts, axis=1)
    g = 0.5 * y * (1.0 + lax.erf(y * (2.0 ** -0.5)))
    glu = jnp.dot(g.astype(bf16), wglu_ref[...], preferred_element_type=f32) + bglu_ref[...]
    s5 = g * _sigmoid(glu) * _silu(zs_ref[...].astype(f32).reshape(rows, d_s5))
    o = og_ref[...].reshape(rows, d_gdn)
    zg = zg_ref[...].astype(f32).reshape(rows, d_gdn)
    gd = []
    for h in range(d_gdn // GDN_HEAD):
        oh = o[:, h * GDN_HEAD:(h + 1) * GDN_HEAD]
        gd.append(oh * lax.rsqrt(jnp.mean(oh * oh, axis=-1, keepdims=True) + NORM_EPS) * nw_ref[...])
    gdn = jnp.concatenate(gd, axis=1) * _silu(zg)
    mix = jnp.concatenate([s5, gdn], axis=1).astype(bf16)
    yv = jnp.dot(mix, wout_ref[...], preferred_element_type=f32).reshape(nb, tt, d)
    r = alpha * x_ref[...] + gate_ref[...] * yv
    mu = jnp.mean(r, axis=-1, keepdims=True)
    var = jnp.mean(jnp.square(r - mu), axis=-1, keepdims=True)
    o_ref[...] = (r - mu) * lax.rsqrt(var + LN_EPS) * lng_ref[...] + lnb_ref[...]


def _post(x, gate, y5, zs, o_gdn, zg, wglu, bglu, nw, wout, lng, lnb, alpha):
    nb, L, d = x.shape
    tt = 32
    npk = y5.shape[0]
    d_s5 = npk * LANES
    d_gdn = o_gdn.shape[2]
    kern = functools.partial(_post_kernel, tt=tt, alpha=alpha)
    tok = lambda i: (0, i, 0)
    c2 = lambda i: (0, 0)
    c3 = lambda i: (0, 0, 0)
    y_spec = pl.BlockSpec((npk, tt // S5_T, S5_T, nb, LANES), lambda i: (0, i, 0, 0, 0))
    return pl.pallas_call(
        kern,
        grid=(L // tt,),
        in_specs=[pl.BlockSpec((nb, tt, d), tok),
                  pl.BlockSpec((nb, 1, d), c3),
                  y_spec,
                  pl.BlockSpec((nb, tt, d_s5), tok),
                  pl.BlockSpec((nb, tt, d_gdn), tok),
                  pl.BlockSpec((nb, tt, d_gdn), tok),
                  pl.BlockSpec(wglu.shape, c2),
                  pl.BlockSpec(bglu.shape, c2),
                  pl.BlockSpec(nw.shape, c2),
                  pl.BlockSpec(wout.shape, c2),
                  pl.BlockSpec(lng.shape, c3),
                  pl.BlockSpec(lnb.shape, c3)],
        out_specs=pl.BlockSpec((nb, tt, d), tok),
        out_shape=jax.ShapeDtypeStruct((nb, L, d), f32),
        compiler_params=_cparams(("parallel",)),
        name="post",
    )(x, gate, y5, zs, o_gdn, zg, wglu, bglu, nw, wout, lng, lnb)


def _layer(x, c, ctx, c_ctx, p, alpha):
    nb, L, d = x.shape
    d_s5 = p['s5_d'].shape[0]
    d_gdn = p['conv_w'].shape[-1] // 3
    nh = d_gdn // GDN_HEAD
    npk = d_s5 // LANES

    pad = (-(nb + 1)) % 8
    cond = jnp.concatenate([c, c_ctx[None, :], jnp.zeros((pad, d), f32)], axis=0)
    m = _ada(cond, p['w_ada'], p['b_ada'])
    shift, scale, gate = m[:nb, :d], m[:nb, d:2 * d], m[:nb, 2 * d:]
    shift_c = jnp.broadcast_to(m[nb:nb + 1, :d], (nb, d))
    scale_c = jnp.broadcast_to(m[nb:nb + 1, d:2 * d], (nb, d))

    w_in = p['w_in']
    wu = w_in[:, :d_s5].astype(bf16)
    wm = w_in[:, d_s5:2 * d_s5 + 4 * d_gdn].astype(bf16)
    ng = 2 * N_DIR * nh
    wg = jnp.pad(w_in[:, 2 * d_s5 + 4 * d_gdn:], ((0, 0), (0, LANES - ng))).astype(bf16)
    gp = jnp.pad(jnp.stack([p['gdn_a_log'].reshape(-1), p['gdn_dt_bias'].reshape(-1)]).astype(f32),
                 ((0, 0), (ng // 2, LANES - ng)))

    u5, zs, qkv, zg, gates = _inproj(x, scale[:, None, :], shift[:, None, :], wu, wm, wg, gp, d_s5, d_gdn)
    u5c, _, qkvc, _, gatesc = _inproj(ctx, scale_c[:, None, :], shift_c[:, None, :], wu, wm, wg, gp, d_s5, d_gdn)

    sw = _s5_weights(p['s5_lambda_re'], p['s5_lambda_im'], p['s5_log_dt'],
                     p['s5_b_re'], p['s5_b_im'], p['s5_c_re'], p['s5_c_im'])
    sdim = sw['a_f'].shape[-1]
    a_f = jnp.broadcast_to(sw['a_f'][:, None, :], (npk, nb, sdim))
    a_r = jnp.broadcast_to(sw['a_r'][:, None, :], (npk, nb, sdim))
    h_zero = jnp.zeros((npk, nb, sdim), f32)
    d_skip = jnp.tile(p['s5_d'].astype(f32).reshape(npk, 1, LANES), (1, 1, S5_T))
    (hc_f,) = _s5_pass(u5c, sw['m_in_f'], a_f, h_zero, rev=False, emit_y=False)
    (hc_r,) = _s5_pass(u5c, sw['m_in_r'], a_r, h_zero, rev=True, emit_y=False)
    y_f, _ = _s5_pass(u5, sw['m_in_f'], a_f, hc_f, sw['m_out_f'], sw['m_intra'], d_skip, rev=False, emit_y=True)
    y_s5, _ = _s5_pass(u5, sw['m_in_r'], a_r, hc_r, sw['m_out_r'], y_add=y_f, rev=True, emit_y=True)

    ql, kl, vl = _gdn_prep(qkv, p['conv_w'], True)
    qc, kc, vc = _gdn_prep(qkvc, p['conv_w'], False)
    (s_ctx,) = _gdn_scan(qc, kc, vc, gatesc, None, emit_o=False)
    o_gdn, _ = _gdn_scan(ql, kl, vl, gates, s_ctx, emit_o=True)

    return _post(x, gate[:, None, :], y_s5, zs, o_gdn, zg,
                 p['w_glu'].astype(bf16), p['b_glu'].astype(f32).reshape(1, d_s5),
                 p['gdn_norm_w'].astype(f32).reshape(1, GDN_HEAD),
                 p['w_out'].astype(bf16), p['ln_g'].astype(f32).reshape(1, 1, d),
                 p['ln_b'].astype(f32).reshape(1, 1, d), alpha)


def kernel(x, c, ctx, c_ctx, w_ada, b_ada, w_in, s5_lambda_re, s5_lambda_im, s5_log_dt, s5_b_re, s5_b_im, s5_c_re, s5_c_im, s5_d, w_glu, b_glu, conv_w, gdn_a_log, gdn_dt_bias, gdn_norm_w, w_out, ln_g, ln_b):
    depth = w_ada.shape[0]
    assert depth == 1, "context-token outputs are only skipped for the last layer"
    alpha = (2.0 * depth) ** 0.25
    p = {
        'w_ada': w_ada[0], 'b_ada': b_ada[0], 'w_in': w_in[0],
        's5_lambda_re': s5_lambda_re[0], 's5_lambda_im': s5_lambda_im[0], 's5_log_dt': s5_log_dt[0],
        's5_b_re': s5_b_re[0], 's5_b_im': s5_b_im[0], 's5_c_re': s5_c_re[0], 's5_c_im': s5_c_im[0],
        's5_d': s5_d[0], 'w_glu': w_glu[0], 'b_glu': b_glu[0], 'conv_w': conv_w[0],
        'gdn_a_log': gdn_a_log[0], 'gdn_dt_bias': gdn_dt_bias[0], 'gdn_norm_w': gdn_norm_w[0],
        'w_out': w_out[0], 'ln_g': ln_g[0], 'ln_b': ln_b[0],
    }
    return _layer(x, c, ctx, c_ctx, p, alpha)
```

```python
import functools
import math

import numpy as np
import jax
import jax.numpy as jnp
from jax import lax
from jax.experimental import pallas as pl
from jax.experimental.pallas import tpu as pltpu

f32 = jnp.float32
bf16 = jnp.bfloat16
HI = lax.Precision.HIGHEST

LANES = 128
S5_GROUP = 16
S5_STATE = 64
S5_T = 8
PACK_G = LANES // S5_GROUP
GDN_HEAD = 128
GDN_CHUNK = 64
GRID_W = 64
N_DIR = 2
LN_EPS = 1e-5
NORM_EPS = 1e-6
VMEM_LIMIT = 56 * 1024 * 1024


def _cparams(sem):
    return pltpu.CompilerParams(dimension_semantics=sem, vmem_limit_bytes=VMEM_LIMIT)


def _sigmoid(x):
    return 0.5 + 0.5 * jnp.tanh(0.5 * x)


def _silu(x):
    h = 0.5 * x
    return h + h * jnp.tanh(h)


def _ada_kernel(c_ref, w_ref, b_ref, o_ref):
    o_ref[...] = jnp.dot(_silu(c_ref[...]), w_ref[...], precision=HI,
                         preferred_element_type=f32) + b_ref[...]


def _ada(cond, w_ada, b_ada):
    rows, d = cond.shape
    n = w_ada.shape[1]
    tn = 512
    return pl.pallas_call(
        _ada_kernel,
        grid=(n // tn,),
        in_specs=[pl.BlockSpec((rows, d), lambda i: (0, 0)),
                  pl.BlockSpec((d, tn), lambda i: (0, i)),
                  pl.BlockSpec((1, tn), lambda i: (0, i))],
        out_specs=pl.BlockSpec((rows, tn), lambda i: (0, i)),
        out_shape=jax.ShapeDtypeStruct((rows, n), f32),
        compiler_params=_cparams(("parallel",)),
        name="ada",
    )(cond, w_ada, b_ada.reshape(1, n))


def _inproj_kernel(x_ref, scale_ref, shift_ref, wu_ref, wm_ref, wg_ref, gp_ref,
                   u_ref, zs_ref, qkv_ref, zg_ref, gate_ref, *, tt, d_s5, d_gdn, n_beta):
    nb, _, d = x_ref.shape
    sc = 1.0 + scale_ref[...]
    sh = shift_ref[...]
    h = (x_ref[...] * sc + sh).reshape(nb * tt, d).astype(bf16)
    r = jnp.dot(h, wm_ref[...], preferred_element_type=f32)
    zs_ref[...] = r[:, :d_s5].reshape(nb, tt, d_s5).astype(bf16)
    qkv_ref[...] = r[:, d_s5:d_s5 + 3 * d_gdn].reshape(nb, tt, 3 * d_gdn)
    zg_ref[...] = r[:, d_s5 + 3 * d_gdn:].reshape(nb, tt, d_gdn).astype(bf16)
    lg = jnp.dot(h, wg_ref[...], preferred_element_type=f32)
    lane = lax.broadcasted_iota(jnp.int32, lg.shape, 1)
    a = lg + gp_ref[1:2, :]
    softplus = jnp.maximum(a, 0.0) + jnp.log1p(jnp.exp(-jnp.abs(a)))
    gates = jnp.where(lane < n_beta, _sigmoid(lg), -jnp.exp(gp_ref[0:1, :]) * softplus)
    gate_ref[...] = gates.reshape(nb, tt, LANES)
    sc2 = sc[:, 0, :]
    sh2 = sh[:, 0, :]
    ht = (jnp.swapaxes(x_ref[...], 0, 1) * sc2[None] + sh2[None]).reshape(tt * nb, d).astype(bf16)
    ru = jnp.dot(ht, wu_ref[...], preferred_element_type=f32)
    for j in range(d_s5 // LANES):
        u_ref[j] = ru[:, j * LANES:(j + 1) * LANES].reshape(tt // S5_T, S5_T, nb, LANES)


def _inproj(x, scale, shift, wu, wm, wg, gp, d_s5, d_gdn):
    nb, L, d = x.shape
    tt = 32
    npk = d_s5 // LANES
    kern = functools.partial(_inproj_kernel, tt=tt, d_s5=d_s5, d_gdn=d_gdn,
                             n_beta=N_DIR * (d_gdn // GDN_HEAD))
    const2 = lambda i: (0, 0)
    const3 = lambda i: (0, 0, 0)
    return pl.pallas_call(
        kern,
        grid=(L // tt,),
        in_specs=[pl.BlockSpec((nb, tt, d), lambda i: (0, i, 0)),
                  pl.BlockSpec((nb, 1, d), const3),
                  pl.BlockSpec((nb, 1, d), const3),
                  pl.BlockSpec(wu.shape, const2),
                  pl.BlockSpec(wm.shape, const2),
                  pl.BlockSpec(wg.shape, const2),
                  pl.BlockSpec(gp.shape, const2)],
        out_specs=[pl.BlockSpec((npk, tt // S5_T, S5_T, nb, LANES), lambda i: (0, i, 0, 0, 0)),
                   pl.BlockSpec((nb, tt, d_s5), lambda i: (0, i, 0)),
                   pl.BlockSpec((nb, tt, 3 * d_gdn), lambda i: (0, i, 0)),
                   pl.BlockSpec((nb, tt, d_gdn), lambda i: (0, i, 0)),
                   pl.BlockSpec((nb, tt, LANES), lambda i: (0, i, 0))],
        out_shape=[jax.ShapeDtypeStruct((npk, L // S5_T, S5_T, nb, LANES), f32),
                   jax.ShapeDtypeStruct((nb, L, d_s5), bf16),
                   jax.ShapeDtypeStruct((nb, L, 3 * d_gdn), f32),
                   jax.ShapeDtypeStruct((nb, L, d_gdn), bf16),
                   jax.ShapeDtypeStruct((nb, L, LANES), f32)],
        compiler_params=_cparams(("parallel",)),
        name="inproj",
    )(x, scale, shift, wu, wm, wg, gp)


def _s5_kernel(*refs, ct, rev, emit_y, intra, add_y):
    it = iter(refs)
    u_ref, cin_ref, at_ref, h0_ref = next(it), next(it), next(it), next(it)
    cout_ref = next(it) if emit_y else None
    cintra_ref = next(it) if intra else None
    d_ref = next(it) if intra else None
    yadd_ref = next(it) if add_y else None
    y_ref = next(it) if emit_y else None
    hfin_ref = next(it)
    h_sc, v_sc, hall_sc, min_sc = next(it), next(it), next(it), next(it)
    mout_sc = next(it) if emit_y else None
    mintra_sc = next(it) if intra else None

    t = pl.program_id(1)
    nt = pl.num_programs(1)
    nb = u_ref.shape[3]
    sdim = h_sc.shape[1]
    half = sdim // 2
    gstate = half // PACK_G

    @pl.when(t == 0)
    def _():
        h_sc[...] = h0_ref[0]
        r_in = lax.broadcasted_iota(jnp.int32, (LANES, sdim), 0) // S5_GROUP
        c_in = (lax.broadcasted_iota(jnp.int32, (LANES, sdim), 1) % half) // gstate
        mask_in = (r_in == c_in).astype(f32)
        for s in range(S5_T):
            min_sc[s * LANES:(s + 1) * LANES, :] = (
                jnp.tile(jnp.concatenate([cin_ref[s, 0, 0], cin_ref[s, 1, 0]], axis=1), (PACK_G, 1))
                * mask_in).astype(bf16)
        if emit_y:
            r_out = lax.broadcasted_iota(jnp.int32, (half, LANES), 0) // gstate
            c_out = lax.broadcasted_iota(jnp.int32, (half, LANES), 1) // S5_GROUP
            mask_out = (r_out == c_out).astype(f32)
            for i in range(S5_T):
                for ri in range(2):
                    blk = cout_ref[i, ri, 0]
                    mout_sc[ri * half:(ri + 1) * half, i * LANES:(i + 1) * LANES] = (
                        jnp.tile(blk, (PACK_G, 1)) * mask_out).astype(bf16)
        if intra:
            r_x = lax.broadcasted_iota(jnp.int32, (LANES, LANES), 0) // S5_GROUP
            c_x = lax.broadcasted_iota(jnp.int32, (LANES, LANES), 1) // S5_GROUP
            mask_x = (r_x == c_x).astype(f32)
            for s in range(S5_T):
                for i in range(S5_T):
                    mintra_sc[s * LANES:(s + 1) * LANES, i * LANES:(i + 1) * LANES] = (
                        jnp.tile(cintra_ref[0, s, i], (PACK_G, 1)) * mask_x).astype(bf16)

    x = jnp.concatenate([u_ref[0, :, i, :, :].reshape(ct * nb, LANES) for i in range(S5_T)], axis=1)
    xb = x.astype(bf16)
    v_sc[...] = jnp.dot(xb, min_sc[...], preferred_element_type=f32)
    are = at_ref[0, :, :half]
    aim = at_ref[0, :, half:]

    def body(s, carry):
        k = (ct - 1 - s) if rev else s
        off = pl.multiple_of(k * nb, nb)
        hre, him = carry
        hall_sc[pl.ds(off, nb), :half] = hre
        hall_sc[pl.ds(off, nb), half:] = him
        vre = v_sc[pl.ds(off, nb), :half]
        vim = v_sc[pl.ds(off, nb), half:]
        return are * hre - aim * him + vre, are * him + aim * hre + vim

    hre, him = lax.fori_loop(0, ct, body, (h_sc[:, :half], h_sc[:, half:]))
    h_sc[:, :half] = hre
    h_sc[:, half:] = him

    @pl.when(t == nt - 1)
    def _():
        hfin_ref[0] = h_sc[...]

    if emit_y:
        y = jnp.dot(hall_sc[...].astype(bf16), mout_sc[...], preferred_element_type=f32)
        if intra:
            y = y + jnp.dot(xb, mintra_sc[...], preferred_element_type=f32) + x * d_ref[0]
        for i in range(S5_T):
            yi = y[:, i * LANES:(i + 1) * LANES].reshape(ct, nb, LANES)
            y_ref[0, :, i, :, :] = yi + yadd_ref[0, :, i, :, :] if add_y else yi


def _s5_pass(u5, c_in, a_t, h0, c_out=None, c_intra=None, d_skip=None, y_add=None, *, rev, emit_y):
    npk, nc, _, nb, _ = u5.shape
    ct = min(32, nc)
    nt = nc // ct
    intra = c_intra is not None
    add_y = y_add is not None
    kdim = S5_T * LANES
    sdim = a_t.shape[-1]
    tmap = (lambda j, t: (j, nt - 1 - t, 0, 0, 0)) if rev else (lambda j, t: (j, t, 0, 0, 0))
    pmap = lambda j, t: (j, 0, 0)
    wmap = lambda j, t: (0, 0, j, 0, 0)
    pmap5 = lambda j, t: (j, 0, 0, 0, 0)
    args = [u5, c_in, a_t, h0]
    in_specs = [pl.BlockSpec((1, ct, S5_T, nb, LANES), tmap),
                pl.BlockSpec(c_in.shape[:2] + (1,) + c_in.shape[3:], wmap),
                pl.BlockSpec((1, nb, sdim), pmap),
                pl.BlockSpec((1, nb, sdim), pmap)]
    scratch = [pltpu.VMEM((nb, sdim), f32),
               pltpu.VMEM((ct * nb, sdim), f32),
               pltpu.VMEM((ct * nb, sdim), f32),
               pltpu.VMEM((kdim, sdim), bf16)]
    if emit_y:
        args.append(c_out)
        in_specs.append(pl.BlockSpec(c_out.shape[:2] + (1,) + c_out.shape[3:], wmap))
        scratch.append(pltpu.VMEM((sdim, kdim), bf16))
    if intra:
        args += [c_intra, d_skip]
        in_specs += [pl.BlockSpec((1,) + c_intra.shape[1:], pmap5), pl.BlockSpec((1, 1, kdim), pmap)]
        scratch.append(pltpu.VMEM((kdim, kdim), bf16))
    if add_y:
        args.append(y_add)
        in_specs.append(pl.BlockSpec((1, ct, S5_T, nb, LANES), tmap))
    out_specs, out_shape = [], []
    if emit_y:
        out_specs.append(pl.BlockSpec((1, ct, S5_T, nb, LANES), tmap))
        out_shape.append(jax.ShapeDtypeStruct(u5.shape, f32))
    out_specs.append(pl.BlockSpec((1, nb, sdim), pmap))
    out_shape.append(jax.ShapeDtypeStruct((npk, nb, sdim), f32))
    kern = functools.partial(_s5_kernel, ct=ct, rev=rev, emit_y=emit_y, intra=intra, add_y=add_y)
    return pl.pallas_call(
        kern,
        grid=(npk, nt),
        in_specs=in_specs,
        out_specs=out_specs,
        out_shape=out_shape,
        scratch_shapes=scratch,
        compiler_params=_cparams(("parallel", "arbitrary")),
        name="s5_rev" if rev else "s5_fwd",
    )(*args)


def _s5_weights(lam_re, lam_im, log_dt, b_re, b_im, c_re, c_im):
    T = S5_T
    ndir, G, P = lam_re.shape
    npk = G // PACK_G
    dt = jnp.exp(log_dt.astype(f32))[..., None]
    lr, li = lam_re.astype(f32), lam_im.astype(f32)
    zr, zi = lr * dt, li * dt

    nr = jnp.expm1(zr) * jnp.cos(zi) - 2.0 * jnp.square(jnp.sin(0.5 * zi))
    ni = jnp.exp(zr) * jnp.sin(zi)
    den = lr * lr + li * li
    qr = (nr * lr + ni * li) / den
    qi = (ni * lr - nr * li) / den
    bre, bim = b_re.astype(f32), b_im.astype(f32)
    bbr = qr[..., None] * bre - qi[..., None] * bim
    bbi = qr[..., None] * bim + qi[..., None] * bre
    cr, ci = c_re.astype(f32), c_im.astype(f32)

    def powers(e, d):
        ev = jnp.asarray(np.asarray(e, np.float32))[:, None, None]
        m = jnp.exp(ev * zr[d])
        return m * jnp.cos(ev * zi[d]), m * jnp.sin(ev * zi[d])

    def c_times_a(e, d):
        pr, pi = powers(e, d)
        pr, pi = pr[:, :, None, :], pi[:, :, None, :]
        return cr[d] * pr - ci[d] * pi, cr[d] * pi + ci[d] * pr

    steps = np.arange(T)

    def pack_in(e, d):
        pr, pi = powers(e, d)
        pr = pr.reshape(T, npk, 1, PACK_G * P)
        pi = pi.reshape(T, npk, 1, PACK_G * P)
        tr = lambda b: b.reshape(npk, PACK_G, P, S5_GROUP).transpose(0, 3, 1, 2).reshape(npk, S5_GROUP, PACK_G * P)
        btr, bti = tr(bbr[d]), tr(bbi[d])
        return jnp.stack([pr * btr - pi * bti, pr * bti + pi * btr], axis=1)

    def pack_out(e, d):
        pr, pi = powers(e, d)
        rep = lambda v: jnp.repeat(v.reshape(T, npk, PACK_G, P).transpose(0, 1, 3, 2), S5_GROUP, axis=-1)
        pr, pi = rep(pr), rep(pi)
        tr = lambda c: c.reshape(npk, PACK_G, S5_GROUP, P).transpose(0, 3, 1, 2).reshape(npk, P, LANES)
        ctr, cti = tr(cr[d]), tr(ci[d])
        return jnp.stack([ctr * pr - cti * pi, -(ctr * pi + cti * pr)], axis=1)

    m_in_f = pack_in(T - 1 - steps, 0)
    m_in_r = pack_in(steps, 1)
    m_out_f = pack_out(steps + 1, 0)
    m_out_r = pack_out(T - steps, 1)

    lag = steps[None, :] - steps[:, None]

    def lag_term(d, sign):
        car, cai = c_times_a(np.abs(lag).reshape(-1), d)
        mask = jnp.asarray((sign * lag >= 0).reshape(-1, 1, 1, 1), f32)
        k = (jnp.einsum('tgop,gpi->tgio', car * mask, bbr[d])
             - jnp.einsum('tgop,gpi->tgio', cai * mask, bbi[d]))
        return k.reshape(T, T, G, S5_GROUP, S5_GROUP)

    toe = lag_term(0, 1) + lag_term(1, -1)
    toe = toe.reshape(T, T, npk, PACK_G, S5_GROUP, S5_GROUP).transpose(2, 0, 1, 4, 3, 5)
    m_intra = toe.reshape(npk, T, T, S5_GROUP, LANES)

    def a_chunk(d):
        pr, pi = powers([T], d)
        return jnp.concatenate([pr.reshape(npk, PACK_G * P), pi.reshape(npk, PACK_G * P)], axis=-1)

    return dict(m_in_f=m_in_f, m_in_r=m_in_r, m_out_f=m_out_f, m_out_r=m_out_r,
                m_intra=m_intra, a_f=a_chunk(0), a_r=a_chunk(1))


def _gdn_prep_kernel(*refs, tile, rows_conv, d_gdn):
    if rows_conv:
        main_ref, up_ref, dn_ref, w_ref, q_ref, k_ref, v_ref = refs
    else:
        main_ref, w_ref, q_ref, k_ref, v_ref = refs
    w = w_ref[...]
    if rows_conv:
        i = pl.program_id(1)
        n = pl.num_programs(1)
        up = up_ref[0] * (i > 0).astype(f32)
        dn = dn_ref[0] * (i < n - 1).astype(f32)
        ext = jnp.concatenate([up, main_ref[0], dn], axis=0)
        base = GRID_W
        drs = (-1, 0, 1)
        col = lax.broadcasted_iota(jnp.int32, (tile, 1), 0) % GRID_W
        mask_m = col != 0
        mask_p = col != GRID_W - 1
    else:
        ext = main_ref[0]
        base = 0
        drs = (0,)
        col = lax.broadcasted_iota(jnp.int32, (tile, 1), 0)
        mask_m = col != 0
        mask_p = col != tile - 1
    acc = {}
    for dc in (-1, 0, 1):
        a = None
        for dr in drs:
            tap = w[(dr + 1) * 3 + (dc + 1)][None, :]
            term = ext[base + GRID_W * dr: base + GRID_W * dr + tile] * tap
            a = term if a is None else a + term
        acc[dc] = a
    conv = (acc[0] + jnp.where(mask_m, pltpu.roll(acc[-1], 1, 0), 0.0)
            + jnp.where(mask_p, pltpu.roll(acc[1], tile - 1, 0), 0.0))
    s = _silu(conv)
    nh = d_gdn // GDN_HEAD
    for h in range(nh):
        qh = s[:, h * GDN_HEAD:(h + 1) * GDN_HEAD]
        kh = s[:, d_gdn + h * GDN_HEAD: d_gdn + (h + 1) * GDN_HEAD]
        qn = qh * (lax.rsqrt(jnp.sum(qh * qh, axis=-1, keepdims=True) + NORM_EPS) * (GDN_HEAD ** -0.5))
        kn = kh * lax.rsqrt(jnp.sum(kh * kh, axis=-1, keepdims=True) + NORM_EPS)
        q_ref[0, :, h * GDN_HEAD:(h + 1) * GDN_HEAD] = qn.astype(bf16)
        k_ref[0, :, h * GDN_HEAD:(h + 1) * GDN_HEAD] = kn.astype(bf16)
    v_ref[0] = s[:, 2 * d_gdn:].astype(bf16)


def _gdn_prep(qkv, conv_w, rows_conv):
    nb, L, c3 = qkv.shape
    d_gdn = c3 // 3
    w9 = conv_w.reshape(9, c3).astype(f32)
    if rows_conv:
        tile = min(512, L)
        nt = L // tile
        r = tile // GRID_W
        nrows = L // GRID_W
        in_specs = [pl.BlockSpec((1, tile, c3), lambda b, i: (b, i, 0)),
                    pl.BlockSpec((1, GRID_W, c3), lambda b, i: (b, jnp.maximum(i * r - 1, 0), 0)),
                    pl.BlockSpec((1, GRID_W, c3), lambda b, i: (b, jnp.minimum((i + 1) * r, nrows - 1), 0)),
                    pl.BlockSpec((9, c3), lambda b, i: (0, 0))]
        args = (qkv, qkv, qkv, w9)
    else:
        tile, nt = L, 1
        in_specs = [pl.BlockSpec((1, tile, c3), lambda b, i: (b, i, 0)),
                    pl.BlockSpec((9, c3), lambda b, i: (0, 0))]
        args = (qkv, w9)
    kern = functools.partial(_gdn_prep_kernel, tile=tile, rows_conv=rows_conv, d_gdn=d_gdn)
    o_spec = pl.BlockSpec((1, tile, d_gdn), lambda b, i: (b, i, 0))
    o_shape = jax.ShapeDtypeStruct((nb, L, d_gdn), bf16)
    return pl.pallas_call(
        kern,
        grid=(nb, nt),
        in_specs=in_specs,
        out_specs=[o_spec, o_spec, o_spec],
        out_shape=[o_shape, o_shape, o_shape],
        compiler_params=_cparams(("parallel", "parallel")),
        name="gdn_prep",
    )(*args)


def _chunk_cumsum(g, rev):
    n = g.shape[0]
    r = lax.broadcasted_iota(jnp.int32, g.shape, 0)
    s = 1
    while s < n:
        if rev:
            g = g + jnp.where(r < n - s, pltpu.roll(g, n - s, 0), 0.0)
        else:
            g = g + jnp.where(r >= s, pltpu.roll(g, s, 0), 0.0)
        s *= 2
    return g


def _tri_solve_rows(apl_sc, tl_sc, dirn, upper):
    nblk = GDN_CHUNK // 8
    base = dirn * GDN_CHUNK
    sub = lax.broadcasted_iota(jnp.int32, (8, LANES), 0)
    for step_blk in range(nblk):
        blk = (nblk - 1 - step_blk) if upper else step_blk
        mblocks = range(blk, nblk) if upper else range(0, blk + 1)

        def solve_row(s, carry, blk=blk, mblocks=mblocks):
            ii = (7 - s) if upper else s
            i = blk * 8 + ii
            acc = {}
            for mb in mblocks:
                arow = apl_sc[i, base + mb * 8:base + (mb + 1) * 8, :]
                jblocks = range(mb, nblk) if upper else range(0, mb + 1)
                for mm in range(8):
                    ab = jnp.broadcast_to(arow[mm:mm + 1, :], (8, LANES))
                    for jb in jblocks:
                        term = ab * tl_sc[mb * 8 + mm, base + jb * 8:base + (jb + 1) * 8, :]
                        key = (jb, mm % 2)
                        acc[key] = term if key not in acc else acc[key] + term
            for jb in (range(blk, nblk) if upper else range(0, blk + 1)):
                tot = acc[(jb, 0)] + acc[(jb, 1)]
                unit_row = jnp.where(sub == ii, 1.0, 0.0) if jb == blk else 0.0
                tl_sc[i, base + jb * 8:base + (jb + 1) * 8, :] = unit_row - tot
            return carry

        lax.fori_loop(0, 8, solve_row, 0)


def _gdn_scan_kernel(*refs, nchunks, nh, emit_o, has_s0):
    it = iter(refs)
    q_ref, k_ref, v_ref, g_ref = next(it), next(it), next(it), next(it)
    s0_ref = next(it) if has_s0 else None
    o_ref = next(it) if emit_o else None
    sfin_ref = next(it)
    s_sc, at_sc, apl_sc, tl_sc, mk_sc, w_sc, gcol_sc, grow_sc = (next(it) for _ in range(8))
    C = GDN_CHUNK
    bb = q_ref.shape[0]
    nunits = bb * nchunks
    s_sc[...] = s0_ref[...] if has_s0 else jnp.zeros(s_sc.shape, f32)
    if emit_o:
        o_ref[...] = jnp.zeros(o_ref.shape, f32)
    row = lax.broadcasted_iota(jnp.int32, (C, N_DIR * C), 0)
    lane = lax.broadcasted_iota(jnp.int32, (C, N_DIR * C), 1)
    fwd_half = lane < C
    col = jnp.where(fwd_half, lane, lane - C)
    tri = (fwd_half & (row >= col)) | (~fwd_half & (row <= col))
    strict = (fwd_half & (row > col)) | (~fwd_half & (row < col))
    nbeta = N_DIR * nh
    gate_lane = lax.broadcasted_iota(jnp.int32, (C, LANES), 1)

    def unit_of(unit):
        bl = unit // nchunks
        return bl, pl.multiple_of((unit % nchunks) * C, C)

    def phase_a(unit, carry):
        bl, off = unit_of(unit)
        gts = g_ref[bl, pl.ds(off, C), :]
        gmix = jnp.where(gate_lane < nbeta, gts,
                         jnp.where(gate_lane < nbeta + nh, _chunk_cumsum(gts, False), _chunk_cumsum(gts, True)))
        g_rows = gmix.T[:2 * nbeta]
        gcol_sc[unit] = gmix
        grow_sc[unit] = g_rows
        for h in range(nh):
            hs = slice(h * GDN_HEAD, (h + 1) * GDN_HEAD)
            cbf, cbr = h, nh + h
            cgf, cgr = nbeta + h, nbeta + nh + h
            kb = k_ref[bl, pl.ds(off, C), hs]
            k2 = jnp.concatenate([kb, kb], axis=0)
            if emit_o:
                kq = jnp.concatenate([kb, q_ref[bl, pl.ds(off, C), hs]], axis=0)
            else:
                kq = kb
            kkqk = lax.dot_general(kq, k2, (((1,), (1,)), ((), ())), preferred_element_type=f32)
            g_col = jnp.where(fwd_half, gmix[:, cgf:cgf + 1], gmix[:, cgr:cgr + 1])
            g_row = jnp.concatenate([g_rows[cgf:cgf + 1, :], g_rows[cgr:cgr + 1, :]], axis=1)
            b_row = jnp.concatenate([g_rows[cbf:cbf + 1, :], g_rows[cbr:cbr + 1, :]], axis=1)
            decay_b = jnp.exp(jnp.where(tri, g_col - g_row, -1e30)) * b_row
            at_sc[unit * nh + h] = jnp.where(strict, kkqk[:C] * decay_b, 0.0)
            if emit_o:
                mk_sc[unit * nh + h, :C, :] = (kkqk[C:] * decay_b).astype(bf16)
        return carry

    lax.fori_loop(0, nunits, phase_a, 0, unroll=2)

    def swap_in(ib, carry):
        rows = pl.ds(pl.multiple_of(ib * 8, 8), 8)
        tl_sc[rows] = jnp.swapaxes(at_sc[:, rows, :], 0, 1)
        return carry

    def to_lanes(i, carry):
        apl_sc[i] = tl_sc[i].T
        return carry

    lax.fori_loop(0, C // 8, swap_in, 0)
    lax.fori_loop(0, C, to_lanes, 0, unroll=4)
    tl_sc[...] = jnp.zeros(tl_sc.shape, f32)
    _tri_solve_rows(apl_sc, tl_sc, 0, False)
    _tri_solve_rows(apl_sc, tl_sc, 1, True)

    def from_lanes(i, carry):
        apl_sc[i] = tl_sc[i].T
        return carry

    def swap_out(ib, carry):
        rows = pl.ds(pl.multiple_of(ib * 8, 8), 8)
        at_sc[:, rows, :] = jnp.swapaxes(apl_sc[rows], 0, 1)
        return carry

    lax.fori_loop(0, C, from_lanes, 0, unroll=4)
    lax.fori_loop(0, C // 8, swap_out, 0)

    u0_sc = (apl_sc, tl_sc)

    def u0_slot(p):
        return p // 2, pl.ds(pl.multiple_of((p % 2) * C, C), C)

    eye_bf = (lax.broadcasted_iota(jnp.int32, (GDN_HEAD, GDN_HEAD), 0)
              == lax.broadcasted_iota(jnp.int32, (GDN_HEAD, GDN_HEAD), 1)).astype(bf16)

    def phase_b2(unit, carry):
        bl, off = unit_of(unit)
        g_rows = grow_sc[unit]
        g_cols = gcol_sc[unit]
        zero_kv = jnp.zeros((C, GDN_HEAD), bf16)
        for h in range(nh):
            hs = slice(h * GDN_HEAD, (h + 1) * GDN_HEAD)
            p = unit * nh + h
            cbf, cbr = h, nh + h
            cgf, cgr = nbeta + h, nbeta + nh + h
            t_both = at_sc[p]
            kb = k_ref[bl, pl.ds(off, C), hs]
            vb = v_ref[bl, pl.ds(off, C), hs]
            g_row = jnp.concatenate([g_rows[cgf:cgf + 1, :], g_rows[cgr:cgr + 1, :]], axis=1)
            b_row = jnp.concatenate([g_rows[cbf:cbf + 1, :], g_rows[cbr:cbr + 1, :]], axis=1)
            g_last = jnp.concatenate([jnp.broadcast_to(g_cols[C - 1:C, cgf:cgf + 1], (1, C)),
                                      jnp.broadcast_to(g_cols[0:1, cgr:cgr + 1], (1, C))], axis=1)
            v_bd = jnp.concatenate([jnp.concatenate([vb, zero_kv], axis=1),
                                    jnp.concatenate([zero_kv, vb], axis=1)], axis=0)
            k_bd = jnp.concatenate([jnp.concatenate([kb, zero_kv], axis=1),
                                    jnp.concatenate([zero_kv, kb], axis=1)], axis=0)
            u0 = jnp.dot(t_both.astype(bf16), v_bd, preferred_element_type=f32)
            w = jnp.dot((t_both * jnp.exp(g_row)).astype(bf16), k_bd, preferred_element_type=f32)
            slab, rows = u0_slot(p)
            for dirn in range(N_DIR):
                u0_sc[dirn][slab, rows, :] = u0[:, dirn * GDN_HEAD:(dirn + 1) * GDN_HEAD]
                w_sc[dirn, p] = w[:, dirn * GDN_HEAD:(dirn + 1) * GDN_HEAD].astype(bf16)
            k_t = lax.dot_general(eye_bf, jnp.concatenate([kb, kb], axis=0), (((1,), (1,)), ((), ())),
                                  preferred_element_type=f32)
            mk_sc[p, C:, :] = (k_t * (b_row * jnp.exp(g_last - g_row))).astype(bf16)
        return carry

    lax.fori_loop(0, nunits, phase_b2, 0, unroll=4)

    def phase_c(step, carry):
        chains = []
        for dirn, unit in ((0, step), (1, nunits - 1 - step)):
            bl, off = unit_of(unit)
            g_col_all = gcol_sc[unit]
            g_tot_all = g_col_all[0:1, :] if dirn else g_col_all[C - 1:C, :]
            for h in range(nh):
                cg = nbeta + dirn * nh + h
                hs = slice(h * GDN_HEAD, (h + 1) * GDN_HEAD)
                p = unit * nh + h
                slab, rows = u0_slot(p)
                lhs = w_sc[dirn, p]
                ch = dict(idx=(bl, dirn, h), off=off, hs=hs, dirn=dirn, u0=u0_sc[dirn][slab, rows, :],
                          g_last=jnp.exp(g_tot_all[:, cg:cg + 1]))
                if emit_o:
                    lhs = jnp.concatenate([lhs, q_ref[bl, pl.ds(off, C), hs]], axis=0)
                    ch['mk'] = mk_sc[p]
                    ch['gam'] = jnp.exp(g_col_all[:, cg:cg + 1])
                else:
                    ch['mk'] = mk_sc[p, C:, :]
                ch['lhs'] = lhs
                chains.append(ch)
        s_old = [s_sc[ch['idx']] for ch in chains]
        r = [jnp.dot(ch['lhs'], s.astype(bf16), preferred_element_type=f32) for ch, s in zip(chains, s_old)]
        zeros = jnp.zeros((C, GDN_HEAD), bf16)
        ub = [(ch['u0'] - ri[:C]).astype(bf16) for ch, ri in zip(chains, r)]
        m = [jnp.dot(ch['mk'], jnp.concatenate([zeros, ubi] if ch['dirn'] else [ubi, zeros], axis=0),
                     preferred_element_type=f32) for ch, ubi in zip(chains, ub)]
        for n, ch in enumerate(chains):
            if emit_o:
                bl = ch['idx'][0]
                o_ref[bl, pl.ds(ch['off'], C), ch['hs']] += ch['gam'] * r[n][C:] + m[n][:C]
                s_sc[ch['idx']] = ch['g_last'] * s_old[n] + m[n][C:]
            else:
                s_sc[ch['idx']] = ch['g_last'] * s_old[n] + m[n]
        return carry

    lax.fori_loop(0, nunits, phase_c, 0)
    sfin_ref[...] = s_sc[...]


def _gdn_scan(q, k, v, gates, s0, *, emit_o):
    nb, L, d_gdn = q.shape
    nh = d_gdn // GDN_HEAD
    nchunks = L // GDN_CHUNK
    C = GDN_CHUNK
    bb = max(1, min(nb, LANES // (nchunks * nh)))
    nunits = bb * nchunks
    nprob = nunits * nh
    assert nprob == LANES, (nb, L, nh)
    has_s0 = s0 is not None
    kern = functools.partial(_gdn_scan_kernel, nchunks=nchunks, nh=nh, emit_o=emit_o, has_s0=has_s0)
    seq = lambda b: (b, 0, 0)
    st = lambda b: (b, 0, 0, 0, 0)
    st_spec = pl.BlockSpec((bb, N_DIR, nh, GDN_HEAD, GDN_HEAD), st)
    st_shape = jax.ShapeDtypeStruct((nb, N_DIR, nh, GDN_HEAD, GDN_HEAD), f32)
    args = [q, k, v, gates]
    in_specs = [pl.BlockSpec((bb, L, d_gdn), seq)] * 3 + [pl.BlockSpec((bb, L, LANES), seq)]
    if has_s0:
        args.append(s0)
        in_specs.append(st_spec)
    out_specs, out_shape = [], []
    if emit_o:
        out_specs.append(pl.BlockSpec((bb, L, d_gdn), seq))
        out_shape.append(jax.ShapeDtypeStruct((nb, L, d_gdn), f32))
    out_specs.append(st_spec)
    out_shape.append(st_shape)
    return pl.pallas_call(
        kern,
        grid=(nb // bb,),
        in_specs=in_specs,
        out_specs=out_specs,
        out_shape=out_shape,
        scratch_shapes=[pltpu.VMEM((bb, N_DIR, nh, GDN_HEAD, GDN_HEAD), f32),
                        pltpu.VMEM((nprob, C, N_DIR * C), f32),
                        pltpu.VMEM((C, N_DIR * C, nprob), f32),
                        pltpu.VMEM((C, N_DIR * C, nprob), f32),
                        pltpu.VMEM((nprob, C + GDN_HEAD, N_DIR * C), bf16),
                        pltpu.VMEM((N_DIR, nprob, C, GDN_HEAD), bf16),
                        pltpu.VMEM((nunits, C, LANES), f32),
                        pltpu.VMEM((nunits, 4 * nh, C), f32)],
        compiler_params=_cparams(("parallel",)),
        name="gdn_scan" if emit_o else "gdn_scan_state",
    )(*args)


def _post_kernel(x_ref, gate_ref, y5_ref, zs_ref, og_ref, zg_ref,
                 wglu_ref, bglu_ref, nw_ref, wout_ref, lng_ref, lnb_ref, o_ref, *, tt, alpha):
    nb, _, d = x_ref.shape
    rows = nb * tt
    npk = y5_ref.shape[0]
    d_s5 = npk * LANES
    d_gdn = og_ref.shape[2]
    parts = []
    for j in range(npk):
        parts.append(jnp.swapaxes(y5_ref[j].reshape(tt, nb, LANES), 0, 1).reshape(rows, LANES))
    y = jnp.concatenate(parts, axis=1)
    g = 0.5 * y * (1.0 + lax.erf(y * (2.0 ** -0.5)))
    glu = jnp.dot(g.astype(bf16), wglu_ref[...], preferred_element_type=f32) + bglu_ref[...]
    s5 = g * _sigmoid(glu) * _silu(zs_ref[...].astype(f32).reshape(rows, d_s5))
    o = og_ref[...].reshape(rows, d_gdn)
    zg = zg_ref[...].astype(f32).reshape(rows, d_gdn)
    gd = []
    for h in range(d_gdn // GDN_HEAD):
        oh = o[:, h * GDN_HEAD:(h + 1) * GDN_HEAD]
        gd.append(oh * lax.rsqrt(jnp.mean(oh * oh, axis=-1, keepdims=True) + NORM_EPS) * nw_ref[...])
    gdn = jnp.concatenate(gd, axis=1) * _silu(zg)
    mix = jnp.concatenate([s5, gdn], axis=1).astype(bf16)
    yv = jnp.dot(mix, wout_ref[...], preferred_element_type=f32).reshape(nb, tt, d)
    r = alpha * x_ref[...] + gate_ref[...] * yv
    mu = jnp.mean(r, axis=-1, keepdims=True)
    var = jnp.mean(jnp.square(r - mu), axis=-1, keepdims=True)
    o_ref[...] = (r - mu) * lax.rsqrt(var + LN_EPS) * lng_ref[...] + lnb_ref[...]


def _post(x, gate, y5, zs, o_gdn, zg, wglu, bglu, nw, wout, lng, lnb, alpha):
    nb, L, d = x.shape
    tt = 32
    npk = y5.shape[0]
    d_s5 = npk * LANES
    d_gdn = o_gdn.shape[2]
    kern = functools.partial(_post_kernel, tt=tt, alpha=alpha)
    tok = lambda i: (0, i, 0)
    c2 = lambda i: (0, 0)
    c3 = lambda i: (0, 0, 0)
    y_spec = pl.BlockSpec((npk, tt // S5_T, S5_T, nb, LANES), lambda i: (0, i, 0, 0, 0))
    return pl.pallas_call(
        kern,
        grid=(L // tt,),
        in_specs=[pl.BlockSpec((nb, tt, d), tok),
                  pl.BlockSpec((nb, 1, d), c3),
                  y_spec,
                  pl.BlockSpec((nb, tt, d_s5), tok),
                  pl.BlockSpec((nb, tt, d_gdn), tok),
                  pl.BlockSpec((nb, tt, d_gdn), tok),
                  pl.BlockSpec(wglu.shape, c2),
                  pl.BlockSpec(bglu.shape, c2),
                  pl.BlockSpec(nw.shape, c2),
                  pl.BlockSpec(wout.shape, c2),
                  pl.BlockSpec(lng.shape, c3),
                  pl.BlockSpec(lnb.shape, c3)],
        out_specs=pl.BlockSpec((nb, tt, d), tok),
        out_shape=jax.ShapeDtypeStruct((nb, L, d), f32),
        compiler_params=_cparams(("parallel",)),
        name="post",
    )(x, gate, y5, zs, o_gdn, zg, wglu, bglu, nw, wout, lng, lnb)


def _layer(x, c, ctx, c_ctx, p, alpha):
    nb, L, d = x.shape
    d_s5 = p['s5_d'].shape[0]
    d_gdn = p['conv_w'].shape[-1] // 3
    nh = d_gdn // GDN_HEAD
    npk = d_s5 // LANES

    pad = (-(nb + 1)) % 8
    cond = jnp.concatenate([c, c_ctx[None, :], jnp.zeros((pad, d), f32)], axis=0)
    m = _ada(cond, p['w_ada'], p['b_ada'])
    shift, scale, gate = m[:nb, :d], m[:nb, d:2 * d], m[:nb, 2 * d:]
    shift_c = jnp.broadcast_to(m[nb:nb + 1, :d], (nb, d))
    scale_c = jnp.broadcast_to(m[nb:nb + 1, d:2 * d], (nb, d))

    w_in = p['w_in']
    wu = w_in[:, :d_s5].astype(bf16)
    wm = w_in[:, d_s5:2 * d_s5 + 4 * d_gdn].astype(bf16)
    ng = 2 * N_DIR * nh
    wg = jnp.pad(w_in[:, 2 * d_s5 + 4 * d_gdn:], ((0, 0), (0, LANES - ng))).astype(bf16)
    gp = jnp.pad(jnp.stack([p['gdn_a_log'].reshape(-1), p['gdn_dt_bias'].reshape(-1)]).astype(f32),
                 ((0, 0), (ng // 2, LANES - ng)))

    u5, zs, qkv, zg, gates = _inproj(x, scale[:, None, :], shift[:, None, :], wu, wm, wg, gp, d_s5, d_gdn)
    u5c, _, qkvc, _, gatesc = _inproj(ctx, scale_c[:, None, :], shift_c[:, None, :], wu, wm, wg, gp, d_s5, d_gdn)

    sw = _s5_weights(p['s5_lambda_re'], p['s5_lambda_im'], p['s5_log_dt'],
                     p['s5_b_re'], p['s5_b_im'], p['s5_c_re'], p['s5_c_im'])
    sdim = sw['a_f'].shape[-1]
    a_f = jnp.broadcast_to(sw['a_f'][:, None, :], (npk, nb, sdim))
    a_r = jnp.broadcast_to(sw['a_r'][:, None, :], (npk, nb, sdim))
    h_zero = jnp.zeros((npk, nb, sdim), f32)
    d_skip = jnp.tile(p['s5_d'].astype(f32).reshape(npk, 1, LANES), (1, 1, S5_T))
    (hc_f,) = _s5_pass(u5c, sw['m_in_f'], a_f, h_zero, rev=False, emit_y=False)
    (hc_r,) = _s5_pass(u5c, sw['m_in_r'], a_r, h_zero, rev=True, emit_y=False)
    y_f, _ = _s5_pass(u5, sw['m_in_f'], a_f, hc_f, sw['m_out_f'], sw['m_intra'], d_skip, rev=False, emit_y=True)
    y_s5, _ = _s5_pass(u5, sw['m_in_r'], a_r, hc_r, sw['m_out_r'], y_add=y_f, rev=True, emit_y=True)

    ql, kl, vl = _gdn_prep(qkv, p['conv_w'], True)
    qc, kc, vc = _gdn_prep(qkvc, p['conv_w'], False)
    (s_ctx,) = _gdn_scan(qc, kc, vc, gatesc, None, emit_o=False)
    o_gdn, _ = _gdn_scan(ql, kl, vl, gates, s_ctx, emit_o=True)

    return _post(x, gate[:, None, :], y_s5, zs, o_gdn, zg,
                 p['w_glu'].astype(bf16), p['b_glu'].astype(f32).reshape(1, d_s5),
                 p['gdn_norm_w'].astype(f32).reshape(1, GDN_HEAD),
                 p['w_out'].astype(bf16), p['ln_g'].astype(f32).reshape(1, 1, d),
                 p['ln_b'].astype(f32).reshape(1, 1, d), alpha)


def kernel(x, c, ctx, c_ctx, w_ada, b_ada, w_in, s5_lambda_re, s5_lambda_im, s5_log_dt, s5_b_re, s5_b_im, s5_c_re, s5_c_im, s5_d, w_glu, b_glu, conv_w, gdn_a_log, gdn_dt_bias, gdn_norm_w, w_out, ln_g, ln_b):
    depth = w_ada.shape[0]
    assert depth == 1, "context-token outputs are only skipped for the last layer"
    alpha = (2.0 * depth) ** 0.25
    p = {
        'w_ada': w_ada[0], 'b_ada': b_ada[0], 'w_in': w_in[0],
        's5_lambda_re': s5_lambda_re[0], 's5_lambda_im': s5_lambda_im[0], 's5_log_dt': s5_log_dt[0],
        's5_b_re': s5_b_re[0], 's5_b_im': s5_b_im[0], 's5_c_re': s5_c_re[0], 's5_c_im': s5_c_im[0],
        's5_d': s5_d[0], 'w_glu': w_glu[0], 'b_glu': b_glu[0], 'conv_w': conv_w[0],
        'gdn_a_log': gdn_a_log[0], 'gdn_dt_bias': gdn_dt_bias[0], 'gdn_norm_w': gdn_norm_w[0],
        'w_out': w_out[0], 'ln_g': ln_g[0], 'ln_b': ln_b[0],
    }
    return _layer(x, c, ctx, c_ctx, p, alpha)
```

```python
import functools
import math

import numpy as np
import jax
import jax.numpy as jnp
from jax import lax
from jax.experimental import pallas as pl
from jax.experimental.pallas import tpu as pltpu

f32 = jnp.float32
bf16 = jnp.bfloat16
HI = lax.Precision.HIGHEST

LANES = 128
S5_GROUP = 16
S5_STATE = 64
S5_T = 8
PACK_G = LANES // S5_GROUP
GDN_HEAD = 128
GDN_CHUNK = 64
GRID_W = 64
N_DIR = 2
LN_EPS = 1e-5
NORM_EPS = 1e-6
VMEM_LIMIT = 56 * 1024 * 1024


def _cparams(sem):
    return pltpu.CompilerParams(dimension_semantics=sem, vmem_limit_bytes=VMEM_LIMIT)


def _sigmoid(x):
    return 0.5 + 0.5 * jnp.tanh(0.5 * x)


def _silu(x):
    h = 0.5 * x
    return h + h * jnp.tanh(h)


def _ada_kernel(c_ref, w_ref, b_ref, o_ref):
    o_ref[...] = jnp.dot(_silu(c_ref[...]), w_ref[...], precision=HI,
                         preferred_element_type=f32) + b_ref[...]


def _ada(cond, w_ada, b_ada):
    rows, d = cond.shape
    n = w_ada.shape[1]
    tn = 512
    return pl.pallas_call(
        _ada_kernel,
        grid=(n // tn,),
        in_specs=[pl.BlockSpec((rows, d), lambda i: (0, 0)),
                  pl.BlockSpec((d, tn), lambda i: (0, i)),
                  pl.BlockSpec((1, tn), lambda i: (0, i))],
        out_specs=pl.BlockSpec((rows, tn), lambda i: (0, i)),
        out_shape=jax.ShapeDtypeStruct((rows, n), f32),
        compiler_params=_cparams(("parallel",)),
        name="ada",
    )(cond, w_ada, b_ada.reshape(1, n))


def _inproj_kernel(x_ref, scale_ref, shift_ref, w_ref, gp_ref,
                   u_ref, zs_ref, qkv_ref, zg_ref, gate_ref, *, tt, d_s5, d_gdn, n_beta):
    nb, _, d = x_ref.shape
    n_main = d_s5 + 4 * d_gdn
    wu_ref = w_ref.at[:, :d_s5]
    wm_ref = w_ref.at[:, d_s5:d_s5 + n_main]
    wg_ref = w_ref.at[:, d_s5 + n_main:]
    sc = 1.0 + scale_ref[...]
    sh = shift_ref[...]
    h = (x_ref[...] * sc + sh).reshape(nb * tt, d).astype(bf16)
    r = jnp.dot(h, wm_ref[...], preferred_element_type=f32)
    zs_ref[...] = r[:, :d_s5].reshape(nb, tt, d_s5).astype(bf16)
    qkv_ref[...] = r[:, d_s5:d_s5 + 3 * d_gdn].reshape(nb, tt, 3 * d_gdn)
    zg_ref[...] = r[:, d_s5 + 3 * d_gdn:].reshape(nb, tt, d_gdn).astype(bf16)
    lg = jnp.dot(h, wg_ref[...], preferred_element_type=f32)
    lane = lax.broadcasted_iota(jnp.int32, lg.shape, 1)
    a = lg + gp_ref[1:2, :]
    softplus = jnp.maximum(a, 0.0) + jnp.log1p(jnp.exp(-jnp.abs(a)))
    gates = jnp.where(lane < n_beta, _sigmoid(lg), -jnp.exp(gp_ref[0:1, :]) * softplus)
    gate_ref[...] = gates.reshape(nb, tt, LANES)
    sc2 = sc[:, 0, :]
    sh2 = sh[:, 0, :]
    ht = (jnp.swapaxes(x_ref[...], 0, 1) * sc2[None] + sh2[None]).reshape(tt * nb, d).astype(bf16)
    ru = jnp.dot(ht, wu_ref[...], preferred_element_type=f32)
    for j in range(d_s5 // LANES):
        u_ref[j] = ru[:, j * LANES:(j + 1) * LANES].reshape(tt // S5_T, S5_T, nb, LANES)


def _inproj(x, scale, shift, w_all, gp, d_s5, d_gdn):
    nb, L, d = x.shape
    tt = 32
    npk = d_s5 // LANES
    kern = functools.partial(_inproj_kernel, tt=tt, d_s5=d_s5, d_gdn=d_gdn,
                             n_beta=N_DIR * (d_gdn // GDN_HEAD))
    const2 = lambda i: (0, 0)
    const3 = lambda i: (0, 0, 0)
    return pl.pallas_call(
        kern,
        grid=(L // tt,),
        in_specs=[pl.BlockSpec((nb, tt, d), lambda i: (0, i, 0)),
                  pl.BlockSpec((nb, 1, d), const3),
                  pl.BlockSpec((nb, 1, d), const3),
                  pl.BlockSpec(w_all.shape, const2),
                  pl.BlockSpec(gp.shape, const2)],
        out_specs=[pl.BlockSpec((npk, tt // S5_T, S5_T, nb, LANES), lambda i: (0, i, 0, 0, 0)),
                   pl.BlockSpec((nb, tt, d_s5), lambda i: (0, i, 0)),
                   pl.BlockSpec((nb, tt, 3 * d_gdn), lambda i: (0, i, 0)),
                   pl.BlockSpec((nb, tt, d_gdn), lambda i: (0, i, 0)),
                   pl.BlockSpec((nb, tt, LANES), lambda i: (0, i, 0))],
        out_shape=[jax.ShapeDtypeStruct((npk, L // S5_T, S5_T, nb, LANES), f32),
                   jax.ShapeDtypeStruct((nb, L, d_s5), bf16),
                   jax.ShapeDtypeStruct((nb, L, 3 * d_gdn), f32),
                   jax.ShapeDtypeStruct((nb, L, d_gdn), bf16),
                   jax.ShapeDtypeStruct((nb, L, LANES), f32)],
        compiler_params=_cparams(("parallel",)),
        name="inproj",
    )(x, scale, shift, w_all, gp)


def _s5_kernel(*refs, ct, rev, emit_y, intra, add_y):
    it = iter(refs)
    u_ref, cin_ref, at_ref, h0_ref = next(it), next(it), next(it), next(it)
    cout_ref = next(it) if emit_y else None
    cintra_ref = next(it) if intra else None
    d_ref = next(it) if intra else None
    yadd_ref = next(it) if add_y else None
    y_ref = next(it) if emit_y else None
    hfin_ref = next(it)
    h_sc, v_sc, hall_sc, min_sc = next(it), next(it), next(it), next(it)
    mout_sc = next(it) if emit_y else None
    mintra_sc = next(it) if intra else None

    t = pl.program_id(1)
    nt = pl.num_programs(1)
    nb = u_ref.shape[3]
    sdim = h_sc.shape[1]
    half = sdim // 2
    gstate = half // PACK_G

    @pl.when(t == 0)
    def _():
        h_sc[...] = h0_ref[0]
        r_in = lax.broadcasted_iota(jnp.int32, (LANES, sdim), 0) // S5_GROUP
        c_in = (lax.broadcasted_iota(jnp.int32, (LANES, sdim), 1) % half) // gstate
        mask_in = (r_in == c_in).astype(f32)
        for s in range(S5_T):
            min_sc[s * LANES:(s + 1) * LANES, :] = (
                jnp.tile(jnp.concatenate([cin_ref[s, 0, 0], cin_ref[s, 1, 0]], axis=1), (PACK_G, 1))
                * mask_in).astype(bf16)
        if emit_y:
            r_out = lax.broadcasted_iota(jnp.int32, (half, LANES), 0) // gstate
            c_out = lax.broadcasted_iota(jnp.int32, (half, LANES), 1) // S5_GROUP
            mask_out = (r_out == c_out).astype(f32)
            for i in range(S5_T):
                for ri in range(2):
                    blk = cout_ref[i, ri, 0]
                    mout_sc[ri * half:(ri + 1) * half, i * LANES:(i + 1) * LANES] = (
                        jnp.tile(blk, (PACK_G, 1)) * mask_out).astype(bf16)
        if intra:
            r_x = lax.broadcasted_iota(jnp.int32, (LANES, LANES), 0) // S5_GROUP
            c_x = lax.broadcasted_iota(jnp.int32, (LANES, LANES), 1) // S5_GROUP
            mask_x = (r_x == c_x).astype(f32)
            for s in range(S5_T):
                for i in range(S5_T):
                    mintra_sc[s * LANES:(s + 1) * LANES, i * LANES:(i + 1) * LANES] = (
                        jnp.tile(cintra_ref[0, s, i], (PACK_G, 1)) * mask_x).astype(bf16)

    x = jnp.concatenate([u_ref[0, :, i, :, :].reshape(ct * nb, LANES) for i in range(S5_T)], axis=1)
    xb = x.astype(bf16)
    v_sc[...] = jnp.dot(xb, min_sc[...], preferred_element_type=f32)
    are = at_ref[0, :, :half]
    aim = at_ref[0, :, half:]

    def body(s, carry):
        k = (ct - 1 - s) if rev else s
        off = pl.multiple_of(k * nb, nb)
        hre, him = carry
        hall_sc[pl.ds(off, nb), :half] = hre
        hall_sc[pl.ds(off, nb), half:] = him
        vre = v_sc[pl.ds(off, nb), :half]
        vim = v_sc[pl.ds(off, nb), half:]
        return are * hre - aim * him + vre, are * him + aim * hre + vim

    hre, him = lax.fori_loop(0, ct, body, (h_sc[:, :half], h_sc[:, half:]))
    h_sc[:, :half] = hre
    h_sc[:, half:] = him

    @pl.when(t == nt - 1)
    def _():
        hfin_ref[0] = h_sc[...]

    if emit_y:
        y = jnp.dot(hall_sc[...].astype(bf16), mout_sc[...], preferred_element_type=f32)
        if intra:
            y = y + jnp.dot(xb, mintra_sc[...], preferred_element_type=f32) + x * d_ref[0]
        for i in range(S5_T):
            yi = y[:, i * LANES:(i + 1) * LANES].reshape(ct, nb, LANES)
            y_ref[0, :, i, :, :] = yi + yadd_ref[0, :, i, :, :] if add_y else yi


def _s5_pass(u5, c_in, a_t, h0, c_out=None, c_intra=None, d_skip=None, y_add=None, *, rev, emit_y):
    npk, nc, _, nb, _ = u5.shape
    ct = min(32, nc)
    nt = nc // ct
    intra = c_intra is not None
    add_y = y_add is not None
    kdim = S5_T * LANES
    sdim = a_t.shape[-1]
    tmap = (lambda j, t: (j, nt - 1 - t, 0, 0, 0)) if rev else (lambda j, t: (j, t, 0, 0, 0))
    pmap = lambda j, t: (j, 0, 0)
    wmap = lambda j, t: (0, 0, j, 0, 0)
    pmap5 = lambda j, t: (j, 0, 0, 0, 0)
    args = [u5, c_in, a_t, h0]
    in_specs = [pl.BlockSpec((1, ct, S5_T, nb, LANES), tmap),
                pl.BlockSpec(c_in.shape[:2] + (1,) + c_in.shape[3:], wmap),
                pl.BlockSpec((1, nb, sdim), pmap),
                pl.BlockSpec((1, nb, sdim), pmap)]
    scratch = [pltpu.VMEM((nb, sdim), f32),
               pltpu.VMEM((ct * nb, sdim), f32),
               pltpu.VMEM((ct * nb, sdim), f32),
               pltpu.VMEM((kdim, sdim), bf16)]
    if emit_y:
        args.append(c_out)
        in_specs.append(pl.BlockSpec(c_out.shape[:2] + (1,) + c_out.shape[3:], wmap))
        scratch.append(pltpu.VMEM((sdim, kdim), bf16))
    if intra:
        args += [c_intra, d_skip]
        in_specs += [pl.BlockSpec((1,) + c_intra.shape[1:], pmap5), pl.BlockSpec((1, 1, kdim), pmap)]
        scratch.append(pltpu.VMEM((kdim, kdim), bf16))
    if add_y:
        args.append(y_add)
        in_specs.append(pl.BlockSpec((1, ct, S5_T, nb, LANES), tmap))
    out_specs, out_shape = [], []
    if emit_y:
        out_specs.append(pl.BlockSpec((1, ct, S5_T, nb, LANES), tmap))
        out_shape.append(jax.ShapeDtypeStruct(u5.shape, f32))
    out_specs.append(pl.BlockSpec((1, nb, sdim), pmap))
    out_shape.append(jax.ShapeDtypeStruct((npk, nb, sdim), f32))
    kern = functools.partial(_s5_kernel, ct=ct, rev=rev, emit_y=emit_y, intra=intra, add_y=add_y)
    return pl.pallas_call(
        kern,
        grid=(npk, nt),
        in_specs=in_specs,
        out_specs=out_specs,
        out_shape=out_shape,
        scratch_shapes=scratch,
        compiler_params=_cparams(("parallel", "arbitrary")),
        name="s5_rev" if rev else "s5_fwd",
    )(*args)


def _s5_weights(lam_re, lam_im, log_dt, b_re, b_im, c_re, c_im):
    T = S5_T
    ndir, G, P = lam_re.shape
    npk = G // PACK_G
    dt = jnp.exp(log_dt.astype(f32))[..., None]
    lr, li = lam_re.astype(f32), lam_im.astype(f32)
    zr, zi = lr * dt, li * dt

    nr = jnp.expm1(zr) * jnp.cos(zi) - 2.0 * jnp.square(jnp.sin(0.5 * zi))
    ni = jnp.exp(zr) * jnp.sin(zi)
    den = lr * lr + li * li
    qr = (nr * lr + ni * li) / den
    qi = (ni * lr - nr * li) / den
    bre, bim = b_re.astype(f32), b_im.astype(f32)
    bbr = qr[..., None] * bre - qi[..., None] * bim
    bbi = qr[..., None] * bim + qi[..., None] * bre
    cr, ci = c_re.astype(f32), c_im.astype(f32)

    def powers(e, d):
        ev = jnp.asarray(np.asarray(e, np.float32))[:, None, None]
        m = jnp.exp(ev * zr[d])
        return m * jnp.cos(ev * zi[d]), m * jnp.sin(ev * zi[d])

    def c_times_a(e, d):
        pr, pi = powers(e, d)
        pr, pi = pr[:, :, None, :], pi[:, :, None, :]
        return cr[d] * pr - ci[d] * pi, cr[d] * pi + ci[d] * pr

    steps = np.arange(T)

    def pack_in(e, d):
        pr, pi = powers(e, d)
        pr = pr.reshape(T, npk, 1, PACK_G * P)
        pi = pi.reshape(T, npk, 1, PACK_G * P)
        tr = lambda b: b.reshape(npk, PACK_G, P, S5_GROUP).transpose(0, 3, 1, 2).reshape(npk, S5_GROUP, PACK_G * P)
        btr, bti = tr(bbr[d]), tr(bbi[d])
        return jnp.stack([pr * btr - pi * bti, pr * bti + pi * btr], axis=1)

    def pack_out(e, d):
        pr, pi = powers(e, d)
        rep = lambda v: jnp.repeat(v.reshape(T, npk, PACK_G, P).transpose(0, 1, 3, 2), S5_GROUP, axis=-1)
        pr, pi = rep(pr), rep(pi)
        tr = lambda c: c.reshape(npk, PACK_G, S5_GROUP, P).transpose(0, 3, 1, 2).reshape(npk, P, LANES)
        ctr, cti = tr(cr[d]), tr(ci[d])
        return jnp.stack([ctr * pr - cti * pi, -(ctr * pi + cti * pr)], axis=1)

    m_in_f = pack_in(T - 1 - steps, 0)
    m_in_r = pack_in(steps, 1)
    m_out_f = pack_out(steps + 1, 0)
    m_out_r = pack_out(T - steps, 1)

    lag = steps[None, :] - steps[:, None]

    def lag_term(d, sign):
        car, cai = c_times_a(np.abs(lag).reshape(-1), d)
        mask = jnp.asarray((sign * lag >= 0).reshape(-1, 1, 1, 1), f32)
        k = (jnp.einsum('tgop,gpi->tgio', car * mask, bbr[d])
             - jnp.einsum('tgop,gpi->tgio', cai * mask, bbi[d]))
        return k.reshape(T, T, G, S5_GROUP, S5_GROUP)

    toe = lag_term(0, 1) + lag_term(1, -1)
    toe = toe.reshape(T, T, npk, PACK_G, S5_GROUP, S5_GROUP).transpose(2, 0, 1, 4, 3, 5)
    m_intra = toe.reshape(npk, T, T, S5_GROUP, LANES)

    def a_chunk(d):
        pr, pi = powers([T], d)
        return jnp.concatenate([pr.reshape(npk, PACK_G * P), pi.reshape(npk, PACK_G * P)], axis=-1)

    return dict(m_in_f=m_in_f, m_in_r=m_in_r, m_out_f=m_out_f, m_out_r=m_out_r,
                m_intra=m_intra, a_f=a_chunk(0), a_r=a_chunk(1))


def _gdn_prep_kernel(*refs, tile, rows_conv, d_gdn):
    if rows_conv:
        main_ref, up_ref, dn_ref, w_ref, q_ref, k_ref, v_ref = refs
    else:
        main_ref, w_ref, q_ref, k_ref, v_ref = refs
    w = w_ref[...]
    if rows_conv:
        i = pl.program_id(1)
        n = pl.num_programs(1)
        up = up_ref[0] * (i > 0).astype(f32)
        dn = dn_ref[0] * (i < n - 1).astype(f32)
        ext = jnp.concatenate([up, main_ref[0], dn], axis=0)
        base = GRID_W
        drs = (-1, 0, 1)
        col = lax.broadcasted_iota(jnp.int32, (tile, 1), 0) % GRID_W
        mask_m = col != 0
        mask_p = col != GRID_W - 1
    else:
        ext = main_ref[0]
        base = 0
        drs = (0,)
        col = lax.broadcasted_iota(jnp.int32, (tile, 1), 0)
        mask_m = col != 0
        mask_p = col != tile - 1
    acc = {}
    for dc in (-1, 0, 1):
        a = None
        for dr in drs:
            tap = w[(dr + 1) * 3 + (dc + 1)][None, :]
            term = ext[base + GRID_W * dr: base + GRID_W * dr + tile] * tap
            a = term if a is None else a + term
        acc[dc] = a
    conv = (acc[0] + jnp.where(mask_m, pltpu.roll(acc[-1], 1, 0), 0.0)
            + jnp.where(mask_p, pltpu.roll(acc[1], tile - 1, 0), 0.0))
    s = _silu(conv)
    nh = d_gdn // GDN_HEAD
    for h in range(nh):
        qh = s[:, h * GDN_HEAD:(h + 1) * GDN_HEAD]
        kh = s[:, d_gdn + h * GDN_HEAD: d_gdn + (h + 1) * GDN_HEAD]
        qn = qh * (lax.rsqrt(jnp.sum(qh * qh, axis=-1, keepdims=True) + NORM_EPS) * (GDN_HEAD ** -0.5))
        kn = kh * lax.rsqrt(jnp.sum(kh * kh, axis=-1, keepdims=True) + NORM_EPS)
        q_ref[0, :, h * GDN_HEAD:(h + 1) * GDN_HEAD] = qn.astype(bf16)
        k_ref[0, :, h * GDN_HEAD:(h + 1) * GDN_HEAD] = kn.astype(bf16)
    v_ref[0] = s[:, 2 * d_gdn:].astype(bf16)


def _gdn_prep(qkv, conv_w, rows_conv):
    nb, L, c3 = qkv.shape
    d_gdn = c3 // 3
    w9 = conv_w.reshape(9, c3).astype(f32)
    if rows_conv:
        tile = min(512, L)
        nt = L // tile
        r = tile // GRID_W
        nrows = L // GRID_W
        in_specs = [pl.BlockSpec((1, tile, c3), lambda b, i: (b, i, 0)),
                    pl.BlockSpec((1, GRID_W, c3), lambda b, i: (b, jnp.maximum(i * r - 1, 0), 0)),
                    pl.BlockSpec((1, GRID_W, c3), lambda b, i: (b, jnp.minimum((i + 1) * r, nrows - 1), 0)),
                    pl.BlockSpec((9, c3), lambda b, i: (0, 0))]
        args = (qkv, qkv, qkv, w9)
    else:
        tile, nt = L, 1
        in_specs = [pl.BlockSpec((1, tile, c3), lambda b, i: (b, i, 0)),
                    pl.BlockSpec((9, c3), lambda b, i: (0, 0))]
        args = (qkv, w9)
    kern = functools.partial(_gdn_prep_kernel, tile=tile, rows_conv=rows_conv, d_gdn=d_gdn)
    o_spec = pl.BlockSpec((1, tile, d_gdn), lambda b, i: (b, i, 0))
    o_shape = jax.ShapeDtypeStruct((nb, L, d_gdn), bf16)
    return pl.pallas_call(
        kern,
        grid=(nb, nt),
        in_specs=in_specs,
        out_specs=[o_spec, o_spec, o_spec],
        out_shape=[o_shape, o_shape, o_shape],
        compiler_params=_cparams(("parallel", "parallel")),
        name="gdn_prep",
    )(*args)


def _chunk_cumsum(g, rev):
    n = g.shape[0]
    r = lax.broadcasted_iota(jnp.int32, g.shape, 0)
    s = 1
    while s < n:
        if rev:
            g = g + jnp.where(r < n - s, pltpu.roll(g, n - s, 0), 0.0)
        else:
            g = g + jnp.where(r >= s, pltpu.roll(g, s, 0), 0.0)
        s *= 2
    return g


def _tri_solve_rows(apl_sc, tl_sc, dirn, upper):
    nblk = GDN_CHUNK // 8
    base = dirn * GDN_CHUNK
    sub = lax.broadcasted_iota(jnp.int32, (8, LANES), 0)
    for step_blk in range(nblk):
        blk = (nblk - 1 - step_blk) if upper else step_blk
        mblocks = range(blk, nblk) if upper else range(0, blk + 1)

        def solve_row(s, carry, blk=blk, mblocks=mblocks):
            ii = (7 - s) if upper else s
            i = blk * 8 + ii
            acc = {}
            for mb in mblocks:
                arow = apl_sc[i, base + mb * 8:base + (mb + 1) * 8, :]
                jblocks = range(mb, nblk) if upper else range(0, mb + 1)
                for mm in range(8):
                    ab = jnp.broadcast_to(arow[mm:mm + 1, :], (8, LANES))
                    for jb in jblocks:
                        term = ab * tl_sc[mb * 8 + mm, base + jb * 8:base + (jb + 1) * 8, :]
                        key = (jb, mm % 2)
                        acc[key] = term if key not in acc else acc[key] + term
            for jb in (range(blk, nblk) if upper else range(0, blk + 1)):
                tot = acc[(jb, 0)] + acc[(jb, 1)]
                unit_row = jnp.where(sub == ii, 1.0, 0.0) if jb == blk else 0.0
                tl_sc[i, base + jb * 8:base + (jb + 1) * 8, :] = unit_row - tot
            return carry

        lax.fori_loop(0, 8, solve_row, 0)


def _gdn_scan_kernel(*refs, nchunks, nh, emit_o, has_s0):
    it = iter(refs)
    q_ref, k_ref, v_ref, g_ref = next(it), next(it), next(it), next(it)
    s0_ref = next(it) if has_s0 else None
    o_ref = next(it) if emit_o else None
    sfin_ref = next(it)
    s_sc, at_sc, apl_sc, tl_sc, mk_sc, w_sc, gcol_sc, grow_sc = (next(it) for _ in range(8))
    C = GDN_CHUNK
    bb = q_ref.shape[0]
    nunits = bb * nchunks
    s_sc[...] = s0_ref[...] if has_s0 else jnp.zeros(s_sc.shape, f32)
    row = lax.broadcasted_iota(jnp.int32, (C, N_DIR * C), 0)
    lane = lax.broadcasted_iota(jnp.int32, (C, N_DIR * C), 1)
    fwd_half = lane < C
    col = jnp.where(fwd_half, lane, lane - C)
    tri = (fwd_half & (row >= col)) | (~fwd_half & (row <= col))
    strict = (fwd_half & (row > col)) | (~fwd_half & (row < col))
    nbeta = N_DIR * nh
    gate_lane = lax.broadcasted_iota(jnp.int32, (C, LANES), 1)

    def unit_of(unit):
        bl = unit // nchunks
        return bl, pl.multiple_of((unit % nchunks) * C, C)

    def phase_a(unit, carry):
        bl, off = unit_of(unit)
        gts = g_ref[bl, pl.ds(off, C), :]
        gmix = jnp.where(gate_lane < nbeta, gts,
                         jnp.where(gate_lane < nbeta + nh, _chunk_cumsum(gts, False), _chunk_cumsum(gts, True)))
        g_rows = gmix.T[:2 * nbeta]
        gcol_sc[unit] = gmix
        grow_sc[unit] = g_rows
        for h in range(nh):
            hs = slice(h * GDN_HEAD, (h + 1) * GDN_HEAD)
            cbf, cbr = h, nh + h
            cgf, cgr = nbeta + h, nbeta + nh + h
            kb = k_ref[bl, pl.ds(off, C), hs]
            k2 = jnp.concatenate([kb, kb], axis=0)
            if emit_o:
                kq = jnp.concatenate([kb, q_ref[bl, pl.ds(off, C), hs]], axis=0)
            else:
                kq = kb
            kkqk = lax.dot_general(kq, k2, (((1,), (1,)), ((), ())), preferred_element_type=f32)
            g_col = jnp.where(fwd_half, gmix[:, cgf:cgf + 1], gmix[:, cgr:cgr + 1])
            g_row = jnp.concatenate([g_rows[cgf:cgf + 1, :], g_rows[cgr:cgr + 1, :]], axis=1)
            b_row = jnp.concatenate([g_rows[cbf:cbf + 1, :], g_rows[cbr:cbr + 1, :]], axis=1)
            decay_b = jnp.exp(jnp.where(tri, g_col - g_row, -1e30)) * b_row
            at_sc[unit * nh + h] = jnp.where(strict, kkqk[:C] * decay_b, 0.0)
            if emit_o:
                mk_sc[unit * nh + h, :C, :] = (kkqk[C:] * decay_b).astype(bf16)
        return carry

    lax.fori_loop(0, nunits, phase_a, 0, unroll=2)

    def swap_in(ib, carry):
        rows = pl.ds(pl.multiple_of(ib * 8, 8), 8)
        tl_sc[rows] = jnp.swapaxes(at_sc[:, rows, :], 0, 1)
        return carry

    def to_lanes(i, carry):
        apl_sc[i] = tl_sc[i].T
        return carry

    lax.fori_loop(0, C // 8, swap_in, 0)
    lax.fori_loop(0, C, to_lanes, 0, unroll=4)
    tl_sc[...] = jnp.zeros(tl_sc.shape, f32)
    _tri_solve_rows(apl_sc, tl_sc, 0, False)
    _tri_solve_rows(apl_sc, tl_sc, 1, True)

    def from_lanes(i, carry):
        apl_sc[i] = tl_sc[i].T
        return carry

    def swap_out(ib, carry):
        rows = pl.ds(pl.multiple_of(ib * 8, 8), 8)
        at_sc[:, rows, :] = jnp.swapaxes(apl_sc[rows], 0, 1)
        return carry

    lax.fori_loop(0, C, from_lanes, 0, unroll=4)
    lax.fori_loop(0, C // 8, swap_out, 0)

    u0_sc = (apl_sc, tl_sc)

    def u0_slot(p):
        return p // 2, pl.ds(pl.multiple_of((p % 2) * C, C), C)

    eye_bf = (lax.broadcasted_iota(jnp.int32, (GDN_HEAD, GDN_HEAD), 0)
              == lax.broadcasted_iota(jnp.int32, (GDN_HEAD, GDN_HEAD), 1)).astype(bf16)

    def phase_b2(unit, carry):
        bl, off = unit_of(unit)
        g_rows = grow_sc[unit]
        g_cols = gcol_sc[unit]
        zero_kv = jnp.zeros((C, GDN_HEAD), bf16)
        for h in range(nh):
            hs = slice(h * GDN_HEAD, (h + 1) * GDN_HEAD)
            p = unit * nh + h
            cbf, cbr = h, nh + h
            cgf, cgr = nbeta + h, nbeta + nh + h
            t_both = at_sc[p]
            kb = k_ref[bl, pl.ds(off, C), hs]
            vb = v_ref[bl, pl.ds(off, C), hs]
            g_row = jnp.concatenate([g_rows[cgf:cgf + 1, :], g_rows[cgr:cgr + 1, :]], axis=1)
            b_row = jnp.concatenate([g_rows[cbf:cbf + 1, :], g_rows[cbr:cbr + 1, :]], axis=1)
            g_last = jnp.concatenate([jnp.broadcast_to(g_cols[C - 1:C, cgf:cgf + 1], (1, C)),
                                      jnp.broadcast_to(g_cols[0:1, cgr:cgr + 1], (1, C))], axis=1)
            v_bd = jnp.concatenate([jnp.concatenate([vb, zero_kv], axis=1),
                                    jnp.concatenate([zero_kv, vb], axis=1)], axis=0)
            k_bd = jnp.concatenate([jnp.concatenate([kb, zero_kv], axis=1),
                                    jnp.concatenate([zero_kv, kb], axis=1)], axis=0)
            u0 = jnp.dot(t_both.astype(bf16), v_bd, preferred_element_type=f32)
            w = jnp.dot((t_both * jnp.exp(g_row)).astype(bf16), k_bd, preferred_element_type=f32)
            slab, rows = u0_slot(p)
            for dirn in range(N_DIR):
                u0_sc[dirn][slab, rows, :] = u0[:, dirn * GDN_HEAD:(dirn + 1) * GDN_HEAD]
                w_sc[dirn, p] = w[:, dirn * GDN_HEAD:(dirn + 1) * GDN_HEAD].astype(bf16)
            k_t = lax.dot_general(eye_bf, jnp.concatenate([kb, kb], axis=0), (((1,), (1,)), ((), ())),
                                  preferred_element_type=f32)
            mk_sc[p, C:, :] = (k_t * (b_row * jnp.exp(g_last - g_row))).astype(bf16)
        return carry

    lax.fori_loop(0, nunits, phase_b2, 0, unroll=4)

    def phase_c(step, carry, accumulate):
        chains = []
        for dirn, unit in ((0, step), (1, nunits - 1 - step)):
            bl, off = unit_of(unit)
            g_col_all = gcol_sc[unit]
            g_tot_all = g_col_all[0:1, :] if dirn else g_col_all[C - 1:C, :]
            for h in range(nh):
                cg = nbeta + dirn * nh + h
                hs = slice(h * GDN_HEAD, (h + 1) * GDN_HEAD)
                p = unit * nh + h
                slab, rows = u0_slot(p)
                lhs = w_sc[dirn, p]
                ch = dict(idx=(bl, dirn, h), off=off, hs=hs, dirn=dirn, u0=u0_sc[dirn][slab, rows, :],
                          g_last=jnp.exp(g_tot_all[:, cg:cg + 1]))
                if emit_o:
                    lhs = jnp.concatenate([lhs, q_ref[bl, pl.ds(off, C), hs]], axis=0)
                    ch['mk'] = mk_sc[p]
                    ch['gam'] = jnp.exp(g_col_all[:, cg:cg + 1])
                else:
                    ch['mk'] = mk_sc[p, C:, :]
                ch['lhs'] = lhs
                chains.append(ch)
        s_old = [s_sc[ch['idx']] for ch in chains]
        r = [jnp.dot(ch['lhs'], s.astype(bf16), preferred_element_type=f32) for ch, s in zip(chains, s_old)]
        zeros = jnp.zeros((C, GDN_HEAD), bf16)
        ub = [(ch['u0'] - ri[:C]).astype(bf16) for ch, ri in zip(chains, r)]
        m = [jnp.dot(ch['mk'], jnp.concatenate([zeros, ubi] if ch['dirn'] else [ubi, zeros], axis=0),
                     preferred_element_type=f32) for ch, ubi in zip(chains, ub)]
        for n, ch in enumerate(chains):
            if emit_o:
                bl = ch['idx'][0]
                o_new = ch['gam'] * r[n][C:] + m[n][:C]
                if accumulate:
                    o_ref[bl, pl.ds(ch['off'], C), ch['hs']] += o_new
                else:
                    o_ref[bl, pl.ds(ch['off'], C), ch['hs']] = o_new
                s_sc[ch['idx']] = ch['g_last'] * s_old[n] + m[n][C:]
            else:
                s_sc[ch['idx']] = ch['g_last'] * s_old[n] + m[n]
        return carry

    if emit_o and bb == 1:
        lax.fori_loop(0, nunits // 2, functools.partial(phase_c, accumulate=False), 0)
        lax.fori_loop(nunits // 2, nunits, functools.partial(phase_c, accumulate=True), 0)
    elif emit_o:
        o_ref[...] = jnp.zeros(o_ref.shape, f32)
        lax.fori_loop(0, nunits, functools.partial(phase_c, accumulate=True), 0)
    else:
        lax.fori_loop(0, nunits, functools.partial(phase_c, accumulate=False), 0)
    sfin_ref[...] = s_sc[...]


def _gdn_scan(q, k, v, gates, s0, *, emit_o):
    nb, L, d_gdn = q.shape
    nh = d_gdn // GDN_HEAD
    nchunks = L // GDN_CHUNK
    C = GDN_CHUNK
    bb = max(1, min(nb, LANES // (nchunks * nh)))
    nunits = bb * nchunks
    nprob = nunits * nh
    assert nprob == LANES, (nb, L, nh)
    has_s0 = s0 is not None
    kern = functools.partial(_gdn_scan_kernel, nchunks=nchunks, nh=nh, emit_o=emit_o, has_s0=has_s0)
    seq = lambda b: (b, 0, 0)
    st = lambda b: (b, 0, 0, 0, 0)
    st_spec = pl.BlockSpec((bb, N_DIR, nh, GDN_HEAD, GDN_HEAD), st)
    st_shape = jax.ShapeDtypeStruct((nb, N_DIR, nh, GDN_HEAD, GDN_HEAD), f32)
    args = [q, k, v, gates]
    in_specs = [pl.BlockSpec((bb, L, d_gdn), seq)] * 3 + [pl.BlockSpec((bb, L, LANES), seq)]
    if has_s0:
        args.append(s0)
        in_specs.append(st_spec)
    out_specs, out_shape = [], []
    if emit_o:
        out_specs.append(pl.BlockSpec((bb, L, d_gdn), seq))
        out_shape.append(jax.ShapeDtypeStruct((nb, L, d_gdn), f32))
    out_specs.append(st_spec)
    out_shape.append(st_shape)
    return pl.pallas_call(
        kern,
        grid=(nb // bb,),
        in_specs=in_specs,
        out_specs=out_specs,
        out_shape=out_shape,
        scratch_shapes=[pltpu.VMEM((bb, N_DIR, nh, GDN_HEAD, GDN_HEAD), f32),
                        pltpu.VMEM((nprob, C, N_DIR * C), f32),
                        pltpu.VMEM((C, N_DIR * C, nprob), f32),
                        pltpu.VMEM((C, N_DIR * C, nprob), f32),
                        pltpu.VMEM((nprob, C + GDN_HEAD, N_DIR * C), bf16),
                        pltpu.VMEM((N_DIR, nprob, C, GDN_HEAD), bf16),
                        pltpu.VMEM((nunits, C, LANES), f32),
                        pltpu.VMEM((nunits, 4 * nh, C), f32)],
        compiler_params=_cparams(("parallel",)),
        name="gdn_scan" if emit_o else "gdn_scan_state",
    )(*args)


def _post_kernel(x_ref, gate_ref, y5_ref, zs_ref, og_ref, zg_ref,
                 wglu_ref, bglu_ref, nw_ref, wout_ref, lng_ref, lnb_ref, o_ref, *, tt, alpha):
    nb, _, d = x_ref.shape
    rows = nb * tt
    npk = y5_ref.shape[0]
    d_s5 = npk * LANES
    d_gdn = og_ref.shape[2]
    parts = []
    for j in range(npk):
        parts.append(jnp.swapaxes(y5_ref[j].reshape(tt, nb, LANES), 0, 1).reshape(rows, LANES))
    y = jnp.concatenate(parts, axis=1)
    g = 0.5 * y * (1.0 + lax.erf(y * (2.0 ** -0.5)))
    glu = jnp.dot(g.astype(bf16), wglu_ref[...], preferred_element_type=f32) + bglu_ref[...]
    s5 = g * _sigmoid(glu) * _silu(zs_ref[...].astype(f32).reshape(rows, d_s5))
    o = og_ref[...].reshape(rows, d_gdn)
    zg = zg_ref[...].astype(f32).reshape(rows, d_gdn)
    gd = []
    for h in range(d_gdn // GDN_HEAD):
        oh = o[:, h * GDN_HEAD:(h + 1) * GDN_HEAD]
        gd.append(oh * lax.rsqrt(jnp.mean(oh * oh, axis=-1, keepdims=True) + NORM_EPS) * nw_ref[...])
    gdn = jnp.concatenate(gd, axis=1) * _silu(zg)
    mix = jnp.concatenate([s5, gdn], axis=1).astype(bf16)
    yv = jnp.dot(mix, wout_ref[...], preferred_element_type=f32).reshape(nb, tt, d)
    r = alpha * x_ref[...] + gate_ref[...] * yv
    mu = jnp.mean(r, axis=-1, keepdims=True)
    var = jnp.mean(jnp.square(r - mu), axis=-1, keepdims=True)
    o_ref[...] = (r - mu) * lax.rsqrt(var + LN_EPS) * lng_ref[...] + lnb_ref[...]


def _post(x, gate, y5, zs, o_gdn, zg, wglu, bglu, nw, wout, lng, lnb, alpha):
    nb, L, d = x.shape
    tt = 32
    npk = y5.shape[0]
    d_s5 = npk * LANES
    d_gdn = o_gdn.shape[2]
    kern = functools.partial(_post_kernel, tt=tt, alpha=alpha)
    tok = lambda i: (0, i, 0)
    c2 = lambda i: (0, 0)
    c3 = lambda i: (0, 0, 0)
    y_spec = pl.BlockSpec((npk, tt // S5_T, S5_T, nb, LANES), lambda i: (0, i, 0, 0, 0))
    return pl.pallas_call(
        kern,
        grid=(L // tt,),
        in_specs=[pl.BlockSpec((nb, tt, d), tok),
                  pl.BlockSpec((nb, 1, d), c3),
                  y_spec,
                  pl.BlockSpec((nb, tt, d_s5), tok),
                  pl.BlockSpec((nb, tt, d_gdn), tok),
                  pl.BlockSpec((nb, tt, d_gdn), tok),
                  pl.BlockSpec(wglu.shape, c2),
                  pl.BlockSpec(bglu.shape, c2),
                  pl.BlockSpec(nw.shape, c2),
                  pl.BlockSpec(wout.shape, c2),
                  pl.BlockSpec(lng.shape, c3),
                  pl.BlockSpec(lnb.shape, c3)],
        out_specs=pl.BlockSpec((nb, tt, d), tok),
        out_shape=jax.ShapeDtypeStruct((nb, L, d), f32),
        compiler_params=_cparams(("parallel",)),
        name="post",
    )(x, gate, y5, zs, o_gdn, zg, wglu, bglu, nw, wout, lng, lnb)


def _layer(x, c, ctx, c_ctx, p, alpha):
    nb, L, d = x.shape
    d_s5 = p['s5_d'].shape[0]
    d_gdn = p['conv_w'].shape[-1] // 3
    nh = d_gdn // GDN_HEAD
    npk = d_s5 // LANES

    pad = (-(nb + 1)) % 8
    cond = jnp.concatenate([c, c_ctx[None, :], jnp.zeros((pad, d), f32)], axis=0)
    m = _ada(cond, p['w_ada'], p['b_ada'])
    shift, scale, gate = m[:nb, :d], m[:nb, d:2 * d], m[:nb, 2 * d:]
    shift_c = jnp.broadcast_to(m[nb:nb + 1, :d], (nb, d))
    scale_c = jnp.broadcast_to(m[nb:nb + 1, d:2 * d], (nb, d))

    ng = 2 * N_DIR * nh
    w_all = jnp.pad(p['w_in'], ((0, 0), (0, LANES - ng))).astype(bf16)
    gp = jnp.pad(jnp.stack([p['gdn_a_log'].reshape(-1), p['gdn_dt_bias'].reshape(-1)]).astype(f32),
                 ((0, 0), (ng // 2, LANES - ng)))

    u5, zs, qkv, zg, gates = _inproj(x, scale[:, None, :], shift[:, None, :], w_all, gp, d_s5, d_gdn)
    u5c, _, qkvc, _, gatesc = _inproj(ctx, scale_c[:, None, :], shift_c[:, None, :], w_all, gp, d_s5, d_gdn)

    sw = _s5_weights(p['s5_lambda_re'], p['s5_lambda_im'], p['s5_log_dt'],
                     p['s5_b_re'], p['s5_b_im'], p['s5_c_re'], p['s5_c_im'])
    sdim = sw['a_f'].shape[-1]
    a_f = jnp.broadcast_to(sw['a_f'][:, None, :], (npk, nb, sdim))
    a_r = jnp.broadcast_to(sw['a_r'][:, None, :], (npk, nb, sdim))
    h_zero = jnp.zeros((npk, nb, sdim), f32)
    d_skip = jnp.tile(p['s5_d'].astype(f32).reshape(npk, 1, LANES), (1, 1, S5_T))
    (hc_f,) = _s5_pass(u5c, sw['m_in_f'], a_f, h_zero, rev=False, emit_y=False)
    (hc_r,) = _s5_pass(u5c, sw['m_in_r'], a_r, h_zero, rev=True, emit_y=False)
    y_f, _ = _s5_pass(u5, sw['m_in_f'], a_f, hc_f, sw['m_out_f'], sw['m_intra'], d_skip, rev=False, emit_y=True)
    y_s5, _ = _s5_pass(u5, sw['m_in_r'], a_r, hc_r, sw['m_out_r'], y_add=y_f, rev=True, emit_y=True)

    ql, kl, vl = _gdn_prep(qkv, p['conv_w'], True)
    qc, kc, vc = _gdn_prep(qkvc, p['conv_w'], False)
    (s_ctx,) = _gdn_scan(qc, kc, vc, gatesc, None, emit_o=False)
    o_gdn, _ = _gdn_scan(ql, kl, vl, gates, s_ctx, emit_o=True)

    return _post(x, gate[:, None, :], y_s5, zs, o_gdn, zg,
                 p['w_glu'].astype(bf16), p['b_glu'].astype(f32).reshape(1, d_s5),
                 p['gdn_norm_w'].astype(f32).reshape(1, GDN_HEAD),
                 p['w_out'].astype(bf16), p['ln_g'].astype(f32).reshape(1, 1, d),
                 p['ln_b'].astype(f32).reshape(1, 1, d), alpha)


def kernel(x, c, ctx, c_ctx, w_ada, b_ada, w_in, s5_lambda_re, s5_lambda_im, s5_log_dt, s5_b_re, s5_b_im, s5_c_re, s5_c_im, s5_d, w_glu, b_glu, conv_w, gdn_a_log, gdn_dt_bias, gdn_norm_w, w_out, ln_g, ln_b):
    depth = w_ada.shape[0]
    assert depth == 1, "context-token outputs are only skipped for the last layer"
    alpha = (2.0 * depth) ** 0.25
    p = {
        'w_ada': w_ada[0], 'b_ada': b_ada[0], 'w_in': w_in[0],
        's5_lambda_re': s5_lambda_re[0], 's5_lambda_im': s5_lambda_im[0], 's5_log_dt': s5_log_dt[0],
        's5_b_re': s5_b_re[0], 's5_b_im': s5_b_im[0], 's5_c_re': s5_c_re[0], 's5_c_im': s5_c_im[0],
        's5_d': s5_d[0], 'w_glu': w_glu[0], 'b_glu': b_glu[0], 'conv_w': conv_w[0],
        'gdn_a_log': gdn_a_log[0], 'gdn_dt_bias': gdn_dt_bias[0], 'gdn_norm_w': gdn_norm_w[0],
        'w_out': w_out[0], 'ln_g': ln_g[0], 'ln_b': ln_b[0],
    }
    return _layer(x, c, ctx, c_ctx, p, alpha)
```

```python
import functools
import math

import numpy as np
import jax
import jax.numpy as jnp
from jax import lax
from jax.experimental import pallas as pl
from jax.experimental.pallas import tpu as pltpu

f32 = jnp.float32
bf16 = jnp.bfloat16
HI = lax.Precision.HIGHEST

LANES = 128
S5_GROUP = 16
S5_STATE = 64
S5_T = 8
PACK_G = LANES // S5_GROUP
GDN_HEAD = 128
GDN_CHUNK = 64
GRID_W = 64
N_DIR = 2
LN_EPS = 1e-5
NORM_EPS = 1e-6
VMEM_LIMIT = 56 * 1024 * 1024


def _cparams(sem):
    return pltpu.CompilerParams(dimension_semantics=sem, vmem_limit_bytes=VMEM_LIMIT)


def _sigmoid(x):
    return 0.5 + 0.5 * jnp.tanh(0.5 * x)


def _silu(x):
    h = 0.5 * x
    return h + h * jnp.tanh(h)


def _ada_kernel(c_ref, w_ref, b_ref, o_ref):
    o_ref[...] = jnp.dot(_silu(c_ref[...]), w_ref[...], precision=HI,
                         preferred_element_type=f32) + b_ref[...]


def _ada(cond, w_ada, b_ada):
    rows, d = cond.shape
    n = w_ada.shape[1]
    tn = 512
    return pl.pallas_call(
        _ada_kernel,
        grid=(n // tn,),
        in_specs=[pl.BlockSpec((rows, d), lambda i: (0, 0)),
                  pl.BlockSpec((d, tn), lambda i: (0, i)),
                  pl.BlockSpec((1, tn), lambda i: (0, i))],
        out_specs=pl.BlockSpec((rows, tn), lambda i: (0, i)),
        out_shape=jax.ShapeDtypeStruct((rows, n), f32),
        compiler_params=_cparams(("parallel",)),
        name="ada",
    )(cond, w_ada, b_ada.reshape(1, n))


def _inproj_kernel(x_ref, scale_ref, shift_ref, w_ref, gp_ref,
                   u_ref, zs_ref, qkv_ref, zg_ref, gate_ref, *, tt, d_s5, d_gdn, n_beta):
    nb, _, d = x_ref.shape
    n_main = d_s5 + 4 * d_gdn
    wu_ref = w_ref.at[:, :d_s5]
    wm_ref = w_ref.at[:, d_s5:d_s5 + n_main]
    wg_ref = w_ref.at[:, d_s5 + n_main:]
    sc = 1.0 + scale_ref[...]
    sh = shift_ref[...]
    h = (x_ref[...] * sc + sh).reshape(nb * tt, d).astype(bf16)
    r = jnp.dot(h, wm_ref[...], preferred_element_type=f32)
    zs_ref[...] = r[:, :d_s5].reshape(nb, tt, d_s5).astype(bf16)
    qkv_ref[...] = r[:, d_s5:d_s5 + 3 * d_gdn].reshape(nb, tt, 3 * d_gdn)
    zg_ref[...] = r[:, d_s5 + 3 * d_gdn:].reshape(nb, tt, d_gdn).astype(bf16)
    lg = jnp.dot(h, wg_ref[...], preferred_element_type=f32)
    lane = lax.broadcasted_iota(jnp.int32, lg.shape, 1)
    a = lg + gp_ref[1:2, :]
    softplus = jnp.maximum(a, 0.0) + jnp.log1p(jnp.exp(-jnp.abs(a)))
    gates = jnp.where(lane < n_beta, _sigmoid(lg), -jnp.exp(gp_ref[0:1, :]) * softplus)
    gate_ref[...] = gates.reshape(nb, tt, LANES)
    sc2 = sc[:, 0, :]
    sh2 = sh[:, 0, :]
    ht = (jnp.swapaxes(x_ref[...], 0, 1) * sc2[None] + sh2[None]).reshape(tt * nb, d).astype(bf16)
    ru = jnp.dot(ht, wu_ref[...], preferred_element_type=f32)
    for j in range(d_s5 // LANES):
        u_ref[j] = ru[:, j * LANES:(j + 1) * LANES].reshape(tt // S5_T, S5_T, nb, LANES)


def _inproj(x, scale, shift, w_all, gp, d_s5, d_gdn):
    nb, L, d = x.shape
    tt = 32
    npk = d_s5 // LANES
    kern = functools.partial(_inproj_kernel, tt=tt, d_s5=d_s5, d_gdn=d_gdn,
                             n_beta=N_DIR * (d_gdn // GDN_HEAD))
    const2 = lambda i: (0, 0)
    const3 = lambda i: (0, 0, 0)
    return pl.pallas_call(
        kern,
        grid=(L // tt,),
        in_specs=[pl.BlockSpec((nb, tt, d), lambda i: (0, i, 0)),
                  pl.BlockSpec((nb, 1, d), const3),
                  pl.BlockSpec((nb, 1, d), const3),
                  pl.BlockSpec(w_all.shape, const2),
                  pl.BlockSpec(gp.shape, const2)],
        out_specs=[pl.BlockSpec((npk, tt // S5_T, S5_T, nb, LANES), lambda i: (0, i, 0, 0, 0)),
                   pl.BlockSpec((nb, tt, d_s5), lambda i: (0, i, 0)),
                   pl.BlockSpec((nb, tt, 3 * d_gdn), lambda i: (0, i, 0)),
                   pl.BlockSpec((nb, tt, d_gdn), lambda i: (0, i, 0)),
                   pl.BlockSpec((nb, tt, LANES), lambda i: (0, i, 0))],
        out_shape=[jax.ShapeDtypeStruct((npk, L // S5_T, S5_T, nb, LANES), f32),
                   jax.ShapeDtypeStruct((nb, L, d_s5), bf16),
                   jax.ShapeDtypeStruct((nb, L, 3 * d_gdn), f32),
                   jax.ShapeDtypeStruct((nb, L, d_gdn), bf16),
                   jax.ShapeDtypeStruct((nb, L, LANES), f32)],
        compiler_params=_cparams(("parallel",)),
        name="inproj",
    )(x, scale, shift, w_all, gp)


def _s5_kernel(*refs, ct, rev, emit_y, intra, add_y):
    it = iter(refs)
    u_ref, cin_ref, at_ref, h0_ref = next(it), next(it), next(it), next(it)
    cout_ref = next(it) if emit_y else None
    cintra_ref = next(it) if intra else None
    d_ref = next(it) if intra else None
    yadd_ref = next(it) if add_y else None
    y_ref = next(it) if emit_y else None
    hfin_ref = next(it)
    h_sc, v_sc, hall_sc, min_sc = next(it), next(it), next(it), next(it)
    mout_sc = next(it) if emit_y else None
    mintra_sc = next(it) if intra else None

    t = pl.program_id(1)
    nt = pl.num_programs(1)
    nb = u_ref.shape[3]
    sdim = h_sc.shape[1]
    half = sdim // 2
    gstate = half // PACK_G

    @pl.when(t == 0)
    def _():
        h_sc[...] = h0_ref[0]
        r_in = lax.broadcasted_iota(jnp.int32, (LANES, sdim), 0) // S5_GROUP
        c_in = (lax.broadcasted_iota(jnp.int32, (LANES, sdim), 1) % half) // gstate
        mask_in = (r_in == c_in).astype(f32)
        for s in range(S5_T):
            min_sc[s * LANES:(s + 1) * LANES, :] = (
                jnp.tile(jnp.concatenate([cin_ref[s, 0, 0], cin_ref[s, 1, 0]], axis=1), (PACK_G, 1))
                * mask_in).astype(bf16)
        if emit_y:
            r_out = lax.broadcasted_iota(jnp.int32, (half, LANES), 0) // gstate
            c_out = lax.broadcasted_iota(jnp.int32, (half, LANES), 1) // S5_GROUP
            mask_out = (r_out == c_out).astype(f32)
            for i in range(S5_T):
                for ri in range(2):
                    blk = cout_ref[i, ri, 0]
                    mout_sc[ri * half:(ri + 1) * half, i * LANES:(i + 1) * LANES] = (
                        jnp.tile(blk, (PACK_G, 1)) * mask_out).astype(bf16)
        if intra:
            r_x = lax.broadcasted_iota(jnp.int32, (LANES, LANES), 0) // S5_GROUP
            c_x = lax.broadcasted_iota(jnp.int32, (LANES, LANES), 1) // S5_GROUP
            mask_x = (r_x == c_x).astype(f32)
            blocks = {}
            for lag in range(1 - S5_T, S5_T):
                k = cintra_ref[0, lag, 0] if lag > 0 else cintra_ref[1, -lag, 0]
                if lag == 0:
                    k = k + cintra_ref[0, 0, 0]
                blocks[lag] = (jnp.tile(k, (PACK_G, 1)) * mask_x).astype(bf16)
            for s in range(S5_T):
                for i in range(S5_T):
                    mintra_sc[s * LANES:(s + 1) * LANES, i * LANES:(i + 1) * LANES] = blocks[i - s]

    x = jnp.concatenate([u_ref[0, :, i, :, :].reshape(ct * nb, LANES) for i in range(S5_T)], axis=1)
    xb = x.astype(bf16)
    v_sc[...] = jnp.dot(xb, min_sc[...], preferred_element_type=f32)
    are = at_ref[0, :, :half]
    aim = at_ref[0, :, half:]

    def body(s, carry):
        k = (ct - 1 - s) if rev else s
        off = pl.multiple_of(k * nb, nb)
        hre, him = carry
        hall_sc[pl.ds(off, nb), :half] = hre
        hall_sc[pl.ds(off, nb), half:] = him
        vre = v_sc[pl.ds(off, nb), :half]
        vim = v_sc[pl.ds(off, nb), half:]
        return are * hre - aim * him + vre, are * him + aim * hre + vim

    hre, him = lax.fori_loop(0, ct, body, (h_sc[:, :half], h_sc[:, half:]))
    h_sc[:, :half] = hre
    h_sc[:, half:] = him

    @pl.when(t == nt - 1)
    def _():
        hfin_ref[0] = h_sc[...]

    if emit_y:
        y = jnp.dot(hall_sc[...].astype(bf16), mout_sc[...], preferred_element_type=f32)
        if intra:
            y = y + jnp.dot(xb, mintra_sc[...], preferred_element_type=f32) + x * d_ref[0]
        for i in range(S5_T):
            yi = y[:, i * LANES:(i + 1) * LANES].reshape(ct, nb, LANES)
            y_ref[0, :, i, :, :] = yi + yadd_ref[0, :, i, :, :] if add_y else yi


def _s5_pass(u5, c_in, a_t, h0, c_out=None, c_intra=None, d_skip=None, y_add=None, *, rev, emit_y):
    npk, nc, _, nb, _ = u5.shape
    ct = min(32, nc)
    nt = nc // ct
    intra = c_intra is not None
    add_y = y_add is not None
    kdim = S5_T * LANES
    sdim = a_t.shape[-1]
    tmap = (lambda j, t: (j, nt - 1 - t, 0, 0, 0)) if rev else (lambda j, t: (j, t, 0, 0, 0))
    pmap = lambda j, t: (j, 0, 0)
    wmap = lambda j, t: (0, 0, j, 0, 0)
    args = [u5, c_in, a_t, h0]
    in_specs = [pl.BlockSpec((1, ct, S5_T, nb, LANES), tmap),
                pl.BlockSpec(c_in.shape[:2] + (1,) + c_in.shape[3:], wmap),
                pl.BlockSpec((1, nb, sdim), pmap),
                pl.BlockSpec((1, nb, sdim), pmap)]
    scratch = [pltpu.VMEM((nb, sdim), f32),
               pltpu.VMEM((ct * nb, sdim), f32),
               pltpu.VMEM((ct * nb, sdim), f32),
               pltpu.VMEM((kdim, sdim), bf16)]
    if emit_y:
        args.append(c_out)
        in_specs.append(pl.BlockSpec(c_out.shape[:2] + (1,) + c_out.shape[3:], wmap))
        scratch.append(pltpu.VMEM((sdim, kdim), bf16))
    if intra:
        args += [c_intra, d_skip]
        in_specs += [pl.BlockSpec(c_intra.shape[:2] + (1,) + c_intra.shape[3:], wmap),
                     pl.BlockSpec((1, 1, kdim), pmap)]
        scratch.append(pltpu.VMEM((kdim, kdim), bf16))
    if add_y:
        args.append(y_add)
        in_specs.append(pl.BlockSpec((1, ct, S5_T, nb, LANES), tmap))
    out_specs, out_shape = [], []
    if emit_y:
        out_specs.append(pl.BlockSpec((1, ct, S5_T, nb, LANES), tmap))
        out_shape.append(jax.ShapeDtypeStruct(u5.shape, f32))
    out_specs.append(pl.BlockSpec((1, nb, sdim), pmap))
    out_shape.append(jax.ShapeDtypeStruct((npk, nb, sdim), f32))
    kern = functools.partial(_s5_kernel, ct=ct, rev=rev, emit_y=emit_y, intra=intra, add_y=add_y)
    return pl.pallas_call(
        kern,
        grid=(npk, nt),
        in_specs=in_specs,
        out_specs=out_specs,
        out_shape=out_shape,
        scratch_shapes=scratch,
        compiler_params=_cparams(("parallel", "arbitrary")),
        name="s5_rev" if rev else "s5_fwd",
    )(*args)


def _s5_weights(lam_re, lam_im, log_dt, b_re, b_im, c_re, c_im):
    T = S5_T
    ndir, G, P = lam_re.shape
    npk = G // PACK_G
    dt = jnp.exp(log_dt.astype(f32))[..., None]
    lr, li = lam_re.astype(f32), lam_im.astype(f32)
    zr, zi = lr * dt, li * dt

    nr = jnp.expm1(zr) * jnp.cos(zi) - 2.0 * jnp.square(jnp.sin(0.5 * zi))
    ni = jnp.exp(zr) * jnp.sin(zi)
    den = lr * lr + li * li
    qr = (nr * lr + ni * li) / den
    qi = (ni * lr - nr * li) / den
    bre, bim = b_re.astype(f32), b_im.astype(f32)
    bbr = qr[..., None] * bre - qi[..., None] * bim
    bbi = qr[..., None] * bim + qi[..., None] * bre
    cr, ci = c_re.astype(f32), c_im.astype(f32)

    def powers(e, d):
        ev = jnp.asarray(np.asarray(e, np.float32))[:, None, None]
        m = jnp.exp(ev * zr[d])
        return m * jnp.cos(ev * zi[d]), m * jnp.sin(ev * zi[d])

    def c_times_a(e, d):
        pr, pi = powers(e, d)
        pr, pi = pr[:, :, None, :], pi[:, :, None, :]
        return cr[d] * pr - ci[d] * pi, cr[d] * pi + ci[d] * pr

    steps = np.arange(T)

    def pack_in(e, d):
        pr, pi = powers(e, d)
        pr = pr.reshape(T, npk, 1, PACK_G * P)
        pi = pi.reshape(T, npk, 1, PACK_G * P)
        tr = lambda b: b.reshape(npk, PACK_G, P, S5_GROUP).transpose(0, 3, 1, 2).reshape(npk, S5_GROUP, PACK_G * P)
        btr, bti = tr(bbr[d]), tr(bbi[d])
        return jnp.stack([pr * btr - pi * bti, pr * bti + pi * btr], axis=1)

    def pack_out(e, d):
        pr, pi = powers(e, d)
        rep = lambda v: jnp.repeat(v.reshape(T, npk, PACK_G, P).transpose(0, 1, 3, 2), S5_GROUP, axis=-1)
        pr, pi = rep(pr), rep(pi)
        tr = lambda c: c.reshape(npk, PACK_G, S5_GROUP, P).transpose(0, 3, 1, 2).reshape(npk, P, LANES)
        ctr, cti = tr(cr[d]), tr(ci[d])
        return jnp.stack([ctr * pr - cti * pi, -(ctr * pi + cti * pr)], axis=1)

    m_in_f = pack_in(T - 1 - steps, 0)
    m_in_r = pack_in(steps, 1)
    m_out_f = pack_out(steps + 1, 0)
    m_out_r = pack_out(T - steps, 1)

    def lag_kernel(d):
        car, cai = c_times_a(steps, d)
        k = jnp.einsum('tgop,gpi->tgio', car, bbr[d]) - jnp.einsum('tgop,gpi->tgio', cai, bbi[d])
        k = k.reshape(T, npk, PACK_G, S5_GROUP, S5_GROUP).transpose(0, 1, 3, 2, 4)
        return k.reshape(T, npk, S5_GROUP, LANES)

    m_intra = jnp.stack([lag_kernel(0), lag_kernel(1)])

    def a_chunk(d):
        pr, pi = powers([T], d)
        return jnp.concatenate([pr.reshape(npk, PACK_G * P), pi.reshape(npk, PACK_G * P)], axis=-1)

    return dict(m_in_f=m_in_f, m_in_r=m_in_r, m_out_f=m_out_f, m_out_r=m_out_r,
                m_intra=m_intra, a_f=a_chunk(0), a_r=a_chunk(1))


def _gdn_prep_kernel(*refs, tile, rows_conv, d_gdn):
    if rows_conv:
        main_ref, up_ref, dn_ref, w_ref, q_ref, k_ref, v_ref = refs
    else:
        main_ref, w_ref, q_ref, k_ref, v_ref = refs
    w = w_ref[...]
    if rows_conv:
        i = pl.program_id(1)
        n = pl.num_programs(1)
        up = up_ref[0] * (i > 0).astype(f32)
        dn = dn_ref[0] * (i < n - 1).astype(f32)
        ext = jnp.concatenate([up, main_ref[0], dn], axis=0)
        base = GRID_W
        drs = (-1, 0, 1)
        col = lax.broadcasted_iota(jnp.int32, (tile, 1), 0) % GRID_W
        mask_m = col != 0
        mask_p = col != GRID_W - 1
    else:
        ext = main_ref[0]
        base = 0
        drs = (0,)
        col = lax.broadcasted_iota(jnp.int32, (tile, 1), 0)
        mask_m = col != 0
        mask_p = col != tile - 1
    acc = {}
    for dc in (-1, 0, 1):
        a = None
        for dr in drs:
            tap = w[(dr + 1) * 3 + (dc + 1)][None, :]
            term = ext[base + GRID_W * dr: base + GRID_W * dr + tile] * tap
            a = term if a is None else a + term
        acc[dc] = a
    conv = (acc[0] + jnp.where(mask_m, pltpu.roll(acc[-1], 1, 0), 0.0)
            + jnp.where(mask_p, pltpu.roll(acc[1], tile - 1, 0), 0.0))
    s = _silu(conv)
    nh = d_gdn // GDN_HEAD
    for h in range(nh):
        qh = s[:, h * GDN_HEAD:(h + 1) * GDN_HEAD]
        kh = s[:, d_gdn + h * GDN_HEAD: d_gdn + (h + 1) * GDN_HEAD]
        qn = qh * (lax.rsqrt(jnp.sum(qh * qh, axis=-1, keepdims=True) + NORM_EPS) * (GDN_HEAD ** -0.5))
        kn = kh * lax.rsqrt(jnp.sum(kh * kh, axis=-1, keepdims=True) + NORM_EPS)
        q_ref[0, :, h * GDN_HEAD:(h + 1) * GDN_HEAD] = qn.astype(bf16)
        k_ref[0, :, h * GDN_HEAD:(h + 1) * GDN_HEAD] = kn.astype(bf16)
    v_ref[0] = s[:, 2 * d_gdn:].astype(bf16)


def _gdn_prep(qkv, conv_w, rows_conv):
    nb, L, c3 = qkv.shape
    d_gdn = c3 // 3
    w9 = conv_w.reshape(9, c3).astype(f32)
    if rows_conv:
        tile = min(512, L)
        nt = L // tile
        r = tile // GRID_W
        nrows = L // GRID_W
        in_specs = [pl.BlockSpec((1, tile, c3), lambda b, i: (b, i, 0)),
                    pl.BlockSpec((1, GRID_W, c3), lambda b, i: (b, jnp.maximum(i * r - 1, 0), 0)),
                    pl.BlockSpec((1, GRID_W, c3), lambda b, i: (b, jnp.minimum((i + 1) * r, nrows - 1), 0)),
                    pl.BlockSpec((9, c3), lambda b, i: (0, 0))]
        args = (qkv, qkv, qkv, w9)
    else:
        tile, nt = L, 1
        in_specs = [pl.BlockSpec((1, tile, c3), lambda b, i: (b, i, 0)),
                    pl.BlockSpec((9, c3), lambda b, i: (0, 0))]
        args = (qkv, w9)
    kern = functools.partial(_gdn_prep_kernel, tile=tile, rows_conv=rows_conv, d_gdn=d_gdn)
    o_spec = pl.BlockSpec((1, tile, d_gdn), lambda b, i: (b, i, 0))
    o_shape = jax.ShapeDtypeStruct((nb, L, d_gdn), bf16)
    return pl.pallas_call(
        kern,
        grid=(nb, nt),
        in_specs=in_specs,
        out_specs=[o_spec, o_spec, o_spec],
        out_shape=[o_shape, o_shape, o_shape],
        compiler_params=_cparams(("parallel", "parallel")),
        name="gdn_prep",
    )(*args)


def _chunk_cumsum(g, rev):
    n = g.shape[0]
    r = lax.broadcasted_iota(jnp.int32, g.shape, 0)
    s = 1
    while s < n:
        if rev:
            g = g + jnp.where(r < n - s, pltpu.roll(g, n - s, 0), 0.0)
        else:
            g = g + jnp.where(r >= s, pltpu.roll(g, s, 0), 0.0)
        s *= 2
    return g


def _tri_solve_rows(apl_sc, tl_sc, dirn, upper):
    nblk = GDN_CHUNK // 8
    base = dirn * GDN_CHUNK
    sub = lax.broadcasted_iota(jnp.int32, (8, LANES), 0)
    for step_blk in range(nblk):
        blk = (nblk - 1 - step_blk) if upper else step_blk
        mblocks = range(blk, nblk) if upper else range(0, blk + 1)

        def solve_row(s, carry, blk=blk, mblocks=mblocks):
            ii = (7 - s) if upper else s
            i = blk * 8 + ii
            acc = {}
            for mb in mblocks:
                arow = apl_sc[i, base + mb * 8:base + (mb + 1) * 8, :]
                jblocks = range(mb, nblk) if upper else range(0, mb + 1)
                for mm in range(8):
                    ab = jnp.broadcast_to(arow[mm:mm + 1, :], (8, LANES))
                    for jb in jblocks:
                        term = ab * tl_sc[mb * 8 + mm, base + jb * 8:base + (jb + 1) * 8, :]
                        key = (jb, mm % 2)
                        acc[key] = term if key not in acc else acc[key] + term
            for jb in (range(blk, nblk) if upper else range(0, blk + 1)):
                tot = acc[(jb, 0)] + acc[(jb, 1)]
                unit_row = jnp.where(sub == ii, 1.0, 0.0) if jb == blk else 0.0
                tl_sc[i, base + jb * 8:base + (jb + 1) * 8, :] = unit_row - tot
            return carry

        lax.fori_loop(0, 8, solve_row, 0)


def _gdn_scan_kernel(*refs, nchunks, nh, emit_o, has_s0):
    it = iter(refs)
    q_ref, k_ref, v_ref, g_ref = next(it), next(it), next(it), next(it)
    s0_ref = next(it) if has_s0 else None
    o_ref = next(it) if emit_o else None
    sfin_ref = next(it)
    s_sc, at_sc, apl_sc, tl_sc, mk_sc, w_sc, gcol_sc, grow_sc = (next(it) for _ in range(8))
    C = GDN_CHUNK
    bb = q_ref.shape[0]
    nunits = bb * nchunks
    s_sc[...] = s0_ref[...] if has_s0 else jnp.zeros(s_sc.shape, f32)
    row = lax.broadcasted_iota(jnp.int32, (C, N_DIR * C), 0)
    lane = lax.broadcasted_iota(jnp.int32, (C, N_DIR * C), 1)
    fwd_half = lane < C
    col = jnp.where(fwd_half, lane, lane - C)
    tri = (fwd_half & (row >= col)) | (~fwd_half & (row <= col))
    strict = (fwd_half & (row > col)) | (~fwd_half & (row < col))
    nbeta = N_DIR * nh
    gate_lane = lax.broadcasted_iota(jnp.int32, (C, LANES), 1)

    def unit_of(unit):
        bl = unit // nchunks
        return bl, pl.multiple_of((unit % nchunks) * C, C)

    def phase_a(unit, carry):
        bl, off = unit_of(unit)
        gts = g_ref[bl, pl.ds(off, C), :]
        gmix = jnp.where(gate_lane < nbeta, gts,
                         jnp.where(gate_lane < nbeta + nh, _chunk_cumsum(gts, False), _chunk_cumsum(gts, True)))
        g_rows = gmix.T[:2 * nbeta]
        gcol_sc[unit] = gmix
        grow_sc[unit] = g_rows
        for h in range(nh):
            hs = slice(h * GDN_HEAD, (h + 1) * GDN_HEAD)
            cbf, cbr = h, nh + h
            cgf, cgr = nbeta + h, nbeta + nh + h
            kb = k_ref[bl, pl.ds(off, C), hs]
            k2 = jnp.concatenate([kb, kb], axis=0)
            if emit_o:
                kq = jnp.concatenate([kb, q_ref[bl, pl.ds(off, C), hs]], axis=0)
            else:
                kq = kb
            kkqk = lax.dot_general(kq, k2, (((1,), (1,)), ((), ())), preferred_element_type=f32)
            g_col = jnp.where(fwd_half, gmix[:, cgf:cgf + 1], gmix[:, cgr:cgr + 1])
            g_row = jnp.concatenate([g_rows[cgf:cgf + 1, :], g_rows[cgr:cgr + 1, :]], axis=1)
            b_row = jnp.concatenate([g_rows[cbf:cbf + 1, :], g_rows[cbr:cbr + 1, :]], axis=1)
            decay_b = jnp.exp(jnp.where(tri, g_col - g_row, -1e30)) * b_row
            at_sc[unit * nh + h] = jnp.where(strict, kkqk[:C] * decay_b, 0.0)
            if emit_o:
                mk_sc[unit * nh + h, :C, :] = (kkqk[C:] * decay_b).astype(bf16)
        return carry

    lax.fori_loop(0, nunits, phase_a, 0, unroll=2)

    def swap_in(ib, carry):
        rows = pl.ds(pl.multiple_of(ib * 8, 8), 8)
        tl_sc[rows] = jnp.swapaxes(at_sc[:, rows, :], 0, 1)
        return carry

    def to_lanes(i, carry):
        apl_sc[i] = tl_sc[i].T
        return carry

    lax.fori_loop(0, C // 8, swap_in, 0)
    lax.fori_loop(0, C, to_lanes, 0, unroll=4)
    tl_sc[...] = jnp.zeros(tl_sc.shape, f32)
    _tri_solve_rows(apl_sc, tl_sc, 0, False)
    _tri_solve_rows(apl_sc, tl_sc, 1, True)

    def from_lanes(i, carry):
        apl_sc[i] = tl_sc[i].T
        return carry

    def swap_out(ib, carry):
        rows = pl.ds(pl.multiple_of(ib * 8, 8), 8)
        at_sc[:, rows, :] = jnp.swapaxes(apl_sc[rows], 0, 1)
        return carry

    lax.fori_loop(0, C, from_lanes, 0, unroll=4)
    lax.fori_loop(0, C // 8, swap_out, 0)

    u0_sc = (apl_sc, tl_sc)

    def u0_slot(p):
        return p // 2, pl.ds(pl.multiple_of((p % 2) * C, C), C)

    eye_bf = (lax.broadcasted_iota(jnp.int32, (GDN_HEAD, GDN_HEAD), 0)
              == lax.broadcasted_iota(jnp.int32, (GDN_HEAD, GDN_HEAD), 1)).astype(bf16)

    def phase_b2(unit, carry):
        bl, off = unit_of(unit)
        g_rows = grow_sc[unit]
        g_cols = gcol_sc[unit]
        zero_kv = jnp.zeros((C, GDN_HEAD), bf16)
        for h in range(nh):
            hs = slice(h * GDN_HEAD, (h + 1) * GDN_HEAD)
            p = unit * nh + h
            cbf, cbr = h, nh + h
            cgf, cgr = nbeta + h, nbeta + nh + h
            t_both = at_sc[p]
            kb = k_ref[bl, pl.ds(off, C), hs]
            vb = v_ref[bl, pl.ds(off, C), hs]
            g_row = jnp.concatenate([g_rows[cgf:cgf + 1, :], g_rows[cgr:cgr + 1, :]], axis=1)
            b_row = jnp.concatenate([g_rows[cbf:cbf + 1, :], g_rows[cbr:cbr + 1, :]], axis=1)
            g_last = jnp.concatenate([jnp.broadcast_to(g_cols[C - 1:C, cgf:cgf + 1], (1, C)),
                                      jnp.broadcast_to(g_cols[0:1, cgr:cgr + 1], (1, C))], axis=1)
            v_bd = jnp.concatenate([jnp.concatenate([vb, zero_kv], axis=1),
                                    jnp.concatenate([zero_kv, vb], axis=1)], axis=0)
            k_bd = jnp.concatenate([jnp.concatenate([kb, zero_kv], axis=1),
                                    jnp.concatenate([zero_kv, kb], axis=1)], axis=0)
            u0 = jnp.dot(t_both.astype(bf16), v_bd, preferred_element_type=f32)
            w = jnp.dot((t_both * jnp.exp(g_row)).astype(bf16), k_bd, preferred_element_type=f32)
            slab, rows = u0_slot(p)
            for dirn in range(N_DIR):
                u0_sc[dirn][slab, rows, :] = u0[:, dirn * GDN_HEAD:(dirn + 1) * GDN_HEAD]
                w_sc[dirn, p] = w[:, dirn * GDN_HEAD:(dirn + 1) * GDN_HEAD].astype(bf16)
            k_t = lax.dot_general(eye_bf, jnp.concatenate([kb, kb], axis=0), (((1,), (1,)), ((), ())),
                                  preferred_element_type=f32)
            mk_sc[p, C:, :] = (k_t * (b_row * jnp.exp(g_last - g_row))).astype(bf16)
        return carry

    lax.fori_loop(0, nunits, phase_b2, 0, unroll=4)

    def phase_c(step, carry, accumulate):
        chains = []
        for dirn, unit in ((0, step), (1, nunits - 1 - step)):
            bl, off = unit_of(unit)
            g_col_all = gcol_sc[unit]
            g_tot_all = g_col_all[0:1, :] if dirn else g_col_all[C - 1:C, :]
            for h in range(nh):
                cg = nbeta + dirn * nh + h
                hs = slice(h * GDN_HEAD, (h + 1) * GDN_HEAD)
                p = unit * nh + h
                slab, rows = u0_slot(p)
                lhs = w_sc[dirn, p]
                ch = dict(idx=(bl, dirn, h), off=off, hs=hs, dirn=dirn, u0=u0_sc[dirn][slab, rows, :],
                          g_last=jnp.exp(g_tot_all[:, cg:cg + 1]))
                if emit_o:
                    lhs = jnp.concatenate([lhs, q_ref[bl, pl.ds(off, C), hs]], axis=0)
                    ch['mk'] = mk_sc[p]
                    ch['gam'] = jnp.exp(g_col_all[:, cg:cg + 1])
                else:
                    ch['mk'] = mk_sc[p, C:, :]
                ch['lhs'] = lhs
                chains.append(ch)
        s_old = [s_sc[ch['idx']] for ch in chains]
        r = [jnp.dot(ch['lhs'], s.astype(bf16), preferred_element_type=f32) for ch, s in zip(chains, s_old)]
        zeros = jnp.zeros((C, GDN_HEAD), bf16)
        ub = [(ch['u0'] - ri[:C]).astype(bf16) for ch, ri in zip(chains, r)]
        m = [jnp.dot(ch['mk'], jnp.concatenate([zeros, ubi] if ch['dirn'] else [ubi, zeros], axis=0),
                     preferred_element_type=f32) for ch, ubi in zip(chains, ub)]
        for n, ch in enumerate(chains):
            if emit_o:
                bl = ch['idx'][0]
                o_new = ch['gam'] * r[n][C:] + m[n][:C]
                if accumulate:
                    o_ref[bl, pl.ds(ch['off'], C), ch['hs']] += o_new
                else:
                    o_ref[bl, pl.ds(ch['off'], C), ch['hs']] = o_new
                s_sc[ch['idx']] = ch['g_last'] * s_old[n] + m[n][C:]
            else:
                s_sc[ch['idx']] = ch['g_last'] * s_old[n] + m[n]
        return carry

    if emit_o and bb == 1:
        lax.fori_loop(0, nunits // 2, functools.partial(phase_c, accumulate=False), 0)
        lax.fori_loop(nunits // 2, nunits, functools.partial(phase_c, accumulate=True), 0)
    elif emit_o:
        o_ref[...] = jnp.zeros(o_ref.shape, f32)
        lax.fori_loop(0, nunits, functools.partial(phase_c, accumulate=True), 0)
    else:
        lax.fori_loop(0, nunits, functools.partial(phase_c, accumulate=False), 0)
    sfin_ref[...] = s_sc[...]


def _gdn_scan(q, k, v, gates, s0, *, emit_o):
    nb, L, d_gdn = q.shape
    nh = d_gdn // GDN_HEAD
    nchunks = L // GDN_CHUNK
    C = GDN_CHUNK
    bb = max(1, min(nb, LANES // (nchunks * nh)))
    nunits = bb * nchunks
    nprob = nunits * nh
    assert nprob == LANES, (nb, L, nh)
    has_s0 = s0 is not None
    kern = functools.partial(_gdn_scan_kernel, nchunks=nchunks, nh=nh, emit_o=emit_o, has_s0=has_s0)
    seq = lambda b: (b, 0, 0)
    st = lambda b: (b, 0, 0, 0, 0)
    st_spec = pl.BlockSpec((bb, N_DIR, nh, GDN_HEAD, GDN_HEAD), st)
    st_shape = jax.ShapeDtypeStruct((nb, N_DIR, nh, GDN_HEAD, GDN_HEAD), f32)
    args = [q, k, v, gates]
    in_specs = [pl.BlockSpec((bb, L, d_gdn), seq)] * 3 + [pl.BlockSpec((bb, L, LANES), seq)]
    if has_s0:
        args.append(s0)
        in_specs.append(st_spec)
    out_specs, out_shape = [], []
    if emit_o:
        out_specs.append(pl.BlockSpec((bb, L, d_gdn), seq))
        out_shape.append(jax.ShapeDtypeStruct((nb, L, d_gdn), f32))
    out_specs.append(st_spec)
    out_shape.append(st_shape)
    return pl.pallas_call(
        kern,
        grid=(nb // bb,),
        in_specs=in_specs,
        out_specs=out_specs,
        out_shape=out_shape,
        scratch_shapes=[pltpu.VMEM((bb, N_DIR, nh, GDN_HEAD, GDN_HEAD), f32),
                        pltpu.VMEM((nprob, C, N_DIR * C), f32),
                        pltpu.VMEM((C, N_DIR * C, nprob), f32),
                        pltpu.VMEM((C, N_DIR * C, nprob), f32),
                        pltpu.VMEM((nprob, C + GDN_HEAD, N_DIR * C), bf16),
                        pltpu.VMEM((N_DIR, nprob, C, GDN_HEAD), bf16),
                        pltpu.VMEM((nunits, C, LANES), f32),
                        pltpu.VMEM((nunits, 4 * nh, C), f32)],
        compiler_params=_cparams(("parallel",)),
        name="gdn_scan" if emit_o else "gdn_scan_state",
    )(*args)


def _post_kernel(x_ref, gate_ref, y5_ref, zs_ref, og_ref, zg_ref,
                 wglu_ref, bglu_ref, nw_ref, wout_ref, lng_ref, lnb_ref, o_ref, *, tt, alpha):
    nb, _, d = x_ref.shape
    rows = nb * tt
    npk = y5_ref.shape[0]
    d_s5 = npk * LANES
    d_gdn = og_ref.shape[2]
    parts = []
    for j in range(npk):
        parts.append(jnp.swapaxes(y5_ref[j].reshape(tt, nb, LANES), 0, 1).reshape(rows, LANES))
    y = jnp.concatenate(parts, axis=1)
    g = 0.5 * y * (1.0 + lax.erf(y * (2.0 ** -0.5)))
    glu = jnp.dot(g.astype(bf16), wglu_ref[...], preferred_element_type=f32) + bglu_ref[...]
    s5 = g * _sigmoid(glu) * _silu(zs_ref[...].astype(f32).reshape(rows, d_s5))
    o = og_ref[...].reshape(rows, d_gdn)
    zg = zg_ref[...].astype(f32).reshape(rows, d_gdn)
    gd = []
    for h in range(d_gdn // GDN_HEAD):
        oh = o[:, h * GDN_HEAD:(h + 1) * GDN_HEAD]
        gd.append(oh * lax.rsqrt(jnp.mean(oh * oh, axis=-1, keepdims=True) + NORM_EPS) * nw_ref[...])
    gdn = jnp.concatenate(gd, axis=1) * _silu(zg)
    mix = jnp.concatenate([s5, gdn], axis=1).astype(bf16)
    yv = jnp.dot(mix, wout_ref[...], preferred_element_type=f32).reshape(nb, tt, d)
    r = alpha * x_ref[...] + gate_ref[...] * yv
    mu = jnp.mean(r, axis=-1, keepdims=True)
    var = jnp.mean(jnp.square(r - mu), axis=-1, keepdims=True)
    o_ref[...] = (r - mu) * lax.rsqrt(var + LN_EPS) * lng_ref[...] + lnb_ref[...]


def _post(x, gate, y5, zs, o_gdn, zg, wglu, bglu, nw, wout, lng, lnb, alpha):
    nb, L, d = x.shape
    tt = 32
    npk = y5.shape[0]
    d_s5 = npk * LANES
    d_gdn = o_gdn.shape[2]
    kern = functools.partial(_post_kernel, tt=tt, alpha=alpha)
    tok = lambda i: (0, i, 0)
    c2 = lambda i: (0, 0)
    c3 = lambda i: (0, 0, 0)
    y_spec = pl.BlockSpec((npk, tt // S5_T, S5_T, nb, LANES), lambda i: (0, i, 0, 0, 0))
    return pl.pallas_call(
        kern,
        grid=(L // tt,),
        in_specs=[pl.BlockSpec((nb, tt, d), tok),
                  pl.BlockSpec((nb, 1, d), c3),
                  y_spec,
                  pl.BlockSpec((nb, tt, d_s5), tok),
                  pl.BlockSpec((nb, tt, d_gdn), tok),
                  pl.BlockSpec((nb, tt, d_gdn), tok),
                  pl.BlockSpec(wglu.shape, c2),
                  pl.BlockSpec(bglu.shape, c2),
                  pl.BlockSpec(nw.shape, c2),
                  pl.BlockSpec(wout.shape, c2),
                  pl.BlockSpec(lng.shape, c3),
                  pl.BlockSpec(lnb.shape, c3)],
        out_specs=pl.BlockSpec((nb, tt, d), tok),
        out_shape=jax.ShapeDtypeStruct((nb, L, d), f32),
        compiler_params=_cparams(("parallel",)),
        name="post",
    )(x, gate, y5, zs, o_gdn, zg, wglu, bglu, nw, wout, lng, lnb)


def _layer(x, c, ctx, c_ctx, p, alpha):
    nb, L, d = x.shape
    d_s5 = p['s5_d'].shape[0]
    d_gdn = p['conv_w'].shape[-1] // 3
    nh = d_gdn // GDN_HEAD
    npk = d_s5 // LANES

    pad = (-(nb + 1)) % 8
    cond = jnp.concatenate([c, c_ctx[None, :], jnp.zeros((pad, d), f32)], axis=0)
    m = _ada(cond, p['w_ada'], p['b_ada'])
    shift, scale, gate = m[:nb, :d], m[:nb, d:2 * d], m[:nb, 2 * d:]
    shift_c = jnp.broadcast_to(m[nb:nb + 1, :d], (nb, d))
    scale_c = jnp.broadcast_to(m[nb:nb + 1, d:2 * d], (nb, d))

    ng = 2 * N_DIR * nh
    w_all = jnp.pad(p['w_in'], ((0, 0), (0, LANES - ng))).astype(bf16)
    gp = jnp.pad(jnp.stack([p['gdn_a_log'].reshape(-1), p['gdn_dt_bias'].reshape(-1)]).astype(f32),
                 ((0, 0), (ng // 2, LANES - ng)))

    u5, zs, qkv, zg, gates = _inproj(x, scale[:, None, :], shift[:, None, :], w_all, gp, d_s5, d_gdn)
    u5c, _, qkvc, _, gatesc = _inproj(ctx, scale_c[:, None, :], shift_c[:, None, :], w_all, gp, d_s5, d_gdn)

    sw = _s5_weights(p['s5_lambda_re'], p['s5_lambda_im'], p['s5_log_dt'],
                     p['s5_b_re'], p['s5_b_im'], p['s5_c_re'], p['s5_c_im'])
    sdim = sw['a_f'].shape[-1]
    a_f = jnp.broadcast_to(sw['a_f'][:, None, :], (npk, nb, sdim))
    a_r = jnp.broadcast_to(sw['a_r'][:, None, :], (npk, nb, sdim))
    h_zero = jnp.zeros((npk, nb, sdim), f32)
    d_skip = jnp.tile(p['s5_d'].astype(f32).reshape(npk, 1, LANES), (1, 1, S5_T))
    (hc_f,) = _s5_pass(u5c, sw['m_in_f'], a_f, h_zero, rev=False, emit_y=False)
    (hc_r,) = _s5_pass(u5c, sw['m_in_r'], a_r, h_zero, rev=True, emit_y=False)
    y_f, _ = _s5_pass(u5, sw['m_in_f'], a_f, hc_f, sw['m_out_f'], sw['m_intra'], d_skip, rev=False, emit_y=True)
    y_s5, _ = _s5_pass(u5, sw['m_in_r'], a_r, hc_r, sw['m_out_r'], y_add=y_f, rev=True, emit_y=True)

    ql, kl, vl = _gdn_prep(qkv, p['conv_w'], True)
    qc, kc, vc = _gdn_prep(qkvc, p['conv_w'], False)
    (s_ctx,) = _gdn_scan(qc, kc, vc, gatesc, None, emit_o=False)
    o_gdn, _ = _gdn_scan(ql, kl, vl, gates, s_ctx, emit_o=True)

    return _post(x, gate[:, None, :], y_s5, zs, o_gdn, zg,
                 p['w_glu'].astype(bf16), p['b_glu'].astype(f32).reshape(1, d_s5),
                 p['gdn_norm_w'].astype(f32).reshape(1, GDN_HEAD),
                 p['w_out'].astype(bf16), p['ln_g'].astype(f32).reshape(1, 1, d),
                 p['ln_b'].astype(f32).reshape(1, 1, d), alpha)


def kernel(x, c, ctx, c_ctx, w_ada, b_ada, w_in, s5_lambda_re, s5_lambda_im, s5_log_dt, s5_b_re, s5_b_im, s5_c_re, s5_c_im, s5_d, w_glu, b_glu, conv_w, gdn_a_log, gdn_dt_bias, gdn_norm_w, w_out, ln_g, ln_b):
    depth = w_ada.shape[0]
    assert depth == 1, "context-token outputs are only skipped for the last layer"
    alpha = (2.0 * depth) ** 0.25
    p = {
        'w_ada': w_ada[0], 'b_ada': b_ada[0], 'w_in': w_in[0],
        's5_lambda_re': s5_lambda_re[0], 's5_lambda_im': s5_lambda_im[0], 's5_log_dt': s5_log_dt[0],
        's5_b_re': s5_b_re[0], 's5_b_im': s5_b_im[0], 's5_c_re': s5_c_re[0], 's5_c_im': s5_c_im[0],
        's5_d': s5_d[0], 'w_glu': w_glu[0], 'b_glu': b_glu[0], 'conv_w': conv_w[0],
        'gdn_a_log': gdn_a_log[0], 'gdn_dt_bias': gdn_dt_bias[0], 'gdn_norm_w': gdn_norm_w[0],
        'w_out': w_out[0], 'ln_g': ln_g[0], 'ln_b': ln_b[0],
    }
    return _layer(x, c, ctx, c_ctx, p, alpha)
```

```python
import functools

import numpy as np
import jax
import jax.numpy as jnp
from jax import lax
from jax.experimental import pallas as pl
from jax.experimental.pallas import tpu as pltpu

f32 = jnp.float32
bf16 = jnp.bfloat16
HI = lax.Precision.HIGHEST

LANES = 128
S5_GROUP = 16
S5_T = 8
PACK_G = LANES // S5_GROUP
GDN_HEAD = 128
GDN_CHUNK = 64
GRID_W = 64
N_DIR = 2
LN_EPS = 1e-5
NORM_EPS = 1e-6
V7X_VMEM_BYTES = 64 * 1024 * 1024
VMEM_LIMIT = V7X_VMEM_BYTES - 8 * 1024 * 1024


def _cparams(sem):
    return pltpu.CompilerParams(dimension_semantics=sem, vmem_limit_bytes=VMEM_LIMIT)


def _sigmoid(x):
    return 0.5 + 0.5 * jnp.tanh(0.5 * x)


def _silu(x):
    h = 0.5 * x
    return h + h * jnp.tanh(h)


def _ada_kernel(c_ref, w_ref, b_ref, o_ref):
    o_ref[...] = jnp.dot(_silu(c_ref[...]), w_ref[...], precision=HI,
                         preferred_element_type=f32) + b_ref[...]


def _ada(cond, w_ada, b_ada):
    rows, d = cond.shape
    n = w_ada.shape[1]
    tn = 512
    return pl.pallas_call(
        _ada_kernel,
        grid=(n // tn,),
        in_specs=[pl.BlockSpec((rows, d), lambda i: (0, 0)),
                  pl.BlockSpec((d, tn), lambda i: (0, i)),
                  pl.BlockSpec((1, tn), lambda i: (0, i))],
        out_specs=pl.BlockSpec((rows, tn), lambda i: (0, i)),
        out_shape=jax.ShapeDtypeStruct((rows, n), f32),
        compiler_params=_cparams(("parallel",)),
        name="ada",
    )(cond, w_ada, b_ada.reshape(1, n))


def _inproj_kernel(x_ref, scale_ref, shift_ref, w_ref, gp_ref,
                   u_ref, zs_ref, qkv_ref, zg_ref, gate_ref, *, tt, d_s5, d_gdn, n_beta):
    nb, _, d = x_ref.shape
    n_main = d_s5 + 4 * d_gdn
    wu_ref = w_ref.at[:, :d_s5]
    wm_ref = w_ref.at[:, d_s5:d_s5 + n_main]
    wg_ref = w_ref.at[:, d_s5 + n_main:]
    sc = 1.0 + scale_ref[...]
    sh = shift_ref[...]
    h = (x_ref[...] * sc + sh).reshape(nb * tt, d).astype(bf16)
    r = jnp.dot(h, wm_ref[...], preferred_element_type=f32)
    zs_ref[...] = r[:, :d_s5].reshape(nb, tt, d_s5).astype(bf16)
    qkv_ref[...] = r[:, d_s5:d_s5 + 3 * d_gdn].reshape(nb, tt, 3 * d_gdn)
    zg_ref[...] = r[:, d_s5 + 3 * d_gdn:].reshape(nb, tt, d_gdn).astype(bf16)
    lg = jnp.dot(h, wg_ref[...], preferred_element_type=f32)
    lane = lax.broadcasted_iota(jnp.int32, lg.shape, 1)
    a = lg + gp_ref[1:2, :]
    softplus = jnp.maximum(a, 0.0) + jnp.log1p(jnp.exp(-jnp.abs(a)))
    gates = jnp.where(lane < n_beta, _sigmoid(lg), -jnp.exp(gp_ref[0:1, :]) * softplus)
    gate_ref[...] = gates.reshape(nb, tt, LANES)
    sc2 = sc[:, 0, :]
    sh2 = sh[:, 0, :]
    ht = (jnp.swapaxes(x_ref[...], 0, 1) * sc2[None] + sh2[None]).reshape(tt * nb, d).astype(bf16)
    ru = jnp.dot(ht, wu_ref[...], preferred_element_type=f32)
    for j in range(d_s5 // LANES):
        u_ref[j] = ru[:, j * LANES:(j + 1) * LANES].reshape(tt // S5_T, S5_T, nb, LANES)


def _inproj(x, scale, shift, w_all, gp, d_s5, d_gdn):
    nb, L, d = x.shape
    tt = 32
    npk = d_s5 // LANES
    kern = functools.partial(_inproj_kernel, tt=tt, d_s5=d_s5, d_gdn=d_gdn,
                             n_beta=N_DIR * (d_gdn // GDN_HEAD))
    const2 = lambda i: (0, 0)
    const3 = lambda i: (0, 0, 0)
    return pl.pallas_call(
        kern,
        grid=(L // tt,),
        in_specs=[pl.BlockSpec((nb, tt, d), lambda i: (0, i, 0)),
                  pl.BlockSpec((nb, 1, d), const3),
                  pl.BlockSpec((nb, 1, d), const3),
                  pl.BlockSpec(w_all.shape, const2),
                  pl.BlockSpec(gp.shape, const2)],
        out_specs=[pl.BlockSpec((npk, tt // S5_T, S5_T, nb, LANES), lambda i: (0, i, 0, 0, 0)),
                   pl.BlockSpec((nb, tt, d_s5), lambda i: (0, i, 0)),
                   pl.BlockSpec((nb, tt, 3 * d_gdn), lambda i: (0, i, 0)),
                   pl.BlockSpec((nb, tt, d_gdn), lambda i: (0, i, 0)),
                   pl.BlockSpec((nb, tt, LANES), lambda i: (0, i, 0))],
        out_shape=[jax.ShapeDtypeStruct((npk, L // S5_T, S5_T, nb, LANES), f32),
                   jax.ShapeDtypeStruct((nb, L, d_s5), bf16),
                   jax.ShapeDtypeStruct((nb, L, 3 * d_gdn), f32),
                   jax.ShapeDtypeStruct((nb, L, d_gdn), bf16),
                   jax.ShapeDtypeStruct((nb, L, LANES), f32)],
        compiler_params=_cparams(("parallel",)),
        name="inproj",
    )(x, scale, shift, w_all, gp)


def _s5_kernel(*refs, ct, rev, emit_y, intra, add_y):
    it = iter(refs)
    u_ref, cin_ref, at_ref, h0_ref = next(it), next(it), next(it), next(it)
    cout_ref = next(it) if emit_y else None
    cintra_ref = next(it) if intra else None
    d_ref = next(it) if intra else None
    yadd_ref = next(it) if add_y else None
    y_ref = next(it) if emit_y else None
    hfin_ref = next(it)
    h_sc, v_sc, hall_sc, min_sc = next(it), next(it), next(it), next(it)
    mout_sc = next(it) if emit_y else None
    mintra_sc = next(it) if intra else None

    t = pl.program_id(1)
    nt = pl.num_programs(1)
    nb = u_ref.shape[3]
    sdim = h_sc.shape[1]
    half = sdim // 2
    gstate = half // PACK_G

    @pl.when(t == 0)
    def _():
        h_sc[...] = h0_ref[0]
        r_in = lax.broadcasted_iota(jnp.int32, (LANES, sdim), 0) // S5_GROUP
        c_in = (lax.broadcasted_iota(jnp.int32, (LANES, sdim), 1) % half) // gstate
        mask_in = (r_in == c_in).astype(f32)
        for s in range(S5_T):
            min_sc[s * LANES:(s + 1) * LANES, :] = (
                jnp.tile(jnp.concatenate([cin_ref[s, 0, 0], cin_ref[s, 1, 0]], axis=1), (PACK_G, 1))
                * mask_in).astype(bf16)
        if emit_y:
            r_out = lax.broadcasted_iota(jnp.int32, (half, LANES), 0) // gstate
            c_out = lax.broadcasted_iota(jnp.int32, (half, LANES), 1) // S5_GROUP
            mask_out = (r_out == c_out).astype(f32)
            for i in range(S5_T):
                for ri in range(2):
                    blk = cout_ref[i, ri, 0]
                    mout_sc[ri * half:(ri + 1) * half, i * LANES:(i + 1) * LANES] = (
                        jnp.tile(blk, (PACK_G, 1)) * mask_out).astype(bf16)
        if intra:
            r_x = lax.broadcasted_iota(jnp.int32, (LANES, LANES), 0) // S5_GROUP
            c_x = lax.broadcasted_iota(jnp.int32, (LANES, LANES), 1) // S5_GROUP
            mask_x = (r_x == c_x).astype(f32)
            blocks = {}
            for lag in range(1 - S5_T, S5_T):
                k = cintra_ref[0, lag, 0] if lag > 0 else cintra_ref[1, -lag, 0]
                if lag == 0:
                    k = k + cintra_ref[0, 0, 0]
                blocks[lag] = (jnp.tile(k, (PACK_G, 1)) * mask_x).astype(bf16)
            for s in range(S5_T):
                for i in range(S5_T):
                    mintra_sc[s * LANES:(s + 1) * LANES, i * LANES:(i + 1) * LANES] = blocks[i - s]

    x = jnp.concatenate([u_ref[0, :, i, :, :].reshape(ct * nb, LANES) for i in range(S5_T)], axis=1)
    xb = x.astype(bf16)
    v_sc[...] = jnp.dot(xb, min_sc[...], preferred_element_type=f32)
    are = at_ref[0, :, :half]
    aim = at_ref[0, :, half:]

    def body(s, carry):
        k = (ct - 1 - s) if rev else s
        off = pl.multiple_of(k * nb, nb)
        hre, him = carry
        hall_sc[pl.ds(off, nb), :half] = hre
        hall_sc[pl.ds(off, nb), half:] = him
        vre = v_sc[pl.ds(off, nb), :half]
        vim = v_sc[pl.ds(off, nb), half:]
        return are * hre - aim * him + vre, are * him + aim * hre + vim

    hre, him = lax.fori_loop(0, ct, body, (h_sc[:, :half], h_sc[:, half:]))
    h_sc[:, :half] = hre
    h_sc[:, half:] = him

    @pl.when(t == nt - 1)
    def _():
        hfin_ref[0] = h_sc[...]

    if emit_y:
        y = jnp.dot(hall_sc[...].astype(bf16), mout_sc[...], preferred_element_type=f32)
        if intra:
            y = y + jnp.dot(xb, mintra_sc[...], preferred_element_type=f32) + x * d_ref[0]
        for i in range(S5_T):
            yi = y[:, i * LANES:(i + 1) * LANES].reshape(ct, nb, LANES)
            y_ref[0, :, i, :, :] = yi + yadd_ref[0, :, i, :, :] if add_y else yi


def _s5_pass(u5, c_in, a_t, h0, c_out=None, c_intra=None, d_skip=None, y_add=None, *, rev, emit_y):
    npk, nc, _, nb, _ = u5.shape
    ct = min(32, nc)
    nt = nc // ct
    intra = c_intra is not None
    add_y = y_add is not None
    kdim = S5_T * LANES
    sdim = a_t.shape[-1]
    tmap = (lambda j, t: (j, nt - 1 - t, 0, 0, 0)) if rev else (lambda j, t: (j, t, 0, 0, 0))
    pmap = lambda j, t: (j, 0, 0)
    wmap = lambda j, t: (0, 0, j, 0, 0)
    args = [u5, c_in, a_t, h0]
    in_specs = [pl.BlockSpec((1, ct, S5_T, nb, LANES), tmap),
                pl.BlockSpec(c_in.shape[:2] + (1,) + c_in.shape[3:], wmap),
                pl.BlockSpec((1, nb, sdim), pmap),
                pl.BlockSpec((1, nb, sdim), pmap)]
    scratch = [pltpu.VMEM((nb, sdim), f32),
               pltpu.VMEM((ct * nb, sdim), f32),
               pltpu.VMEM((ct * nb, sdim), f32),
               pltpu.VMEM((kdim, sdim), bf16)]
    if emit_y:
        args.append(c_out)
        in_specs.append(pl.BlockSpec(c_out.shape[:2] + (1,) + c_out.shape[3:], wmap))
        scratch.append(pltpu.VMEM((sdim, kdim), bf16))
    if intra:
        args += [c_intra, d_skip]
        in_specs += [pl.BlockSpec(c_intra.shape[:2] + (1,) + c_intra.shape[3:], wmap),
                     pl.BlockSpec((1, 1, kdim), pmap)]
        scratch.append(pltpu.VMEM((kdim, kdim), bf16))
    if add_y:
        args.append(y_add)
        in_specs.append(pl.BlockSpec((1, ct, S5_T, nb, LANES), tmap))
    out_specs, out_shape = [], []
    if emit_y:
        out_specs.append(pl.BlockSpec((1, ct, S5_T, nb, LANES), tmap))
        out_shape.append(jax.ShapeDtypeStruct(u5.shape, f32))
    out_specs.append(pl.BlockSpec((1, nb, sdim), pmap))
    out_shape.append(jax.ShapeDtypeStruct((npk, nb, sdim), f32))
    kern = functools.partial(_s5_kernel, ct=ct, rev=rev, emit_y=emit_y, intra=intra, add_y=add_y)
    return pl.pallas_call(
        kern,
        grid=(npk, nt),
        in_specs=in_specs,
        out_specs=out_specs,
        out_shape=out_shape,
        scratch_shapes=scratch,
        compiler_params=_cparams(("parallel", "arbitrary")),
        name="s5_rev" if rev else "s5_fwd",
    )(*args)


def _s5_weights(lam_re, lam_im, log_dt, b_re, b_im, c_re, c_im):
    T = S5_T
    ndir, G, P = lam_re.shape
    npk = G // PACK_G
    dt = jnp.exp(log_dt.astype(f32))[..., None]
    lr, li = lam_re.astype(f32), lam_im.astype(f32)
    zr, zi = lr * dt, li * dt

    nr = jnp.expm1(zr) * jnp.cos(zi) - 2.0 * jnp.square(jnp.sin(0.5 * zi))
    ni = jnp.exp(zr) * jnp.sin(zi)
    den = lr * lr + li * li
    qr = (nr * lr + ni * li) / den
    qi = (ni * lr - nr * li) / den
    bre, bim = b_re.astype(f32), b_im.astype(f32)
    bbr = qr[..., None] * bre - qi[..., None] * bim
    bbi = qr[..., None] * bim + qi[..., None] * bre
    cr, ci = c_re.astype(f32), c_im.astype(f32)

    def powers(e, d):
        ev = jnp.asarray(np.asarray(e, np.float32))[:, None, None]
        m = jnp.exp(ev * zr[d])
        return m * jnp.cos(ev * zi[d]), m * jnp.sin(ev * zi[d])

    def c_times_a(e, d):
        pr, pi = powers(e, d)
        pr, pi = pr[:, :, None, :], pi[:, :, None, :]
        return cr[d] * pr - ci[d] * pi, cr[d] * pi + ci[d] * pr

    steps = np.arange(T)

    def pack_in(e, d):
        pr, pi = powers(e, d)
        pr = pr.reshape(T, npk, 1, PACK_G * P)
        pi = pi.reshape(T, npk, 1, PACK_G * P)
        tr = lambda b: b.reshape(npk, PACK_G, P, S5_GROUP).transpose(0, 3, 1, 2).reshape(npk, S5_GROUP, PACK_G * P)
        btr, bti = tr(bbr[d]), tr(bbi[d])
        return jnp.stack([pr * btr - pi * bti, pr * bti + pi * btr], axis=1)

    def pack_out(e, d):
        pr, pi = powers(e, d)
        rep = lambda v: jnp.repeat(v.reshape(T, npk, PACK_G, P).transpose(0, 1, 3, 2), S5_GROUP, axis=-1)
        pr, pi = rep(pr), rep(pi)
        tr = lambda c: c.reshape(npk, PACK_G, S5_GROUP, P).transpose(0, 3, 1, 2).reshape(npk, P, LANES)
        ctr, cti = tr(cr[d]), tr(ci[d])
        return jnp.stack([ctr * pr - cti * pi, -(ctr * pi + cti * pr)], axis=1)

    m_in_f = pack_in(T - 1 - steps, 0)
    m_in_r = pack_in(steps, 1)
    m_out_f = pack_out(steps + 1, 0)
    m_out_r = pack_out(T - steps, 1)

    def lag_kernel(d):
        car, cai = c_times_a(steps, d)
        k = jnp.einsum('tgop,gpi->tgio', car, bbr[d]) - jnp.einsum('tgop,gpi->tgio', cai, bbi[d])
        k = k.reshape(T, npk, PACK_G, S5_GROUP, S5_GROUP).transpose(0, 1, 3, 2, 4)
        return k.reshape(T, npk, S5_GROUP, LANES)

    m_intra = jnp.stack([lag_kernel(0), lag_kernel(1)])

    def a_chunk(d):
        pr, pi = powers([T], d)
        return jnp.concatenate([pr.reshape(npk, PACK_G * P), pi.reshape(npk, PACK_G * P)], axis=-1)

    return dict(m_in_f=m_in_f, m_in_r=m_in_r, m_out_f=m_out_f, m_out_r=m_out_r,
                m_intra=m_intra, a_f=a_chunk(0), a_r=a_chunk(1))


def _gdn_prep_kernel(*refs, tile, rows_conv, d_gdn):
    if rows_conv:
        main_ref, up_ref, dn_ref, w_ref, q_ref, k_ref, v_ref = refs
    else:
        main_ref, w_ref, q_ref, k_ref, v_ref = refs
    w = w_ref[...]
    if rows_conv:
        i = pl.program_id(1)
        n = pl.num_programs(1)
        up = up_ref[0] * (i > 0).astype(f32)
        dn = dn_ref[0] * (i < n - 1).astype(f32)
        ext = jnp.concatenate([up, main_ref[0], dn], axis=0)
        base = GRID_W
        drs = (-1, 0, 1)
        col = lax.broadcasted_iota(jnp.int32, (tile, 1), 0) % GRID_W
        mask_m = col != 0
        mask_p = col != GRID_W - 1
    else:
        ext = main_ref[0]
        base = 0
        drs = (0,)
        col = lax.broadcasted_iota(jnp.int32, (tile, 1), 0)
        mask_m = col != 0
        mask_p = col != tile - 1
    acc = {}
    for dc in (-1, 0, 1):
        a = None
        for dr in drs:
            tap = w[(dr + 1) * 3 + (dc + 1)][None, :]
            term = ext[base + GRID_W * dr: base + GRID_W * dr + tile] * tap
            a = term if a is None else a + term
        acc[dc] = a
    conv = (acc[0] + jnp.where(mask_m, pltpu.roll(acc[-1], 1, 0), 0.0)
            + jnp.where(mask_p, pltpu.roll(acc[1], tile - 1, 0), 0.0))
    s = _silu(conv)
    nh = d_gdn // GDN_HEAD
    for h in range(nh):
        qh = s[:, h * GDN_HEAD:(h + 1) * GDN_HEAD]
        kh = s[:, d_gdn + h * GDN_HEAD: d_gdn + (h + 1) * GDN_HEAD]
        qn = qh * (lax.rsqrt(jnp.sum(qh * qh, axis=-1, keepdims=True) + NORM_EPS) * (GDN_HEAD ** -0.5))
        kn = kh * lax.rsqrt(jnp.sum(kh * kh, axis=-1, keepdims=True) + NORM_EPS)
        q_ref[0, :, h * GDN_HEAD:(h + 1) * GDN_HEAD] = qn.astype(bf16)
        k_ref[0, :, h * GDN_HEAD:(h + 1) * GDN_HEAD] = kn.astype(bf16)
    v_ref[0] = s[:, 2 * d_gdn:].astype(bf16)


def _gdn_prep(qkv, conv_w, rows_conv):
    nb, L, c3 = qkv.shape
    d_gdn = c3 // 3
    w9 = conv_w.reshape(9, c3).astype(f32)
    if rows_conv:
        tile = min(512, L)
        nt = L // tile
        r = tile // GRID_W
        nrows = L // GRID_W
        in_specs = [pl.BlockSpec((1, tile, c3), lambda b, i: (b, i, 0)),
                    pl.BlockSpec((1, GRID_W, c3), lambda b, i: (b, jnp.maximum(i * r - 1, 0), 0)),
                    pl.BlockSpec((1, GRID_W, c3), lambda b, i: (b, jnp.minimum((i + 1) * r, nrows - 1), 0)),
                    pl.BlockSpec((9, c3), lambda b, i: (0, 0))]
        args = (qkv, qkv, qkv, w9)
    else:
        tile, nt = L, 1
        in_specs = [pl.BlockSpec((1, tile, c3), lambda b, i: (b, i, 0)),
                    pl.BlockSpec((9, c3), lambda b, i: (0, 0))]
        args = (qkv, w9)
    kern = functools.partial(_gdn_prep_kernel, tile=tile, rows_conv=rows_conv, d_gdn=d_gdn)
    o_spec = pl.BlockSpec((1, tile, d_gdn), lambda b, i: (b, i, 0))
    o_shape = jax.ShapeDtypeStruct((nb, L, d_gdn), bf16)
    return pl.pallas_call(
        kern,
        grid=(nb, nt),
        in_specs=in_specs,
        out_specs=[o_spec, o_spec, o_spec],
        out_shape=[o_shape, o_shape, o_shape],
        compiler_params=_cparams(("parallel", "parallel")),
        name="gdn_prep",
    )(*args)


def _chunk_cumsum(g, rev):
    n = g.shape[0]
    r = lax.broadcasted_iota(jnp.int32, g.shape, 0)
    s = 1
    while s < n:
        if rev:
            g = g + jnp.where(r < n - s, pltpu.roll(g, n - s, 0), 0.0)
        else:
            g = g + jnp.where(r >= s, pltpu.roll(g, s, 0), 0.0)
        s *= 2
    return g


def _tri_solve_rows(apl_sc, tl_sc, dirn, upper):
    nblk = GDN_CHUNK // 8
    base = dirn * GDN_CHUNK
    sub = lax.broadcasted_iota(jnp.int32, (8, LANES), 0)
    for step_blk in range(nblk):
        blk = (nblk - 1 - step_blk) if upper else step_blk
        mblocks = range(blk, nblk) if upper else range(0, blk + 1)

        def solve_row(s, carry, blk=blk, mblocks=mblocks):
            ii = (7 - s) if upper else s
            i = blk * 8 + ii
            acc = {}
            for mb in mblocks:
                arow = apl_sc[i, base + mb * 8:base + (mb + 1) * 8, :]
                jblocks = range(mb, nblk) if upper else range(0, mb + 1)
                for mm in range(8):
                    ab = jnp.broadcast_to(arow[mm:mm + 1, :], (8, LANES))
                    for jb in jblocks:
                        term = ab * tl_sc[mb * 8 + mm, base + jb * 8:base + (jb + 1) * 8, :]
                        key = (jb, mm % 2)
                        acc[key] = term if key not in acc else acc[key] + term
            for jb in (range(blk, nblk) if upper else range(0, blk + 1)):
                tot = acc[(jb, 0)] + acc[(jb, 1)]
                unit_row = jnp.where(sub == ii, 1.0, 0.0) if jb == blk else 0.0
                tl_sc[i, base + jb * 8:base + (jb + 1) * 8, :] = unit_row - tot
            return carry

        lax.fori_loop(0, 8, solve_row, 0)


def _gdn_scan_kernel(*refs, nchunks, nh, emit_o, has_s0):
    it = iter(refs)
    q_ref, k_ref, v_ref, g_ref = next(it), next(it), next(it), next(it)
    s0_ref = next(it) if has_s0 else None
    o_ref = next(it) if emit_o else None
    sfin_ref = next(it)
    s_sc, at_sc, apl_sc, tl_sc, mk_sc, w_sc, gcol_sc, grow_sc = (next(it) for _ in range(8))
    C = GDN_CHUNK
    bb = q_ref.shape[0]
    nunits = bb * nchunks
    s_sc[...] = s0_ref[...] if has_s0 else jnp.zeros(s_sc.shape, f32)
    row = lax.broadcasted_iota(jnp.int32, (C, N_DIR * C), 0)
    lane = lax.broadcasted_iota(jnp.int32, (C, N_DIR * C), 1)
    fwd_half = lane < C
    col = jnp.where(fwd_half, lane, lane - C)
    tri = (fwd_half & (row >= col)) | (~fwd_half & (row <= col))
    strict = (fwd_half & (row > col)) | (~fwd_half & (row < col))
    nbeta = N_DIR * nh
    gate_lane = lax.broadcasted_iota(jnp.int32, (C, LANES), 1)

    def unit_of(unit):
        bl = unit // nchunks
        return bl, pl.multiple_of((unit % nchunks) * C, C)

    def phase_a(unit, carry):
        bl, off = unit_of(unit)
        gts = g_ref[bl, pl.ds(off, C), :]
        gmix = jnp.where(gate_lane < nbeta, gts,
                         jnp.where(gate_lane < nbeta + nh, _chunk_cumsum(gts, False), _chunk_cumsum(gts, True)))
        g_rows = gmix.T[:2 * nbeta]
        gcol_sc[unit] = gmix
        grow_sc[unit] = g_rows
        for h in range(nh):
            hs = slice(h * GDN_HEAD, (h + 1) * GDN_HEAD)
            cbf, cbr = h, nh + h
            cgf, cgr = nbeta + h, nbeta + nh + h
            kb = k_ref[bl, pl.ds(off, C), hs]
            k2 = jnp.concatenate([kb, kb], axis=0)
            if emit_o:
                kq = jnp.concatenate([kb, q_ref[bl, pl.ds(off, C), hs]], axis=0)
            else:
                kq = kb
            kkqk = lax.dot_general(kq, k2, (((1,), (1,)), ((), ())), preferred_element_type=f32)
            g_col = jnp.where(fwd_half, gmix[:, cgf:cgf + 1], gmix[:, cgr:cgr + 1])
            g_row = jnp.concatenate([g_rows[cgf:cgf + 1, :], g_rows[cgr:cgr + 1, :]], axis=1)
            b_row = jnp.concatenate([g_rows[cbf:cbf + 1, :], g_rows[cbr:cbr + 1, :]], axis=1)
            decay_b = jnp.exp(jnp.where(tri, g_col - g_row, -1e30)) * b_row
            at_sc[unit * nh + h] = jnp.where(strict, kkqk[:C] * decay_b, 0.0)
            if emit_o:
                mk_sc[unit * nh + h, :C, :] = (kkqk[C:] * decay_b).astype(bf16)
        return carry

    lax.fori_loop(0, nunits, phase_a, 0, unroll=2)

    def swap_in(ib, carry):
        rows = pl.ds(pl.multiple_of(ib * 8, 8), 8)
        tl_sc[rows] = jnp.swapaxes(at_sc[:, rows, :], 0, 1)
        return carry

    def to_lanes(i, carry):
        apl_sc[i] = tl_sc[i].T
        return carry

    lax.fori_loop(0, C // 8, swap_in, 0)
    lax.fori_loop(0, C, to_lanes, 0, unroll=4)
    tl_sc[...] = jnp.zeros(tl_sc.shape, f32)
    _tri_solve_rows(apl_sc, tl_sc, 0, False)
    _tri_solve_rows(apl_sc, tl_sc, 1, True)

    def from_lanes(i, carry):
        apl_sc[i] = tl_sc[i].T
        return carry

    def swap_out(ib, carry):
        rows = pl.ds(pl.multiple_of(ib * 8, 8), 8)
        at_sc[:, rows, :] = jnp.swapaxes(apl_sc[rows], 0, 1)
        return carry

    lax.fori_loop(0, C, from_lanes, 0, unroll=4)
    lax.fori_loop(0, C // 8, swap_out, 0)

    u0_sc = (apl_sc, tl_sc)

    def u0_slot(p):
        return p // 2, pl.ds(pl.multiple_of((p % 2) * C, C), C)

    eye_bf = (lax.broadcasted_iota(jnp.int32, (GDN_HEAD, GDN_HEAD), 0)
              == lax.broadcasted_iota(jnp.int32, (GDN_HEAD, GDN_HEAD), 1)).astype(bf16)

    def phase_b2(unit, carry):
        bl, off = unit_of(unit)
        g_rows = grow_sc[unit]
        g_cols = gcol_sc[unit]
        zero_kv = jnp.zeros((C, GDN_HEAD), bf16)
        for h in range(nh):
            hs = slice(h * GDN_HEAD, (h + 1) * GDN_HEAD)
            p = unit * nh + h
            cbf, cbr = h, nh + h
            cgf, cgr = nbeta + h, nbeta + nh + h
            t_both = at_sc[p]
            kb = k_ref[bl, pl.ds(off, C), hs]
            vb = v_ref[bl, pl.ds(off, C), hs]
            g_row = jnp.concatenate([g_rows[cgf:cgf + 1, :], g_rows[cgr:cgr + 1, :]], axis=1)
            b_row = jnp.concatenate([g_rows[cbf:cbf + 1, :], g_rows[cbr:cbr + 1, :]], axis=1)
            g_last = jnp.concatenate([jnp.broadcast_to(g_cols[C - 1:C, cgf:cgf + 1], (1, C)),
                                      jnp.broadcast_to(g_cols[0:1, cgr:cgr + 1], (1, C))], axis=1)
            v_bd = jnp.concatenate([jnp.concatenate([vb, zero_kv], axis=1),
                                    jnp.concatenate([zero_kv, vb], axis=1)], axis=0)
            k_bd = jnp.concatenate([jnp.concatenate([kb, zero_kv], axis=1),
                                    jnp.concatenate([zero_kv, kb], axis=1)], axis=0)
            u0 = jnp.dot(t_both.astype(bf16), v_bd, preferred_element_type=f32)
            w = jnp.dot((t_both * jnp.exp(g_row)).astype(bf16), k_bd, preferred_element_type=f32)
            slab, rows = u0_slot(p)
            for dirn in range(N_DIR):
                u0_sc[dirn][slab, rows, :] = u0[:, dirn * GDN_HEAD:(dirn + 1) * GDN_HEAD]
                w_sc[dirn, p] = w[:, dirn * GDN_HEAD:(dirn + 1) * GDN_HEAD].astype(bf16)
            k_t = lax.dot_general(eye_bf, jnp.concatenate([kb, kb], axis=0), (((1,), (1,)), ((), ())),
                                  preferred_element_type=f32)
            mk_sc[p, C:, :] = (k_t * (b_row * jnp.exp(g_last - g_row))).astype(bf16)
        return carry

    lax.fori_loop(0, nunits, phase_b2, 0, unroll=4)

    def phase_c(step, carry, accumulate):
        chains = []
        for dirn, unit in ((0, step), (1, nunits - 1 - step)):
            bl, off = unit_of(unit)
            g_col_all = gcol_sc[unit]
            g_tot_all = g_col_all[0:1, :] if dirn else g_col_all[C - 1:C, :]
            for h in range(nh):
                cg = nbeta + dirn * nh + h
                hs = slice(h * GDN_HEAD, (h + 1) * GDN_HEAD)
                p = unit * nh + h
                slab, rows = u0_slot(p)
                lhs = w_sc[dirn, p]
                ch = dict(idx=(bl, dirn, h), off=off, hs=hs, dirn=dirn, u0=u0_sc[dirn][slab, rows, :],
                          g_last=jnp.exp(g_tot_all[:, cg:cg + 1]))
                if emit_o:
                    lhs = jnp.concatenate([lhs, q_ref[bl, pl.ds(off, C), hs]], axis=0)
                    ch['mk'] = mk_sc[p]
                    ch['gam'] = jnp.exp(g_col_all[:, cg:cg + 1])
                else:
                    ch['mk'] = mk_sc[p, C:, :]
                ch['lhs'] = lhs
                chains.append(ch)
        s_old = [s_sc[ch['idx']] for ch in chains]
        r = [jnp.dot(ch['lhs'], s.astype(bf16), preferred_element_type=f32) for ch, s in zip(chains, s_old)]
        zeros = jnp.zeros((C, GDN_HEAD), bf16)
        ub = [(ch['u0'] - ri[:C]).astype(bf16) for ch, ri in zip(chains, r)]
        m = [jnp.dot(ch['mk'], jnp.concatenate([zeros, ubi] if ch['dirn'] else [ubi, zeros], axis=0),
                     preferred_element_type=f32) for ch, ubi in zip(chains, ub)]
        for n, ch in enumerate(chains):
            if emit_o:
                bl = ch['idx'][0]
                o_new = ch['gam'] * r[n][C:] + m[n][:C]
                if accumulate:
                    o_ref[bl, pl.ds(ch['off'], C), ch['hs']] += o_new
                else:
                    o_ref[bl, pl.ds(ch['off'], C), ch['hs']] = o_new
                s_sc[ch['idx']] = ch['g_last'] * s_old[n] + m[n][C:]
            else:
                s_sc[ch['idx']] = ch['g_last'] * s_old[n] + m[n]
        return carry

    if emit_o and bb == 1:
        lax.fori_loop(0, nunits // 2, functools.partial(phase_c, accumulate=False), 0)
        lax.fori_loop(nunits // 2, nunits, functools.partial(phase_c, accumulate=True), 0)
    elif emit_o:
        o_ref[...] = jnp.zeros(o_ref.shape, f32)
        lax.fori_loop(0, nunits, functools.partial(phase_c, accumulate=True), 0)
    else:
        lax.fori_loop(0, nunits, functools.partial(phase_c, accumulate=False), 0)
    sfin_ref[...] = s_sc[...]


def _gdn_scan(q, k, v, gates, s0, *, emit_o):
    nb, L, d_gdn = q.shape
    nh = d_gdn // GDN_HEAD
    nchunks = L // GDN_CHUNK
    C = GDN_CHUNK
    bb = max(1, min(nb, LANES // (nchunks * nh)))
    nunits = bb * nchunks
    nprob = nunits * nh
    assert nprob == LANES, (nb, L, nh)
    has_s0 = s0 is not None
    kern = functools.partial(_gdn_scan_kernel, nchunks=nchunks, nh=nh, emit_o=emit_o, has_s0=has_s0)
    seq = lambda b: (b, 0, 0)
    st = lambda b: (b, 0, 0, 0, 0)
    st_spec = pl.BlockSpec((bb, N_DIR, nh, GDN_HEAD, GDN_HEAD), st)
    st_shape = jax.ShapeDtypeStruct((nb, N_DIR, nh, GDN_HEAD, GDN_HEAD), f32)
    args = [q, k, v, gates]
    in_specs = [pl.BlockSpec((bb, L, d_gdn), seq)] * 3 + [pl.BlockSpec((bb, L, LANES), seq)]
    if has_s0:
        args.append(s0)
        in_specs.append(st_spec)
    out_specs, out_shape = [], []
    if emit_o:
        out_specs.append(pl.BlockSpec((bb, L, d_gdn), seq))
        out_shape.append(jax.ShapeDtypeStruct((nb, L, d_gdn), f32))
    out_specs.append(st_spec)
    out_shape.append(st_shape)
    return pl.pallas_call(
        kern,
        grid=(nb // bb,),
        in_specs=in_specs,
        out_specs=out_specs,
        out_shape=out_shape,
        scratch_shapes=[pltpu.VMEM((bb, N_DIR, nh, GDN_HEAD, GDN_HEAD), f32),
                        pltpu.VMEM((nprob, C, N_DIR * C), f32),
                        pltpu.VMEM((C, N_DIR * C, nprob), f32),
                        pltpu.VMEM((C, N_DIR * C, nprob), f32),
                        pltpu.VMEM((nprob, C + GDN_HEAD, N_DIR * C), bf16),
                        pltpu.VMEM((N_DIR, nprob, C, GDN_HEAD), bf16),
                        pltpu.VMEM((nunits, C, LANES), f32),
                        pltpu.VMEM((nunits, 4 * nh, C), f32)],
        compiler_params=_cparams(("parallel",)),
        name="gdn_scan" if emit_o else "gdn_scan_state",
    )(*args)


def _post_kernel(x_ref, gate_ref, y5_ref, zs_ref, og_ref, zg_ref,
                 wglu_ref, bglu_ref, nw_ref, wout_ref, lng_ref, lnb_ref, o_ref, *, tt, alpha):
    nb, _, d = x_ref.shape
    rows = nb * tt
    npk = y5_ref.shape[0]
    d_s5 = npk * LANES
    d_gdn = og_ref.shape[2]
    parts = []
    for j in range(npk):
        parts.append(jnp.swapaxes(y5_ref[j].reshape(tt, nb, LANES), 0, 1).reshape(rows, LANES))
    y = jnp.concatenate(parts, axis=1)
    g = 0.5 * y * (1.0 + lax.erf(y * (2.0 ** -0.5)))
    glu = jnp.dot(g.astype(bf16), wglu_ref[...], preferred_element_type=f32) + bglu_ref[...]
    s5 = g * _sigmoid(glu) * _silu(zs_ref[...].astype(f32).reshape(rows, d_s5))
    o = og_ref[...].reshape(rows, d_gdn)
    zg = zg_ref[...].astype(f32).reshape(rows, d_gdn)
    gd = []
    for h in range(d_gdn // GDN_HEAD):
        oh = o[:, h * GDN_HEAD:(h + 1) * GDN_HEAD]
        gd.append(oh * lax.rsqrt(jnp.mean(oh * oh, axis=-1, keepdims=True) + NORM_EPS) * nw_ref[...])
    gdn = jnp.concatenate(gd, axis=1) * _silu(zg)
    mix = jnp.concatenate([s5, gdn], axis=1).astype(bf16)
    yv = jnp.dot(mix, wout_ref[...], preferred_element_type=f32).reshape(nb, tt, d)
    r = alpha * x_ref[...] + gate_ref[...] * yv
    mu = jnp.mean(r, axis=-1, keepdims=True)
    var = jnp.mean(jnp.square(r - mu), axis=-1, keepdims=True)
    o_ref[...] = (r - mu) * lax.rsqrt(var + LN_EPS) * lng_ref[...] + lnb_ref[...]


def _post(x, gate, y5, zs, o_gdn, zg, wglu, bglu, nw, wout, lng, lnb, alpha):
    nb, L, d = x.shape
    tt = 32
    npk = y5.shape[0]
    d_s5 = npk * LANES
    d_gdn = o_gdn.shape[2]
    kern = functools.partial(_post_kernel, tt=tt, alpha=alpha)
    tok = lambda i: (0, i, 0)
    c2 = lambda i: (0, 0)
    c3 = lambda i: (0, 0, 0)
    y_spec = pl.BlockSpec((npk, tt // S5_T, S5_T, nb, LANES), lambda i: (0, i, 0, 0, 0))
    return pl.pallas_call(
        kern,
        grid=(L // tt,),
        in_specs=[pl.BlockSpec((nb, tt, d), tok),
                  pl.BlockSpec((nb, 1, d), c3),
                  y_spec,
                  pl.BlockSpec((nb, tt, d_s5), tok),
                  pl.BlockSpec((nb, tt, d_gdn), tok),
                  pl.BlockSpec((nb, tt, d_gdn), tok),
                  pl.BlockSpec(wglu.shape, c2),
                  pl.BlockSpec(bglu.shape, c2),
                  pl.BlockSpec(nw.shape, c2),
                  pl.BlockSpec(wout.shape, c2),
                  pl.BlockSpec(lng.shape, c3),
                  pl.BlockSpec(lnb.shape, c3)],
        out_specs=pl.BlockSpec((nb, tt, d), tok),
        out_shape=jax.ShapeDtypeStruct((nb, L, d), f32),
        compiler_params=_cparams(("parallel",)),
        name="post",
    )(x, gate, y5, zs, o_gdn, zg, wglu, bglu, nw, wout, lng, lnb)


def _layer(x, c, ctx, c_ctx, p, alpha):
    nb, L, d = x.shape
    d_s5 = p['s5_d'].shape[0]
    d_gdn = p['conv_w'].shape[-1] // 3
    nh = d_gdn // GDN_HEAD
    npk = d_s5 // LANES

    pad = (-(nb + 1)) % 8
    cond = jnp.concatenate([c, c_ctx[None, :], jnp.zeros((pad, d), f32)], axis=0)
    m = _ada(cond, p['w_ada'], p['b_ada'])
    shift, scale, gate = m[:nb, :d], m[:nb, d:2 * d], m[:nb, 2 * d:]
    shift_c = jnp.broadcast_to(m[nb:nb + 1, :d], (nb, d))
    scale_c = jnp.broadcast_to(m[nb:nb + 1, d:2 * d], (nb, d))

    ng = 2 * N_DIR * nh
    w_all = jnp.pad(p['w_in'], ((0, 0), (0, LANES - ng))).astype(bf16)
    gp = jnp.pad(jnp.stack([p['gdn_a_log'].reshape(-1), p['gdn_dt_bias'].reshape(-1)]).astype(f32),
                 ((0, 0), (ng // 2, LANES - ng)))

    u5, zs, qkv, zg, gates = _inproj(x, scale[:, None, :], shift[:, None, :], w_all, gp, d_s5, d_gdn)
    u5c, _, qkvc, _, gatesc = _inproj(ctx, scale_c[:, None, :], shift_c[:, None, :], w_all, gp, d_s5, d_gdn)

    sw = _s5_weights(p['s5_lambda_re'], p['s5_lambda_im'], p['s5_log_dt'],
                     p['s5_b_re'], p['s5_b_im'], p['s5_c_re'], p['s5_c_im'])
    sdim = sw['a_f'].shape[-1]
    a_f = jnp.broadcast_to(sw['a_f'][:, None, :], (npk, nb, sdim))
    a_r = jnp.broadcast_to(sw['a_r'][:, None, :], (npk, nb, sdim))
    h_zero = jnp.zeros((npk, nb, sdim), f32)
    d_skip = jnp.tile(p['s5_d'].astype(f32).reshape(npk, 1, LANES), (1, 1, S5_T))
    (hc_f,) = _s5_pass(u5c, sw['m_in_f'], a_f, h_zero, rev=False, emit_y=False)
    (hc_r,) = _s5_pass(u5c, sw['m_in_r'], a_r, h_zero, rev=True, emit_y=False)
    y_f, _ = _s5_pass(u5, sw['m_in_f'], a_f, hc_f, sw['m_out_f'], sw['m_intra'], d_skip, rev=False, emit_y=True)
    y_s5, _ = _s5_pass(u5, sw['m_in_r'], a_r, hc_r, sw['m_out_r'], y_add=y_f, rev=True, emit_y=True)

    ql, kl, vl = _gdn_prep(qkv, p['conv_w'], True)
    qc, kc, vc = _gdn_prep(qkvc, p['conv_w'], False)
    (s_ctx,) = _gdn_scan(qc, kc, vc, gatesc, None, emit_o=False)
    o_gdn, _ = _gdn_scan(ql, kl, vl, gates, s_ctx, emit_o=True)

    return _post(x, gate[:, None, :], y_s5, zs, o_gdn, zg,
                 p['w_glu'].astype(bf16), p['b_glu'].astype(f32).reshape(1, d_s5),
                 p['gdn_norm_w'].astype(f32).reshape(1, GDN_HEAD),
                 p['w_out'].astype(bf16), p['ln_g'].astype(f32).reshape(1, 1, d),
                 p['ln_b'].astype(f32).reshape(1, 1, d), alpha)


def kernel(x, c, ctx, c_ctx, w_ada, b_ada, w_in, s5_lambda_re, s5_lambda_im, s5_log_dt, s5_b_re, s5_b_im, s5_c_re, s5_c_im, s5_d, w_glu, b_glu, conv_w, gdn_a_log, gdn_dt_bias, gdn_norm_w, w_out, ln_g, ln_b):
    depth = w_ada.shape[0]
    assert depth == 1, "context-token outputs are only skipped for the last layer"
    alpha = (2.0 * depth) ** 0.25
    p = {
        'w_ada': w_ada[0], 'b_ada': b_ada[0], 'w_in': w_in[0],
        's5_lambda_re': s5_lambda_re[0], 's5_lambda_im': s5_lambda_im[0], 's5_log_dt': s5_log_dt[0],
        's5_b_re': s5_b_re[0], 's5_b_im': s5_b_im[0], 's5_c_re': s5_c_re[0], 's5_c_im': s5_c_im[0],
        's5_d': s5_d[0], 'w_glu': w_glu[0], 'b_glu': b_glu[0], 'conv_w': conv_w[0],
        'gdn_a_log': gdn_a_log[0], 'gdn_dt_bias': gdn_dt_bias[0], 'gdn_norm_w': gdn_norm_w[0],
        'w_out': w_out[0], 'ln_g': ln_g[0], 'ln_b': ln_b[0],
    }
    return _layer(x, c, ctx, c_ctx, p, alpha)
```

```python
import functools

import numpy as np
import jax
import jax.numpy as jnp
from jax import lax
from jax.experimental import pallas as pl
from jax.experimental.pallas import tpu as pltpu

f32 = jnp.float32
bf16 = jnp.bfloat16
HI = lax.Precision.HIGHEST

LANES = 128
S5_GROUP = 16
S5_T = 8
PACK_G = LANES // S5_GROUP
GDN_HEAD = 128
GDN_CHUNK = 64
GRID_W = 64
N_DIR = 2
LN_EPS = 1e-5
NORM_EPS = 1e-6
V7X_VMEM_BYTES = 64 * 1024 * 1024
VMEM_LIMIT = V7X_VMEM_BYTES - 8 * 1024 * 1024


def _cparams(sem):
    return pltpu.CompilerParams(dimension_semantics=sem, vmem_limit_bytes=VMEM_LIMIT)


def _sigmoid(x):
    return 0.5 + 0.5 * jnp.tanh(0.5 * x)


def _silu(x):
    h = 0.5 * x
    return h + h * jnp.tanh(h)


def _ada_kernel(c_ref, w_ref, b_ref, o_ref):
    o_ref[...] = jnp.dot(_silu(c_ref[...]), w_ref[...], precision=HI,
                         preferred_element_type=f32) + b_ref[...]


def _ada(cond, w_ada, b_ada):
    rows, d = cond.shape
    n = w_ada.shape[1]
    tn = 512
    return pl.pallas_call(
        _ada_kernel,
        grid=(n // tn,),
        in_specs=[pl.BlockSpec((rows, d), lambda i: (0, 0)),
                  pl.BlockSpec((d, tn), lambda i: (0, i)),
                  pl.BlockSpec((1, tn), lambda i: (0, i))],
        out_specs=pl.BlockSpec((rows, tn), lambda i: (0, i)),
        out_shape=jax.ShapeDtypeStruct((rows, n), f32),
        compiler_params=_cparams(("parallel",)),
        name="ada",
    )(cond, w_ada, b_ada.reshape(1, n))


def _inproj_kernel(x_ref, scale_ref, shift_ref, w_ref, gp_ref,
                   u_ref, zs_ref, qkv_ref, zg_ref, gate_ref, *, tt, d_s5, d_gdn, n_beta):
    nb, _, d = x_ref.shape
    n_main = d_s5 + 4 * d_gdn
    wu_ref = w_ref.at[:, :d_s5]
    wm_ref = w_ref.at[:, d_s5:d_s5 + n_main]
    wg_ref = w_ref.at[:, d_s5 + n_main:]
    sc = 1.0 + scale_ref[...]
    sh = shift_ref[...]
    h = (x_ref[...] * sc + sh).reshape(nb * tt, d).astype(bf16)
    r = jnp.dot(h, wm_ref[...], preferred_element_type=f32)
    zs_ref[...] = r[:, :d_s5].reshape(nb, tt, d_s5).astype(bf16)
    qkv_ref[...] = r[:, d_s5:d_s5 + 3 * d_gdn].reshape(nb, tt, 3 * d_gdn)
    zg_ref[...] = r[:, d_s5 + 3 * d_gdn:].reshape(nb, tt, d_gdn).astype(bf16)
    lg = jnp.dot(h, wg_ref[...], preferred_element_type=f32)
    lane = lax.broadcasted_iota(jnp.int32, lg.shape, 1)
    a = lg + gp_ref[1:2, :]
    softplus = jnp.maximum(a, 0.0) + jnp.log1p(jnp.exp(-jnp.abs(a)))
    gates = jnp.where(lane < n_beta, _sigmoid(lg), -jnp.exp(gp_ref[0:1, :]) * softplus)
    gate_ref[...] = gates.reshape(nb, tt, LANES)
    sc2 = sc[:, 0, :]
    sh2 = sh[:, 0, :]
    ht = (jnp.swapaxes(x_ref[...], 0, 1) * sc2[None] + sh2[None]).reshape(tt * nb, d).astype(bf16)
    ru = jnp.dot(ht, wu_ref[...], preferred_element_type=f32)
    for j in range(d_s5 // LANES):
        u_ref[j] = ru[:, j * LANES:(j + 1) * LANES].reshape(tt // S5_T, S5_T, nb, LANES)


def _inproj(x, scale, shift, w_all, gp, d_s5, d_gdn):
    nb, L, d = x.shape
    tt = 32
    npk = d_s5 // LANES
    kern = functools.partial(_inproj_kernel, tt=tt, d_s5=d_s5, d_gdn=d_gdn,
                             n_beta=N_DIR * (d_gdn // GDN_HEAD))
    const2 = lambda i: (0, 0)
    const3 = lambda i: (0, 0, 0)
    return pl.pallas_call(
        kern,
        grid=(L // tt,),
        in_specs=[pl.BlockSpec((nb, tt, d), lambda i: (0, i, 0)),
                  pl.BlockSpec((nb, 1, d), const3),
                  pl.BlockSpec((nb, 1, d), const3),
                  pl.BlockSpec(w_all.shape, const2),
                  pl.BlockSpec(gp.shape, const2)],
        out_specs=[pl.BlockSpec((npk, tt // S5_T, S5_T, nb, LANES), lambda i: (0, i, 0, 0, 0)),
                   pl.BlockSpec((nb, tt, d_s5), lambda i: (0, i, 0)),
                   pl.BlockSpec((nb, tt, 3 * d_gdn), lambda i: (0, i, 0)),
                   pl.BlockSpec((nb, tt, d_gdn), lambda i: (0, i, 0)),
                   pl.BlockSpec((nb, tt, LANES), lambda i: (0, i, 0))],
        out_shape=[jax.ShapeDtypeStruct((npk, L // S5_T, S5_T, nb, LANES), f32),
                   jax.ShapeDtypeStruct((nb, L, d_s5), bf16),
                   jax.ShapeDtypeStruct((nb, L, 3 * d_gdn), f32),
                   jax.ShapeDtypeStruct((nb, L, d_gdn), bf16),
                   jax.ShapeDtypeStruct((nb, L, LANES), f32)],
        compiler_params=_cparams(("parallel",)),
        name="inproj",
    )(x, scale, shift, w_all, gp)


def _s5_kernel(*refs, ct, rev, emit_y, intra, add_y):
    it = iter(refs)
    u_ref, cin_ref, at_ref, h0_ref = next(it), next(it), next(it), next(it)
    cout_ref = next(it) if emit_y else None
    cintra_ref = next(it) if intra else None
    d_ref = next(it) if intra else None
    yadd_ref = next(it) if add_y else None
    y_ref = next(it) if emit_y else None
    hfin_ref = next(it)
    h_sc, v_sc, hall_sc, min_sc = next(it), next(it), next(it), next(it)
    mout_sc = next(it) if emit_y else None
    mintra_sc = next(it) if intra else None

    t = pl.program_id(1)
    nt = pl.num_programs(1)
    nb = u_ref.shape[3]
    sdim = h_sc.shape[1]
    half = sdim // 2
    gstate = half // PACK_G

    @pl.when(t == 0)
    def _():
        h_sc[...] = h0_ref[0]
        r_in = lax.broadcasted_iota(jnp.int32, (LANES, sdim), 0) // S5_GROUP
        c_in = (lax.broadcasted_iota(jnp.int32, (LANES, sdim), 1) % half) // gstate
        mask_in = (r_in == c_in).astype(f32)
        for s in range(S5_T):
            min_sc[s * LANES:(s + 1) * LANES, :] = (
                jnp.tile(jnp.concatenate([cin_ref[s, 0, 0], cin_ref[s, 1, 0]], axis=1), (PACK_G, 1))
                * mask_in).astype(bf16)
        if emit_y:
            r_out = lax.broadcasted_iota(jnp.int32, (half, LANES), 0) // gstate
            c_out = lax.broadcasted_iota(jnp.int32, (half, LANES), 1) // S5_GROUP
            mask_out = (r_out == c_out).astype(f32)
            for i in range(S5_T):
                for ri in range(2):
                    blk = cout_ref[i, ri, 0]
                    mout_sc[ri * half:(ri + 1) * half, i * LANES:(i + 1) * LANES] = (
                        jnp.tile(blk, (PACK_G, 1)) * mask_out).astype(bf16)
        if intra:
            r_x = lax.broadcasted_iota(jnp.int32, (LANES, LANES), 0) // S5_GROUP
            c_x = lax.broadcasted_iota(jnp.int32, (LANES, LANES), 1) // S5_GROUP
            mask_x = (r_x == c_x).astype(f32)
            blocks = {}
            for lag in range(1 - S5_T, S5_T):
                k = cintra_ref[0, lag, 0] if lag > 0 else cintra_ref[1, -lag, 0]
                if lag == 0:
                    k = k + cintra_ref[0, 0, 0]
                blocks[lag] = (jnp.tile(k, (PACK_G, 1)) * mask_x).astype(bf16)
            for s in range(S5_T):
                for i in range(S5_T):
                    mintra_sc[s * LANES:(s + 1) * LANES, i * LANES:(i + 1) * LANES] = blocks[i - s]

    x = jnp.concatenate([u_ref[0, :, i, :, :].reshape(ct * nb, LANES) for i in range(S5_T)], axis=1)
    xb = x.astype(bf16)
    v_sc[...] = jnp.dot(xb, min_sc[...], preferred_element_type=f32)
    are = at_ref[0, :, :half]
    aim = at_ref[0, :, half:]

    def body(s, carry):
        k = (ct - 1 - s) if rev else s
        off = pl.multiple_of(k * nb, nb)
        hre, him = carry
        hall_sc[pl.ds(off, nb), :half] = hre
        hall_sc[pl.ds(off, nb), half:] = him
        vre = v_sc[pl.ds(off, nb), :half]
        vim = v_sc[pl.ds(off, nb), half:]
        return are * hre - aim * him + vre, are * him + aim * hre + vim

    hre, him = lax.fori_loop(0, ct, body, (h_sc[:, :half], h_sc[:, half:]))
    h_sc[:, :half] = hre
    h_sc[:, half:] = him

    @pl.when(t == nt - 1)
    def _():
        hfin_ref[0] = h_sc[...]

    if emit_y:
        y = jnp.dot(hall_sc[...].astype(bf16), mout_sc[...], preferred_element_type=f32)
        if intra:
            y = y + jnp.dot(xb, mintra_sc[...], preferred_element_type=f32) + x * d_ref[0]
        for i in range(S5_T):
            yi = y[:, i * LANES:(i + 1) * LANES].reshape(ct, nb, LANES)
            y_ref[0, :, i, :, :] = yi + yadd_ref[0, :, i, :, :] if add_y else yi


def _s5_pass(u5, c_in, a_t, h0, c_out=None, c_intra=None, d_skip=None, y_add=None, *, rev, emit_y):
    npk, nc, _, nb, _ = u5.shape
    ct = min(32, nc)
    nt = nc // ct
    intra = c_intra is not None
    add_y = y_add is not None
    kdim = S5_T * LANES
    sdim = a_t.shape[-1]
    tmap = (lambda j, t: (j, nt - 1 - t, 0, 0, 0)) if rev else (lambda j, t: (j, t, 0, 0, 0))
    pmap = lambda j, t: (j, 0, 0)
    wmap = lambda j, t: (0, 0, j, 0, 0)
    args = [u5, c_in, a_t, h0]
    in_specs = [pl.BlockSpec((1, ct, S5_T, nb, LANES), tmap),
                pl.BlockSpec(c_in.shape[:2] + (1,) + c_in.shape[3:], wmap),
                pl.BlockSpec((1, nb, sdim), pmap),
                pl.BlockSpec((1, nb, sdim), pmap)]
    scratch = [pltpu.VMEM((nb, sdim), f32),
               pltpu.VMEM((ct * nb, sdim), f32),
               pltpu.VMEM((ct * nb, sdim), f32),
               pltpu.VMEM((kdim, sdim), bf16)]
    if emit_y:
        args.append(c_out)
        in_specs.append(pl.BlockSpec(c_out.shape[:2] + (1,) + c_out.shape[3:], wmap))
        scratch.append(pltpu.VMEM((sdim, kdim), bf16))
    if intra:
        args += [c_intra, d_skip]
        in_specs += [pl.BlockSpec(c_intra.shape[:2] + (1,) + c_intra.shape[3:], wmap),
                     pl.BlockSpec((1, 1, kdim), pmap)]
        scratch.append(pltpu.VMEM((kdim, kdim), bf16))
    if add_y:
        args.append(y_add)
        in_specs.append(pl.BlockSpec((1, ct, S5_T, nb, LANES), tmap))
    out_specs, out_shape = [], []
    if emit_y:
        out_specs.append(pl.BlockSpec((1, ct, S5_T, nb, LANES), tmap))
        out_shape.append(jax.ShapeDtypeStruct(u5.shape, f32))
    out_specs.append(pl.BlockSpec((1, nb, sdim), pmap))
    out_shape.append(jax.ShapeDtypeStruct((npk, nb, sdim), f32))
    kern = functools.partial(_s5_kernel, ct=ct, rev=rev, emit_y=emit_y, intra=intra, add_y=add_y)
    return pl.pallas_call(
        kern,
        grid=(npk, nt),
        in_specs=in_specs,
        out_specs=out_specs,
        out_shape=out_shape,
        scratch_shapes=scratch,
        compiler_params=_cparams(("parallel", "arbitrary")),
        name="s5_rev" if rev else "s5_fwd",
    )(*args)


def _s5_weights(lam_re, lam_im, log_dt, b_re, b_im, c_re, c_im):
    T = S5_T
    ndir, G, P = lam_re.shape
    npk = G // PACK_G
    dt = jnp.exp(log_dt.astype(f32))[..., None]
    lr, li = lam_re.astype(f32), lam_im.astype(f32)
    zr, zi = lr * dt, li * dt

    nr = jnp.expm1(zr) * jnp.cos(zi) - 2.0 * jnp.square(jnp.sin(0.5 * zi))
    ni = jnp.exp(zr) * jnp.sin(zi)
    den = lr * lr + li * li
    qr = (nr * lr + ni * li) / den
    qi = (ni * lr - nr * li) / den
    bre, bim = b_re.astype(f32), b_im.astype(f32)
    bbr = qr[..., None] * bre - qi[..., None] * bim
    bbi = qr[..., None] * bim + qi[..., None] * bre
    cr, ci = c_re.astype(f32), c_im.astype(f32)

    def powers(e, d):
        ev = jnp.asarray(np.asarray(e, np.float32))[:, None, None]
        m = jnp.exp(ev * zr[d])
        return m * jnp.cos(ev * zi[d]), m * jnp.sin(ev * zi[d])

    def c_times_a(e, d):
        pr, pi = powers(e, d)
        pr, pi = pr[:, :, None, :], pi[:, :, None, :]
        return cr[d] * pr - ci[d] * pi, cr[d] * pi + ci[d] * pr

    steps = np.arange(T)

    def pack_in(e, d):
        pr, pi = powers(e, d)
        pr = pr.reshape(T, npk, 1, PACK_G * P)
        pi = pi.reshape(T, npk, 1, PACK_G * P)
        tr = lambda b: b.reshape(npk, PACK_G, P, S5_GROUP).transpose(0, 3, 1, 2).reshape(npk, S5_GROUP, PACK_G * P)
        btr, bti = tr(bbr[d]), tr(bbi[d])
        return jnp.stack([pr * btr - pi * bti, pr * bti + pi * btr], axis=1)

    def pack_out(e, d):
        pr, pi = powers(e, d)
        rep = lambda v: jnp.repeat(v.reshape(T, npk, PACK_G, P).transpose(0, 1, 3, 2), S5_GROUP, axis=-1)
        pr, pi = rep(pr), rep(pi)
        tr = lambda c: c.reshape(npk, PACK_G, S5_GROUP, P).transpose(0, 3, 1, 2).reshape(npk, P, LANES)
        ctr, cti = tr(cr[d]), tr(ci[d])
        return jnp.stack([ctr * pr - cti * pi, -(ctr * pi + cti * pr)], axis=1)

    m_in_f = pack_in(T - 1 - steps, 0)
    m_in_r = pack_in(steps, 1)
    m_out_f = pack_out(steps + 1, 0)
    m_out_r = pack_out(T - steps, 1)

    def lag_kernel(d):
        car, cai = c_times_a(steps, d)
        k = jnp.einsum('tgop,gpi->tgio', car, bbr[d]) - jnp.einsum('tgop,gpi->tgio', cai, bbi[d])
        k = k.reshape(T, npk, PACK_G, S5_GROUP, S5_GROUP).transpose(0, 1, 3, 2, 4)
        return k.reshape(T, npk, S5_GROUP, LANES)

    m_intra = jnp.stack([lag_kernel(0), lag_kernel(1)])

    def a_chunk(d):
        pr, pi = powers([T], d)
        return jnp.concatenate([pr.reshape(npk, PACK_G * P), pi.reshape(npk, PACK_G * P)], axis=-1)

    return dict(m_in_f=m_in_f, m_in_r=m_in_r, m_out_f=m_out_f, m_out_r=m_out_r,
                m_intra=m_intra, a_f=a_chunk(0), a_r=a_chunk(1))


def _gdn_prep_kernel(*refs, tile, rows_conv, d_gdn):
    if rows_conv:
        main_ref, up_ref, dn_ref, w_ref, q_ref, k_ref, v_ref = refs
    else:
        main_ref, w_ref, q_ref, k_ref, v_ref = refs
    w = w_ref[...]
    if rows_conv:
        i = pl.program_id(1)
        n = pl.num_programs(1)
        up = up_ref[0] * (i > 0).astype(f32)
        dn = dn_ref[0] * (i < n - 1).astype(f32)
        ext = jnp.concatenate([up, main_ref[0], dn], axis=0)
        base = GRID_W
        drs = (-1, 0, 1)
        col = lax.broadcasted_iota(jnp.int32, (tile, 1), 0) % GRID_W
        mask_m = col != 0
        mask_p = col != GRID_W - 1
    else:
        ext = main_ref[0]
        base = 0
        drs = (0,)
        col = lax.broadcasted_iota(jnp.int32, (tile, 1), 0)
        mask_m = col != 0
        mask_p = col != tile - 1
    acc = {}
    for dc in (-1, 0, 1):
        a = None
        for dr in drs:
            tap = w[(dr + 1) * 3 + (dc + 1)][None, :]
            term = ext[base + GRID_W * dr: base + GRID_W * dr + tile] * tap
            a = term if a is None else a + term
        acc[dc] = a
    conv = (acc[0] + jnp.where(mask_m, pltpu.roll(acc[-1], 1, 0), 0.0)
            + jnp.where(mask_p, pltpu.roll(acc[1], tile - 1, 0), 0.0))
    s = _silu(conv)
    nh = d_gdn // GDN_HEAD
    for h in range(nh):
        qh = s[:, h * GDN_HEAD:(h + 1) * GDN_HEAD]
        kh = s[:, d_gdn + h * GDN_HEAD: d_gdn + (h + 1) * GDN_HEAD]
        qn = qh * (lax.rsqrt(jnp.sum(qh * qh, axis=-1, keepdims=True) + NORM_EPS) * (GDN_HEAD ** -0.5))
        kn = kh * lax.rsqrt(jnp.sum(kh * kh, axis=-1, keepdims=True) + NORM_EPS)
        q_ref[0, :, h * GDN_HEAD:(h + 1) * GDN_HEAD] = qn.astype(bf16)
        k_ref[0, :, h * GDN_HEAD:(h + 1) * GDN_HEAD] = kn.astype(bf16)
    v_ref[0] = s[:, 2 * d_gdn:].astype(bf16)


def _gdn_prep(qkv, conv_w, rows_conv):
    nb, L, c3 = qkv.shape
    d_gdn = c3 // 3
    w9 = conv_w.reshape(9, c3).astype(f32)
    if rows_conv:
        tile = min(512, L)
        nt = L // tile
        r = tile // GRID_W
        nrows = L // GRID_W
        in_specs = [pl.BlockSpec((1, tile, c3), lambda b, i: (b, i, 0)),
                    pl.BlockSpec((1, GRID_W, c3), lambda b, i: (b, jnp.maximum(i * r - 1, 0), 0)),
                    pl.BlockSpec((1, GRID_W, c3), lambda b, i: (b, jnp.minimum((i + 1) * r, nrows - 1), 0)),
                    pl.BlockSpec((9, c3), lambda b, i: (0, 0))]
        args = (qkv, qkv, qkv, w9)
    else:
        tile, nt = L, 1
        in_specs = [pl.BlockSpec((1, tile, c3), lambda b, i: (b, i, 0)),
                    pl.BlockSpec((9, c3), lambda b, i: (0, 0))]
        args = (qkv, w9)
    kern = functools.partial(_gdn_prep_kernel, tile=tile, rows_conv=rows_conv, d_gdn=d_gdn)
    o_spec = pl.BlockSpec((1, tile, d_gdn), lambda b, i: (b, i, 0))
    o_shape = jax.ShapeDtypeStruct((nb, L, d_gdn), bf16)
    return pl.pallas_call(
        kern,
        grid=(nb, nt),
        in_specs=in_specs,
        out_specs=[o_spec, o_spec, o_spec],
        out_shape=[o_shape, o_shape, o_shape],
        compiler_params=_cparams(("parallel", "parallel")),
        name="gdn_prep",
    )(*args)


def _chunk_cumsum(g, rev):
    n = g.shape[0]
    r = lax.broadcasted_iota(jnp.int32, g.shape, 0)
    s = 1
    while s < n:
        if rev:
            g = g + jnp.where(r < n - s, pltpu.roll(g, n - s, 0), 0.0)
        else:
            g = g + jnp.where(r >= s, pltpu.roll(g, s, 0), 0.0)
        s *= 2
    return g


def _tri_solve_rows(apl_sc, tl_sc, dirn, upper):
    nblk = GDN_CHUNK // 8
    base = dirn * GDN_CHUNK
    sub = lax.broadcasted_iota(jnp.int32, (8, LANES), 0)
    for step_blk in range(nblk):
        blk = (nblk - 1 - step_blk) if upper else step_blk
        mblocks = range(blk, nblk) if upper else range(0, blk + 1)

        def solve_row(s, carry, blk=blk, mblocks=mblocks):
            ii = (7 - s) if upper else s
            i = blk * 8 + ii
            acc = {}
            for mb in mblocks:
                arow = apl_sc[i, base + mb * 8:base + (mb + 1) * 8, :]
                jblocks = range(mb, nblk) if upper else range(0, mb + 1)
                for mm in range(8):
                    ab = jnp.broadcast_to(arow[mm:mm + 1, :], (8, LANES))
                    for jb in jblocks:
                        term = ab * tl_sc[mb * 8 + mm, base + jb * 8:base + (jb + 1) * 8, :]
                        key = (jb, mm % 2)
                        acc[key] = term if key not in acc else acc[key] + term
            for jb in (range(blk, nblk) if upper else range(0, blk + 1)):
                tot = acc[(jb, 0)] + acc[(jb, 1)]
                unit_row = jnp.where(sub == ii, 1.0, 0.0) if jb == blk else 0.0
                tl_sc[i, base + jb * 8:base + (jb + 1) * 8, :] = unit_row - tot
            return carry

        lax.fori_loop(0, 8, solve_row, 0)


def _gdn_scan_kernel(*refs, nchunks, nh, emit_o, has_s0):
    it = iter(refs)
    q_ref, k_ref, v_ref, g_ref = next(it), next(it), next(it), next(it)
    s0_ref = next(it) if has_s0 else None
    o_ref = next(it) if emit_o else None
    sfin_ref = next(it)
    s_sc, at_sc, apl_sc, tl_sc, mk_sc, w_sc, gcol_sc, grow_sc = (next(it) for _ in range(8))
    C = GDN_CHUNK
    bb = q_ref.shape[0]
    nunits = bb * nchunks
    s_sc[...] = s0_ref[...] if has_s0 else jnp.zeros(s_sc.shape, f32)
    row = lax.broadcasted_iota(jnp.int32, (C, N_DIR * C), 0)
    lane = lax.broadcasted_iota(jnp.int32, (C, N_DIR * C), 1)
    fwd_half = lane < C
    col = jnp.where(fwd_half, lane, lane - C)
    tri = (fwd_half & (row >= col)) | (~fwd_half & (row <= col))
    strict = (fwd_half & (row > col)) | (~fwd_half & (row < col))
    nbeta = N_DIR * nh
    gate_lane = lax.broadcasted_iota(jnp.int32, (C, LANES), 1)

    def unit_of(unit):
        bl = unit // nchunks
        return bl, pl.multiple_of((unit % nchunks) * C, C)

    def phase_a(unit, carry):
        bl, off = unit_of(unit)
        gts = g_ref[bl, pl.ds(off, C), :]
        gmix = jnp.where(gate_lane < nbeta, gts,
                         jnp.where(gate_lane < nbeta + nh, _chunk_cumsum(gts, False), _chunk_cumsum(gts, True)))
        g_rows = gmix.T[:2 * nbeta]
        gcol_sc[unit] = gmix
        grow_sc[unit] = g_rows
        for h in range(nh):
            hs = slice(h * GDN_HEAD, (h + 1) * GDN_HEAD)
            cbf, cbr = h, nh + h
            cgf, cgr = nbeta + h, nbeta + nh + h
            kb = k_ref[bl, pl.ds(off, C), hs]
            k2 = jnp.concatenate([kb, kb], axis=0)
            if emit_o:
                kq = jnp.concatenate([kb, q_ref[bl, pl.ds(off, C), hs]], axis=0)
            else:
                kq = kb
            kkqk = lax.dot_general(kq, k2, (((1,), (1,)), ((), ())), preferred_element_type=f32)
            g_col = jnp.where(fwd_half, gmix[:, cgf:cgf + 1], gmix[:, cgr:cgr + 1])
            g_row = jnp.concatenate([g_rows[cgf:cgf + 1, :], g_rows[cgr:cgr + 1, :]], axis=1)
            b_row = jnp.concatenate([g_rows[cbf:cbf + 1, :], g_rows[cbr:cbr + 1, :]], axis=1)
            decay_b = jnp.exp(jnp.where(tri, g_col - g_row, -1e30)) * b_row
            at_sc[unit * nh + h] = jnp.where(strict, kkqk[:C] * decay_b, 0.0)
            if emit_o:
                mk_sc[unit * nh + h, :C, :] = (kkqk[C:] * decay_b).astype(bf16)
        return carry

    lax.fori_loop(0, nunits, phase_a, 0, unroll=4)

    def swap_in(ib, carry):
        rows = pl.ds(pl.multiple_of(ib * 8, 8), 8)
        tl_sc[rows] = jnp.swapaxes(at_sc[:, rows, :], 0, 1)
        return carry

    def to_lanes(i, carry):
        apl_sc[i] = tl_sc[i].T
        return carry

    lax.fori_loop(0, C // 8, swap_in, 0)
    lax.fori_loop(0, C, to_lanes, 0, unroll=4)
    tl_sc[...] = jnp.zeros(tl_sc.shape, f32)
    _tri_solve_rows(apl_sc, tl_sc, 0, False)
    _tri_solve_rows(apl_sc, tl_sc, 1, True)

    def from_lanes(i, carry):
        apl_sc[i] = tl_sc[i].T
        return carry

    def swap_out(ib, carry):
        rows = pl.ds(pl.multiple_of(ib * 8, 8), 8)
        at_sc[:, rows, :] = jnp.swapaxes(apl_sc[rows], 0, 1)
        return carry

    lax.fori_loop(0, C, from_lanes, 0, unroll=4)
    lax.fori_loop(0, C // 8, swap_out, 0)

    u0_sc = (apl_sc, tl_sc)

    def u0_slot(p):
        return p // 2, pl.ds(pl.multiple_of((p % 2) * C, C), C)

    eye_bf = (lax.broadcasted_iota(jnp.int32, (GDN_HEAD, GDN_HEAD), 0)
              == lax.broadcasted_iota(jnp.int32, (GDN_HEAD, GDN_HEAD), 1)).astype(bf16)

    def phase_b2(unit, carry):
        bl, off = unit_of(unit)
        g_rows = grow_sc[unit]
        g_cols = gcol_sc[unit]
        zero_kv = jnp.zeros((C, GDN_HEAD), bf16)
        for h in range(nh):
            hs = slice(h * GDN_HEAD, (h + 1) * GDN_HEAD)
            p = unit * nh + h
            cbf, cbr = h, nh + h
            cgf, cgr = nbeta + h, nbeta + nh + h
            t_both = at_sc[p]
            kb = k_ref[bl, pl.ds(off, C), hs]
            vb = v_ref[bl, pl.ds(off, C), hs]
            g_row = jnp.concatenate([g_rows[cgf:cgf + 1, :], g_rows[cgr:cgr + 1, :]], axis=1)
            b_row = jnp.concatenate([g_rows[cbf:cbf + 1, :], g_rows[cbr:cbr + 1, :]], axis=1)
            g_last = jnp.concatenate([jnp.broadcast_to(g_cols[C - 1:C, cgf:cgf + 1], (1, C)),
                                      jnp.broadcast_to(g_cols[0:1, cgr:cgr + 1], (1, C))], axis=1)
            v_bd = jnp.concatenate([jnp.concatenate([vb, zero_kv], axis=1),
                                    jnp.concatenate([zero_kv, vb], axis=1)], axis=0)
            k_bd = jnp.concatenate([jnp.concatenate([kb, zero_kv], axis=1),
                                    jnp.concatenate([zero_kv, kb], axis=1)], axis=0)
            u0 = jnp.dot(t_both.astype(bf16), v_bd, preferred_element_type=f32)
            w = jnp.dot((t_both * jnp.exp(g_row)).astype(bf16), k_bd, preferred_element_type=f32)
            slab, rows = u0_slot(p)
            for dirn in range(N_DIR):
                u0_sc[dirn][slab, rows, :] = u0[:, dirn * GDN_HEAD:(dirn + 1) * GDN_HEAD]
                w_sc[dirn, p] = w[:, dirn * GDN_HEAD:(dirn + 1) * GDN_HEAD].astype(bf16)
            k_t = lax.dot_general(eye_bf, jnp.concatenate([kb, kb], axis=0), (((1,), (1,)), ((), ())),
                                  preferred_element_type=f32)
            mk_sc[p, C:, :] = (k_t * (b_row * jnp.exp(g_last - g_row))).astype(bf16)
        return carry

    lax.fori_loop(0, nunits, phase_b2, 0, unroll=8)

    def phase_c(step, carry, accumulate):
        chains = []
        for dirn, unit in ((0, step), (1, nunits - 1 - step)):
            bl, off = unit_of(unit)
            g_col_all = gcol_sc[unit]
            g_tot_all = g_col_all[0:1, :] if dirn else g_col_all[C - 1:C, :]
            for h in range(nh):
                cg = nbeta + dirn * nh + h
                hs = slice(h * GDN_HEAD, (h + 1) * GDN_HEAD)
                p = unit * nh + h
                slab, rows = u0_slot(p)
                lhs = w_sc[dirn, p]
                ch = dict(idx=(bl, dirn, h), off=off, hs=hs, dirn=dirn, u0=u0_sc[dirn][slab, rows, :],
                          g_last=jnp.exp(g_tot_all[:, cg:cg + 1]))
                if emit_o:
                    lhs = jnp.concatenate([lhs, q_ref[bl, pl.ds(off, C), hs]], axis=0)
                    ch['mk'] = mk_sc[p]
                    ch['gam'] = jnp.exp(g_col_all[:, cg:cg + 1])
                else:
                    ch['mk'] = mk_sc[p, C:, :]
                ch['lhs'] = lhs
                chains.append(ch)
        s_old = [s_sc[ch['idx']] for ch in chains]
        r = [jnp.dot(ch['lhs'], s.astype(bf16), preferred_element_type=f32) for ch, s in zip(chains, s_old)]
        zeros = jnp.zeros((C, GDN_HEAD), bf16)
        ub = [(ch['u0'] - ri[:C]).astype(bf16) for ch, ri in zip(chains, r)]
        m = [jnp.dot(ch['mk'], jnp.concatenate([zeros, ubi] if ch['dirn'] else [ubi, zeros], axis=0),
                     preferred_element_type=f32) for ch, ubi in zip(chains, ub)]
        for n, ch in enumerate(chains):
            if emit_o:
                bl = ch['idx'][0]
                o_new = ch['gam'] * r[n][C:] + m[n][:C]
                if accumulate:
                    o_ref[bl, pl.ds(ch['off'], C), ch['hs']] += o_new
                else:
                    o_ref[bl, pl.ds(ch['off'], C), ch['hs']] = o_new
                s_sc[ch['idx']] = ch['g_last'] * s_old[n] + m[n][C:]
            else:
                s_sc[ch['idx']] = ch['g_last'] * s_old[n] + m[n]
        return carry

    if emit_o and bb == 1:
        lax.fori_loop(0, nunits // 2, functools.partial(phase_c, accumulate=False), 0)
        lax.fori_loop(nunits // 2, nunits, functools.partial(phase_c, accumulate=True), 0)
    elif emit_o:
        o_ref[...] = jnp.zeros(o_ref.shape, f32)
        lax.fori_loop(0, nunits, functools.partial(phase_c, accumulate=True), 0)
    else:
        lax.fori_loop(0, nunits, functools.partial(phase_c, accumulate=False), 0)
    sfin_ref[...] = s_sc[...]


def _gdn_scan(q, k, v, gates, s0, *, emit_o):
    nb, L, d_gdn = q.shape
    nh = d_gdn // GDN_HEAD
    nchunks = L // GDN_CHUNK
    C = GDN_CHUNK
    bb = max(1, min(nb, LANES // (nchunks * nh)))
    nunits = bb * nchunks
    nprob = nunits * nh
    assert nprob == LANES, (nb, L, nh)
    has_s0 = s0 is not None
    kern = functools.partial(_gdn_scan_kernel, nchunks=nchunks, nh=nh, emit_o=emit_o, has_s0=has_s0)
    seq = lambda b: (b, 0, 0)
    st = lambda b: (b, 0, 0, 0, 0)
    st_spec = pl.BlockSpec((bb, N_DIR, nh, GDN_HEAD, GDN_HEAD), st)
    st_shape = jax.ShapeDtypeStruct((nb, N_DIR, nh, GDN_HEAD, GDN_HEAD), f32)
    args = [q, k, v, gates]
    in_specs = [pl.BlockSpec((bb, L, d_gdn), seq)] * 3 + [pl.BlockSpec((bb, L, LANES), seq)]
    if has_s0:
        args.append(s0)
        in_specs.append(st_spec)
    out_specs, out_shape = [], []
    if emit_o:
        out_specs.append(pl.BlockSpec((bb, L, d_gdn), seq))
        out_shape.append(jax.ShapeDtypeStruct((nb, L, d_gdn), f32))
    out_specs.append(st_spec)
    out_shape.append(st_shape)
    return pl.pallas_call(
        kern,
        grid=(nb // bb,),
        in_specs=in_specs,
        out_specs=out_specs,
        out_shape=out_shape,
        scratch_shapes=[pltpu.VMEM((bb, N_DIR, nh, GDN_HEAD, GDN_HEAD), f32),
                        pltpu.VMEM((nprob, C, N_DIR * C), f32),
                        pltpu.VMEM((C, N_DIR * C, nprob), f32),
                        pltpu.VMEM((C, N_DIR * C, nprob), f32),
                        pltpu.VMEM((nprob, C + GDN_HEAD, N_DIR * C), bf16),
                        pltpu.VMEM((N_DIR, nprob, C, GDN_HEAD), bf16),
                        pltpu.VMEM((nunits, C, LANES), f32),
                        pltpu.VMEM((nunits, 4 * nh, C), f32)],
        compiler_params=_cparams(("parallel",)),
        name="gdn_scan" if emit_o else "gdn_scan_state",
    )(*args)


def _post_kernel(x_ref, gate_ref, y5_ref, zs_ref, og_ref, zg_ref,
                 wglu_ref, bglu_ref, nw_ref, wout_ref, lng_ref, lnb_ref, o_ref, *, tt, alpha):
    nb, _, d = x_ref.shape
    rows = nb * tt
    npk = y5_ref.shape[0]
    d_s5 = npk * LANES
    d_gdn = og_ref.shape[2]
    parts = []
    for j in range(npk):
        parts.append(jnp.swapaxes(y5_ref[j].reshape(tt, nb, LANES), 0, 1).reshape(rows, LANES))
    y = jnp.concatenate(parts, axis=1)
    g = 0.5 * y * (1.0 + lax.erf(y * (2.0 ** -0.5)))
    glu = jnp.dot(g.astype(bf16), wglu_ref[...], preferred_element_type=f32) + bglu_ref[...]
    s5 = g * _sigmoid(glu) * _silu(zs_ref[...].astype(f32).reshape(rows, d_s5))
    o = og_ref[...].reshape(rows, d_gdn)
    zg = zg_ref[...].astype(f32).reshape(rows, d_gdn)
    gd = []
    for h in range(d_gdn // GDN_HEAD):
        oh = o[:, h * GDN_HEAD:(h + 1) * GDN_HEAD]
        gd.append(oh * lax.rsqrt(jnp.mean(oh * oh, axis=-1, keepdims=True) + NORM_EPS) * nw_ref[...])
    gdn = jnp.concatenate(gd, axis=1) * _silu(zg)
    mix = jnp.concatenate([s5, gdn], axis=1).astype(bf16)
    yv = jnp.dot(mix, wout_ref[...], preferred_element_type=f32).reshape(nb, tt, d)
    r = alpha * x_ref[...] + gate_ref[...] * yv
    mu = jnp.mean(r, axis=-1, keepdims=True)
    var = jnp.mean(jnp.square(r - mu), axis=-1, keepdims=True)
    o_ref[...] = (r - mu) * lax.rsqrt(var + LN_EPS) * lng_ref[...] + lnb_ref[...]


def _post(x, gate, y5, zs, o_gdn, zg, wglu, bglu, nw, wout, lng, lnb, alpha):
    nb, L, d = x.shape
    tt = 32
    npk = y5.shape[0]
    d_s5 = npk * LANES
    d_gdn = o_gdn.shape[2]
    kern = functools.partial(_post_kernel, tt=tt, alpha=alpha)
    tok = lambda i: (0, i, 0)
    c2 = lambda i: (0, 0)
    c3 = lambda i: (0, 0, 0)
    y_spec = pl.BlockSpec((npk, tt // S5_T, S5_T, nb, LANES), lambda i: (0, i, 0, 0, 0))
    return pl.pallas_call(
        kern,
        grid=(L // tt,),
        in_specs=[pl.BlockSpec((nb, tt, d), tok),
                  pl.BlockSpec((nb, 1, d), c3),
                  y_spec,
                  pl.BlockSpec((nb, tt, d_s5), tok),
                  pl.BlockSpec((nb, tt, d_gdn), tok),
                  pl.BlockSpec((nb, tt, d_gdn), tok),
                  pl.BlockSpec(wglu.shape, c2),
                  pl.BlockSpec(bglu.shape, c2),
                  pl.BlockSpec(nw.shape, c2),
                  pl.BlockSpec(wout.shape, c2),
                  pl.BlockSpec(lng.shape, c3),
                  pl.BlockSpec(lnb.shape, c3)],
        out_specs=pl.BlockSpec((nb, tt, d), tok),
        out_shape=jax.ShapeDtypeStruct((nb, L, d), f32),
        compiler_params=_cparams(("parallel",)),
        name="post",
    )(x, gate, y5, zs, o_gdn, zg, wglu, bglu, nw, wout, lng, lnb)


def _layer(x, c, ctx, c_ctx, p, alpha):
    nb, L, d = x.shape
    d_s5 = p['s5_d'].shape[0]
    d_gdn = p['conv_w'].shape[-1] // 3
    nh = d_gdn // GDN_HEAD
    npk = d_s5 // LANES

    pad = (-(nb + 1)) % 8
    cond = jnp.concatenate([c, c_ctx[None, :], jnp.zeros((pad, d), f32)], axis=0)
    m = _ada(cond, p['w_ada'], p['b_ada'])
    shift, scale, gate = m[:nb, :d], m[:nb, d:2 * d], m[:nb, 2 * d:]
    shift_c = jnp.broadcast_to(m[nb:nb + 1, :d], (nb, d))
    scale_c = jnp.broadcast_to(m[nb:nb + 1, d:2 * d], (nb, d))

    ng = 2 * N_DIR * nh
    w_all = jnp.pad(p['w_in'], ((0, 0), (0, LANES - ng))).astype(bf16)
    gp = jnp.pad(jnp.stack([p['gdn_a_log'].reshape(-1), p['gdn_dt_bias'].reshape(-1)]).astype(f32),
                 ((0, 0), (ng // 2, LANES - ng)))

    u5, zs, qkv, zg, gates = _inproj(x, scale[:, None, :], shift[:, None, :], w_all, gp, d_s5, d_gdn)
    u5c, _, qkvc, _, gatesc = _inproj(ctx, scale_c[:, None, :], shift_c[:, None, :], w_all, gp, d_s5, d_gdn)

    sw = _s5_weights(p['s5_lambda_re'], p['s5_lambda_im'], p['s5_log_dt'],
                     p['s5_b_re'], p['s5_b_im'], p['s5_c_re'], p['s5_c_im'])
    sdim = sw['a_f'].shape[-1]
    a_f = jnp.broadcast_to(sw['a_f'][:, None, :], (npk, nb, sdim))
    a_r = jnp.broadcast_to(sw['a_r'][:, None, :], (npk, nb, sdim))
    h_zero = jnp.zeros((npk, nb, sdim), f32)
    d_skip = jnp.tile(p['s5_d'].astype(f32).reshape(npk, 1, LANES), (1, 1, S5_T))
    (hc_f,) = _s5_pass(u5c, sw['m_in_f'], a_f, h_zero, rev=False, emit_y=False)
    (hc_r,) = _s5_pass(u5c, sw['m_in_r'], a_r, h_zero, rev=True, emit_y=False)
    y_f, _ = _s5_pass(u5, sw['m_in_f'], a_f, hc_f, sw['m_out_f'], sw['m_intra'], d_skip, rev=False, emit_y=True)
    y_s5, _ = _s5_pass(u5, sw['m_in_r'], a_r, hc_r, sw['m_out_r'], y_add=y_f, rev=True, emit_y=True)

    ql, kl, vl = _gdn_prep(qkv, p['conv_w'], True)
    qc, kc, vc = _gdn_prep(qkvc, p['conv_w'], False)
    (s_ctx,) = _gdn_scan(qc, kc, vc, gatesc, None, emit_o=False)
    o_gdn, _ = _gdn_scan(ql, kl, vl, gates, s_ctx, emit_o=True)

    return _post(x, gate[:, None, :], y_s5, zs, o_gdn, zg,
                 p['w_glu'].astype(bf16), p['b_glu'].astype(f32).reshape(1, d_s5),
                 p['gdn_norm_w'].astype(f32).reshape(1, GDN_HEAD),
                 p['w_out'].astype(bf16), p['ln_g'].astype(f32).reshape(1, 1, d),
                 p['ln_b'].astype(f32).reshape(1, 1, d), alpha)


def kernel(x, c, ctx, c_ctx, w_ada, b_ada, w_in, s5_lambda_re, s5_lambda_im, s5_log_dt, s5_b_re, s5_b_im, s5_c_re, s5_c_im, s5_d, w_glu, b_glu, conv_w, gdn_a_log, gdn_dt_bias, gdn_norm_w, w_out, ln_g, ln_b):
    depth = w_ada.shape[0]
    assert depth == 1, "context-token outputs are only skipped for the last layer"
    alpha = (2.0 * depth) ** 0.25
    p = {
        'w_ada': w_ada[0], 'b_ada': b_ada[0], 'w_in': w_in[0],
        's5_lambda_re': s5_lambda_re[0], 's5_lambda_im': s5_lambda_im[0], 's5_log_dt': s5_log_dt[0],
        's5_b_re': s5_b_re[0], 's5_b_im': s5_b_im[0], 's5_c_re': s5_c_re[0], 's5_c_im': s5_c_im[0],
        's5_d': s5_d[0], 'w_glu': w_glu[0], 'b_glu': b_glu[0], 'conv_w': conv_w[0],
        'gdn_a_log': gdn_a_log[0], 'gdn_dt_bias': gdn_dt_bias[0], 'gdn_norm_w': gdn_norm_w[0],
        'w_out': w_out[0], 'ln_g': ln_g[0], 'ln_b': ln_b[0],
    }
    return _layer(x, c, ctx, c_ctx, p, alpha)
```
